```python
import math
import jax
import jax.numpy as jnp
from jax import lax
import numpy as np

D_MODEL = 1024
BATCH = 8
SEQ = 4096
DEPTH = 4

N_META = 16
CHUNK = 64
META_PAD = CHUNK - N_META
CONV_K = 4
N_BRANCH = 3
BRANCH_WIDTH = 768

S5_WIDTH = BRANCH_WIDTH
S5_GROUP = 16
S5_GROUPS = S5_WIDTH // S5_GROUP
S5_STATE = 64
S5_STEP_MIN = 1e-3
S5_STEP_MAX = 1e-1

SSD_HEAD_DIM = 64
SSD_HEADS = BRANCH_WIDTH // SSD_HEAD_DIM
SSD_WIDTH = SSD_HEADS * SSD_HEAD_DIM
SSD_GROUPS = 2
SSD_STATE = 128
SSD_CONV_WIDTH = SSD_WIDTH + 2 * SSD_GROUPS * SSD_STATE

GDN_HEAD_DIM = 128
GDN_HEADS = BRANCH_WIDTH // GDN_HEAD_DIM
GDN_WIDTH = GDN_HEADS * GDN_HEAD_DIM

IN_SPLITS = (S5_WIDTH, S5_WIDTH,
             SSD_CONV_WIDTH, SSD_HEADS, SSD_WIDTH,
             3 * GDN_WIDTH, GDN_HEADS, GDN_HEADS, GDN_WIDTH,
             N_BRANCH * D_MODEL)
IN_WIDTH = sum(IN_SPLITS)

ALPHA = (2 * DEPTH) ** 0.25
BETA = (8 * DEPTH) ** -0.25
LN_EPS = 1e-5

kernel_name = 'hybrid_s5_ssd_gdn_gated_merge'


def _split_points():
    pts, acc = [], 0
    for w in IN_SPLITS[:-1]:
        acc += w
        pts.append(acc)
    return pts


def _layer_norm(z, g, b):
    zf = z.astype(jnp.float32)
    mu = jnp.mean(zf, axis=-1, keepdims=True)
    var = jnp.mean(jnp.square(zf - mu), axis=-1, keepdims=True)
    return ((zf - mu) * lax.rsqrt(var + LN_EPS) * g + b).astype(z.dtype)


def _rms_norm(z, g):
    zf = z.astype(jnp.float32)
    return zf * lax.rsqrt(jnp.mean(zf * zf, axis=-1, keepdims=True) + LN_EPS) * g.astype(jnp.float32)


def _l2norm(z):
    return z * lax.rsqrt(jnp.sum(z * z, axis=-1, keepdims=True) + 1e-6)


def _front_pad(z):
    return jnp.pad(z, [(0, 0), (META_PAD, 0)] + [(0, 0)] * (z.ndim - 2))


def _causal_dwconv(u, w):
    k, ch = w.shape
    return lax.conv_general_dilated(u, w.astype(u.dtype)[:, None, :], window_strides=(1,),
                                    padding=[(k - 1, 0)], dimension_numbers=('NWC', 'WIO', 'NWC'),
                                    feature_group_count=ch)


def _complex_affine_combine(earlier, later):
    a1r, a1i, b1r, b1i = earlier
    a2r, a2i, b2r, b2i = later
    return (a2r * a1r - a2i * a1i, a2r * a1i + a2i * a1r,
            a2r * b1r - a2i * b1i + b2r, a2r * b1i + a2i * b1r + b2i)


def _s5_branch(u, z, a_re, a_im, log_step, b_re, b_im, c_re, c_im, d, w_glu, b_glu):
    bsz, t, _ = u.shape
    uf = u.astype(jnp.float32)
    ug = uf.reshape(bsz, t, S5_GROUPS, S5_GROUP)
    lam_re = jnp.minimum(a_re.astype(jnp.float32), -1e-4)
    lam_im = a_im.astype(jnp.float32)
    step = jnp.exp(log_step.astype(jnp.float32))[:, None]
    mag = jnp.exp(lam_re * step)
    abar_re, abar_im = mag * jnp.cos(lam_im * step), mag * jnp.sin(lam_im * step)
    den = lam_re * lam_re + lam_im * lam_im
    nr, ni = abar_re - 1.0, abar_im
    coef_re = (nr * lam_re + ni * lam_im) / den
    coef_im = (ni * lam_re - nr * lam_im) / den
    bbar_re = coef_re[..., None] * b_re - coef_im[..., None] * b_im
    bbar_im = coef_re[..., None] * b_im + coef_im[..., None] * b_re
    bu_re = jnp.einsum('btgc,gpc->btgp', ug, bbar_re)
    bu_im = jnp.einsum('btgc,gpc->btgp', ug, bbar_im)
    ae_re = jnp.broadcast_to(abar_re, (1, t, S5_GROUPS, S5_STATE))
    ae_im = jnp.broadcast_to(abar_im, (1, t, S5_GROUPS, S5_STATE))
    _, _, s_re, s_im = lax.associative_scan(_complex_affine_combine, (ae_re, ae_im, bu_re, bu_im), axis=1)
    y = jnp.einsum('btgp,gcp->btgc', s_re, c_re) - jnp.einsum('btgp,gcp->btgc', s_im, c_im)
    y = y.reshape(bsz, t, S5_WIDTH) + d * uf
    v = jax.nn.gelu(y)
    v = v * jax.nn.sigmoid(v @ w_glu + b_glu)
    return (v * jax.nn.silu(z.astype(jnp.float32))).astype(u.dtype)


def _ssd_chunked(x, dt, a, b, c):
    bsz, t, h, p = x.shape
    g, n = b.shape[2], b.shape[3]
    r = h // g
    nc = t // CHUNK
    xd = (x * dt[..., None]).reshape(bsz, nc, CHUNK, g, r, p)
    acum = jnp.cumsum((dt * a).reshape(bsz, nc, CHUNK, g, r), axis=2)
    bc = b.reshape(bsz, nc, CHUNK, g, n)
    cc = c.reshape(bsz, nc, CHUNK, g, n)
    causal = jnp.tril(jnp.ones((CHUNK, CHUNK), dtype=bool))
    seg = acum[:, :, :, None] - acum[:, :, None]
    decay = jnp.exp(jnp.where(causal[:, :, None, None], seg, -jnp.inf))
    scores = jnp.einsum('bclgn,bcsgn->bclsg', cc, bc)[..., None] * decay
    y_diag = jnp.einsum('bclsgr,bcsgrp->bclgrp', scores, xd)
    to_end = jnp.exp(acum[:, :, -1:] - acum)
    states = jnp.einsum('bcsgn,bcsgr,bcsgrp->bcgrpn', bc, to_end, xd)
    chunk_decay = jnp.exp(acum[:, :, -1])

    def step(state, inp):
        st, dec = inp
        return state * dec[..., None, None] + st, state

    init = jnp.zeros((bsz, g, r, p, n), x.dtype)
    _, prev = lax.scan(step, init, (jnp.moveaxis(states, 1, 0), jnp.moveaxis(chunk_decay, 1, 0)))
    y_off = jnp.einsum('bclgn,cbgrpn,bclgr->bclgrp', cc, prev, jnp.exp(acum))
    return (y_diag + y_off).reshape(bsz, t, h, p)


def _ssd_branch(xbc, dt_raw, z, conv_w, conv_b, dt_bias, a_log, d, norm_g):
    bsz, t, _ = xbc.shape
    xbc = jax.nn.silu(_causal_dwconv(xbc, conv_w) + conv_b).astype(jnp.float32)
    xs, bs, cs = jnp.split(xbc, [SSD_WIDTH, SSD_WIDTH + SSD_GROUPS * SSD_STATE], axis=-1)
    xs = xs.reshape(bsz, t, SSD_HEADS, SSD_HEAD_DIM)
    bs = bs.reshape(bsz, t, SSD_GROUPS, SSD_STATE)
    cs = cs.reshape(bsz, t, SSD_GROUPS, SSD_STATE)
    dt = jax.nn.softplus(dt_raw.astype(jnp.float32) + dt_bias.astype(jnp.float32))
    a = -jnp.exp(a_log.astype(jnp.float32))
    y = _ssd_chunked(_front_pad(xs), _front_pad(dt), a, _front_pad(bs), _front_pad(cs))[:, META_PAD:]
    y = y + xs * d.astype(jnp.float32)[:, None]
    y = y.reshape(bsz, t, SSD_WIDTH) * jax.nn.silu(z.astype(jnp.float32))
    return _rms_norm(y, norm_g).astype(z.dtype)


def _gated_delta_chunked(q, k, v, beta, g):
    bsz, t, h, dk = k.shape
    dv = v.shape[-1]
    nc = t // CHUNK
    q, k, v, beta, g = (u.reshape(bsz, nc, CHUNK, *u.shape[2:]) for u in (q, k, v, beta, g))
    gcum = jnp.cumsum(g, axis=2)
    causal = jnp.tril(jnp.ones((CHUNK, CHUNK), dtype=bool))
    strict = jnp.tril(jnp.ones((CHUNK, CHUNK), dtype=bool), -1)
    seg = gcum[:, :, :, None, :] - gcum[:, :, None, :, :]
    gamma = jnp.exp(jnp.where(causal[:, :, None], seg, -jnp.inf))
    kk = jnp.einsum('bclhd,bcshd->bclsh', k, k)
    a_mat = jnp.where(strict[:, :, None], kk * gamma * beta[:, :, :, None, :], 0.0)
    rhs = jnp.concatenate([v * beta[..., None], k * (beta * jnp.exp(gcum))[..., None]], axis=-1)
    sol = lax.linalg.triangular_solve(a_mat.transpose(0, 1, 4, 2, 3), rhs.transpose(0, 1, 3, 2, 4),
                                      left_side=True, lower=True, unit_diagonal=True)
    u_c, w_c = sol[..., :dv], sol[..., dv:]
    attn = jnp.einsum('bclhd,bcshd->bchls', q, k) * gamma.transpose(0, 1, 4, 2, 3)
    qg = (q * jnp.exp(gcum)[..., None]).transpose(0, 1, 3, 2, 4)
    kd = (k * jnp.exp(gcum[:, :, -1:] - gcum)[..., None]).transpose(0, 1, 3, 2, 4)
    last = jnp.exp(gcum[:, :, -1])

    def step(state, inp):
        u_i, w_i, qg_i, kd_i, attn_i, last_i = inp
        v_new = u_i - jnp.einsum('bhlk,bhkv->bhlv', w_i, state)
        o = jnp.einsum('bhlk,bhkv->bhlv', qg_i, state) + jnp.einsum('bhls,bhsv->bhlv', attn_i, v_new)
        state = state * last_i[..., None, None] + jnp.einsum('bhlk,bhlv->bhkv', kd_i, v_new)
        return state, o

    init = jnp.zeros((bsz, h, dk, dv), q.dtype)
    xs = tuple(jnp.moveaxis(z, 1, 0) for z in (u_c, w_c, qg, kd, attn, last))
    _, o = lax.scan(step, init, xs)
    return o.transpose(1, 0, 3, 2, 4).reshape(bsz, t, h, dv)


def _gdn_branch(qkv, a_raw, b_raw, z, conv_w, dt_bias, a_log, norm_g):
    bsz, t, _ = qkv.shape
    qkv = jax.nn.silu(_causal_dwconv(qkv, conv_w)).astype(jnp.float32)
    q, k, v = jnp.split(qkv, 3, axis=-1)
    heads = lambda u: u.reshape(bsz, t, GDN_HEADS, GDN_HEAD_DIM)
    q = _l2norm(heads(q)) * GDN_HEAD_DIM ** -0.5
    k = _l2norm(heads(k))
    v = heads(v)
    beta = jax.nn.sigmoid(b_raw.astype(jnp.float32))
    g = -jnp.exp(a_log.astype(jnp.float32)) * jax.nn.softplus(a_raw.astype(jnp.float32) + dt_bias.astype(jnp.float32))
    o = _gated_delta_chunked(_front_pad(q), _front_pad(k), _front_pad(v), _front_pad(beta), _front_pad(g))[:, META_PAD:]
    o = _rms_norm(o, norm_g).reshape(bsz, t, GDN_WIDTH) * jax.nn.silu(z.astype(jnp.float32))
    return o.astype(z.dtype)


def _inv_softplus(y):
    return y + jnp.log(-jnp.expm1(-y))


def _log_uniform(key, shape, lo, hi):
    return jnp.exp(jax.random.uniform(key, shape, jnp.float32, math.log(lo), math.log(hi)))


def _fwd_setup_inputs(seed: int = 0) -> dict:
    key = jax.random.key(seed)
    ks = jax.random.split(key, 32)

    def nrm(i, shape, scale):
        return scale * jax.random.normal(ks[i], shape, jnp.float32)

    n_idx = jnp.arange(S5_STATE, dtype=jnp.float32)
    s5_shape = (DEPTH, S5_GROUPS, S5_STATE)
    return {
        'x': nrm(0, (BATCH, SEQ, D_MODEL), 1.0),
        'meta': nrm(1, (N_META, D_MODEL), 1.0),
        'ln_in_g': 1.0 + nrm(2, (D_MODEL,), 0.02),
        'ln_in_b': nrm(3, (D_MODEL,), 0.02),
        'w_in': nrm(4, (DEPTH, D_MODEL, IN_WIDTH), D_MODEL ** -0.5),
        's5_a_re': -0.5 + nrm(5, s5_shape, 0.01),
        's5_a_im': math.pi * n_idx + nrm(6, s5_shape, 0.01),
        's5_log_step': jax.random.uniform(ks[7], (DEPTH, S5_GROUPS), jnp.float32, math.log(S5_STEP_MIN), math.log(S5_STEP_MAX)),
        's5_b_re': nrm(8, (DEPTH, S5_GROUPS, S5_STATE, S5_GROUP), S5_GROUP ** -0.5),
        's5_b_im': nrm(9, (DEPTH, S5_GROUPS, S5_STATE, S5_GROUP), S5_GROUP ** -0.5),
        's5_c_re': nrm(10, (DEPTH, S5_GROUPS, S5_GROUP, S5_STATE), S5_STATE ** -0.5),
        's5_c_im': nrm(11, (DEPTH, S5_GROUPS, S5_GROUP, S5_STATE), S5_STATE ** -0.5),
        's5_d': nrm(12, (DEPTH, S5_WIDTH), 1.0),
        's5_w_glu': nrm(13, (DEPTH, S5_WIDTH, S5_WIDTH), S5_WIDTH ** -0.5),
        's5_b_glu': nrm(14, (DEPTH, S5_WIDTH), 0.02),
        'ssd_conv_w': nrm(15, (DEPTH, CONV_K, SSD_CONV_WIDTH), CONV_K ** -0.5),
        'ssd_conv_b': nrm(16, (DEPTH, SSD_CONV_WIDTH), 0.02),
        'ssd_dt_bias': _inv_softplus(_log_uniform(ks[17], (DEPTH, SSD_HEADS), 1e-3, 1e-1)),
        'ssd_a_log': jnp.log(jax.random.uniform(ks[18], (DEPTH, SSD_HEADS), jnp.float32, 1.0, 16.0)),
        'ssd_d': 1.0 + nrm(19, (DEPTH, SSD_HEADS), 0.1),
        'ssd_norm_g': 1.0 + nrm(20, (DEPTH, SSD_WIDTH), 0.02),
        'gdn_conv_w': nrm(21, (DEPTH, CONV_K, 3 * GDN_WIDTH), CONV_K ** -0.5),
        'gdn_dt_bias': _inv_softplus(_log_uniform(ks[22], (DEPTH, GDN_HEADS), 1e-3, 1e-1)),
        'gdn_a_log': jnp.log(jax.random.uniform(ks[23], (DEPTH, GDN_HEADS), jnp.float32, 1.0, 16.0)),
        'gdn_norm_g': 1.0 + nrm(24, (DEPTH, GDN_HEAD_DIM), 0.02),
        'w_branch': nrm(25, (DEPTH, N_BRANCH, BRANCH_WIDTH, D_MODEL), BRANCH_WIDTH ** -0.5 * BETA),
        'b_gate': nrm(26, (DEPTH, N_BRANCH, D_MODEL), 0.02),
        'w_out': nrm(27, (DEPTH, D_MODEL, D_MODEL), D_MODEL ** -0.5 * BETA),
        'ln_g': 1.0 + nrm(28, (DEPTH, D_MODEL), 0.02),
        'ln_b': nrm(29, (DEPTH, D_MODEL), 0.02),
    }


def _fwd_reference(x, meta, ln_in_g, ln_in_b, w_in, s5_a_re, s5_a_im, s5_log_step, s5_b_re, s5_b_im,
              s5_c_re, s5_c_im, s5_d, s5_w_glu, s5_b_glu, ssd_conv_w, ssd_conv_b, ssd_dt_bias,
              ssd_a_log, ssd_d, ssd_norm_g, gdn_conv_w, gdn_dt_bias, gdn_a_log, gdn_norm_g,
              w_branch, b_gate, w_out, ln_g, ln_b):
    bsz = x.shape[0]
    h = jnp.concatenate([jnp.broadcast_to(meta[None].astype(x.dtype), (bsz, N_META, D_MODEL)), x], axis=1)
    h = _layer_norm(h, ln_in_g, ln_in_b)
    t = h.shape[1]
    points = _split_points()
    for layer in range(DEPTH):
        proj = h @ w_in[layer]
        (s5_u, s5_z, ssd_xbc, ssd_dt, ssd_z, gdn_qkv, gdn_a, gdn_b, gdn_z,
         gate_logits) = jnp.split(proj, points, axis=-1)
        y_a = _s5_branch(s5_u, s5_z, s5_a_re[layer], s5_a_im[layer], s5_log_step[layer], s5_b_re[layer],
                         s5_b_im[layer], s5_c_re[layer], s5_c_im[layer], s5_d[layer], s5_w_glu[layer],
                         s5_b_glu[layer])
        y_b = _ssd_branch(ssd_xbc, ssd_dt, ssd_z, ssd_conv_w[layer], ssd_conv_b[layer], ssd_dt_bias[layer],
                          ssd_a_log[layer], ssd_d[layer], ssd_norm_g[layer])
        y_c = _gdn_branch(gdn_qkv, gdn_a, gdn_b, gdn_z, gdn_conv_w[layer], gdn_dt_bias[layer],
                          gdn_a_log[layer], gdn_norm_g[layer])
        branches = jnp.stack([y_a, y_b, y_c], axis=2)
        outs = jnp.einsum('btnw,nwd->btnd', branches, w_branch[layer])
        gates = jax.nn.sigmoid(gate_logits.reshape(bsz, t, N_BRANCH, D_MODEL) + b_gate[layer])
        merged = jnp.sum(gates * outs, axis=2)
        h = _layer_norm(ALPHA * h + merged @ w_out[layer], ln_g[layer], ln_b[layer])
    return h[:, N_META:]


import jax as _jax
import jax.numpy as _jnp

TWIN_FORMAT = 'train_step'
FWD_PARAMS = ['x', 'meta', 'ln_in_g', 'ln_in_b', 'w_in', 's5_a_re', 's5_a_im', 's5_log_step', 's5_b_re', 's5_b_im', 's5_c_re', 's5_c_im', 's5_d', 's5_w_glu', 's5_b_glu', 'ssd_conv_w', 'ssd_conv_b', 'ssd_dt_bias', 'ssd_a_log', 'ssd_d', 'ssd_norm_g', 'gdn_conv_w', 'gdn_dt_bias', 'gdn_a_log', 'gdn_norm_g', 'w_branch', 'b_gate', 'w_out', 'ln_g', 'ln_b']
TWIN_WEIGHTS = ['meta', 'ln_in_g', 'ln_in_b', 'w_in', 's5_a_re', 's5_a_im', 's5_log_step', 's5_b_re', 's5_b_im', 's5_c_re', 's5_c_im', 's5_d', 's5_w_glu', 's5_b_glu', 'ssd_conv_w', 'ssd_conv_b', 'ssd_dt_bias', 'ssd_a_log', 'ssd_d', 'ssd_norm_g', 'gdn_conv_w', 'gdn_dt_bias', 'gdn_a_log', 'gdn_norm_g', 'w_branch', 'b_gate', 'w_out', 'ln_g', 'ln_b']
TWIN_DIFF_INPUT = 'x'
TWIN_INPUTS = ['x', 'meta', 'ln_in_g', 'ln_in_b', 'w_in', 's5_a_re', 's5_a_im', 's5_log_step', 's5_b_re', 's5_b_im', 's5_c_re', 's5_c_im', 's5_d', 's5_w_glu', 's5_b_glu', 'ssd_conv_w', 'ssd_conv_b', 'ssd_dt_bias', 'ssd_a_log', 'ssd_d', 'ssd_norm_g', 'gdn_conv_w', 'gdn_dt_bias', 'gdn_a_log', 'gdn_norm_g', 'w_branch', 'b_gate', 'w_out', 'ln_g', 'ln_b', 'loss_target', 'm_meta', 'm_ln_in_g', 'm_ln_in_b', 'm_w_in', 'm_s5_a_re', 'm_s5_a_im', 'm_s5_log_step', 'm_s5_b_re', 'm_s5_b_im', 'm_s5_c_re', 'm_s5_c_im', 'm_s5_d', 'm_s5_w_glu', 'm_s5_b_glu', 'm_ssd_conv_w', 'm_ssd_conv_b', 'm_ssd_dt_bias', 'm_ssd_a_log', 'm_ssd_d', 'm_ssd_norm_g', 'm_gdn_conv_w', 'm_gdn_dt_bias', 'm_gdn_a_log', 'm_gdn_norm_g', 'm_w_branch', 'm_b_gate', 'm_w_out', 'm_ln_g', 'm_ln_b', 'v_meta', 'v_ln_in_g', 'v_ln_in_b', 'v_w_in', 'v_s5_a_re', 'v_s5_a_im', 'v_s5_log_step', 'v_s5_b_re', 'v_s5_b_im', 'v_s5_c_re', 'v_s5_c_im', 'v_s5_d', 'v_s5_w_glu', 'v_s5_b_glu', 'v_ssd_conv_w', 'v_ssd_conv_b', 'v_ssd_dt_bias', 'v_ssd_a_log', 'v_ssd_d', 'v_ssd_norm_g', 'v_gdn_conv_w', 'v_gdn_dt_bias', 'v_gdn_a_log', 'v_gdn_norm_g', 'v_w_branch', 'v_b_gate', 'v_w_out', 'v_ln_g', 'v_ln_b']
TWIN_OUTPUTS = ['loss', 'grad_x', 'grad_meta', 'grad_ln_in_g', 'grad_ln_in_b', 'grad_w_in', 'grad_s5_a_re', 'grad_s5_a_im', 'grad_s5_log_step', 'grad_s5_b_re', 'grad_s5_b_im', 'grad_s5_c_re', 'grad_s5_c_im', 'grad_s5_d', 'grad_s5_w_glu', 'grad_s5_b_glu', 'grad_ssd_conv_w', 'grad_ssd_conv_b', 'grad_ssd_dt_bias', 'grad_ssd_a_log', 'grad_ssd_d', 'grad_ssd_norm_g', 'grad_gdn_conv_w', 'grad_gdn_dt_bias', 'grad_gdn_a_log', 'grad_gdn_norm_g', 'grad_w_branch', 'grad_b_gate', 'grad_w_out', 'grad_ln_g', 'grad_ln_b', 'delta_meta', 'delta_ln_in_g', 'delta_ln_in_b', 'delta_w_in', 'delta_s5_a_re', 'delta_s5_a_im', 'delta_s5_log_step', 'delta_s5_b_re', 'delta_s5_b_im', 'delta_s5_c_re', 'delta_s5_c_im', 'delta_s5_d', 'delta_s5_w_glu', 'delta_s5_b_glu', 'delta_ssd_conv_w', 'delta_ssd_conv_b', 'delta_ssd_dt_bias', 'delta_ssd_a_log', 'delta_ssd_d', 'delta_ssd_norm_g', 'delta_gdn_conv_w', 'delta_gdn_dt_bias', 'delta_gdn_a_log', 'delta_gdn_norm_g', 'delta_w_branch', 'delta_b_gate', 'delta_w_out', 'delta_ln_g', 'delta_ln_b', 'new_m_meta', 'new_m_ln_in_g', 'new_m_ln_in_b', 'new_m_w_in', 'new_m_s5_a_re', 'new_m_s5_a_im', 'new_m_s5_log_step', 'new_m_s5_b_re', 'new_m_s5_b_im', 'new_m_s5_c_re', 'new_m_s5_c_im', 'new_m_s5_d', 'new_m_s5_w_glu', 'new_m_s5_b_glu', 'new_m_ssd_conv_w', 'new_m_ssd_conv_b', 'new_m_ssd_dt_bias', 'new_m_ssd_a_log', 'new_m_ssd_d', 'new_m_ssd_norm_g', 'new_m_gdn_conv_w', 'new_m_gdn_dt_bias', 'new_m_gdn_a_log', 'new_m_gdn_norm_g', 'new_m_w_branch', 'new_m_b_gate', 'new_m_w_out', 'new_m_ln_g', 'new_m_ln_b', 'new_v_meta', 'new_v_ln_in_g', 'new_v_ln_in_b', 'new_v_w_in', 'new_v_s5_a_re', 'new_v_s5_a_im', 'new_v_s5_log_step', 'new_v_s5_b_re', 'new_v_s5_b_im', 'new_v_s5_c_re', 'new_v_s5_c_im', 'new_v_s5_d', 'new_v_s5_w_glu', 'new_v_s5_b_glu', 'new_v_ssd_conv_w', 'new_v_ssd_conv_b', 'new_v_ssd_dt_bias', 'new_v_ssd_a_log', 'new_v_ssd_d', 'new_v_ssd_norm_g', 'new_v_gdn_conv_w', 'new_v_gdn_dt_bias', 'new_v_gdn_a_log', 'new_v_gdn_norm_g', 'new_v_w_branch', 'new_v_b_gate', 'new_v_w_out', 'new_v_ln_g', 'new_v_ln_b']
TWIN_LEAF_KINDS = {'loss': 'loss', 'grad_x': 'grad_x', 'grad_meta': 'grad_w', 'grad_ln_in_g': 'grad_w', 'grad_ln_in_b': 'grad_w', 'grad_w_in': 'grad_w', 'grad_s5_a_re': 'grad_w', 'grad_s5_a_im': 'grad_w', 'grad_s5_log_step': 'grad_w', 'grad_s5_b_re': 'grad_w', 'grad_s5_b_im': 'grad_w', 'grad_s5_c_re': 'grad_w', 'grad_s5_c_im': 'grad_w', 'grad_s5_d': 'grad_w', 'grad_s5_w_glu': 'grad_w', 'grad_s5_b_glu': 'grad_w', 'grad_ssd_conv_w': 'grad_w', 'grad_ssd_conv_b': 'grad_w', 'grad_ssd_dt_bias': 'grad_w', 'grad_ssd_a_log': 'grad_w', 'grad_ssd_d': 'grad_w', 'grad_ssd_norm_g': 'grad_w', 'grad_gdn_conv_w': 'grad_w', 'grad_gdn_dt_bias': 'grad_w', 'grad_gdn_a_log': 'grad_w', 'grad_gdn_norm_g': 'grad_w', 'grad_w_branch': 'grad_w', 'grad_b_gate': 'grad_w', 'grad_w_out': 'grad_w', 'grad_ln_g': 'grad_w', 'grad_ln_b': 'grad_w', 'delta_meta': 'delta_w', 'delta_ln_in_g': 'delta_w', 'delta_ln_in_b': 'delta_w', 'delta_w_in': 'delta_w', 'delta_s5_a_re': 'delta_w', 'delta_s5_a_im': 'delta_w', 'delta_s5_log_step': 'delta_w', 'delta_s5_b_re': 'delta_w', 'delta_s5_b_im': 'delta_w', 'delta_s5_c_re': 'delta_w', 'delta_s5_c_im': 'delta_w', 'delta_s5_d': 'delta_w', 'delta_s5_w_glu': 'delta_w', 'delta_s5_b_glu': 'delta_w', 'delta_ssd_conv_w': 'delta_w', 'delta_ssd_conv_b': 'delta_w', 'delta_ssd_dt_bias': 'delta_w', 'delta_ssd_a_log': 'delta_w', 'delta_ssd_d': 'delta_w', 'delta_ssd_norm_g': 'delta_w', 'delta_gdn_conv_w': 'delta_w', 'delta_gdn_dt_bias': 'delta_w', 'delta_gdn_a_log': 'delta_w', 'delta_gdn_norm_g': 'delta_w', 'delta_w_branch': 'delta_w', 'delta_b_gate': 'delta_w', 'delta_w_out': 'delta_w', 'delta_ln_g': 'delta_w', 'delta_ln_b': 'delta_w', 'new_m_meta': 'new_m', 'new_m_ln_in_g': 'new_m', 'new_m_ln_in_b': 'new_m', 'new_m_w_in': 'new_m', 'new_m_s5_a_re': 'new_m', 'new_m_s5_a_im': 'new_m', 'new_m_s5_log_step': 'new_m', 'new_m_s5_b_re': 'new_m', 'new_m_s5_b_im': 'new_m', 'new_m_s5_c_re': 'new_m', 'new_m_s5_c_im': 'new_m', 'new_m_s5_d': 'new_m', 'new_m_s5_w_glu': 'new_m', 'new_m_s5_b_glu': 'new_m', 'new_m_ssd_conv_w': 'new_m', 'new_m_ssd_conv_b': 'new_m', 'new_m_ssd_dt_bias': 'new_m', 'new_m_ssd_a_log': 'new_m', 'new_m_ssd_d': 'new_m', 'new_m_ssd_norm_g': 'new_m', 'new_m_gdn_conv_w': 'new_m', 'new_m_gdn_dt_bias': 'new_m', 'new_m_gdn_a_log': 'new_m', 'new_m_gdn_norm_g': 'new_m', 'new_m_w_branch': 'new_m', 'new_m_b_gate': 'new_m', 'new_m_w_out': 'new_m', 'new_m_ln_g': 'new_m', 'new_m_ln_b': 'new_m', 'new_v_meta': 'new_v', 'new_v_ln_in_g': 'new_v', 'new_v_ln_in_b': 'new_v', 'new_v_w_in': 'new_v', 'new_v_s5_a_re': 'new_v', 'new_v_s5_a_im': 'new_v', 'new_v_s5_log_step': 'new_v', 'new_v_s5_b_re': 'new_v', 'new_v_s5_b_im': 'new_v', 'new_v_s5_c_re': 'new_v', 'new_v_s5_c_im': 'new_v', 'new_v_s5_d': 'new_v', 'new_v_s5_w_glu': 'new_v', 'new_v_s5_b_glu': 'new_v', 'new_v_ssd_conv_w': 'new_v', 'new_v_ssd_conv_b': 'new_v', 'new_v_ssd_dt_bias': 'new_v', 'new_v_ssd_a_log': 'new_v', 'new_v_ssd_d': 'new_v', 'new_v_ssd_norm_g': 'new_v', 'new_v_gdn_conv_w': 'new_v', 'new_v_gdn_dt_bias': 'new_v', 'new_v_gdn_a_log': 'new_v', 'new_v_gdn_norm_g': 'new_v', 'new_v_w_branch': 'new_v', 'new_v_b_gate': 'new_v', 'new_v_w_out': 'new_v', 'new_v_ln_g': 'new_v', 'new_v_ln_b': 'new_v'}


def _forward(args):
    return _fwd_reference(*[args[k] for k in FWD_PARAMS])


def _output_shape():
    def fwd():
        inp = _fwd_setup_inputs(0)
        return _fwd_reference(*[inp[k] for k in FWD_PARAMS])
    out = _jax.eval_shape(fwd)
    return out.shape, out.dtype

N_MICROBATCH = 1
ADAM_LR = 0.001
ADAM_B1 = 0.9
ADAM_B2 = 0.999
ADAM_EPS = 1e-08
ADAM_WD = 0.01
ADAM_STEP = 10
PER_EXAMPLE_BATCH_AXIS = {'x': 0, 'loss_target': 0}
SHARED_INPUTS = []
_WEIGHT_DTYPES = {'meta': _jnp.float32, 'ln_in_g': _jnp.float32, 'ln_in_b': _jnp.float32, 'w_in': _jnp.float32, 's5_a_re': _jnp.float32, 's5_a_im': _jnp.float32, 's5_log_step': _jnp.float32, 's5_b_re': _jnp.float32, 's5_b_im': _jnp.float32, 's5_c_re': _jnp.float32, 's5_c_im': _jnp.float32, 's5_d': _jnp.float32, 's5_w_glu': _jnp.float32, 's5_b_glu': _jnp.float32, 'ssd_conv_w': _jnp.float32, 'ssd_conv_b': _jnp.float32, 'ssd_dt_bias': _jnp.float32, 'ssd_a_log': _jnp.float32, 'ssd_d': _jnp.float32, 'ssd_norm_g': _jnp.float32, 'gdn_conv_w': _jnp.float32, 'gdn_dt_bias': _jnp.float32, 'gdn_a_log': _jnp.float32, 'gdn_norm_g': _jnp.float32, 'w_branch': _jnp.float32, 'b_gate': _jnp.float32, 'w_out': _jnp.float32, 'ln_g': _jnp.float32, 'ln_b': _jnp.float32}
MOMENT_SCALE = {'meta': 7.974020e-04, 'ln_in_g': 1.174710e+00, 'ln_in_b': 5.489572e-01, 'w_in': 6.035069e-03, 's5_a_re': 2.414810e-04, 's5_a_im': 2.392079e-04, 's5_log_step': 2.157019e-01, 's5_b_re': 1.123023e-04, 's5_b_im': 1.130162e-04, 's5_c_re': 2.276753e-04, 's5_c_im': 2.267376e-04, 's5_d': 2.707476e-03, 's5_w_glu': 7.121913e-04, 's5_b_glu': 1.093929e-03, 'ssd_conv_w': 9.883020e-03, 'ssd_conv_b': 1.524051e-02, 'ssd_dt_bias': 2.549769e-02, 'ssd_a_log': 3.886532e-02, 'ssd_d': 5.033077e-02, 'ssd_norm_g': 1.283561e-02, 'gdn_conv_w': 4.973199e-03, 'gdn_dt_bias': 2.353671e-02, 'gdn_a_log': 2.368380e-02, 'gdn_norm_g': 1.736907e-02, 'w_branch': 1.740055e-02, 'b_gate': 2.942379e-03, 'w_out': 3.011528e-02, 'ln_g': 1.607827e+01, 'ln_b': 9.083174e-01}


def _to_microbatches(a, axis):
    t = _jnp.moveaxis(a, axis, 0)
    t = t.reshape((N_MICROBATCH, t.shape[0] // N_MICROBATCH) + t.shape[1:])
    return _jnp.moveaxis(t, 1, axis + 1)


def setup_inputs(seed: int = 0) -> dict:
    inp = _fwd_setup_inputs(seed)
    key = _jax.random.fold_in(_jax.random.key(seed), 7919)
    shape, _ = _output_shape()
    out = dict(inp)
    out["loss_target"] = _jax.random.normal(_jax.random.fold_in(key, 0), shape, _jnp.float32)
    for i, name in enumerate(TWIN_WEIGHTS):
        w = inp[name].astype(_jnp.float32)
        if MOMENT_SCALE is None:
            s = _jnp.sqrt(_jnp.mean(_jnp.square(w)) + 1e-30)
        else:
            s = MOMENT_SCALE[name]
        km, kv = _jax.random.split(_jax.random.fold_in(key, i + 1))
        out[name] = w
        out["m_" + name] = s * _jax.random.normal(km, w.shape, _jnp.float32)
        out["v_" + name] = (s * s) * _jax.random.uniform(kv, w.shape, _jnp.float32, 0.5, 1.5)
    if N_MICROBATCH > 1:
        for name, axis in PER_EXAMPLE_BATCH_AXIS.items():
            out[name] = _to_microbatches(out[name], axis)
    return {'x': out['x'], 'meta': out['meta'], 'ln_in_g': out['ln_in_g'], 'ln_in_b': out['ln_in_b'], 'w_in': out['w_in'], 's5_a_re': out['s5_a_re'], 's5_a_im': out['s5_a_im'], 's5_log_step': out['s5_log_step'], 's5_b_re': out['s5_b_re'], 's5_b_im': out['s5_b_im'], 's5_c_re': out['s5_c_re'], 's5_c_im': out['s5_c_im'], 's5_d': out['s5_d'], 's5_w_glu': out['s5_w_glu'], 's5_b_glu': out['s5_b_glu'], 'ssd_conv_w': out['ssd_conv_w'], 'ssd_conv_b': out['ssd_conv_b'], 'ssd_dt_bias': out['ssd_dt_bias'], 'ssd_a_log': out['ssd_a_log'], 'ssd_d': out['ssd_d'], 'ssd_norm_g': out['ssd_norm_g'], 'gdn_conv_w': out['gdn_conv_w'], 'gdn_dt_bias': out['gdn_dt_bias'], 'gdn_a_log': out['gdn_a_log'], 'gdn_norm_g': out['gdn_norm_g'], 'w_branch': out['w_branch'], 'b_gate': out['b_gate'], 'w_out': out['w_out'], 'ln_g': out['ln_g'], 'ln_b': out['ln_b'], 'loss_target': out['loss_target'], 'm_meta': out['m_meta'], 'm_ln_in_g': out['m_ln_in_g'], 'm_ln_in_b': out['m_ln_in_b'], 'm_w_in': out['m_w_in'], 'm_s5_a_re': out['m_s5_a_re'], 'm_s5_a_im': out['m_s5_a_im'], 'm_s5_log_step': out['m_s5_log_step'], 'm_s5_b_re': out['m_s5_b_re'], 'm_s5_b_im': out['m_s5_b_im'], 'm_s5_c_re': out['m_s5_c_re'], 'm_s5_c_im': out['m_s5_c_im'], 'm_s5_d': out['m_s5_d'], 'm_s5_w_glu': out['m_s5_w_glu'], 'm_s5_b_glu': out['m_s5_b_glu'], 'm_ssd_conv_w': out['m_ssd_conv_w'], 'm_ssd_conv_b': out['m_ssd_conv_b'], 'm_ssd_dt_bias': out['m_ssd_dt_bias'], 'm_ssd_a_log': out['m_ssd_a_log'], 'm_ssd_d': out['m_ssd_d'], 'm_ssd_norm_g': out['m_ssd_norm_g'], 'm_gdn_conv_w': out['m_gdn_conv_w'], 'm_gdn_dt_bias': out['m_gdn_dt_bias'], 'm_gdn_a_log': out['m_gdn_a_log'], 'm_gdn_norm_g': out['m_gdn_norm_g'], 'm_w_branch': out['m_w_branch'], 'm_b_gate': out['m_b_gate'], 'm_w_out': out['m_w_out'], 'm_ln_g': out['m_ln_g'], 'm_ln_b': out['m_ln_b'], 'v_meta': out['v_meta'], 'v_ln_in_g': out['v_ln_in_g'], 'v_ln_in_b': out['v_ln_in_b'], 'v_w_in': out['v_w_in'], 'v_s5_a_re': out['v_s5_a_re'], 'v_s5_a_im': out['v_s5_a_im'], 'v_s5_log_step': out['v_s5_log_step'], 'v_s5_b_re': out['v_s5_b_re'], 'v_s5_b_im': out['v_s5_b_im'], 'v_s5_c_re': out['v_s5_c_re'], 'v_s5_c_im': out['v_s5_c_im'], 'v_s5_d': out['v_s5_d'], 'v_s5_w_glu': out['v_s5_w_glu'], 'v_s5_b_glu': out['v_s5_b_glu'], 'v_ssd_conv_w': out['v_ssd_conv_w'], 'v_ssd_conv_b': out['v_ssd_conv_b'], 'v_ssd_dt_bias': out['v_ssd_dt_bias'], 'v_ssd_a_log': out['v_ssd_a_log'], 'v_ssd_d': out['v_ssd_d'], 'v_ssd_norm_g': out['v_ssd_norm_g'], 'v_gdn_conv_w': out['v_gdn_conv_w'], 'v_gdn_dt_bias': out['v_gdn_dt_bias'], 'v_gdn_a_log': out['v_gdn_a_log'], 'v_gdn_norm_g': out['v_gdn_norm_g'], 'v_w_branch': out['v_w_branch'], 'v_b_gate': out['v_b_gate'], 'v_w_out': out['v_w_out'], 'v_ln_g': out['v_ln_g'], 'v_ln_b': out['v_ln_b']}


def _loss(weights, diff, rest, loss_target):
    with _jax.named_scope("forward"):
        args = {**rest, TWIN_DIFF_INPUT: diff, **{k: w.astype(_WEIGHT_DTYPES[k]) for k, w in weights.items()}}
        y = _forward(args)
    with _jax.named_scope("loss_head"):
        err = _jnp.square(y.astype(_jnp.float32) - loss_target)
        return 0.5 * _jnp.sum(_jnp.mean(err, axis=-1)) if err.ndim else 0.5 * err


def _adamw(w, g, m, v):
    m = ADAM_B1 * m + (1.0 - ADAM_B1) * g
    v = ADAM_B2 * v + (1.0 - ADAM_B2) * _jnp.square(g)
    m_hat = m / (1.0 - ADAM_B1 ** ADAM_STEP)
    v_hat = v / (1.0 - ADAM_B2 ** ADAM_STEP)
    delta = -ADAM_LR * (m_hat / (_jnp.sqrt(v_hat) + ADAM_EPS) + ADAM_WD * w)
    return delta, m, v


def reference(x, meta, ln_in_g, ln_in_b, w_in, s5_a_re, s5_a_im, s5_log_step, s5_b_re, s5_b_im, s5_c_re, s5_c_im, s5_d, s5_w_glu, s5_b_glu, ssd_conv_w, ssd_conv_b, ssd_dt_bias, ssd_a_log, ssd_d, ssd_norm_g, gdn_conv_w, gdn_dt_bias, gdn_a_log, gdn_norm_g, w_branch, b_gate, w_out, ln_g, ln_b, loss_target, m_meta, m_ln_in_g, m_ln_in_b, m_w_in, m_s5_a_re, m_s5_a_im, m_s5_log_step, m_s5_b_re, m_s5_b_im, m_s5_c_re, m_s5_c_im, m_s5_d, m_s5_w_glu, m_s5_b_glu, m_ssd_conv_w, m_ssd_conv_b, m_ssd_dt_bias, m_ssd_a_log, m_ssd_d, m_ssd_norm_g, m_gdn_conv_w, m_gdn_dt_bias, m_gdn_a_log, m_gdn_norm_g, m_w_branch, m_b_gate, m_w_out, m_ln_g, m_ln_b, v_meta, v_ln_in_g, v_ln_in_b, v_w_in, v_s5_a_re, v_s5_a_im, v_s5_log_step, v_s5_b_re, v_s5_b_im, v_s5_c_re, v_s5_c_im, v_s5_d, v_s5_w_glu, v_s5_b_glu, v_ssd_conv_w, v_ssd_conv_b, v_ssd_dt_bias, v_ssd_a_log, v_ssd_d, v_ssd_norm_g, v_gdn_conv_w, v_gdn_dt_bias, v_gdn_a_log, v_gdn_norm_g, v_w_branch, v_b_gate, v_w_out, v_ln_g, v_ln_b):
    given = dict(x=x, meta=meta, ln_in_g=ln_in_g, ln_in_b=ln_in_b, w_in=w_in, s5_a_re=s5_a_re, s5_a_im=s5_a_im, s5_log_step=s5_log_step, s5_b_re=s5_b_re, s5_b_im=s5_b_im, s5_c_re=s5_c_re, s5_c_im=s5_c_im, s5_d=s5_d, s5_w_glu=s5_w_glu, s5_b_glu=s5_b_glu, ssd_conv_w=ssd_conv_w, ssd_conv_b=ssd_conv_b, ssd_dt_bias=ssd_dt_bias, ssd_a_log=ssd_a_log, ssd_d=ssd_d, ssd_norm_g=ssd_norm_g, gdn_conv_w=gdn_conv_w, gdn_dt_bias=gdn_dt_bias, gdn_a_log=gdn_a_log, gdn_norm_g=gdn_norm_g, w_branch=w_branch, b_gate=b_gate, w_out=w_out, ln_g=ln_g, ln_b=ln_b, loss_target=loss_target, m_meta=m_meta, m_ln_in_g=m_ln_in_g, m_ln_in_b=m_ln_in_b, m_w_in=m_w_in, m_s5_a_re=m_s5_a_re, m_s5_a_im=m_s5_a_im, m_s5_log_step=m_s5_log_step, m_s5_b_re=m_s5_b_re, m_s5_b_im=m_s5_b_im, m_s5_c_re=m_s5_c_re, m_s5_c_im=m_s5_c_im, m_s5_d=m_s5_d, m_s5_w_glu=m_s5_w_glu, m_s5_b_glu=m_s5_b_glu, m_ssd_conv_w=m_ssd_conv_w, m_ssd_conv_b=m_ssd_conv_b, m_ssd_dt_bias=m_ssd_dt_bias, m_ssd_a_log=m_ssd_a_log, m_ssd_d=m_ssd_d, m_ssd_norm_g=m_ssd_norm_g, m_gdn_conv_w=m_gdn_conv_w, m_gdn_dt_bias=m_gdn_dt_bias, m_gdn_a_log=m_gdn_a_log, m_gdn_norm_g=m_gdn_norm_g, m_w_branch=m_w_branch, m_b_gate=m_b_gate, m_w_out=m_w_out, m_ln_g=m_ln_g, m_ln_b=m_ln_b, v_meta=v_meta, v_ln_in_g=v_ln_in_g, v_ln_in_b=v_ln_in_b, v_w_in=v_w_in, v_s5_a_re=v_s5_a_re, v_s5_a_im=v_s5_a_im, v_s5_log_step=v_s5_log_step, v_s5_b_re=v_s5_b_re, v_s5_b_im=v_s5_b_im, v_s5_c_re=v_s5_c_re, v_s5_c_im=v_s5_c_im, v_s5_d=v_s5_d, v_s5_w_glu=v_s5_w_glu, v_s5_b_glu=v_s5_b_glu, v_ssd_conv_w=v_ssd_conv_w, v_ssd_conv_b=v_ssd_conv_b, v_ssd_dt_bias=v_ssd_dt_bias, v_ssd_a_log=v_ssd_a_log, v_ssd_d=v_ssd_d, v_ssd_norm_g=v_ssd_norm_g, v_gdn_conv_w=v_gdn_conv_w, v_gdn_dt_bias=v_gdn_dt_bias, v_gdn_a_log=v_gdn_a_log, v_gdn_norm_g=v_gdn_norm_g, v_w_branch=v_w_branch, v_b_gate=v_b_gate, v_w_out=v_w_out, v_ln_g=v_ln_g, v_ln_b=v_ln_b)
    weights = {n: given[n] for n in TWIN_WEIGHTS}
    shared = {n: given[n] for n in SHARED_INPUTS}
    per_example = {n: given[n] for n in ['x']}
    grad_fn = _jax.value_and_grad(_loss, argnums=(0, 1))

    def one_microbatch(ex, loss_target):
        ex = dict(ex)
        diff = ex.pop(TWIN_DIFF_INPUT)
        return grad_fn(weights, diff, {**shared, **ex}, loss_target)

    if N_MICROBATCH == 1:
        loss, (grad_w, grad_x) = one_microbatch(per_example, given["loss_target"])
    else:
        def body(carry, xs):
            loss_sum, grad_sum = carry
            l_k, (gw_k, gx_k) = one_microbatch(xs[0], xs[1])
            with _jax.named_scope("update"):
                return (loss_sum + l_k, _jax.tree.map(_jnp.add, grad_sum, gw_k)), gx_k

        init = (_jnp.zeros((), _jnp.float32), _jax.tree.map(_jnp.zeros_like, weights))
        (loss, grad_w), grad_x = _jax.lax.scan(body, init, (per_example, given["loss_target"]))
    with _jax.named_scope("update"):
        delta_w, new_m, new_v = {}, {}, {}
        for n in TWIN_WEIGHTS:
            delta_w[n], new_m[n], new_v[n] = _adamw(weights[n], grad_w[n], given["m_" + n], given["v_" + n])
    return (loss, grad_x, *[grad_w[n] for n in TWIN_WEIGHTS], *[delta_w[n] for n in TWIN_WEIGHTS],
            *[new_m[n] for n in TWIN_WEIGHTS], *[new_v[n] for n in TWIN_WEIGHTS])
```

```python
import functools
import math

import jax
import jax.numpy as jnp
from jax import lax
from jax.experimental import pallas as pl
from jax.experimental.pallas import tpu as pltpu

F32 = jnp.float32
BF16 = jnp.bfloat16
MESH = pl.DeviceIdType.MESH
HIGHEST = lax.Precision.HIGHEST

D_MODEL = 1024
DEPTH = 4
N_META = 16
CHUNK = 64
PAD = CHUNK - N_META
CONV_K = 4
HALO = 8
WIDTH = 768
S5_GROUPS, S5_GROUP, S5_STATE = 48, 16, 64
S5_BLOCKS = 6
SSD_HEADS, SSD_HEAD_DIM, SSD_GROUPS, SSD_STATE = 12, 64, 2, 128
SSD_PAIRS = 6
GDN_HEADS, GDN_DIM = 6, 128
LANES = 128
SMALL_ROWS = 24
ALPHA = (2 * DEPTH) ** 0.25
LN_EPS = 1e-5
P_S5U, P_S5Z, P_XBC, P_DT, P_SSDZ, P_QKV, P_GA, P_GB, P_GDNZ, P_GATE, P_END = (
    0, 768, 1536, 2816, 2828, 3596, 5900, 5906, 5912, 6680, 9752)
ADAM_LR, ADAM_B1, ADAM_B2, ADAM_EPS, ADAM_WD, ADAM_STEP = 0.001, 0.9, 0.999, 1e-08, 0.01, 10
NEG = -1e30


def _pick(n, cands):
    for c in cands:
        if n % c == 0:
            return c
    raise ValueError(f"no tile for {n} in {cands}")


def _cparams(sem, vmem_mb):
    return pltpu.CompilerParams(dimension_semantics=sem, vmem_limit_bytes=vmem_mb << 20)


_DIMS = {"nn": (((1,), (0,)), ((), ())), "nt": (((1,), (1,)), ((), ())), "tn": (((0,), (0,)), ((), ()))}


def _dot(a, b, mode, hi):
    if hi:
        return lax.dot_general(a, b, _DIMS[mode], precision=HIGHEST, preferred_element_type=F32)
    return lax.dot_general(a.astype(BF16), b.astype(BF16), _DIMS[mode], preferred_element_type=F32)


@functools.partial(jax.custom_vjp, nondiff_argnums=(2, 3))
def _mm(a, b, mode="nn", hi=False):
    return _dot(a, b, mode, hi)


def _mm_fwd(a, b, mode, hi):
    return _dot(a, b, mode, hi), (a, b)


def _mm_bwd(mode, hi, res, g):
    a, b = res
    if mode == "nn":
        return _dot(g, b, "nt", hi), _dot(a, g, "tn", hi)
    if mode == "nt":
        return _dot(g, b, "nn", hi), _dot(g, a, "tn", hi)
    return _dot(b, g, "nt", hi), _dot(a, g, "nn", hi)


_mm.defvjp(_mm_fwd, _mm_bwd)


@functools.partial(jax.custom_vjp, nondiff_argnums=(1,))
def _roll_rows(x, k):
    return pltpu.roll(x, k % x.shape[0], 0)


def _roll_fwd(x, k):
    return _roll_rows(x, k), None


def _roll_bwd(k, _, g):
    return (_roll_rows(g, -k),)


_roll_rows.defvjp(_roll_fwd, _roll_bwd)


def _iota(shape, dim):
    return lax.broadcasted_iota(jnp.int32, shape, dim)


def _valid_rows(row0, n):
    return (row0 + _iota((n, 1), 0)) >= PAD


def _lane(x, idx):
    return jnp.sum(jnp.where(_iota(x.shape, 1) == idx, x, 0.0), axis=1, keepdims=True)


def _row(x, idx):
    return jnp.sum(jnp.where(_iota(x.shape, 0) == idx, x, 0.0), axis=0, keepdims=True)


def _layer_norm(z, g, b):
    mu = jnp.mean(z, axis=-1, keepdims=True)
    zc = z - mu
    var = jnp.mean(zc * zc, axis=-1, keepdims=True)
    return zc * lax.rsqrt(var + LN_EPS) * g + b


def _rms_norm(z, g):
    return z * lax.rsqrt(jnp.mean(z * z, axis=-1, keepdims=True) + LN_EPS) * g


def _causal_conv(halo, x, w, row0):
    t = x.shape[0]
    xc = jnp.concatenate([halo, x], axis=0)
    acc = None
    for j in range(CONV_K):
        term = _roll_rows(xc, CONV_K - 1 - j)[HALO:HALO + t] * _row(w, j)
        acc = term if acc is None else acc + term
    return acc, x[t - HALO:t]


def _tri(n, strict=False):
    r, c = _iota((n, n), 0), _iota((n, n), 1)
    return (r > c) if strict else (r >= c)


def _scan_op(step, name, *, tile, nb, row_kinds, param_kinds, carry_shapes, out_widths, vmem_mb=48):
    n_rows, n_par, n_car, n_out = len(row_kinds), len(param_kinds), len(carry_shapes), len(out_widths)

    def dims(rows):
        for k, a in zip(row_kinds, rows):
            if k in "bs":
                return a.shape[0], a.shape[0] // tile
        raise ValueError("need a row input")

    def row_spec(kind, a, rev, nt):
        ti = (lambda i: nt - 1 - i) if rev else (lambda i: i)
        if kind == "b":
            return pl.BlockSpec((tile, a.shape[1] // nb), lambda b, i: (ti(i), b))
        if kind == "s":
            return pl.BlockSpec((tile, a.shape[1]), lambda b, i: (ti(i), 0))
        return pl.BlockSpec((None, a.shape[1], a.shape[2]), lambda b, i: (ti(i), 0, 0))

    def par_spec(kind, a):
        if kind == "b":
            return pl.BlockSpec((None, a.shape[1], a.shape[2]), lambda b, i: (b, 0, 0))
        return pl.BlockSpec(a.shape, lambda b, i: (0, 0))

    def fwd_call(rows, params):
        length, nt = dims(rows)

        def body(*refs):
            r_in = refs[:n_rows]
            p_in = refs[n_rows:n_rows + n_par]
            o_out = refs[n_rows + n_par:n_rows + n_par + n_out]
            s_out = refs[n_rows + n_par + n_out:n_rows + n_par + n_out + n_car]
            c_scr = refs[n_rows + n_par + n_out + n_car:]
            b, i = pl.program_id(0), pl.program_id(1)

            if n_car:
                @pl.when(i == 0)
                def _():
                    for c in c_scr:
                        c[...] = jnp.zeros_like(c)

            cin = tuple(c[...] for c in c_scr)
            for s, c in zip(s_out, cin):
                s[...] = c
            new_c, outs = step(cin, tuple(r[...] for r in r_in), tuple(p[...] for p in p_in), b, i * tile)
            for c, v in zip(c_scr, new_c):
                c[...] = v
            for o, v in zip(o_out, outs):
                o[...] = v

        out_shape = [jax.ShapeDtypeStruct((length, nb * w), F32) for w in out_widths]
        out_shape += [jax.ShapeDtypeStruct((nb, nt) + tuple(s), F32) for s in carry_shapes]
        out_specs = [pl.BlockSpec((tile, w), lambda b, i: (i, b)) for w in out_widths]
        out_specs += [pl.BlockSpec((None, None) + tuple(s), lambda b, i: (b, i, 0, 0)) for s in carry_shapes]
        res = pl.pallas_call(
            body, name=name + "_fwd", grid=(nb, nt),
            in_specs=[row_spec(k, a, False, nt) for k, a in zip(row_kinds, rows)]
            + [par_spec(k, a) for k, a in zip(param_kinds, params)],
            out_specs=out_specs, out_shape=out_shape,
            scratch_shapes=[pltpu.VMEM(tuple(s), F32) for s in carry_shapes],
            compiler_params=_cparams(("arbitrary", "arbitrary"), vmem_mb),
        )(*rows, *params)
        return tuple(res[:n_out]), tuple(res[n_out:])

    def bwd_call(rows, params, saved, douts):
        length, nt = dims(rows)

        def body(*refs):
            k0 = 0
            r_in = refs[k0:k0 + n_rows]; k0 += n_rows
            p_in = refs[k0:k0 + n_par]; k0 += n_par
            s_in = refs[k0:k0 + n_car]; k0 += n_car
            g_in = refs[k0:k0 + n_out]; k0 += n_out
            dr_out = refs[k0:k0 + n_rows]; k0 += n_rows
            dp_out = refs[k0:k0 + n_par]; k0 += n_par
            dc_scr = refs[k0:]
            b, i = pl.program_id(0), pl.program_id(1)
            row0 = (nt - 1 - i) * tile

            @pl.when(i == 0)
            def _():
                for c in dc_scr:
                    c[...] = jnp.zeros_like(c)
                for p in dp_out:
                    p[...] = jnp.zeros_like(p)

            def f(c, r, p):
                return step(c, r, p, b, row0)

            _, vjp = jax.vjp(f, tuple(s[...] for s in s_in), tuple(r[...] for r in r_in),
                             tuple(p[...] for p in p_in))
            dc, dr, dp = vjp((tuple(c[...] for c in dc_scr), tuple(g[...] for g in g_in)))
            for c, v in zip(dc_scr, dc):
                c[...] = v
            for o, v in zip(dr_out, dr):
                o[...] = v
            for o, v in zip(dp_out, dp):
                o[...] += v

        rev = lambda i: nt - 1 - i
        in_specs = [row_spec(k, a, True, nt) for k, a in zip(row_kinds, rows)]
        in_specs += [par_spec(k, a) for k, a in zip(param_kinds, params)]
        in_specs += [pl.BlockSpec((None, None) + tuple(s), lambda b, i: (b, rev(i), 0, 0)) for s in carry_shapes]
        in_specs += [pl.BlockSpec((tile, w), lambda b, i: (rev(i), b)) for w in out_widths]
        out_shape, out_specs = [], []
        for k, a in zip(row_kinds, rows):
            if k == "b":
                out_shape.append(jax.ShapeDtypeStruct(a.shape, F32))
                out_specs.append(pl.BlockSpec((tile, a.shape[1] // nb), lambda b, i: (rev(i), b)))
            elif k == "s":
                out_shape.append(jax.ShapeDtypeStruct((nb,) + a.shape, F32))
                out_specs.append(pl.BlockSpec((None, tile, a.shape[1]), lambda b, i: (b, rev(i), 0)))
            else:
                out_shape.append(jax.ShapeDtypeStruct((nb,) + a.shape, F32))
                out_specs.append(pl.BlockSpec((None, None, a.shape[1], a.shape[2]), lambda b, i: (b, rev(i), 0, 0)))
        for k, a in zip(param_kinds, params):
            shp = a.shape[1:] if k == "b" else a.shape
            out_shape.append(jax.ShapeDtypeStruct((nb,) + tuple(shp), F32))
            out_specs.append(pl.BlockSpec((None,) + tuple(shp), lambda b, i: (b, 0, 0)))
        res = pl.pallas_call(
            body, name=name + "_bwd", grid=(nb, nt), in_specs=in_specs, out_specs=out_specs, out_shape=out_shape,
            scratch_shapes=[pltpu.VMEM(tuple(s), F32) for s in carry_shapes],
            compiler_params=_cparams(("arbitrary", "arbitrary"), vmem_mb),
        )(*rows, *params, *saved, *douts)
        drows = tuple(r if k == "b" else jnp.sum(r, axis=0) for k, r in zip(row_kinds, res[:n_rows]))
        dpars = tuple(p if k == "b" else jnp.sum(p, axis=0) for k, p in zip(param_kinds, res[n_rows:]))
        return drows, dpars

    @jax.custom_vjp
    def op(rows, params):
        return fwd_call(rows, params)[0]

    def op_fwd(rows, params):
        outs, saved = fwd_call(rows, params)
        return outs, (rows, params, saved)

    def op_bwd(res, douts):
        rows, params, saved = res
        return bwd_call(rows, params, saved, tuple(douts))

    op.defvjp(op_fwd, op_bwd)
    return op


_TM = (832, 512, 256, 128)
_TN = (768, 640, 512, 384, 256, 128)


def _mm_fwd_call(x, w, name):
    m, k = x.shape
    n = w.shape[1]
    tm, tn = _pick(m, _TM), _pick(n, _TN)

    def body(x_ref, w_ref, o_ref):
        o_ref[...] = _dot(x_ref[...], w_ref[...], "nn", False)

    return pl.pallas_call(
        body, name=name, grid=(n // tn, m // tm),
        in_specs=[pl.BlockSpec((tm, k), lambda j, i: (i, 0)), pl.BlockSpec((k, tn), lambda j, i: (0, j))],
        out_specs=pl.BlockSpec((tm, tn), lambda j, i: (i, j)),
        out_shape=jax.ShapeDtypeStruct((m, n), F32),
        compiler_params=_cparams(("parallel", "parallel"), 48),
    )(x, w)


def _mm_dx_call(g, w, name):
    m, n = g.shape
    k = w.shape[0]
    tm, tn = _pick(m, _TM), _pick(n, _TN)

    def body(g_ref, w_ref, o_ref):
        @pl.when(pl.program_id(1) == 0)
        def _():
            o_ref[...] = jnp.zeros_like(o_ref)

        o_ref[...] += _dot(g_ref[...], w_ref[...], "nt", False)

    return pl.pallas_call(
        body, name=name, grid=(m // tm, n // tn),
        in_specs=[pl.BlockSpec((tm, tn), lambda i, j: (i, j)), pl.BlockSpec((k, tn), lambda i, j: (0, j))],
        out_specs=pl.BlockSpec((tm, k), lambda i, j: (i, 0)),
        out_shape=jax.ShapeDtypeStruct((m, k), F32),
        compiler_params=_cparams(("parallel", "arbitrary"), 48),
    )(g, w)


def _mm_dw_call(x, g, name):
    m, k = x.shape
    n = g.shape[1]
    tm, tn = _pick(m, _TM), _pick(n, _TN)

    def body(x_ref, g_ref, o_ref):
        @pl.when(pl.program_id(1) == 0)
        def _():
            o_ref[...] = jnp.zeros_like(o_ref)

        o_ref[...] += _dot(x_ref[...], g_ref[...], "tn", False)

    return pl.pallas_call(
        body, name=name, grid=(n // tn, m // tm),
        in_specs=[pl.BlockSpec((tm, k), lambda j, i: (i, 0)), pl.BlockSpec((tm, tn), lambda j, i: (i, j))],
        out_specs=pl.BlockSpec((k, tn), lambda j, i: (0, j)),
        out_shape=jax.ShapeDtypeStruct((k, n), F32),
        compiler_params=_cparams(("parallel", "arbitrary"), 48),
    )(x, g)


def _dense(name):
    @jax.custom_vjp
    def op(x, w):
        return _mm_fwd_call(x, w, name + "_fwd")

    def op_fwd(x, w):
        return _mm_fwd_call(x, w, name + "_fwd"), (x, w)

    def op_bwd(res, g):
        x, w = res
        return _mm_dx_call(g, w, name + "_dx"), _mm_dw_call(x, g, name + "_dw")

    op.defvjp(op_fwd, op_bwd)
    return op


def _ln_in_step(c, rows, params, b, row0):
    (z,), (g, bb) = rows, params
    return (), (jnp.where(_valid_rows(row0, z.shape[0]), _layer_norm(z, g, bb), 0.0),)


def _ln_res_step(c, rows, params, b, row0):
    (h, o), (g, bb) = rows, params
    return (), (jnp.where(_valid_rows(row0, h.shape[0]), _layer_norm(ALPHA * h + o, g, bb), 0.0),)


def _s5_prep_step(c, rows, params, b, row0):
    a_re, a_im, log_step, b_re, b_im = rows
    lam_re = jnp.minimum(a_re, -1e-4)
    lam_im = a_im
    step = jnp.exp(log_step)
    mag = jnp.exp(lam_re * step)
    abar_re, abar_im = mag * jnp.cos(lam_im * step), mag * jnp.sin(lam_im * step)
    den = lam_re * lam_re + lam_im * lam_im
    nr, ni = abar_re - 1.0, abar_im
    coef_re = (nr * lam_re + ni * lam_im) / den
    coef_im = (ni * lam_re - nr * lam_im) / den
    return (), (abar_re, abar_im, coef_re * b_re - coef_im * b_im, coef_re * b_im + coef_im * b_re)


def _s5_scan_step(c, rows, params, b, row0):
    (c_re, c_im), (u,) = c, rows
    bd_re, bd_im, a_re, a_im, cd_re, cd_im = params
    t = u.shape[0]
    ridx = _iota((t, 1), 0)
    first = ridx == 0
    s_re = _mm(u, bd_re) + jnp.where(first, a_re * c_re - a_im * c_im, 0.0)
    s_im = _mm(u, bd_im) + jnp.where(first, a_re * c_im + a_im * c_re, 0.0)
    p_re, p_im = a_re, a_im
    d = 1
    while d < t:
        keep = ridx >= d
        sh_re = jnp.where(keep, _roll_rows(s_re, d), 0.0)
        sh_im = jnp.where(keep, _roll_rows(s_im, d), 0.0)
        s_re, s_im = s_re + p_re * sh_re - p_im * sh_im, s_im + p_re * sh_im + p_im * sh_re
        p_re, p_im = p_re * p_re - p_im * p_im, 2.0 * p_re * p_im
        d *= 2
    y = _mm(s_re, cd_re) - _mm(s_im, cd_im)
    return (_row(s_re, t - 1), _row(s_im, t - 1)), (y,)


def _s5_post_step(c, rows, params, b, row0):
    (y, u, z), (d, w_glu, b_glu) = rows, params
    v = jax.nn.gelu(y + d * u)
    v = v * jax.nn.sigmoid(_mm(v, w_glu) + b_glu)
    return (), (v * jax.nn.silu(z),)


def _ssd_step(c, rows, params, b, row0):
    hx, hb, hc, state = c
    x_raw, b_raw, c_raw, small, small_t = rows
    wx, wb, wc, bx, bb, bc, d_l, bias_l, alog_l, bias_c, alog_c = params
    t = x_raw.shape[0]
    valid = _valid_rows(row0, t)
    xs, hx2 = _causal_conv(hx, x_raw, wx, row0)
    bs, hb2 = _causal_conv(hb, b_raw, wb, row0)
    cs, hc2 = _causal_conv(hc, c_raw, wc, row0)
    xs = jnp.where(valid, jax.nn.silu(xs + bx), 0.0)
    bs = jnp.where(valid, jax.nn.silu(bs + bb), 0.0)
    cs = jnp.where(valid, jax.nn.silu(cs + bc), 0.0)
    lane = _iota((1, LANES), 1)
    low = lane < SSD_HEAD_DIM
    h0 = 2 * b
    dt_all = jnp.where(valid, jax.nn.softplus(small + bias_l), 0.0)
    a_all = -jnp.exp(alog_l)
    dt_l = jnp.where(low, _lane(dt_all, h0), _lane(dt_all, h0 + 1))
    a_l = jnp.where(low, _lane(a_all, h0), _lane(a_all, h0 + 1))
    acum_l = _mm(jnp.where(_tri(t), 1.0, 0.0), dt_l * a_l, "nn", True)
    valid_t = (row0 + _iota((1, t), 1)) >= PAD
    dta_t = jnp.where(valid_t, jax.nn.softplus(small_t + bias_c), 0.0) * (-jnp.exp(alog_c))
    acum_t = _mm(dta_t, jnp.where(_iota((t, t), 0) <= _iota((t, t), 1), 1.0, 0.0), "nn", True)
    xd = xs * dt_l
    scores = _mm(cs, bs, "nt")
    causal = _tri(t)
    y = None
    for hh in range(2):
        col = acum_l[:, 0:1] if hh == 0 else acum_l[:, SSD_HEAD_DIM:SSD_HEAD_DIM + 1]
        seg = col - _row(acum_t, h0 + hh)
        decay = jnp.exp(jnp.where(causal, seg, NEG))
        part = _mm(scores * decay, jnp.where(low if hh == 0 else ~low, xd, 0.0))
        y = part if y is None else y + part
    last = _row(acum_l, t - 1)
    new_states = _mm(xd * jnp.exp(last - acum_l), bs, "tn")
    y = y + _mm(cs, state, "nt") * jnp.exp(acum_l)
    cd = jnp.exp(last)
    cd_col = jnp.where(_iota((LANES, 1), 0) < SSD_HEAD_DIM, cd[:, 0:1], cd[:, SSD_HEAD_DIM:SSD_HEAD_DIM + 1])
    return (hx2, hb2, hc2, state * cd_col + new_states), (y + xs * d_l,)


def _ssd_post_step(c, rows, params, b, row0):
    (y, z), (g,) = rows, params
    return (), (_rms_norm(y * jax.nn.silu(z), g),)


def _unit_lower_inverse(a):
    n = a.shape[0]
    eye = jnp.where(_iota((n, n), 0) == _iota((n, n), 1), 1.0, 0.0)
    inv = eye - a
    p = _mm(a, a, "nn", True)
    k = 2
    while k < n:
        inv = inv + _mm(inv, p, "nn", True)
        k *= 2
        if k < n:
            p = _mm(p, p, "nn", True)
    return inv


def _gdn_step(c, rows, params, b, row0):
    hq, hk, hv, state = c
    q_raw, k_raw, v_raw, z, small, small_t = rows
    wq, wk, wv, bias_l, alog_l, bias_c, alog_c, norm_g = params
    t = q_raw.shape[0]
    valid = _valid_rows(row0, t)
    q, hq2 = _causal_conv(hq, q_raw, wq, row0)
    k, hk2 = _causal_conv(hk, k_raw, wk, row0)
    v, hv2 = _causal_conv(hv, v_raw, wv, row0)
    q = jnp.where(valid, jax.nn.silu(q), 0.0)
    k = jnp.where(valid, jax.nn.silu(k), 0.0)
    v = jnp.where(valid, jax.nn.silu(v), 0.0)
    q = q * lax.rsqrt(jnp.sum(q * q, axis=-1, keepdims=True) + 1e-6) * (GDN_DIM ** -0.5)
    k = k * lax.rsqrt(jnp.sum(k * k, axis=-1, keepdims=True) + 1e-6)
    ia, ib = SSD_HEADS + b, SSD_HEADS + GDN_HEADS + b
    beta = jnp.where(valid, _lane(jax.nn.sigmoid(small), ib), 0.0)
    g_col = jnp.where(valid, _lane(-jnp.exp(alog_l) * jax.nn.softplus(small + bias_l), ia), 0.0)
    valid_t = (row0 + _iota((1, t), 1)) >= PAD
    g_t = jnp.where(valid_t, -jnp.exp(alog_c) * jax.nn.softplus(small_t + bias_c), 0.0)
    g_row = _row(g_t, ia)
    causal, strict = _tri(t), _tri(t, True)
    lower = jnp.where(causal, 1.0, 0.0)
    gc_cb = _mm(lower, jnp.broadcast_to(g_col, (t, t)), "nn", True)
    gc_rb = _mm(jnp.broadcast_to(g_row, (t, t)), lower, "nt", True)
    gamma = jnp.exp(jnp.where(causal, gc_cb - gc_rb, NEG))
    gcum = gc_cb[:, 0:1]
    a_mat = jnp.where(strict, _mm(k, k, "nt") * gamma * beta, 0.0)
    inv = _unit_lower_inverse(a_mat)
    egc = jnp.exp(gcum)
    u_c = _mm(inv, v * beta, "nn", True)
    w_c = _mm(inv, k * (beta * egc), "nn", True)
    attn = _mm(q, k, "nt") * gamma
    glast = _row(gcum, t - 1)
    v_new = u_c - _mm(w_c, state)
    o = _mm(q * egc, state) + _mm(attn, v_new)
    new_state = state * jnp.exp(glast) + _mm(k * jnp.exp(glast - gcum), v_new, "tn")
    out = _rms_norm(o, norm_g) * jax.nn.silu(z)
    return (hq2, hk2, hv2, new_state), (out,)


def _gate_merge_step(c, rows, params, b, row0):
    (oa, ob, oc, gl), (bg,) = rows, params
    acc = None
    for n, o in enumerate((oa, ob, oc)):
        term = jax.nn.sigmoid(gl[:, n * D_MODEL:(n + 1) * D_MODEL] + bg[:, n * D_MODEL:(n + 1) * D_MODEL]) * o
        acc = term if acc is None else acc + term
    return (), (acc,)


def _loss_tile(n):
    return _pick(n, (512, 256, 128, 64))


def _loss_fwd_call(y, tgt):
    n, d = y.shape
    tile = _loss_tile(n)

    def body(y_ref, t_ref, o_ref):
        @pl.when(pl.program_id(0) == 0)
        def _():
            o_ref[...] = jnp.zeros_like(o_ref)

        e = y_ref[...] - t_ref[...]
        o_ref[...] += jnp.sum(jnp.sum(e * e, axis=1, keepdims=True), axis=0, keepdims=True) * (0.5 / d)

    out = pl.pallas_call(
        body, name="loss_fwd", grid=(n // tile,),
        in_specs=[pl.BlockSpec((tile, d), lambda i: (i, 0)), pl.BlockSpec((tile, d), lambda i: (i, 0))],
        out_specs=pl.BlockSpec((8, LANES), lambda i: (0, 0)),
        out_shape=jax.ShapeDtypeStruct((8, LANES), F32),
        compiler_params=_cparams(("arbitrary",), 32),
    )(y, tgt)
    return out[0, 0]


def _loss_bwd_call(y, tgt, g):
    n, d = y.shape
    tile = _loss_tile(n)

    def body(y_ref, t_ref, g_ref, o_ref):
        o_ref[...] = (y_ref[...] - t_ref[...]) * (g_ref[...][0:1, 0:1] * (1.0 / d))

    return pl.pallas_call(
        body, name="loss_bwd", grid=(n // tile,),
        in_specs=[pl.BlockSpec((tile, d), lambda i: (i, 0)), pl.BlockSpec((tile, d), lambda i: (i, 0)),
                  pl.BlockSpec((8, LANES), lambda i: (0, 0))],
        out_specs=pl.BlockSpec((tile, d), lambda i: (i, 0)),
        out_shape=jax.ShapeDtypeStruct((n, d), F32),
        compiler_params=_cparams(("parallel",), 32),
    )(y, tgt, jnp.broadcast_to(g, (8, LANES)).astype(F32))


@jax.custom_vjp
def _loss_op(y, tgt):
    return _loss_fwd_call(y, tgt)


def _loss_op_fwd(y, tgt):
    return _loss_fwd_call(y, tgt), (y, tgt)


def _loss_op_bwd(res, g):
    y, tgt = res
    return _loss_bwd_call(y, tgt, g), jnp.zeros_like(tgt)


_loss_op.defvjp(_loss_op_fwd, _loss_op_bwd)


def _rowwise(step, name, tile, n_rows, n_params, out_widths, vmem_mb=48):
    return _scan_op(step, name, tile=tile, nb=1, row_kinds="s" * n_rows, param_kinds="s" * n_params,
                    carry_shapes=(), out_widths=out_widths, vmem_mb=vmem_mb)


def _block_diag(x, nblk):
    bsz, _, r, c = x.shape
    eye = jnp.eye(nblk, dtype=x.dtype)
    return jnp.einsum("bgrc,gh->bgrhc", x, eye).reshape(bsz, nblk * r, nblk * c)


def _per_block(v, nb, rep=1):
    rows = v.shape[0]
    out = v.reshape(rows, -1, LANES).transpose(1, 0, 2)
    return jnp.repeat(out, rep, axis=0) if rep > 1 else out


def _layer(h, w):
    length = h.shape[0]
    nt = length // CHUNK
    t_row = _pick(length, (208, 128))
    t_s5 = _pick(length, (160, 128))
    row2 = lambda v: v.reshape(1, -1)
    w_in = w["w_in"]
    seg = lambda a, bnd: w_in[:, a:bnd]
    w_small = jnp.concatenate([seg(P_DT, P_SSDZ), seg(P_GA, P_GDNZ),
                               jnp.zeros((D_MODEL, LANES - SMALL_ROWS), F32)], axis=1)
    s5u = _dense("proj_s5u")(h, seg(P_S5U, P_S5Z))
    s5z = _dense("proj_s5z")(h, seg(P_S5Z, P_XBC))
    xbc = _dense("proj_xbc")(h, seg(P_XBC, P_DT))
    small = _dense("proj_small")(h, w_small)
    ssdz = _dense("proj_ssdz")(h, seg(P_SSDZ, P_QKV))
    qkv = _dense("proj_qkv")(h, seg(P_QKV, P_GA))
    gdnz = _dense("proj_gdnz")(h, seg(P_GDNZ, P_GATE))
    gl = _dense("proj_gate")(h, seg(P_GATE, P_END))
    small_t = small[:, :SMALL_ROWS].reshape(nt, CHUNK, SMALL_ROWS).transpose(0, 2, 1)

    rep = lambda v: jnp.repeat(v, S5_GROUP, axis=1)
    prep = _rowwise(_s5_prep_step, "s5_prep", S5_GROUPS, 5, 0, (S5_STATE * S5_GROUP,) * 4)
    abar_re, abar_im, bbar_re, bbar_im = prep(
        (rep(w["s5_a_re"]), rep(w["s5_a_im"]), jnp.broadcast_to(w["s5_log_step"][:, None], (S5_GROUPS, S5_STATE * S5_GROUP)),
         w["s5_b_re"].reshape(S5_GROUPS, -1), w["s5_b_im"].reshape(S5_GROUPS, -1)), ())
    gpb = S5_GROUPS // S5_BLOCKS
    lanes = gpb * S5_STATE
    to_bd = lambda bb: _block_diag(bb.reshape(S5_BLOCKS, gpb, S5_STATE, S5_GROUP).transpose(0, 1, 3, 2), gpb)
    to_cd = lambda cc: _block_diag(cc.reshape(S5_BLOCKS, gpb, S5_GROUP, S5_STATE).transpose(0, 1, 3, 2), gpb)
    to_a = lambda a: a[:, ::S5_GROUP].reshape(S5_BLOCKS, 1, lanes)
    s5_scan = _scan_op(_s5_scan_step, "s5_scan", tile=t_s5, nb=S5_BLOCKS, row_kinds="b", param_kinds="bbbbbb",
                       carry_shapes=((1, lanes), (1, lanes)), out_widths=(LANES,))
    (y_ssm,) = s5_scan((s5u,), (to_bd(bbar_re), to_bd(bbar_im), to_a(abar_re), to_a(abar_im),
                                to_cd(w["s5_c_re"]), to_cd(w["s5_c_im"])))
    s5_post = _rowwise(_s5_post_step, "s5_post", t_row, 3, 3, (WIDTH,))
    (y_a,) = s5_post((y_ssm, s5u, s5z), (row2(w["s5_d"]), w["s5_w_glu"], row2(w["s5_b_glu"])))

    zeros = lambda n: jnp.zeros((n,), F32)
    bias = jnp.concatenate([w["ssd_dt_bias"], w["gdn_dt_bias"]])
    alog = jnp.concatenate([w["ssd_a_log"], w["gdn_a_log"]])
    bias_l = jnp.concatenate([bias, zeros(LANES - 18)]).reshape(1, LANES)
    alog_l = jnp.concatenate([alog, zeros(LANES - 18)]).reshape(1, LANES)
    bias_c = jnp.concatenate([bias, zeros(SMALL_ROWS - 18)]).reshape(SMALL_ROWS, 1)
    alog_c = jnp.concatenate([alog, zeros(SMALL_ROWS - 18)]).reshape(SMALL_ROWS, 1)
    pad_w = lambda cw: jnp.concatenate([cw, jnp.zeros((HALO - CONV_K, cw.shape[1]), F32)], axis=0)

    cw = pad_w(w["ssd_conv_w"])
    cb = row2(w["ssd_conv_b"])
    grp = SSD_GROUPS * SSD_STATE
    expand = lambda v: jnp.repeat(v.reshape(v.shape[0], SSD_GROUPS, SSD_STATE), SSD_PAIRS // SSD_GROUPS, axis=1).reshape(v.shape[0], -1)
    ssd = _scan_op(_ssd_step, "ssd_scan", tile=CHUNK, nb=SSD_PAIRS, row_kinds="bbbst", param_kinds="bbbbbbbssss",
                   carry_shapes=((HALO, LANES),) * 3 + ((LANES, SSD_STATE),), out_widths=(LANES,))
    (y_ssd,) = ssd(
        (xbc[:, :WIDTH], expand(xbc[:, WIDTH:WIDTH + grp]), expand(xbc[:, WIDTH + grp:]), small, small_t),
        (_per_block(cw[:, :WIDTH], SSD_PAIRS), _per_block(cw[:, WIDTH:WIDTH + grp], SSD_PAIRS, 3),
         _per_block(cw[:, WIDTH + grp:], SSD_PAIRS, 3),
         _per_block(cb[:, :WIDTH], SSD_PAIRS), _per_block(cb[:, WIDTH:WIDTH + grp], SSD_PAIRS, 3),
         _per_block(cb[:, WIDTH + grp:], SSD_PAIRS, 3),
         _per_block(row2(jnp.repeat(w["ssd_d"], SSD_HEAD_DIM)), SSD_PAIRS),
         bias_l, alog_l, bias_c, alog_c))
    ssd_post = _rowwise(_ssd_post_step, "ssd_post", t_row, 2, 1, (WIDTH,))
    (y_b,) = ssd_post((y_ssd, ssdz), (row2(w["ssd_norm_g"]),))

    gw = pad_w(w["gdn_conv_w"])
    gdn = _scan_op(_gdn_step, "gdn_scan", tile=CHUNK, nb=GDN_HEADS, row_kinds="bbbbst", param_kinds="bbbsssss",
                   carry_shapes=((HALO, LANES),) * 3 + ((GDN_DIM, GDN_DIM),), out_widths=(LANES,))
    (y_c,) = gdn(
        (qkv[:, :WIDTH], qkv[:, WIDTH:2 * WIDTH], qkv[:, 2 * WIDTH:], gdnz, small, small_t),
        (_per_block(gw[:, :WIDTH], GDN_HEADS), _per_block(gw[:, WIDTH:2 * WIDTH], GDN_HEADS),
         _per_block(gw[:, 2 * WIDTH:], GDN_HEADS), bias_l, alog_l, bias_c, alog_c, row2(w["gdn_norm_g"])))

    oa = _dense("branch_a")(y_a, w["w_branch"][0])
    ob = _dense("branch_b")(y_b, w["w_branch"][1])
    oc = _dense("branch_c")(y_c, w["w_branch"][2])
    merge = _rowwise(_gate_merge_step, "gate_merge", t_row, 4, 1, (D_MODEL,))
    (merged,) = merge((oa, ob, oc, gl), (w["b_gate"].reshape(1, -1),))
    out = _dense("out_proj")(merged, w["w_out"])
    ln = _rowwise(_ln_res_step, "ln_res", t_row, 2, 2, (D_MODEL,))
    (h_new,) = ln((h, out), (row2(w["ln_g"]), row2(w["ln_b"])))
    return h_new


_LAYER_KEYS = ("w_in", "s5_a_re", "s5_a_im", "s5_log_step", "s5_b_re", "s5_b_im", "s5_c_re", "s5_c_im", "s5_d",
               "s5_w_glu", "s5_b_glu", "ssd_conv_w", "ssd_conv_b", "ssd_dt_bias", "ssd_a_log", "ssd_d", "ssd_norm_g",
               "gdn_conv_w", "gdn_dt_bias", "gdn_a_log", "gdn_norm_g", "w_branch", "b_gate", "w_out", "ln_g", "ln_b")
_WEIGHT_KEYS = ("meta", "ln_in_g", "ln_in_b") + _LAYER_KEYS


def _local_loss(weights, x, target):
    seq = x.shape[0]
    hcat = jnp.concatenate([jnp.zeros((PAD, D_MODEL), F32), weights["meta"], x], axis=0)
    length = hcat.shape[0]
    ln_in = _rowwise(_ln_in_step, "ln_in", _pick(length, (416, 256, 128)), 1, 2, (D_MODEL,))
    (h,) = ln_in((hcat,), (weights["ln_in_g"].reshape(1, -1), weights["ln_in_b"].reshape(1, -1)))

    def body(hh, w):
        return _layer(hh, w), None

    h, _ = lax.scan(body, h, {k: weights[k] for k in _LAYER_KEYS})
    return _loss_op(h[length - seq:], target)


_ANY = pl.BlockSpec(memory_space=pl.ANY)
_BLOCK_BYTES = 4 << 20


def _chip_exchange(arrays, all_to_all, name):
    n = len(arrays)

    def body(*refs):
        ins, outs = refs[:n], refs[n:2 * n]
        send_sems, recv_sems, local_sems = refs[2 * n:]
        mx, my, mc = lax.axis_index("x"), lax.axis_index("y"), lax.axis_index("c")
        me = 2 * mx + my
        peers = [(1 - mx, my), (mx, 1 - my), (1 - mx, 1 - my)]
        own_copies, sends = [], []
        for a, (src, dst) in enumerate(zip(ins, outs)):
            own = pltpu.make_async_copy(src.at[me] if all_to_all else src, dst.at[me], local_sems.at[a])
            own.start()
            own_copies.append(own)
            for k, (px, py) in enumerate(peers):
                cp = pltpu.make_async_remote_copy(
                    src_ref=src.at[2 * px + py] if all_to_all else src, dst_ref=dst.at[me],
                    send_sem=send_sems.at[a, k], recv_sem=recv_sems.at[a, k],
                    device_id=(px, py, mc), device_id_type=MESH)
                cp.start()
                sends.append(cp)
        for a, (src, dst) in enumerate(zip(ins, outs)):
            for k, (px, py) in enumerate(peers):
                pltpu.make_async_remote_copy(
                    src_ref=src.at[me] if all_to_all else src, dst_ref=dst.at[2 * px + py],
                    send_sem=send_sems.at[a, k], recv_sem=recv_sems.at[a, k],
                    device_id=(px, py, mc), device_id_type=MESH).wait_recv()
        for cp in sends:
            cp.wait_send()
        for cp in own_copies:
            cp.wait()

    out_shape = [jax.ShapeDtypeStruct(a.shape if all_to_all else (4,) + a.shape, a.dtype) for a in arrays]
    return pl.pallas_call(
        body, name=name, in_specs=[_ANY] * n, out_specs=[_ANY] * n, out_shape=out_shape,
        scratch_shapes=[pltpu.SemaphoreType.DMA((n, 3)), pltpu.SemaphoreType.DMA((n, 3)), pltpu.SemaphoreType.DMA((n,))],
    )(*arrays)


def _core_swap(arrays, name):
    n = len(arrays)

    def body(*refs):
        ins, outs = refs[:n], refs[n:2 * n]
        send_sems, recv_sems = refs[2 * n:]
        sibling = (lax.axis_index("x"), lax.axis_index("y"), 1 - lax.axis_index("c"))
        copies = [pltpu.make_async_remote_copy(src_ref=s, dst_ref=d, send_sem=send_sems.at[a], recv_sem=recv_sems.at[a],
                                               device_id=sibling, device_id_type=MESH)
                  for a, (s, d) in enumerate(zip(ins, outs))]
        for cp in copies:
            cp.start()
        for cp in copies:
            cp.wait()

    return pl.pallas_call(
        body, name=name, in_specs=[_ANY] * n, out_specs=[_ANY] * n,
        out_shape=[jax.ShapeDtypeStruct(a.shape, a.dtype) for a in arrays],
        scratch_shapes=[pltpu.SemaphoreType.DMA((n,)), pltpu.SemaphoreType.DMA((n,))],
    )(*arrays)


def _as_rows(a, lead=0):
    shp = a.shape
    return a.reshape(shp[:lead] + (-1, shp[-1]))


def _sum4_call(x, name):
    _, r, c = x.shape
    tr = _pick(r, [t for t in (512, 256, 128, 64, 32, 16, 8) if 16 * t * c <= _BLOCK_BYTES] + [r])

    def body(x_ref, o_ref):
        o_ref[...] = (x_ref[0] + x_ref[1]) + (x_ref[2] + x_ref[3])

    return pl.pallas_call(
        body, name=name, grid=(r // tr,),
        in_specs=[pl.BlockSpec((4, tr, c), lambda i: (0, i, 0))],
        out_specs=pl.BlockSpec((tr, c), lambda i: (i, 0)),
        out_shape=jax.ShapeDtypeStruct((r, c), F32),
        compiler_params=_cparams(("parallel",), 48),
    )(x)


def _adam_call(w, g0, g1, m, v, name):
    r, c = w.shape
    tr = _pick(r, [t for t in (512, 256, 128, 64, 32, 16, 8) if 4 * t * c <= _BLOCK_BYTES // 4] + [r])
    bc1 = 1.0 - ADAM_B1 ** ADAM_STEP
    bc2 = 1.0 - ADAM_B2 ** ADAM_STEP

    def body(w_ref, g0_ref, g1_ref, m_ref, v_ref, g_out, d_out, m_out, v_out):
        g = g0_ref[...] + g1_ref[...]
        m_new = ADAM_B1 * m_ref[...] + (1.0 - ADAM_B1) * g
        v_new = ADAM_B2 * v_ref[...] + (1.0 - ADAM_B2) * (g * g)
        m_hat = m_new / bc1
        v_hat = v_new / bc2
        g_out[...] = g
        d_out[...] = -ADAM_LR * (m_hat / (jnp.sqrt(v_hat) + ADAM_EPS) + ADAM_WD * w_ref[...])
        m_out[...] = m_new
        v_out[...] = v_new

    spec = pl.BlockSpec((tr, c), lambda i: (i, 0))
    return pl.pallas_call(
        body, name=name, grid=(r // tr,), in_specs=[spec] * 5, out_specs=[spec] * 4,
        out_shape=[jax.ShapeDtypeStruct((r, c), F32)] * 4,
        compiler_params=_cparams(("parallel",), 48),
    )(w, g0, g1, m, v)


_SHARDED = {"meta": (1, False), "w_in": (2, True), "s5_w_glu": (1, True), "ssd_conv_w": (2, False),
            "gdn_conv_w": (2, False), "w_branch": (3, True), "b_gate": (2, False), "w_out": (1, True)}


def _pack(arrs):
    flat = jnp.concatenate([a.reshape(-1) for a in arrs])
    n = flat.shape[0]
    rows = -(-n // (256 * LANES)) * 256
    return jnp.concatenate([flat, jnp.zeros((rows * LANES - n,), F32)]).reshape(rows, LANES)


def _unpack(packed, like):
    flat = packed.reshape(-1)
    out, off = [], 0
    for a in like:
        out.append(flat[off:off + a.size].reshape(a.shape))
        off += a.size
    return out


def kernel(x, meta, ln_in_g, ln_in_b, w_in, s5_a_re, s5_a_im, s5_log_step, s5_b_re, s5_b_im, s5_c_re, s5_c_im, s5_d, s5_w_glu, s5_b_glu, ssd_conv_w, ssd_conv_b, ssd_dt_bias, ssd_a_log, ssd_d, ssd_norm_g, gdn_conv_w, gdn_dt_bias, gdn_a_log, gdn_norm_g, w_branch, b_gate, w_out, ln_g, ln_b, loss_target, m_meta, m_ln_in_g, m_ln_in_b, m_w_in, m_s5_a_re, m_s5_a_im, m_s5_log_step, m_s5_b_re, m_s5_b_im, m_s5_c_re, m_s5_c_im, m_s5_d, m_s5_w_glu, m_s5_b_glu, m_ssd_conv_w, m_ssd_conv_b, m_ssd_dt_bias, m_ssd_a_log, m_ssd_d, m_ssd_norm_g, m_gdn_conv_w, m_gdn_dt_bias, m_gdn_a_log, m_gdn_norm_g, m_w_branch, m_b_gate, m_w_out, m_ln_g, m_ln_b, v_meta, v_ln_in_g, v_ln_in_b, v_w_in, v_s5_a_re, v_s5_a_im, v_s5_log_step, v_s5_b_re, v_s5_b_im, v_s5_c_re, v_s5_c_im, v_s5_d, v_s5_w_glu, v_s5_b_glu, v_ssd_conv_w, v_ssd_conv_b, v_ssd_dt_bias, v_ssd_a_log, v_ssd_d, v_ssd_norm_g, v_gdn_conv_w, v_gdn_dt_bias, v_gdn_a_log, v_gdn_norm_g, v_w_branch, v_b_gate, v_w_out, v_ln_g, v_ln_b):
    args = dict(locals())
    shards = {k: args[k] for k in _WEIGHT_KEYS}
    moms = {k: (args["m_" + k], args["v_" + k]) for k in _WEIGHT_KEYS}

    names = list(_SHARDED)
    sent = [shards[k].astype(BF16) if _SHARDED[k][1] else shards[k] for k in names]
    gathered = _chip_exchange(sent, False, "gather_weights")
    full = dict(shards)
    for k, g in zip(names, gathered):
        ax = _SHARDED[k][0]
        full[k] = jnp.concatenate([g[j] for j in range(4)], axis=ax).astype(F32)

    loss, (grads, grad_x) = jax.value_and_grad(_local_loss, argnums=(0, 1))(full, x[0], loss_target[0])
    loss = lax.psum(loss, ("x", "y", "c"))

    blocks = []
    for k in names:
        ax = _SHARDED[k][0]
        blocks.append(jnp.stack(jnp.split(grads[k], 4, axis=ax), axis=0))
    small_names = [k for k in _WEIGHT_KEYS if k not in _SHARDED]
    packed = _pack([grads[k] for k in small_names])
    arrived = _chip_exchange(blocks, True, "scatter_grads")
    (packed4,) = _chip_exchange([packed], False, "gather_small_grads")
    partial = [_sum4_call(_as_rows(a, 1), "sum_chips_" + k) for k, a in zip(names, arrived)]
    partial.append(_sum4_call(packed4, "sum_chips_small"))
    other = _core_swap(partial, "swap_cores")

    outs = {}
    for k, p, o in zip(names, partial[:-1], other[:-1]):
        shp = shards[k].shape
        res = _adam_call(_as_rows(shards[k]), p, o, _as_rows(moms[k][0]), _as_rows(moms[k][1]), "adamw_" + k)
        outs[k] = [r.reshape(shp) for r in res]
    like = [shards[k] for k in small_names]
    res = _adam_call(_pack(like), partial[-1], other[-1], _pack([moms[k][0] for k in small_names]),
                     _pack([moms[k][1] for k in small_names]), "adamw_small")
    for idx in range(4):
        for k, a in zip(small_names, _unpack(res[idx], like)):
            outs.setdefault(k, [None] * 4)[idx] = a

    result = [loss, grad_x[None]]
    for idx in range(4):
        result += [outs[k][idx] for k in _WEIGHT_KEYS]
    return tuple(result)
```

```python
import functools

import jax
import jax.numpy as jnp
from jax import lax
from jax.experimental import pallas as pl
from jax.experimental.pallas import tpu as pltpu

F32 = jnp.float32
BF16 = jnp.bfloat16
MESH = pl.DeviceIdType.MESH
F32_DOT = lax.Precision.HIGH

D_MODEL = 1024
DEPTH = 4
N_META = 16
CHUNK = 64
PAD = CHUNK - N_META
CONV_K = 4
HALO = 8
WIDTH = 768
S5_GROUPS, S5_GROUP, S5_STATE = 48, 16, 64
S5_BLOCKS = 6
SSD_HEADS, SSD_HEAD_DIM, SSD_GROUPS, SSD_STATE = 12, 64, 2, 128
SSD_PAIRS = 6
GDN_HEADS, GDN_DIM = 6, 128
LANES = 128
SMALL_ROWS = 24
ALPHA = (2 * DEPTH) ** 0.25
LN_EPS = 1e-5
P_S5U, P_S5Z, P_XBC, P_DT, P_SSDZ, P_QKV, P_GA, P_GB, P_GDNZ, P_GATE, P_END = (
    0, 768, 1536, 2816, 2828, 3596, 5900, 5906, 5912, 6680, 9752)
ADAM_LR, ADAM_B1, ADAM_B2, ADAM_EPS, ADAM_WD, ADAM_STEP = 0.001, 0.9, 0.999, 1e-08, 0.01, 10
NEG = -1e30


def _pick(n, cands):
    for c in cands:
        if n % c == 0:
            return c
    raise ValueError(f"no tile for {n} in {cands}")


def _cparams(sem, vmem_mb):
    return pltpu.CompilerParams(dimension_semantics=sem, vmem_limit_bytes=vmem_mb << 20)


_DIMS = {"nn": (((1,), (0,)), ((), ())), "nt": (((1,), (1,)), ((), ())), "tn": (((0,), (0,)), ((), ()))}


def _dot(a, b, mode, hi):
    if hi:
        return lax.dot_general(a, b, _DIMS[mode], precision=F32_DOT, preferred_element_type=F32)
    return lax.dot_general(a.astype(BF16), b.astype(BF16), _DIMS[mode], preferred_element_type=F32)


@functools.partial(jax.custom_vjp, nondiff_argnums=(2, 3))
def _mm(a, b, mode="nn", hi=False):
    return _dot(a, b, mode, hi)


def _mm_fwd(a, b, mode, hi):
    return _dot(a, b, mode, hi), (a, b)


def _mm_bwd(mode, hi, res, g):
    a, b = res
    if mode == "nn":
        return _dot(g, b, "nt", hi), _dot(a, g, "tn", hi)
    if mode == "nt":
        return _dot(g, b, "nn", hi), _dot(g, a, "tn", hi)
    return _dot(b, g, "nt", hi), _dot(a, g, "nn", hi)


_mm.defvjp(_mm_fwd, _mm_bwd)


@functools.partial(jax.custom_vjp, nondiff_argnums=(1,))
def _roll_rows(x, k):
    return pltpu.roll(x, k % x.shape[0], 0)


def _roll_fwd(x, k):
    return _roll_rows(x, k), None


def _roll_bwd(k, _, g):
    return (_roll_rows(g, -k),)


_roll_rows.defvjp(_roll_fwd, _roll_bwd)


def _iota(shape, dim):
    return lax.broadcasted_iota(jnp.int32, shape, dim)


def _valid_rows(row0, n):
    return (row0 + _iota((n, 1), 0)) >= PAD


def _lane(x, idx):
    return jnp.sum(jnp.where(_iota(x.shape, 1) == idx, x, 0.0), axis=1, keepdims=True)


def _row(x, idx):
    return jnp.sum(jnp.where(_iota(x.shape, 0) == idx, x, 0.0), axis=0, keepdims=True)


def _layer_norm(z, g, b):
    mu = jnp.mean(z, axis=-1, keepdims=True)
    zc = z - mu
    var = jnp.mean(zc * zc, axis=-1, keepdims=True)
    return zc * lax.rsqrt(var + LN_EPS) * g + b


def _rms_norm(z, g):
    return z * lax.rsqrt(jnp.mean(z * z, axis=-1, keepdims=True) + LN_EPS) * g


def _causal_conv(halo, x, w, row0):
    t = x.shape[0]
    xc = jnp.concatenate([halo, x], axis=0)
    acc = None
    for j in range(CONV_K):
        term = _roll_rows(xc, CONV_K - 1 - j)[HALO:HALO + t] * _row(w, j)
        acc = term if acc is None else acc + term
    return acc, x[t - HALO:t]


def _tri(n, strict=False):
    r, c = _iota((n, n), 0), _iota((n, n), 1)
    return (r > c) if strict else (r >= c)


def _scan_op(step, name, *, tile, nb, row_kinds, param_kinds, carry_shapes, out_widths, vmem_mb=48):
    n_rows, n_par, n_car, n_out = len(row_kinds), len(param_kinds), len(carry_shapes), len(out_widths)

    def dims(rows):
        for k, a in zip(row_kinds, rows):
            if k in "bs":
                return a.shape[0], a.shape[0] // tile
        raise ValueError("need a row input")

    def row_spec(kind, a, rev, nt):
        ti = (lambda i: nt - 1 - i) if rev else (lambda i: i)
        if kind == "b":
            return pl.BlockSpec((tile, a.shape[1] // nb), lambda b, i: (ti(i), b))
        if kind == "s":
            return pl.BlockSpec((tile, a.shape[1]), lambda b, i: (ti(i), 0))
        return pl.BlockSpec((None, a.shape[1], a.shape[2]), lambda b, i: (ti(i), 0, 0))

    def par_spec(kind, a):
        if kind == "b":
            return pl.BlockSpec((None, a.shape[1], a.shape[2]), lambda b, i: (b, 0, 0))
        return pl.BlockSpec(a.shape, lambda b, i: (0, 0))

    def fwd_call(rows, params):
        length, nt = dims(rows)

        def body(*refs):
            r_in = refs[:n_rows]
            p_in = refs[n_rows:n_rows + n_par]
            o_out = refs[n_rows + n_par:n_rows + n_par + n_out]
            s_out = refs[n_rows + n_par + n_out:n_rows + n_par + n_out + n_car]
            c_scr = refs[n_rows + n_par + n_out + n_car:]
            b, i = pl.program_id(0), pl.program_id(1)

            if n_car:
                @pl.when(i == 0)
                def _():
                    for c in c_scr:
                        c[...] = jnp.zeros_like(c)

            cin = tuple(c[...] for c in c_scr)
            for s, c in zip(s_out, cin):
                s[...] = c
            new_c, outs = step(cin, tuple(r[...] for r in r_in), tuple(p[...] for p in p_in), b, i * tile)
            for c, v in zip(c_scr, new_c):
                c[...] = v
            for o, v in zip(o_out, outs):
                o[...] = v

        out_shape = [jax.ShapeDtypeStruct((length, nb * w), F32) for w in out_widths]
        out_shape += [jax.ShapeDtypeStruct((nb, nt) + tuple(s), F32) for s in carry_shapes]
        out_specs = [pl.BlockSpec((tile, w), lambda b, i: (i, b)) for w in out_widths]
        out_specs += [pl.BlockSpec((None, None) + tuple(s), lambda b, i: (b, i, 0, 0)) for s in carry_shapes]
        res = pl.pallas_call(
            body, name=name + "_fwd", grid=(nb, nt),
            in_specs=[row_spec(k, a, False, nt) for k, a in zip(row_kinds, rows)]
            + [par_spec(k, a) for k, a in zip(param_kinds, params)],
            out_specs=out_specs, out_shape=out_shape,
            scratch_shapes=[pltpu.VMEM(tuple(s), F32) for s in carry_shapes],
            compiler_params=_cparams(("arbitrary", "arbitrary"), vmem_mb),
        )(*rows, *params)
        return tuple(res[:n_out]), tuple(res[n_out:])

    def bwd_call(rows, params, saved, douts):
        length, nt = dims(rows)

        def body(*refs):
            k0 = 0
            r_in = refs[k0:k0 + n_rows]; k0 += n_rows
            p_in = refs[k0:k0 + n_par]; k0 += n_par
            s_in = refs[k0:k0 + n_car]; k0 += n_car
            g_in = refs[k0:k0 + n_out]; k0 += n_out
            dr_out = refs[k0:k0 + n_rows]; k0 += n_rows
            dp_out = refs[k0:k0 + n_par]; k0 += n_par
            dc_scr = refs[k0:]
            b, i = pl.program_id(0), pl.program_id(1)
            row0 = (nt - 1 - i) * tile

            @pl.when(i == 0)
            def _():
                for c in dc_scr:
                    c[...] = jnp.zeros_like(c)
                for p in dp_out:
                    p[...] = jnp.zeros_like(p)

            def f(c, r, p):
                return step(c, r, p, b, row0)

            _, vjp = jax.vjp(f, tuple(s[...] for s in s_in), tuple(r[...] for r in r_in),
                             tuple(p[...] for p in p_in))
            dc, dr, dp = vjp((tuple(c[...] for c in dc_scr), tuple(g[...] for g in g_in)))
            for c, v in zip(dc_scr, dc):
                c[...] = v
            for o, v in zip(dr_out, dr):
                o[...] = v
            for o, v in zip(dp_out, dp):
                o[...] += v

        rev = lambda i: nt - 1 - i
        in_specs = [row_spec(k, a, True, nt) for k, a in zip(row_kinds, rows)]
        in_specs += [par_spec(k, a) for k, a in zip(param_kinds, params)]
        in_specs += [pl.BlockSpec((None, None) + tuple(s), lambda b, i: (b, rev(i), 0, 0)) for s in carry_shapes]
        in_specs += [pl.BlockSpec((tile, w), lambda b, i: (rev(i), b)) for w in out_widths]
        out_shape, out_specs = [], []
        for k, a in zip(row_kinds, rows):
            if k == "b":
                out_shape.append(jax.ShapeDtypeStruct(a.shape, F32))
                out_specs.append(pl.BlockSpec((tile, a.shape[1] // nb), lambda b, i: (rev(i), b)))
            elif k == "s":
                out_shape.append(jax.ShapeDtypeStruct((nb,) + a.shape, F32))
                out_specs.append(pl.BlockSpec((None, tile, a.shape[1]), lambda b, i: (b, rev(i), 0)))
            else:
                out_shape.append(jax.ShapeDtypeStruct((nb,) + a.shape, F32))
                out_specs.append(pl.BlockSpec((None, None, a.shape[1], a.shape[2]), lambda b, i: (b, rev(i), 0, 0)))
        for k, a in zip(param_kinds, params):
            shp = a.shape[1:] if k == "b" else a.shape
            out_shape.append(jax.ShapeDtypeStruct((nb,) + tuple(shp), F32))
            out_specs.append(pl.BlockSpec((None,) + tuple(shp), lambda b, i: (b, 0, 0)))
        res = pl.pallas_call(
            body, name=name + "_bwd", grid=(nb, nt), in_specs=in_specs, out_specs=out_specs, out_shape=out_shape,
            scratch_shapes=[pltpu.VMEM(tuple(s), F32) for s in carry_shapes],
            compiler_params=_cparams(("arbitrary", "arbitrary"), vmem_mb),
        )(*rows, *params, *saved, *douts)
        fold = (lambda a: a[0]) if nb == 1 else (lambda a: jnp.sum(a, axis=0))
        drows = tuple(r if k == "b" else fold(r) for k, r in zip(row_kinds, res[:n_rows]))
        dpars = tuple(p if k == "b" else fold(p) for k, p in zip(param_kinds, res[n_rows:]))
        return drows, dpars

    @jax.custom_vjp
    def op(rows, params):
        return fwd_call(rows, params)[0]

    def op_fwd(rows, params):
        outs, saved = fwd_call(rows, params)
        return outs, (rows, params, saved)

    def op_bwd(res, douts):
        rows, params, saved = res
        return bwd_call(rows, params, saved, tuple(douts))

    op.defvjp(op_fwd, op_bwd)
    return op


_TM = (832, 512, 256, 128)
_TN = (768, 640, 512, 384, 256, 128)


def _mm_fwd_call(x, w, name):
    m, k = x.shape
    n = w.shape[1]
    tm, tn = _pick(m, _TM), _pick(n, _TN)

    def body(x_ref, w_ref, o_ref):
        o_ref[...] = _dot(x_ref[...], w_ref[...], "nn", False)

    return pl.pallas_call(
        body, name=name, grid=(n // tn, m // tm),
        in_specs=[pl.BlockSpec((tm, k), lambda j, i: (i, 0)), pl.BlockSpec((k, tn), lambda j, i: (0, j))],
        out_specs=pl.BlockSpec((tm, tn), lambda j, i: (i, j)),
        out_shape=jax.ShapeDtypeStruct((m, n), F32),
        compiler_params=_cparams(("parallel", "parallel"), 48),
    )(x, w)


def _mm_dx_call(g, w, name):
    m, n = g.shape
    k = w.shape[0]
    tm, tn = _pick(m, _TM), _pick(n, _TN)

    def body(g_ref, w_ref, o_ref):
        @pl.when(pl.program_id(1) == 0)
        def _():
            o_ref[...] = jnp.zeros_like(o_ref)

        o_ref[...] += _dot(g_ref[...], w_ref[...], "nt", False)

    return pl.pallas_call(
        body, name=name, grid=(m // tm, n // tn),
        in_specs=[pl.BlockSpec((tm, tn), lambda i, j: (i, j)), pl.BlockSpec((k, tn), lambda i, j: (0, j))],
        out_specs=pl.BlockSpec((tm, k), lambda i, j: (i, 0)),
        out_shape=jax.ShapeDtypeStruct((m, k), F32),
        compiler_params=_cparams(("parallel", "arbitrary"), 48),
    )(g, w)


def _mm_dw_call(x, g, name):
    m, k = x.shape
    n = g.shape[1]
    tm, tn = _pick(m, _TM), _pick(n, _TN)

    def body(x_ref, g_ref, o_ref):
        @pl.when(pl.program_id(1) == 0)
        def _():
            o_ref[...] = jnp.zeros_like(o_ref)

        o_ref[...] += _dot(x_ref[...], g_ref[...], "tn", False)

    return pl.pallas_call(
        body, name=name, grid=(n // tn, m // tm),
        in_specs=[pl.BlockSpec((tm, k), lambda j, i: (i, 0)), pl.BlockSpec((tm, tn), lambda j, i: (i, j))],
        out_specs=pl.BlockSpec((k, tn), lambda j, i: (0, j)),
        out_shape=jax.ShapeDtypeStruct((k, n), F32),
        compiler_params=_cparams(("parallel", "arbitrary"), 48),
    )(x, g)


def _dense(name):
    @jax.custom_vjp
    def op(x, w):
        return _mm_fwd_call(x, w, name + "_fwd")

    def op_fwd(x, w):
        return _mm_fwd_call(x, w, name + "_fwd"), (x, w)

    def op_bwd(res, g):
        x, w = res
        return _mm_dx_call(g, w, name + "_dx"), _mm_dw_call(x, g, name + "_dw")

    op.defvjp(op_fwd, op_bwd)
    return op


def _ln_in_step(c, rows, params, b, row0):
    (z,), (g, bb) = rows, params
    return (), (jnp.where(_valid_rows(row0, z.shape[0]), _layer_norm(z, g, bb), 0.0),)


def _ln_res_step(c, rows, params, b, row0):
    (h, o), (g, bb) = rows, params
    return (), (jnp.where(_valid_rows(row0, h.shape[0]), _layer_norm(ALPHA * h + o, g, bb), 0.0),)


def _s5_prep_step(c, rows, params, b, row0):
    a_re, a_im, log_step, b_re, b_im = rows
    lam_re = jnp.minimum(a_re, -1e-4)
    lam_im = a_im
    step = jnp.exp(log_step)
    mag = jnp.exp(lam_re * step)
    abar_re, abar_im = mag * jnp.cos(lam_im * step), mag * jnp.sin(lam_im * step)
    den = lam_re * lam_re + lam_im * lam_im
    nr, ni = abar_re - 1.0, abar_im
    coef_re = (nr * lam_re + ni * lam_im) / den
    coef_im = (ni * lam_re - nr * lam_im) / den
    return (), (abar_re, abar_im, coef_re * b_re - coef_im * b_im, coef_re * b_im + coef_im * b_re)


def _s5_scan_step(c, rows, params, b, row0):
    (c_re, c_im), (u,) = c, rows
    bd_re, bd_im, a_re, a_im, cd_re, cd_im = params
    t = u.shape[0]
    ridx = _iota((t, 1), 0)
    first = ridx == 0
    s_re = _mm(u, bd_re) + jnp.where(first, a_re * c_re - a_im * c_im, 0.0)
    s_im = _mm(u, bd_im) + jnp.where(first, a_re * c_im + a_im * c_re, 0.0)
    p_re, p_im = a_re, a_im
    d = 1
    while d < t:
        keep = ridx >= d
        sh_re = jnp.where(keep, _roll_rows(s_re, d), 0.0)
        sh_im = jnp.where(keep, _roll_rows(s_im, d), 0.0)
        s_re, s_im = s_re + p_re * sh_re - p_im * sh_im, s_im + p_re * sh_im + p_im * sh_re
        p_re, p_im = p_re * p_re - p_im * p_im, 2.0 * p_re * p_im
        d *= 2
    y = _mm(s_re, cd_re) - _mm(s_im, cd_im)
    return (_row(s_re, t - 1), _row(s_im, t - 1)), (y,)


def _s5_post_step(c, rows, params, b, row0):
    (y, u, z), (d, w_glu, b_glu) = rows, params
    v = jax.nn.gelu(y + d * u)
    v = v * jax.nn.sigmoid(_mm(v, w_glu) + b_glu)
    return (), (v * jax.nn.silu(z),)


def _ssd_step(c, rows, params, b, row0):
    halo, state = c
    xbc_raw, z, small, small_t = rows
    cw, cb, d_l, bias_l, alog_l, bias_c, alog_c, norm_g = params
    t = xbc_raw.shape[0]
    grp = SSD_GROUPS * SSD_STATE
    valid = _valid_rows(row0, t)
    conv, halo2 = _causal_conv(halo, xbc_raw, cw, row0)
    act = jnp.where(valid, jax.nn.silu(conv + cb), 0.0)
    low = _iota((1, LANES), 1) < SSD_HEAD_DIM
    dt_all = jnp.where(valid, jax.nn.softplus(small + bias_l), 0.0)
    a_all = -jnp.exp(alog_l)
    valid_t = (row0 + _iota((1, t), 1)) >= PAD
    dta_t = jnp.where(valid_t, jax.nn.softplus(small_t + bias_c), 0.0) * (-jnp.exp(alog_c))
    acum_t = _mm(dta_t, jnp.where(_iota((t, t), 0) <= _iota((t, t), 1), 1.0, 0.0), "nn", True)
    causal = _tri(t)
    acum_all = _mm(jnp.where(causal, 1.0, 0.0), dt_all * a_all, "nn", True)
    last_all = _row(acum_all, t - 1)
    low_rows = _iota((LANES, 1), 0) < SSD_HEAD_DIM
    pairs, groups = range(SSD_PAIRS), range(SSD_GROUPS)
    grp_of = [p // (SSD_PAIRS // SSD_GROUPS) for p in pairs]
    bs = [act[:, WIDTH + g * SSD_STATE:WIDTH + (g + 1) * SSD_STATE] for g in groups]
    cs = [act[:, WIDTH + grp + g * SSD_STATE:WIDTH + grp + (g + 1) * SSD_STATE] for g in groups]
    scores = [_mm(cs[g], bs[g], "nt") for g in groups]
    per_lane = lambda v, p: jnp.where(low, v[:, 2 * p:2 * p + 1], v[:, 2 * p + 1:2 * p + 2])
    dt_l = [per_lane(dt_all, p) for p in pairs]
    acum_l = [per_lane(acum_all, p) for p in pairs]
    last_l = [per_lane(last_all, p) for p in pairs]
    st = [state[p * LANES:(p + 1) * LANES] for p in pairs]
    xd = [act[:, p * LANES:(p + 1) * LANES] * dt_l[p] for p in pairs]
    decay = [jnp.exp(jnp.where(causal, acum_all[:, h:h + 1] - _row(acum_t, h), NEG)) for h in range(SSD_HEADS)]
    y_lo = [_mm(scores[grp_of[p]] * decay[2 * p], jnp.where(low, xd[p], 0.0)) for p in pairs]
    y_hi = [_mm(scores[grp_of[p]] * decay[2 * p + 1], jnp.where(low, 0.0, xd[p])) for p in pairs]
    y_off = [_mm(cs[grp_of[p]], st[p], "nt") * jnp.exp(acum_l[p]) for p in pairs]
    new_st = [_mm(xd[p] * jnp.exp(last_l[p] - acum_l[p]), bs[grp_of[p]], "tn") for p in pairs]
    cd = jnp.exp(last_all)
    new_st = [st[p] * jnp.where(low_rows, cd[:, 2 * p:2 * p + 1], cd[:, 2 * p + 1:2 * p + 2]) + new_st[p] for p in pairs]
    y = jnp.concatenate([y_lo[p] + y_hi[p] + y_off[p] for p in pairs], axis=1) + act[:, :WIDTH] * d_l
    out = _rms_norm(y * jax.nn.silu(z), norm_g)
    return (halo2, jnp.concatenate(new_st, axis=0)), (out,)


def _unit_lower_inverse(mats):
    n = mats[0].shape[0]
    eye = jnp.where(_iota((n, n), 0) == _iota((n, n), 1), 1.0, 0.0)
    inv = [eye - a for a in mats]
    p = [_mm(a, a, "nn", True) for a in mats]
    k = 2
    while k < n:
        inv = [i + _mm(i, q, "nn", True) for i, q in zip(inv, p)]
        k *= 2
        if k < n:
            p = [_mm(q, q, "nn", True) for q in p]
    return inv


def _gdn_step(c, rows, params, b, row0):
    halo, state = c
    qkv_raw, z, small, small_t = rows
    cw, bias_l, alog_l, bias_c, alog_c, norm_g = params
    t = qkv_raw.shape[0]
    valid = _valid_rows(row0, t)
    conv, halo2 = _causal_conv(halo, qkv_raw, cw, row0)
    act = jnp.where(valid, jax.nn.silu(conv), 0.0)
    beta_all = jnp.where(valid, jax.nn.sigmoid(small), 0.0)
    g_all = jnp.where(valid, -jnp.exp(alog_l) * jax.nn.softplus(small + bias_l), 0.0)
    valid_t = (row0 + _iota((1, t), 1)) >= PAD
    g_t = jnp.where(valid_t, -jnp.exp(alog_c) * jax.nn.softplus(small_t + bias_c), 0.0)
    causal, strict = _tri(t), _tri(t, True)
    gcum_all = _mm(jnp.where(causal, 1.0, 0.0), g_all, "nn", True)
    gcum_t = _mm(g_t, jnp.where(_iota((t, t), 0) <= _iota((t, t), 1), 1.0, 0.0), "nn", True)
    heads = range(GDN_HEADS)
    part = lambda h, n: act[:, n * WIDTH + h * GDN_DIM:n * WIDTH + (h + 1) * GDN_DIM]
    unit = lambda x: x * lax.rsqrt(jnp.sum(x * x, axis=-1, keepdims=True) + 1e-6)
    q = [unit(part(h, 0)) * (GDN_DIM ** -0.5) for h in heads]
    k = [unit(part(h, 1)) for h in heads]
    st = [state[h * GDN_DIM:(h + 1) * GDN_DIM] for h in heads]
    ia = [SSD_HEADS + h for h in heads]
    beta = [beta_all[:, ia[h] + GDN_HEADS:ia[h] + GDN_HEADS + 1] for h in heads]
    gcum = [gcum_all[:, ia[h]:ia[h] + 1] for h in heads]
    gamma = [jnp.exp(jnp.where(causal, gcum[h] - _row(gcum_t, ia[h]), NEG)) for h in heads]
    egc = [jnp.exp(gcum[h]) for h in heads]
    a_mat = [jnp.where(strict, _mm(k[h], k[h], "nt") * gamma[h] * beta[h], 0.0) for h in heads]
    inv = _unit_lower_inverse(a_mat)
    rhs = [jnp.concatenate([part(h, 2) * beta[h], k[h] * (beta[h] * egc[h])], axis=1) for h in heads]
    sol = [_mm(inv[h], rhs[h], "nn", True) for h in heads]
    attn = [_mm(q[h], k[h], "nt") * gamma[h] for h in heads]
    from_state = [_mm(jnp.concatenate([sol[h][:, GDN_DIM:], q[h] * egc[h]], axis=0), st[h]) for h in heads]
    v_new = [sol[h][:, :GDN_DIM] - from_state[h][:t] for h in heads]
    o = [from_state[h][t:] + _mm(attn[h], v_new[h]) for h in heads]
    glast = [_row(gcum[h], t - 1) for h in heads]
    new_st = [st[h] * jnp.exp(glast[h]) + _mm(k[h] * jnp.exp(glast[h] - gcum[h]), v_new[h], "tn") for h in heads]
    out = jnp.concatenate([_rms_norm(o[h], norm_g) for h in heads], axis=1) * jax.nn.silu(z)
    return (halo2, jnp.concatenate(new_st, axis=0)), (out,)


def _gate_merge_step(c, rows, params, b, row0):
    (oa, ob, oc, gl), (bg,) = rows, params
    acc = None
    for n, o in enumerate((oa, ob, oc)):
        term = jax.nn.sigmoid(gl[:, n * D_MODEL:(n + 1) * D_MODEL] + bg[:, n * D_MODEL:(n + 1) * D_MODEL]) * o
        acc = term if acc is None else acc + term
    return (), (acc,)


def _loss_tile(n):
    return _pick(n, (512, 256, 128, 64))


def _loss_fwd_call(y, tgt):
    n, d = y.shape
    tile = _loss_tile(n)

    def body(y_ref, t_ref, o_ref):
        @pl.when(pl.program_id(0) == 0)
        def _():
            o_ref[...] = jnp.zeros_like(o_ref)

        e = y_ref[...] - t_ref[...]
        o_ref[...] += jnp.sum(jnp.sum(e * e, axis=1, keepdims=True), axis=0, keepdims=True) * (0.5 / d)

    out = pl.pallas_call(
        body, name="loss_fwd", grid=(n // tile,),
        in_specs=[pl.BlockSpec((tile, d), lambda i: (i, 0)), pl.BlockSpec((tile, d), lambda i: (i, 0))],
        out_specs=pl.BlockSpec((8, LANES), lambda i: (0, 0)),
        out_shape=jax.ShapeDtypeStruct((8, LANES), F32),
        compiler_params=_cparams(("arbitrary",), 32),
    )(y, tgt)
    return out[0, 0]


def _loss_bwd_call(y, tgt, g):
    n, d = y.shape
    tile = _loss_tile(n)

    def body(y_ref, t_ref, g_ref, o_ref):
        o_ref[...] = (y_ref[...] - t_ref[...]) * (g_ref[...][0:1, 0:1] * (1.0 / d))

    return pl.pallas_call(
        body, name="loss_bwd", grid=(n // tile,),
        in_specs=[pl.BlockSpec((tile, d), lambda i: (i, 0)), pl.BlockSpec((tile, d), lambda i: (i, 0)),
                  pl.BlockSpec((8, LANES), lambda i: (0, 0))],
        out_specs=pl.BlockSpec((tile, d), lambda i: (i, 0)),
        out_shape=jax.ShapeDtypeStruct((n, d), F32),
        compiler_params=_cparams(("parallel",), 32),
    )(y, tgt, jnp.broadcast_to(g, (8, LANES)).astype(F32))


@jax.custom_vjp
def _loss_op(y, tgt):
    return _loss_fwd_call(y, tgt)


def _loss_op_fwd(y, tgt):
    return _loss_fwd_call(y, tgt), (y, tgt)


def _loss_op_bwd(res, g):
    y, tgt = res
    return _loss_bwd_call(y, tgt, g), jnp.zeros_like(tgt)


_loss_op.defvjp(_loss_op_fwd, _loss_op_bwd)


def _rowwise(step, name, tile, n_rows, n_params, out_widths, vmem_mb=48):
    return _scan_op(step, name, tile=tile, nb=1, row_kinds="s" * n_rows, param_kinds="s" * n_params,
                    carry_shapes=(), out_widths=out_widths, vmem_mb=vmem_mb)


def _block_diag(x, nblk):
    bsz, _, r, c = x.shape
    eye = jnp.eye(nblk, dtype=x.dtype)
    return jnp.einsum("bgrc,gh->bgrhc", x, eye).reshape(bsz, nblk * r, nblk * c)


def _layer(h, w):
    length = h.shape[0]
    nt = length // CHUNK
    t_row = _pick(length, (208, 128))
    t_s5 = _pick(length, (160, 128))
    row2 = lambda v: v.reshape(1, -1)
    w_in = w["w_in"]
    seg = lambda a, bnd: w_in[:, a:bnd]
    w_small = jnp.concatenate([seg(P_DT, P_SSDZ), seg(P_GA, P_GDNZ),
                               jnp.zeros((D_MODEL, LANES - SMALL_ROWS), F32)], axis=1)
    s5u = _dense("proj_s5u")(h, seg(P_S5U, P_S5Z))
    s5z = _dense("proj_s5z")(h, seg(P_S5Z, P_XBC))
    xbc = _dense("proj_xbc")(h, seg(P_XBC, P_DT))
    small = _dense("proj_small")(h, w_small)
    ssdz = _dense("proj_ssdz")(h, seg(P_SSDZ, P_QKV))
    qkv = _dense("proj_qkv")(h, seg(P_QKV, P_GA))
    gdnz = _dense("proj_gdnz")(h, seg(P_GDNZ, P_GATE))
    gl = _dense("proj_gate")(h, seg(P_GATE, P_END))
    small_t = small[:, :SMALL_ROWS].reshape(nt, CHUNK, SMALL_ROWS).transpose(0, 2, 1)

    rep = lambda v: jnp.repeat(v, S5_GROUP, axis=1)
    prep = _rowwise(_s5_prep_step, "s5_prep", S5_GROUPS, 5, 0, (S5_STATE * S5_GROUP,) * 4)
    abar_re, abar_im, bbar_re, bbar_im = prep(
        (rep(w["s5_a_re"]), rep(w["s5_a_im"]), jnp.broadcast_to(w["s5_log_step"][:, None], (S5_GROUPS, S5_STATE * S5_GROUP)),
         w["s5_b_re"].reshape(S5_GROUPS, -1), w["s5_b_im"].reshape(S5_GROUPS, -1)), ())
    gpb = S5_GROUPS // S5_BLOCKS
    lanes = gpb * S5_STATE
    to_bd = lambda bb: _block_diag(bb.reshape(S5_BLOCKS, gpb, S5_STATE, S5_GROUP).transpose(0, 1, 3, 2), gpb)
    to_cd = lambda cc: _block_diag(cc.reshape(S5_BLOCKS, gpb, S5_GROUP, S5_STATE).transpose(0, 1, 3, 2), gpb)
    to_a = lambda a: a[:, ::S5_GROUP].reshape(S5_BLOCKS, 1, lanes)
    s5_scan = _scan_op(_s5_scan_step, "s5_scan", tile=t_s5, nb=S5_BLOCKS, row_kinds="b", param_kinds="bbbbbb",
                       carry_shapes=((1, lanes), (1, lanes)), out_widths=(LANES,))
    (y_ssm,) = s5_scan((s5u,), (to_bd(bbar_re), to_bd(bbar_im), to_a(abar_re), to_a(abar_im),
                                to_cd(w["s5_c_re"]), to_cd(w["s5_c_im"])))
    s5_post = _rowwise(_s5_post_step, "s5_post", t_row, 3, 3, (WIDTH,))
    (y_a,) = s5_post((y_ssm, s5u, s5z), (row2(w["s5_d"]), w["s5_w_glu"], row2(w["s5_b_glu"])))

    zeros = lambda n: jnp.zeros((n,), F32)
    bias = jnp.concatenate([w["ssd_dt_bias"], w["gdn_dt_bias"]])
    alog = jnp.concatenate([w["ssd_a_log"], w["gdn_a_log"]])
    bias_l = jnp.concatenate([bias, zeros(LANES - 18)]).reshape(1, LANES)
    alog_l = jnp.concatenate([alog, zeros(LANES - 18)]).reshape(1, LANES)
    bias_c = jnp.concatenate([bias, zeros(SMALL_ROWS - 18)]).reshape(SMALL_ROWS, 1)
    alog_c = jnp.concatenate([alog, zeros(SMALL_ROWS - 18)]).reshape(SMALL_ROWS, 1)
    pad_w = lambda cw: jnp.concatenate([cw, jnp.zeros((HALO - CONV_K, cw.shape[1]), F32)], axis=0)

    ssd = _scan_op(_ssd_step, "ssd_scan", tile=CHUNK, nb=1, row_kinds="ssst", param_kinds="s" * 8,
                   carry_shapes=((HALO, xbc.shape[1]), (SSD_HEADS * SSD_HEAD_DIM, SSD_STATE)), out_widths=(WIDTH,))
    (y_b,) = ssd((xbc, ssdz, small, small_t),
                 (pad_w(w["ssd_conv_w"]), row2(w["ssd_conv_b"]), row2(jnp.repeat(w["ssd_d"], SSD_HEAD_DIM)),
                  bias_l, alog_l, bias_c, alog_c, row2(w["ssd_norm_g"])))

    gdn = _scan_op(_gdn_step, "gdn_scan", tile=CHUNK, nb=1, row_kinds="ssst", param_kinds="s" * 6,
                   carry_shapes=((HALO, qkv.shape[1]), (GDN_HEADS * GDN_DIM, GDN_DIM)), out_widths=(WIDTH,))
    (y_c,) = gdn((qkv, gdnz, small, small_t),
                 (pad_w(w["gdn_conv_w"]), bias_l, alog_l, bias_c, alog_c, row2(w["gdn_norm_g"])))

    oa = _dense("branch_a")(y_a, w["w_branch"][0])
    ob = _dense("branch_b")(y_b, w["w_branch"][1])
    oc = _dense("branch_c")(y_c, w["w_branch"][2])
    merge = _rowwise(_gate_merge_step, "gate_merge", t_row, 4, 1, (D_MODEL,))
    (merged,) = merge((oa, ob, oc, gl), (w["b_gate"].reshape(1, -1),))
    out = _dense("out_proj")(merged, w["w_out"])
    ln = _rowwise(_ln_res_step, "ln_res", t_row, 2, 2, (D_MODEL,))
    (h_new,) = ln((h, out), (row2(w["ln_g"]), row2(w["ln_b"])))
    return h_new


_LAYER_KEYS = ("w_in", "s5_a_re", "s5_a_im", "s5_log_step", "s5_b_re", "s5_b_im", "s5_c_re", "s5_c_im", "s5_d",
               "s5_w_glu", "s5_b_glu", "ssd_conv_w", "ssd_conv_b", "ssd_dt_bias", "ssd_a_log", "ssd_d", "ssd_norm_g",
               "gdn_conv_w", "gdn_dt_bias", "gdn_a_log", "gdn_norm_g", "w_branch", "b_gate", "w_out", "ln_g", "ln_b")
_WEIGHT_KEYS = ("meta", "ln_in_g", "ln_in_b") + _LAYER_KEYS


def _local_loss(weights, x, target):
    seq = x.shape[0]
    hcat = jnp.concatenate([jnp.zeros((PAD, D_MODEL), F32), weights["meta"], x], axis=0)
    length = hcat.shape[0]
    ln_in = _rowwise(_ln_in_step, "ln_in", _pick(length, (416, 256, 128)), 1, 2, (D_MODEL,))
    (h,) = ln_in((hcat,), (weights["ln_in_g"].reshape(1, -1), weights["ln_in_b"].reshape(1, -1)))

    for layer in range(DEPTH):
        h = _layer(h, {k: weights[k][layer] for k in _LAYER_KEYS})
    return _loss_op(h[length - seq:], target)


_ANY = pl.BlockSpec(memory_space=pl.ANY)
_BLOCK_BYTES = 4 << 20


def _chip_exchange(arrays, all_to_all, name):
    n = len(arrays)

    def body(*refs):
        ins, outs = refs[:n], refs[n:2 * n]
        send_sems, recv_sems, local_sems = refs[2 * n:]
        mx, my, mc = lax.axis_index("x"), lax.axis_index("y"), lax.axis_index("c")
        me = 2 * mx + my
        peers = [(1 - mx, my), (mx, 1 - my), (1 - mx, 1 - my)]
        own_copies, sends = [], []
        for a, (src, dst) in enumerate(zip(ins, outs)):
            own = pltpu.make_async_copy(src.at[me] if all_to_all else src, dst.at[me], local_sems.at[a])
            own.start()
            own_copies.append(own)
            for k, (px, py) in enumerate(peers):
                cp = pltpu.make_async_remote_copy(
                    src_ref=src.at[2 * px + py] if all_to_all else src, dst_ref=dst.at[me],
                    send_sem=send_sems.at[a, k], recv_sem=recv_sems.at[a, k],
                    device_id=(px, py, mc), device_id_type=MESH)
                cp.start()
                sends.append(cp)
        for a, (src, dst) in enumerate(zip(ins, outs)):
            for k, (px, py) in enumerate(peers):
                pltpu.make_async_remote_copy(
                    src_ref=src.at[me] if all_to_all else src, dst_ref=dst.at[2 * px + py],
                    send_sem=send_sems.at[a, k], recv_sem=recv_sems.at[a, k],
                    device_id=(px, py, mc), device_id_type=MESH).wait_recv()
        for cp in sends:
            cp.wait_send()
        for cp in own_copies:
            cp.wait()

    out_shape = [jax.ShapeDtypeStruct(a.shape if all_to_all else (4,) + a.shape, a.dtype) for a in arrays]
    return pl.pallas_call(
        body, name=name, in_specs=[_ANY] * n, out_specs=[_ANY] * n, out_shape=out_shape,
        scratch_shapes=[pltpu.SemaphoreType.DMA((n, 3)), pltpu.SemaphoreType.DMA((n, 3)), pltpu.SemaphoreType.DMA((n,))],
    )(*arrays)


def _core_swap(arrays, name):
    n = len(arrays)

    def body(*refs):
        ins, outs = refs[:n], refs[n:2 * n]
        send_sems, recv_sems = refs[2 * n:]
        sibling = (lax.axis_index("x"), lax.axis_index("y"), 1 - lax.axis_index("c"))
        copies = [pltpu.make_async_remote_copy(src_ref=s, dst_ref=d, send_sem=send_sems.at[a], recv_sem=recv_sems.at[a],
                                               device_id=sibling, device_id_type=MESH)
                  for a, (s, d) in enumerate(zip(ins, outs))]
        for cp in copies:
            cp.start()
        for cp in copies:
            cp.wait()

    return pl.pallas_call(
        body, name=name, in_specs=[_ANY] * n, out_specs=[_ANY] * n,
        out_shape=[jax.ShapeDtypeStruct(a.shape, a.dtype) for a in arrays],
        scratch_shapes=[pltpu.SemaphoreType.DMA((n,)), pltpu.SemaphoreType.DMA((n,))],
    )(*arrays)


def _as_rows(a, lead=0):
    shp = a.shape
    return a.reshape(shp[:lead] + (-1, shp[-1]))


def _sum4_call(x, name):
    _, r, c = x.shape
    tr = _pick(r, [t for t in (512, 256, 128, 64, 32, 16, 8) if 16 * t * c <= _BLOCK_BYTES] + [r])

    def body(x_ref, o_ref):
        o_ref[...] = (x_ref[0] + x_ref[1]) + (x_ref[2] + x_ref[3])

    return pl.pallas_call(
        body, name=name, grid=(r // tr,),
        in_specs=[pl.BlockSpec((4, tr, c), lambda i: (0, i, 0))],
        out_specs=pl.BlockSpec((tr, c), lambda i: (i, 0)),
        out_shape=jax.ShapeDtypeStruct((r, c), F32),
        compiler_params=_cparams(("parallel",), 48),
    )(x)


def _adam_call(w, g0, g1, m, v, name):
    r, c = w.shape
    tr = _pick(r, [t for t in (512, 256, 128, 64, 32, 16, 8) if 4 * t * c <= _BLOCK_BYTES // 4] + [r])
    bc1 = 1.0 - ADAM_B1 ** ADAM_STEP
    bc2 = 1.0 - ADAM_B2 ** ADAM_STEP

    def body(w_ref, g0_ref, g1_ref, m_ref, v_ref, g_out, d_out, m_out, v_out):
        g = g0_ref[...] + g1_ref[...]
        m_new = ADAM_B1 * m_ref[...] + (1.0 - ADAM_B1) * g
        v_new = ADAM_B2 * v_ref[...] + (1.0 - ADAM_B2) * (g * g)
        m_hat = m_new / bc1
        v_hat = v_new / bc2
        g_out[...] = g
        d_out[...] = -ADAM_LR * (m_hat / (jnp.sqrt(v_hat) + ADAM_EPS) + ADAM_WD * w_ref[...])
        m_out[...] = m_new
        v_out[...] = v_new

    spec = pl.BlockSpec((tr, c), lambda i: (i, 0))
    return pl.pallas_call(
        body, name=name, grid=(r // tr,), in_specs=[spec] * 5, out_specs=[spec] * 4,
        out_shape=[jax.ShapeDtypeStruct((r, c), F32)] * 4,
        compiler_params=_cparams(("parallel",), 48),
    )(w, g0, g1, m, v)


_SHARDED = {"meta": (1, False), "w_in": (2, True), "s5_w_glu": (1, True), "ssd_conv_w": (2, False),
            "gdn_conv_w": (2, False), "w_branch": (3, True), "b_gate": (2, False), "w_out": (1, True)}


def _pack(arrs):
    flat = jnp.concatenate([a.reshape(-1) for a in arrs])
    n = flat.shape[0]
    rows = -(-n // (256 * LANES)) * 256
    return jnp.concatenate([flat, jnp.zeros((rows * LANES - n,), F32)]).reshape(rows, LANES)


def _unpack(packed, like):
    flat = packed.reshape(-1)
    out, off = [], 0
    for a in like:
        out.append(flat[off:off + a.size].reshape(a.shape))
        off += a.size
    return out


def kernel(x, meta, ln_in_g, ln_in_b, w_in, s5_a_re, s5_a_im, s5_log_step, s5_b_re, s5_b_im, s5_c_re, s5_c_im, s5_d, s5_w_glu, s5_b_glu, ssd_conv_w, ssd_conv_b, ssd_dt_bias, ssd_a_log, ssd_d, ssd_norm_g, gdn_conv_w, gdn_dt_bias, gdn_a_log, gdn_norm_g, w_branch, b_gate, w_out, ln_g, ln_b, loss_target, m_meta, m_ln_in_g, m_ln_in_b, m_w_in, m_s5_a_re, m_s5_a_im, m_s5_log_step, m_s5_b_re, m_s5_b_im, m_s5_c_re, m_s5_c_im, m_s5_d, m_s5_w_glu, m_s5_b_glu, m_ssd_conv_w, m_ssd_conv_b, m_ssd_dt_bias, m_ssd_a_log, m_ssd_d, m_ssd_norm_g, m_gdn_conv_w, m_gdn_dt_bias, m_gdn_a_log, m_gdn_norm_g, m_w_branch, m_b_gate, m_w_out, m_ln_g, m_ln_b, v_meta, v_ln_in_g, v_ln_in_b, v_w_in, v_s5_a_re, v_s5_a_im, v_s5_log_step, v_s5_b_re, v_s5_b_im, v_s5_c_re, v_s5_c_im, v_s5_d, v_s5_w_glu, v_s5_b_glu, v_ssd_conv_w, v_ssd_conv_b, v_ssd_dt_bias, v_ssd_a_log, v_ssd_d, v_ssd_norm_g, v_gdn_conv_w, v_gdn_dt_bias, v_gdn_a_log, v_gdn_norm_g, v_w_branch, v_b_gate, v_w_out, v_ln_g, v_ln_b):
    args = dict(locals())
    shards = {k: args[k] for k in _WEIGHT_KEYS}
    moms = {k: (args["m_" + k], args["v_" + k]) for k in _WEIGHT_KEYS}

    names = list(_SHARDED)
    sent = [shards[k].astype(BF16) if _SHARDED[k][1] else shards[k] for k in names]
    gathered = _chip_exchange(sent, False, "gather_weights")
    full = dict(shards)
    for k, g in zip(names, gathered):
        ax = _SHARDED[k][0]
        full[k] = jnp.concatenate([g[j] for j in range(4)], axis=ax).astype(F32)

    loss, (grads, grad_x) = jax.value_and_grad(_local_loss, argnums=(0, 1))(full, x[0], loss_target[0])
    loss = lax.psum(loss, ("x", "y", "c"))

    blocks = []
    for k in names:
        ax = _SHARDED[k][0]
        blocks.append(jnp.stack(jnp.split(grads[k], 4, axis=ax), axis=0))
    small_names = [k for k in _WEIGHT_KEYS if k not in _SHARDED]
    packed = _pack([grads[k] for k in small_names])
    arrived = _chip_exchange(blocks, True, "scatter_grads")
    (packed4,) = _chip_exchange([packed], False, "gather_small_grads")
    partial = [_sum4_call(_as_rows(a, 1), "sum_chips_" + k) for k, a in zip(names, arrived)]
    partial.append(_sum4_call(packed4, "sum_chips_small"))
    other = _core_swap(partial, "swap_cores")

    outs = {}
    for k, p, o in zip(names, partial[:-1], other[:-1]):
        shp = shards[k].shape
        res = _adam_call(_as_rows(shards[k]), p, o, _as_rows(moms[k][0]), _as_rows(moms[k][1]), "adamw_" + k)
        outs[k] = [r.reshape(shp) for r in res]
    like = [shards[k] for k in small_names]
    res = _adam_call(_pack(like), partial[-1], other[-1], _pack([moms[k][0] for k in small_names]),
                     _pack([moms[k][1] for k in small_names]), "adamw_small")
    for idx in range(4):
        for k, a in zip(small_names, _unpack(res[idx], like)):
            outs.setdefault(k, [None] * 4)[idx] = a

    result = [loss, grad_x[None]]
    for idx in range(4):
        result += [outs[k][idx] for k in _WEIGHT_KEYS]
    return tuple(result)
```

```python
import functools

import jax
import jax.numpy as jnp
from jax import lax
from jax.experimental import pallas as pl
from jax.experimental.pallas import tpu as pltpu

F32 = jnp.float32
BF16 = jnp.bfloat16
MESH = pl.DeviceIdType.MESH
F32_DOT = lax.Precision.HIGH

D_MODEL = 1024
DEPTH = 4
N_META = 16
CHUNK = 64
PAD = CHUNK - N_META
CONV_K = 4
HALO = 8
WIDTH = 768
S5_GROUPS, S5_GROUP, S5_STATE = 48, 16, 64
S5_BLOCKS = 6
SSD_HEADS, SSD_HEAD_DIM, SSD_GROUPS, SSD_STATE = 12, 64, 2, 128
SSD_PAIRS = 6
GDN_HEADS, GDN_DIM = 6, 128
LANES = 128
SMALL_ROWS = 24
ALPHA = (2 * DEPTH) ** 0.25
LN_EPS = 1e-5
P_S5U, P_S5Z, P_XBC, P_DT, P_SSDZ, P_QKV, P_GA, P_GB, P_GDNZ, P_GATE, P_END = (
    0, 768, 1536, 2816, 2828, 3596, 5900, 5906, 5912, 6680, 9752)
ADAM_LR, ADAM_B1, ADAM_B2, ADAM_EPS, ADAM_WD, ADAM_STEP = 0.001, 0.9, 0.999, 1e-08, 0.01, 10
NEG = -1e30


def _pick(n, cands):
    for c in cands:
        if n % c == 0:
            return c
    raise ValueError(f"no tile for {n} in {cands}")


def _cparams(sem, vmem_mb):
    return pltpu.CompilerParams(dimension_semantics=sem, vmem_limit_bytes=vmem_mb << 20)


_DIMS = {"nn": (((1,), (0,)), ((), ())), "nt": (((1,), (1,)), ((), ())), "tn": (((0,), (0,)), ((), ()))}


def _dot(a, b, mode, hi):
    if hi:
        return lax.dot_general(a, b, _DIMS[mode], precision=F32_DOT, preferred_element_type=F32)
    return lax.dot_general(a.astype(BF16), b.astype(BF16), _DIMS[mode], preferred_element_type=F32)


@functools.partial(jax.custom_vjp, nondiff_argnums=(2, 3))
def _mm(a, b, mode="nn", hi=False):
    return _dot(a, b, mode, hi)


def _mm_fwd(a, b, mode, hi):
    return _dot(a, b, mode, hi), (a, b)


def _mm_bwd(mode, hi, res, g):
    a, b = res
    if mode == "nn":
        return _dot(g, b, "nt", hi), _dot(a, g, "tn", hi)
    if mode == "nt":
        return _dot(g, b, "nn", hi), _dot(g, a, "tn", hi)
    return _dot(b, g, "nt", hi), _dot(a, g, "nn", hi)


_mm.defvjp(_mm_fwd, _mm_bwd)


@functools.partial(jax.custom_vjp, nondiff_argnums=(1,))
def _roll_rows(x, k):
    return pltpu.roll(x, k % x.shape[0], 0)


def _roll_fwd(x, k):
    return _roll_rows(x, k), None


def _roll_bwd(k, _, g):
    return (_roll_rows(g, -k),)


_roll_rows.defvjp(_roll_fwd, _roll_bwd)


def _iota(shape, dim):
    return lax.broadcasted_iota(jnp.int32, shape, dim)


def _valid_rows(row0, n):
    return (row0 + _iota((n, 1), 0)) >= PAD


def _lane(x, idx):
    return jnp.sum(jnp.where(_iota(x.shape, 1) == idx, x, 0.0), axis=1, keepdims=True)


def _row(x, idx):
    return jnp.sum(jnp.where(_iota(x.shape, 0) == idx, x, 0.0), axis=0, keepdims=True)


def _layer_norm(z, g, b):
    mu = jnp.mean(z, axis=-1, keepdims=True)
    zc = z - mu
    var = jnp.mean(zc * zc, axis=-1, keepdims=True)
    return zc * lax.rsqrt(var + LN_EPS) * g + b


def _rms_norm(z, g):
    return z * lax.rsqrt(jnp.mean(z * z, axis=-1, keepdims=True) + LN_EPS) * g


def _causal_conv(halo, x, w, row0):
    t = x.shape[0]
    xc = jnp.concatenate([halo, x], axis=0)
    acc = None
    for j in range(CONV_K):
        term = _roll_rows(xc, CONV_K - 1 - j)[HALO:HALO + t] * _row(w, j)
        acc = term if acc is None else acc + term
    return acc, x[t - HALO:t]


def _tri(n, strict=False):
    r, c = _iota((n, n), 0), _iota((n, n), 1)
    return (r > c) if strict else (r >= c)


def _scan_op(step, name, *, tile, nb, row_kinds, param_kinds, carry_shapes, out_widths, vmem_mb=48):
    n_rows, n_par, n_car, n_out = len(row_kinds), len(param_kinds), len(carry_shapes), len(out_widths)

    def dims(rows):
        for k, a in zip(row_kinds, rows):
            if k in "bs":
                return a.shape[0], a.shape[0] // tile
        raise ValueError("need a row input")

    def row_spec(kind, a, rev, nt):
        ti = (lambda i: nt - 1 - i) if rev else (lambda i: i)
        if kind == "b":
            return pl.BlockSpec((tile, a.shape[1] // nb), lambda b, i: (ti(i), b))
        if kind == "s":
            return pl.BlockSpec((tile, a.shape[1]), lambda b, i: (ti(i), 0))
        return pl.BlockSpec((None, a.shape[1], a.shape[2]), lambda b, i: (ti(i), 0, 0))

    def par_spec(kind, a):
        if kind == "b":
            return pl.BlockSpec((None, a.shape[1], a.shape[2]), lambda b, i: (b, 0, 0))
        return pl.BlockSpec(a.shape, lambda b, i: (0, 0))

    def fwd_call(rows, params):
        length, nt = dims(rows)

        def body(*refs):
            r_in = refs[:n_rows]
            p_in = refs[n_rows:n_rows + n_par]
            o_out = refs[n_rows + n_par:n_rows + n_par + n_out]
            s_out = refs[n_rows + n_par + n_out:n_rows + n_par + n_out + n_car]
            c_scr = refs[n_rows + n_par + n_out + n_car:]
            b, i = pl.program_id(0), pl.program_id(1)

            if n_car:
                @pl.when(i == 0)
                def _():
                    for c in c_scr:
                        c[...] = jnp.zeros_like(c)

            cin = tuple(c[...] for c in c_scr)
            for s, c in zip(s_out, cin):
                s[...] = c
            new_c, outs = step(cin, tuple(r[...] for r in r_in), tuple(p[...] for p in p_in), b, i * tile)
            for c, v in zip(c_scr, new_c):
                c[...] = v
            for o, v in zip(o_out, outs):
                o[...] = v

        out_shape = [jax.ShapeDtypeStruct((length, nb * w), F32) for w in out_widths]
        out_shape += [jax.ShapeDtypeStruct((nb, nt) + tuple(s), F32) for s in carry_shapes]
        out_specs = [pl.BlockSpec((tile, w), lambda b, i: (i, b)) for w in out_widths]
        out_specs += [pl.BlockSpec((None, None) + tuple(s), lambda b, i: (b, i, 0, 0)) for s in carry_shapes]
        res = pl.pallas_call(
            body, name=name + "_fwd", grid=(nb, nt),
            in_specs=[row_spec(k, a, False, nt) for k, a in zip(row_kinds, rows)]
            + [par_spec(k, a) for k, a in zip(param_kinds, params)],
            out_specs=out_specs, out_shape=out_shape,
            scratch_shapes=[pltpu.VMEM(tuple(s), F32) for s in carry_shapes],
            compiler_params=_cparams(("arbitrary", "arbitrary"), vmem_mb),
        )(*rows, *params)
        return tuple(res[:n_out]), tuple(res[n_out:])

    def bwd_call(rows, params, saved, douts):
        length, nt = dims(rows)

        def body(*refs):
            k0 = 0
            r_in = refs[k0:k0 + n_rows]; k0 += n_rows
            p_in = refs[k0:k0 + n_par]; k0 += n_par
            s_in = refs[k0:k0 + n_car]; k0 += n_car
            g_in = refs[k0:k0 + n_out]; k0 += n_out
            dr_out = refs[k0:k0 + n_rows]; k0 += n_rows
            dp_out = refs[k0:k0 + n_par]; k0 += n_par
            dc_scr = refs[k0:]
            b, i = pl.program_id(0), pl.program_id(1)
            row0 = (nt - 1 - i) * tile

            @pl.when(i == 0)
            def _():
                for c in dc_scr:
                    c[...] = jnp.zeros_like(c)
                for p in dp_out:
                    p[...] = jnp.zeros_like(p)

            def f(c, r, p):
                return step(c, r, p, b, row0)

            _, vjp = jax.vjp(f, tuple(s[...] for s in s_in), tuple(r[...] for r in r_in),
                             tuple(p[...] for p in p_in))
            dc, dr, dp = vjp((tuple(c[...] for c in dc_scr), tuple(g[...] for g in g_in)))
            for c, v in zip(dc_scr, dc):
                c[...] = v
            for o, v in zip(dr_out, dr):
                o[...] = v
            for o, v in zip(dp_out, dp):
                o[...] += v

        rev = lambda i: nt - 1 - i
        in_specs = [row_spec(k, a, True, nt) for k, a in zip(row_kinds, rows)]
        in_specs += [par_spec(k, a) for k, a in zip(param_kinds, params)]
        in_specs += [pl.BlockSpec((None, None) + tuple(s), lambda b, i: (b, rev(i), 0, 0)) for s in carry_shapes]
        in_specs += [pl.BlockSpec((tile, w), lambda b, i: (rev(i), b)) for w in out_widths]
        out_shape, out_specs = [], []
        for k, a in zip(row_kinds, rows):
            if k == "b":
                out_shape.append(jax.ShapeDtypeStruct(a.shape, F32))
                out_specs.append(pl.BlockSpec((tile, a.shape[1] // nb), lambda b, i: (rev(i), b)))
            elif k == "s":
                out_shape.append(jax.ShapeDtypeStruct((nb,) + a.shape, F32))
                out_specs.append(pl.BlockSpec((None, tile, a.shape[1]), lambda b, i: (b, rev(i), 0)))
            else:
                out_shape.append(jax.ShapeDtypeStruct((nb,) + a.shape, F32))
                out_specs.append(pl.BlockSpec((None, None, a.shape[1], a.shape[2]), lambda b, i: (b, rev(i), 0, 0)))
        for k, a in zip(param_kinds, params):
            shp = a.shape[1:] if k == "b" else a.shape
            out_shape.append(jax.ShapeDtypeStruct((nb,) + tuple(shp), F32))
            out_specs.append(pl.BlockSpec((None,) + tuple(shp), lambda b, i: (b, 0, 0)))
        res = pl.pallas_call(
            body, name=name + "_bwd", grid=(nb, nt), in_specs=in_specs, out_specs=out_specs, out_shape=out_shape,
            scratch_shapes=[pltpu.VMEM(tuple(s), F32) for s in carry_shapes],
            compiler_params=_cparams(("arbitrary", "arbitrary"), vmem_mb),
        )(*rows, *params, *saved, *douts)
        fold = (lambda a: a[0]) if nb == 1 else (lambda a: jnp.sum(a, axis=0))
        drows = tuple(r if k == "b" else fold(r) for k, r in zip(row_kinds, res[:n_rows]))
        dpars = tuple(p if k == "b" else fold(p) for k, p in zip(param_kinds, res[n_rows:]))
        return drows, dpars

    @jax.custom_vjp
    def op(rows, params):
        return fwd_call(rows, params)[0]

    def op_fwd(rows, params):
        outs, saved = fwd_call(rows, params)
        return outs, (rows, params, saved)

    def op_bwd(res, douts):
        rows, params, saved = res
        return bwd_call(rows, params, saved, tuple(douts))

    op.defvjp(op_fwd, op_bwd)
    return op


_TM = (832, 512, 256, 128)
_TN = (768, 640, 512, 384, 256, 128)


def _mm_fwd_call(x, w, name):
    m, k = x.shape
    n = w.shape[1]
    tm, tn = _pick(m, _TM), _pick(n, _TN)

    def body(x_ref, w_ref, o_ref):
        o_ref[...] = _dot(x_ref[...], w_ref[...], "nn", False)

    return pl.pallas_call(
        body, name=name, grid=(n // tn, m // tm),
        in_specs=[pl.BlockSpec((tm, k), lambda j, i: (i, 0)), pl.BlockSpec((k, tn), lambda j, i: (0, j))],
        out_specs=pl.BlockSpec((tm, tn), lambda j, i: (i, j)),
        out_shape=jax.ShapeDtypeStruct((m, n), F32),
        compiler_params=_cparams(("parallel", "parallel"), 48),
    )(x, w)


def _mm_dx_call(g, w, name):
    m, n = g.shape
    k = w.shape[0]
    tm, tn = _pick(m, _TM), _pick(n, _TN)

    def body(g_ref, w_ref, o_ref):
        @pl.when(pl.program_id(1) == 0)
        def _():
            o_ref[...] = jnp.zeros_like(o_ref)

        o_ref[...] += _dot(g_ref[...], w_ref[...], "nt", False)

    return pl.pallas_call(
        body, name=name, grid=(m // tm, n // tn),
        in_specs=[pl.BlockSpec((tm, tn), lambda i, j: (i, j)), pl.BlockSpec((k, tn), lambda i, j: (0, j))],
        out_specs=pl.BlockSpec((tm, k), lambda i, j: (i, 0)),
        out_shape=jax.ShapeDtypeStruct((m, k), F32),
        compiler_params=_cparams(("parallel", "arbitrary"), 48),
    )(g, w)


def _mm_dw_call(x, g, name):
    m, k = x.shape
    n = g.shape[1]
    tm, tn = _pick(m, _TM), _pick(n, _TN)

    def body(x_ref, g_ref, o_ref):
        @pl.when(pl.program_id(1) == 0)
        def _():
            o_ref[...] = jnp.zeros_like(o_ref)

        o_ref[...] += _dot(x_ref[...], g_ref[...], "tn", False)

    return pl.pallas_call(
        body, name=name, grid=(n // tn, m // tm),
        in_specs=[pl.BlockSpec((tm, k), lambda j, i: (i, 0)), pl.BlockSpec((tm, tn), lambda j, i: (i, j))],
        out_specs=pl.BlockSpec((k, tn), lambda j, i: (0, j)),
        out_shape=jax.ShapeDtypeStruct((k, n), F32),
        compiler_params=_cparams(("parallel", "arbitrary"), 48),
    )(x, g)


def _dense(name):
    @jax.custom_vjp
    def op(x, w):
        return _mm_fwd_call(x, w, name + "_fwd")

    def op_fwd(x, w):
        return _mm_fwd_call(x, w, name + "_fwd"), (x, w)

    def op_bwd(res, g):
        x, w = res
        return _mm_dx_call(g, w, name + "_dx"), _mm_dw_call(x, g, name + "_dw")

    op.defvjp(op_fwd, op_bwd)
    return op


def _ln_in_step(c, rows, params, b, row0):
    (z,), (g, bb) = rows, params
    return (), (jnp.where(_valid_rows(row0, z.shape[0]), _layer_norm(z, g, bb), 0.0),)


def _ln_res_step(c, rows, params, b, row0):
    (h, o), (g, bb) = rows, params
    return (), (jnp.where(_valid_rows(row0, h.shape[0]), _layer_norm(ALPHA * h + o, g, bb), 0.0),)


def _s5_prep_step(c, rows, params, b, row0):
    a_re, a_im, log_step, b_re, b_im = rows
    lam_re = jnp.minimum(a_re, -1e-4)
    lam_im = a_im
    step = jnp.exp(log_step)
    mag = jnp.exp(lam_re * step)
    abar_re, abar_im = mag * jnp.cos(lam_im * step), mag * jnp.sin(lam_im * step)
    den = lam_re * lam_re + lam_im * lam_im
    nr, ni = abar_re - 1.0, abar_im
    coef_re = (nr * lam_re + ni * lam_im) / den
    coef_im = (ni * lam_re - nr * lam_im) / den
    return (), (abar_re, abar_im, coef_re * b_re - coef_im * b_im, coef_re * b_im + coef_im * b_re)


def _s5_scan_step(c, rows, params, b, row0):
    (c_re, c_im), (u,) = c, rows
    bd_re, bd_im, a_re, a_im, cd_re, cd_im = params
    t = u.shape[0]
    ridx = _iota((t, 1), 0)
    first = ridx == 0
    s_re = _mm(u, bd_re) + jnp.where(first, a_re * c_re - a_im * c_im, 0.0)
    s_im = _mm(u, bd_im) + jnp.where(first, a_re * c_im + a_im * c_re, 0.0)
    p_re, p_im = a_re, a_im
    d = 1
    while d < t:
        keep = ridx >= d
        sh_re = jnp.where(keep, _roll_rows(s_re, d), 0.0)
        sh_im = jnp.where(keep, _roll_rows(s_im, d), 0.0)
        s_re, s_im = s_re + p_re * sh_re - p_im * sh_im, s_im + p_re * sh_im + p_im * sh_re
        p_re, p_im = p_re * p_re - p_im * p_im, 2.0 * p_re * p_im
        d *= 2
    y = _mm(s_re, cd_re) - _mm(s_im, cd_im)
    return (_row(s_re, t - 1), _row(s_im, t - 1)), (y,)


def _s5_post_step(c, rows, params, b, row0):
    (y, u, z), (d, w_glu, b_glu) = rows, params
    v = jax.nn.gelu(y + d * u)
    v = v * jax.nn.sigmoid(_mm(v, w_glu) + b_glu)
    return (), (v * jax.nn.silu(z),)


def _ssd_step(c, rows, params, b, row0):
    halo, state = c
    xbc_raw, z, small, small_t = rows
    cw, cb, d_l, bias_l, alog_l, bias_c, alog_c, norm_g = params
    t = xbc_raw.shape[0]
    grp = SSD_GROUPS * SSD_STATE
    valid = _valid_rows(row0, t)
    conv, halo2 = _causal_conv(halo, xbc_raw, cw, row0)
    act = jnp.where(valid, jax.nn.silu(conv + cb), 0.0)
    low = _iota((1, LANES), 1) < SSD_HEAD_DIM
    dt_all = jnp.where(valid, jax.nn.softplus(small + bias_l), 0.0)
    a_all = -jnp.exp(alog_l)
    valid_t = (row0 + _iota((1, t), 1)) >= PAD
    dta_t = jnp.where(valid_t, jax.nn.softplus(small_t + bias_c), 0.0) * (-jnp.exp(alog_c))
    acum_t = _mm(dta_t, jnp.where(_iota((t, t), 0) <= _iota((t, t), 1), 1.0, 0.0), "nn", True)
    causal = _tri(t)
    acum_all = _mm(jnp.where(causal, 1.0, 0.0), dt_all * a_all, "nn", True)
    last_all = _row(acum_all, t - 1)
    low_rows = _iota((LANES, 1), 0) < SSD_HEAD_DIM
    pairs, groups = range(SSD_PAIRS), range(SSD_GROUPS)
    grp_of = [p // (SSD_PAIRS // SSD_GROUPS) for p in pairs]
    bs = [act[:, WIDTH + g * SSD_STATE:WIDTH + (g + 1) * SSD_STATE] for g in groups]
    cs = [act[:, WIDTH + grp + g * SSD_STATE:WIDTH + grp + (g + 1) * SSD_STATE] for g in groups]
    scores = [_mm(cs[g], bs[g], "nt") for g in groups]
    per_lane = lambda v, p: jnp.where(low, v[:, 2 * p:2 * p + 1], v[:, 2 * p + 1:2 * p + 2])
    dt_l = [per_lane(dt_all, p) for p in pairs]
    acum_l = [per_lane(acum_all, p) for p in pairs]
    last_l = [per_lane(last_all, p) for p in pairs]
    st = [state[p * LANES:(p + 1) * LANES] for p in pairs]
    xd = [act[:, p * LANES:(p + 1) * LANES] * dt_l[p] for p in pairs]
    decay = [jnp.exp(jnp.where(causal, acum_all[:, h:h + 1] - _row(acum_t, h), NEG)) for h in range(SSD_HEADS)]
    y_lo = [_mm(scores[grp_of[p]] * decay[2 * p], jnp.where(low, xd[p], 0.0)) for p in pairs]
    y_hi = [_mm(scores[grp_of[p]] * decay[2 * p + 1], jnp.where(low, 0.0, xd[p])) for p in pairs]
    y_off = [_mm(cs[grp_of[p]], st[p], "nt") * jnp.exp(acum_l[p]) for p in pairs]
    new_st = [_mm(xd[p] * jnp.exp(last_l[p] - acum_l[p]), bs[grp_of[p]], "tn") for p in pairs]
    cd = jnp.exp(last_all)
    new_st = [st[p] * jnp.where(low_rows, cd[:, 2 * p:2 * p + 1], cd[:, 2 * p + 1:2 * p + 2]) + new_st[p] for p in pairs]
    y = jnp.concatenate([y_lo[p] + y_hi[p] + y_off[p] for p in pairs], axis=1) + act[:, :WIDTH] * d_l
    out = _rms_norm(y * jax.nn.silu(z), norm_g)
    return (halo2, jnp.concatenate(new_st, axis=0)), (out,)


def _unit_lower_inverse(mats):
    n = mats[0].shape[0]
    eye = jnp.where(_iota((n, n), 0) == _iota((n, n), 1), 1.0, 0.0)
    inv = [eye - a for a in mats]
    p = [_mm(a, a, "nn", True) for a in mats]
    k = 2
    while k < n:
        inv = [i + _mm(i, q, "nn", True) for i, q in zip(inv, p)]
        k *= 2
        if k < n:
            p = [_mm(q, q, "nn", True) for q in p]
    return inv


def _gdn_step(c, rows, params, b, row0):
    halo, state = c
    qkv_raw, z, small, small_t = rows
    cw, bias_l, alog_l, bias_c, alog_c, norm_g = params
    t = qkv_raw.shape[0]
    valid = _valid_rows(row0, t)
    conv, halo2 = _causal_conv(halo, qkv_raw, cw, row0)
    act = jnp.where(valid, jax.nn.silu(conv), 0.0)
    beta_all = jnp.where(valid, jax.nn.sigmoid(small), 0.0)
    g_all = jnp.where(valid, -jnp.exp(alog_l) * jax.nn.softplus(small + bias_l), 0.0)
    valid_t = (row0 + _iota((1, t), 1)) >= PAD
    g_t = jnp.where(valid_t, -jnp.exp(alog_c) * jax.nn.softplus(small_t + bias_c), 0.0)
    causal, strict = _tri(t), _tri(t, True)
    gcum_all = _mm(jnp.where(causal, 1.0, 0.0), g_all, "nn", True)
    gcum_t = _mm(g_t, jnp.where(_iota((t, t), 0) <= _iota((t, t), 1), 1.0, 0.0), "nn", True)
    heads = range(GDN_HEADS)
    part = lambda h, n: act[:, n * WIDTH + h * GDN_DIM:n * WIDTH + (h + 1) * GDN_DIM]
    unit = lambda x: x * lax.rsqrt(jnp.sum(x * x, axis=-1, keepdims=True) + 1e-6)
    q = [unit(part(h, 0)) * (GDN_DIM ** -0.5) for h in heads]
    k = [unit(part(h, 1)) for h in heads]
    st = [state[h * GDN_DIM:(h + 1) * GDN_DIM] for h in heads]
    ia = [SSD_HEADS + h for h in heads]
    beta = [beta_all[:, ia[h] + GDN_HEADS:ia[h] + GDN_HEADS + 1] for h in heads]
    gcum = [gcum_all[:, ia[h]:ia[h] + 1] for h in heads]
    gamma = [jnp.exp(jnp.where(causal, gcum[h] - _row(gcum_t, ia[h]), NEG)) for h in heads]
    egc = [jnp.exp(gcum[h]) for h in heads]
    a_mat = [jnp.where(strict, _mm(k[h], k[h], "nt") * gamma[h] * beta[h], 0.0) for h in heads]
    inv = _unit_lower_inverse(a_mat)
    rhs = [jnp.concatenate([part(h, 2) * beta[h], k[h] * (beta[h] * egc[h])], axis=1) for h in heads]
    sol = [_mm(inv[h], rhs[h], "nn", True) for h in heads]
    attn = [_mm(q[h], k[h], "nt") * gamma[h] for h in heads]
    from_state = [_mm(jnp.concatenate([sol[h][:, GDN_DIM:], q[h] * egc[h]], axis=0), st[h]) for h in heads]
    v_new = [sol[h][:, :GDN_DIM] - from_state[h][:t] for h in heads]
    o = [from_state[h][t:] + _mm(attn[h], v_new[h]) for h in heads]
    glast = [_row(gcum[h], t - 1) for h in heads]
    new_st = [st[h] * jnp.exp(glast[h]) + _mm(k[h] * jnp.exp(glast[h] - gcum[h]), v_new[h], "tn") for h in heads]
    out = jnp.concatenate([_rms_norm(o[h], norm_g) for h in heads], axis=1) * jax.nn.silu(z)
    return (halo2, jnp.concatenate(new_st, axis=0)), (out,)


def _gate_merge_step(c, rows, params, b, row0):
    (oa, ob, oc, gl), (bg,) = rows, params
    acc = None
    for n, o in enumerate((oa, ob, oc)):
        term = jax.nn.sigmoid(gl[:, n * D_MODEL:(n + 1) * D_MODEL] + bg[:, n * D_MODEL:(n + 1) * D_MODEL]) * o
        acc = term if acc is None else acc + term
    return (), (acc,)


def _loss_tile(n):
    return _pick(n, (512, 256, 128, 64))


def _loss_fwd_call(y, tgt):
    n, d = y.shape
    tile = _loss_tile(n)

    def body(y_ref, t_ref, o_ref):
        @pl.when(pl.program_id(0) == 0)
        def _():
            o_ref[...] = jnp.zeros_like(o_ref)

        e = y_ref[...] - t_ref[...]
        o_ref[...] += jnp.sum(jnp.sum(e * e, axis=1, keepdims=True), axis=0, keepdims=True) * (0.5 / d)

    out = pl.pallas_call(
        body, name="loss_fwd", grid=(n // tile,),
        in_specs=[pl.BlockSpec((tile, d), lambda i: (i, 0)), pl.BlockSpec((tile, d), lambda i: (i, 0))],
        out_specs=pl.BlockSpec((8, LANES), lambda i: (0, 0)),
        out_shape=jax.ShapeDtypeStruct((8, LANES), F32),
        compiler_params=_cparams(("arbitrary",), 32),
    )(y, tgt)
    return out[0, 0]


def _loss_bwd_call(y, tgt, g):
    n, d = y.shape
    tile = _loss_tile(n)

    def body(y_ref, t_ref, g_ref, o_ref):
        o_ref[...] = (y_ref[...] - t_ref[...]) * (g_ref[...][0:1, 0:1] * (1.0 / d))

    return pl.pallas_call(
        body, name="loss_bwd", grid=(n // tile,),
        in_specs=[pl.BlockSpec((tile, d), lambda i: (i, 0)), pl.BlockSpec((tile, d), lambda i: (i, 0)),
                  pl.BlockSpec((8, LANES), lambda i: (0, 0))],
        out_specs=pl.BlockSpec((tile, d), lambda i: (i, 0)),
        out_shape=jax.ShapeDtypeStruct((n, d), F32),
        compiler_params=_cparams(("parallel",), 32),
    )(y, tgt, jnp.broadcast_to(g, (8, LANES)).astype(F32))


@jax.custom_vjp
def _loss_op(y, tgt):
    return _loss_fwd_call(y, tgt)


def _loss_op_fwd(y, tgt):
    return _loss_fwd_call(y, tgt), (y, tgt)


def _loss_op_bwd(res, g):
    y, tgt = res
    return _loss_bwd_call(y, tgt, g), jnp.zeros_like(tgt)


_loss_op.defvjp(_loss_op_fwd, _loss_op_bwd)


def _rowwise(step, name, tile, n_rows, n_params, out_widths, vmem_mb=48):
    return _scan_op(step, name, tile=tile, nb=1, row_kinds="s" * n_rows, param_kinds="s" * n_params,
                    carry_shapes=(), out_widths=out_widths, vmem_mb=vmem_mb)


def _block_diag(x, nblk):
    bsz, _, r, c = x.shape
    eye = jnp.eye(nblk, dtype=x.dtype)
    return jnp.einsum("bgrc,gh->bgrhc", x, eye).reshape(bsz, nblk * r, nblk * c)


_PROJ = {"s5u": (P_S5U, P_S5Z), "s5z": (P_S5Z, P_XBC), "xbc": (P_XBC, P_DT), "ssdz": (P_SSDZ, P_QKV),
         "qkv": (P_QKV, P_GA), "gdnz": (P_GDNZ, P_GATE), "gate": (P_GATE, P_END)}


def _prepare(weights):
    w = weights
    w_in = w["w_in"]
    seg = lambda a, bnd: w_in[:, :, a:bnd]
    zeros = lambda *shape: jnp.zeros((DEPTH,) + shape, F32)
    row3 = lambda v: v.reshape(DEPTH, 1, -1)
    p = {"w_" + k: seg(a, bnd) for k, (a, bnd) in _PROJ.items()}
    p["w_small"] = jnp.concatenate([seg(P_DT, P_SSDZ), seg(P_GA, P_GDNZ), zeros(D_MODEL, LANES - SMALL_ROWS)], axis=2)

    rows, wide = DEPTH * S5_GROUPS, S5_STATE * S5_GROUP
    flat = lambda v: v.reshape(rows, -1)
    rep = lambda v: jnp.repeat(flat(v), S5_GROUP, axis=1)
    prep = _rowwise(_s5_prep_step, "s5_prep", S5_GROUPS, 5, 0, (wide,) * 4)
    abar_re, abar_im, bbar_re, bbar_im = prep(
        (rep(w["s5_a_re"]), rep(w["s5_a_im"]), jnp.broadcast_to(flat(w["s5_log_step"]), (rows, wide)),
         flat(w["s5_b_re"]), flat(w["s5_b_im"])), ())
    gpb = S5_GROUPS // S5_BLOCKS
    lanes, chans, nblk = gpb * S5_STATE, gpb * S5_GROUP, DEPTH * S5_BLOCKS
    to_bd = lambda bb: _block_diag(bb.reshape(nblk, gpb, S5_STATE, S5_GROUP).transpose(0, 1, 3, 2), gpb).reshape(
        DEPTH, S5_BLOCKS, chans, lanes)
    to_cd = lambda cc: _block_diag(cc.reshape(nblk, gpb, S5_GROUP, S5_STATE).transpose(0, 1, 3, 2), gpb).reshape(
        DEPTH, S5_BLOCKS, lanes, chans)
    to_a = lambda a: a[:, ::S5_GROUP].reshape(DEPTH, S5_BLOCKS, 1, lanes)
    p.update(s5_bd_re=to_bd(bbar_re), s5_bd_im=to_bd(bbar_im), s5_a_re=to_a(abar_re), s5_a_im=to_a(abar_im),
             s5_cd_re=to_cd(w["s5_c_re"]), s5_cd_im=to_cd(w["s5_c_im"]),
             s5_d=row3(w["s5_d"]), s5_w_glu=w["s5_w_glu"], s5_b_glu=row3(w["s5_b_glu"]))

    bias = jnp.concatenate([w["ssd_dt_bias"], w["gdn_dt_bias"]], axis=1)
    alog = jnp.concatenate([w["ssd_a_log"], w["gdn_a_log"]], axis=1)
    on_lanes = lambda v: jnp.concatenate([v, zeros(LANES - v.shape[1])], axis=1).reshape(DEPTH, 1, LANES)
    on_rows = lambda v: jnp.concatenate([v, zeros(SMALL_ROWS - v.shape[1])], axis=1).reshape(DEPTH, SMALL_ROWS, 1)
    pad_w = lambda cw: jnp.concatenate([cw, zeros(HALO - CONV_K, cw.shape[2])], axis=1)
    p.update(bias_l=on_lanes(bias), alog_l=on_lanes(alog), bias_c=on_rows(bias), alog_c=on_rows(alog),
             ssd_cw=pad_w(w["ssd_conv_w"]), ssd_cb=row3(w["ssd_conv_b"]),
             ssd_d=row3(jnp.repeat(w["ssd_d"], SSD_HEAD_DIM, axis=1)), ssd_norm_g=row3(w["ssd_norm_g"]),
             gdn_cw=pad_w(w["gdn_conv_w"]), gdn_norm_g=row3(w["gdn_norm_g"]),
             w_branch=w["w_branch"], b_gate=row3(w["b_gate"]), w_out=w["w_out"], ln_g=row3(w["ln_g"]), ln_b=row3(w["ln_b"]))
    return p


def _layer(h, p):
    length = h.shape[0]
    nt = length // CHUNK
    t_row = _pick(length, (208, 128))
    t_s5 = _pick(length, (160, 128))
    proj = {k: _dense("proj_" + k)(h, p["w_" + k]) for k in list(_PROJ) + ["small"]}
    small = proj["small"]
    small_t = small[:, :SMALL_ROWS].reshape(nt, CHUNK, SMALL_ROWS).transpose(0, 2, 1)

    lanes = p["s5_a_re"].shape[-1]
    s5_scan = _scan_op(_s5_scan_step, "s5_scan", tile=t_s5, nb=S5_BLOCKS, row_kinds="b", param_kinds="bbbbbb",
                       carry_shapes=((1, lanes), (1, lanes)), out_widths=(LANES,))
    (y_ssm,) = s5_scan((proj["s5u"],), (p["s5_bd_re"], p["s5_bd_im"], p["s5_a_re"], p["s5_a_im"],
                                        p["s5_cd_re"], p["s5_cd_im"]))
    s5_post = _rowwise(_s5_post_step, "s5_post", t_row, 3, 3, (WIDTH,))
    (y_a,) = s5_post((y_ssm, proj["s5u"], proj["s5z"]), (p["s5_d"], p["s5_w_glu"], p["s5_b_glu"]))

    scalars = (p["bias_l"], p["alog_l"], p["bias_c"], p["alog_c"])
    ssd = _scan_op(_ssd_step, "ssd_scan", tile=CHUNK, nb=1, row_kinds="ssst", param_kinds="s" * 8,
                   carry_shapes=((HALO, P_DT - P_XBC), (SSD_HEADS * SSD_HEAD_DIM, SSD_STATE)), out_widths=(WIDTH,))
    (y_b,) = ssd((proj["xbc"], proj["ssdz"], small, small_t),
                 (p["ssd_cw"], p["ssd_cb"], p["ssd_d"]) + scalars + (p["ssd_norm_g"],))
    gdn = _scan_op(_gdn_step, "gdn_scan", tile=CHUNK, nb=1, row_kinds="ssst", param_kinds="s" * 6,
                   carry_shapes=((HALO, P_GA - P_QKV), (GDN_HEADS * GDN_DIM, GDN_DIM)), out_widths=(WIDTH,))
    (y_c,) = gdn((proj["qkv"], proj["gdnz"], small, small_t), (p["gdn_cw"],) + scalars + (p["gdn_norm_g"],))

    branch = [_dense("branch_" + n)(y, p["w_branch"][i]) for i, (n, y) in enumerate(zip("abc", (y_a, y_b, y_c)))]
    merge = _rowwise(_gate_merge_step, "gate_merge", t_row, 4, 1, (D_MODEL,))
    (merged,) = merge((*branch, proj["gate"]), (p["b_gate"],))
    out = _dense("out_proj")(merged, p["w_out"])
    ln = _rowwise(_ln_res_step, "ln_res", t_row, 2, 2, (D_MODEL,))
    (h_new,) = ln((h, out), (p["ln_g"], p["ln_b"]))
    return h_new


_LAYER_KEYS = ("w_in", "s5_a_re", "s5_a_im", "s5_log_step", "s5_b_re", "s5_b_im", "s5_c_re", "s5_c_im", "s5_d",
               "s5_w_glu", "s5_b_glu", "ssd_conv_w", "ssd_conv_b", "ssd_dt_bias", "ssd_a_log", "ssd_d", "ssd_norm_g",
               "gdn_conv_w", "gdn_dt_bias", "gdn_a_log", "gdn_norm_g", "w_branch", "b_gate", "w_out", "ln_g", "ln_b")
_WEIGHT_KEYS = ("meta", "ln_in_g", "ln_in_b") + _LAYER_KEYS


def _local_loss(weights, x, target):
    seq = x.shape[0]
    hcat = jnp.concatenate([jnp.zeros((PAD, D_MODEL), F32), weights["meta"], x], axis=0)
    length = hcat.shape[0]
    ln_in = _rowwise(_ln_in_step, "ln_in", _pick(length, (416, 256, 128)), 1, 2, (D_MODEL,))
    (h,) = ln_in((hcat,), (weights["ln_in_g"].reshape(1, -1), weights["ln_in_b"].reshape(1, -1)))

    prepared = _prepare(weights)
    for layer in range(DEPTH):
        h = _layer(h, {k: v[layer] for k, v in prepared.items()})
    return _loss_op(h[length - seq:], target)


_ANY = pl.BlockSpec(memory_space=pl.ANY)
_BLOCK_BYTES = 4 << 20


def _chip_exchange(arrays, all_to_all, name):
    n = len(arrays)

    def body(*refs):
        ins, outs = refs[:n], refs[n:2 * n]
        send_sems, recv_sems, local_sems = refs[2 * n:]
        mx, my, mc = lax.axis_index("x"), lax.axis_index("y"), lax.axis_index("c")
        me = 2 * mx + my
        peers = [(1 - mx, my), (mx, 1 - my), (1 - mx, 1 - my)]
        own_copies, sends = [], []
        for a, (src, dst) in enumerate(zip(ins, outs)):
            own = pltpu.make_async_copy(src.at[me] if all_to_all else src, dst.at[me], local_sems.at[a])
            own.start()
            own_copies.append(own)
            for k, (px, py) in enumerate(peers):
                cp = pltpu.make_async_remote_copy(
                    src_ref=src.at[2 * px + py] if all_to_all else src, dst_ref=dst.at[me],
                    send_sem=send_sems.at[a, k], recv_sem=recv_sems.at[a, k],
                    device_id=(px, py, mc), device_id_type=MESH)
                cp.start()
                sends.append(cp)
        for a, (src, dst) in enumerate(zip(ins, outs)):
            for k, (px, py) in enumerate(peers):
                pltpu.make_async_remote_copy(
                    src_ref=src.at[me] if all_to_all else src, dst_ref=dst.at[2 * px + py],
                    send_sem=send_sems.at[a, k], recv_sem=recv_sems.at[a, k],
                    device_id=(px, py, mc), device_id_type=MESH).wait_recv()
        for cp in sends:
            cp.wait_send()
        for cp in own_copies:
            cp.wait()

    out_shape = [jax.ShapeDtypeStruct(a.shape if all_to_all else (4,) + a.shape, a.dtype) for a in arrays]
    return pl.pallas_call(
        body, name=name, in_specs=[_ANY] * n, out_specs=[_ANY] * n, out_shape=out_shape,
        scratch_shapes=[pltpu.SemaphoreType.DMA((n, 3)), pltpu.SemaphoreType.DMA((n, 3)), pltpu.SemaphoreType.DMA((n,))],
    )(*arrays)


def _gather_two_level(arrays, name):
    n = len(arrays)

    def body(*refs):
        ins, outs = refs[:n], refs[n:2 * n]
        send_sems, recv_sems, local_sems = refs[2 * n:]
        mx, my, mc = lax.axis_index("x"), lax.axis_index("y"), lax.axis_index("c")
        me = 2 * mx + my
        sibling = (mx, my, 1 - mc)
        chips = [(1 - mx, my), (mx, 1 - my), (1 - mx, 1 - my)]

        def copy(a, k, src, chip, core, to):
            return pltpu.make_async_remote_copy(
                src_ref=src, dst_ref=outs[a].at[chip, core], send_sem=send_sems.at[a, k], recv_sem=recv_sems.at[a, k],
                device_id=to, device_id_type=MESH)

        own, sends = [], []
        for a in range(n):
            cp = pltpu.make_async_copy(ins[a], outs[a].at[me, mc], local_sems.at[a])
            cp.start()
            own.append(cp)
            sends.append(copy(a, 0, ins[a], me, mc, sibling))
            sends += [copy(a, 1 + j, ins[a], me, mc, (px, py, mc)) for j, (px, py) in enumerate(chips)]
        for cp in sends:
            cp.start()
        passed = []
        for a in range(n):
            for j, (px, py) in enumerate(chips):
                chip = 2 * px + py
                copy(a, 1 + j, ins[a], chip, mc, sibling).wait_recv()
                cp = copy(a, 4 + j, outs[a].at[chip, mc], chip, mc, sibling)
                cp.start()
                passed.append(cp)
        for a in range(n):
            copy(a, 0, ins[a], me, 1 - mc, sibling).wait_recv()
            for j, (px, py) in enumerate(chips):
                copy(a, 4 + j, ins[a], 2 * px + py, 1 - mc, sibling).wait_recv()
        for cp in sends + passed:
            cp.wait_send()
        for cp in own:
            cp.wait()

    return pl.pallas_call(
        body, name=name, in_specs=[_ANY] * n, out_specs=[_ANY] * n,
        out_shape=[jax.ShapeDtypeStruct((4, 2) + a.shape, a.dtype) for a in arrays],
        scratch_shapes=[pltpu.SemaphoreType.DMA((n, 7)), pltpu.SemaphoreType.DMA((n, 7)), pltpu.SemaphoreType.DMA((n,))],
    )(*arrays)


def _core_exchange(arrays, mode, name):
    n = len(arrays)

    def body(*refs):
        ins, outs = refs[:n], refs[n:2 * n]
        send_sems, recv_sems, local_sems = refs[2 * n:]
        mc = lax.axis_index("c")
        sibling = (lax.axis_index("x"), lax.axis_index("y"), 1 - mc)
        own, copies, landing = [], [], []
        for a, (s, d) in enumerate(zip(ins, outs)):
            if mode == "share":
                cp = pltpu.make_async_copy(s, d.at[mc], local_sems.at[a])
                cp.start()
                own.append(cp)
            src = s.at[1 - mc] if mode == "other_half" else s
            copies.append(pltpu.make_async_remote_copy(
                src_ref=src, dst_ref=d.at[mc] if mode == "share" else d, send_sem=send_sems.at[a],
                recv_sem=recv_sems.at[a], device_id=sibling, device_id_type=MESH))
            landing.append(pltpu.make_async_remote_copy(
                src_ref=src, dst_ref=d.at[1 - mc] if mode == "share" else d, send_sem=send_sems.at[a],
                recv_sem=recv_sems.at[a], device_id=sibling, device_id_type=MESH))
        for cp in copies:
            cp.start()
        for cp in landing:
            cp.wait_recv()
        for cp in copies:
            cp.wait_send()
        for cp in own:
            cp.wait()

    shape = {"swap": lambda a: a.shape, "other_half": lambda a: a.shape[1:], "share": lambda a: (2,) + a.shape}[mode]
    return pl.pallas_call(
        body, name=name, in_specs=[_ANY] * n, out_specs=[_ANY] * n,
        out_shape=[jax.ShapeDtypeStruct(shape(a), a.dtype) for a in arrays],
        scratch_shapes=[pltpu.SemaphoreType.DMA((n,)), pltpu.SemaphoreType.DMA((n,)), pltpu.SemaphoreType.DMA((n,))],
    )(*arrays)


def _as_rows(a, lead=0):
    shp = a.shape
    return a.reshape(shp[:lead] + (-1, shp[-1]))


def _sum4_call(x, name):
    _, r, c = x.shape
    tr = _pick(r, [t for t in (512, 256, 128, 64, 32, 16, 8) if 16 * t * c <= _BLOCK_BYTES] + [r])

    def body(x_ref, o_ref):
        part = [x_ref[j].astype(F32) for j in range(4)]
        o_ref[...] = (part[0] + part[1]) + (part[2] + part[3])

    return pl.pallas_call(
        body, name=name, grid=(r // tr,),
        in_specs=[pl.BlockSpec((4, tr, c), lambda i: (0, i, 0))],
        out_specs=pl.BlockSpec((tr, c), lambda i: (i, 0)),
        out_shape=jax.ShapeDtypeStruct((r, c), F32),
        compiler_params=_cparams(("parallel",), 48),
    )(x)


def _add_to_bf16_call(a, b, name):
    _, r, c = a.shape
    tr = _pick(r, [t for t in (512, 256, 128, 64, 32, 16) if 16 * t * c <= _BLOCK_BYTES] + [r])

    def body(a_ref, b_ref, o_ref):
        o_ref[...] = (a_ref[...] + b_ref[...]).astype(BF16)

    spec = pl.BlockSpec((4, tr, c), lambda i: (0, i, 0))
    return pl.pallas_call(
        body, name=name, grid=(r // tr,), in_specs=[spec, spec], out_specs=spec,
        out_shape=jax.ShapeDtypeStruct(a.shape, BF16), compiler_params=_cparams(("parallel",), 48),
    )(a, b)


def _adam_call(w, grads, m, v, name):
    r, c = w.shape
    n_g = len(grads)
    tr = _pick(r, [t for t in (512, 256, 128, 64, 32, 16, 8) if 4 * t * c <= _BLOCK_BYTES // 4] + [r])
    bc1 = 1.0 - ADAM_B1 ** ADAM_STEP
    bc2 = 1.0 - ADAM_B2 ** ADAM_STEP

    def body(*refs):
        w_ref, g_refs = refs[0], refs[1:1 + n_g]
        m_ref, v_ref, g_out, d_out, m_out, v_out = refs[1 + n_g:]
        g = g_refs[0][...]
        for g_ref in g_refs[1:]:
            g = g + g_ref[...]
        m_new = ADAM_B1 * m_ref[...] + (1.0 - ADAM_B1) * g
        v_new = ADAM_B2 * v_ref[...] + (1.0 - ADAM_B2) * (g * g)
        m_hat = m_new / bc1
        v_hat = v_new / bc2
        g_out[...] = g
        d_out[...] = -ADAM_LR * (m_hat / (jnp.sqrt(v_hat) + ADAM_EPS) + ADAM_WD * w_ref[...])
        m_out[...] = m_new
        v_out[...] = v_new

    spec = pl.BlockSpec((tr, c), lambda i: (i, 0))
    return pl.pallas_call(
        body, name=name, grid=(r // tr,), in_specs=[spec] * (3 + n_g), out_specs=[spec] * 4,
        out_shape=[jax.ShapeDtypeStruct((r, c), F32)] * 4,
        compiler_params=_cparams(("parallel",), 48),
    )(w, *grads, m, v)


_SHARDED = {"meta": (1, False), "w_in": (2, True), "s5_w_glu": (1, True), "ssd_conv_w": (2, False),
            "gdn_conv_w": (2, False), "w_branch": (3, True), "b_gate": (2, False), "w_out": (1, True)}


def _pack(arrs):
    flat = jnp.concatenate([a.reshape(-1) for a in arrs])
    n = flat.shape[0]
    rows = -(-n // (256 * LANES)) * 256
    return jnp.concatenate([flat, jnp.zeros((rows * LANES - n,), F32)]).reshape(rows, LANES)


def _unpack(packed, like):
    flat = packed.reshape(-1)
    out, off = [], 0
    for a in like:
        out.append(flat[off:off + a.size].reshape(a.shape))
        off += a.size
    return out


def kernel(x, meta, ln_in_g, ln_in_b, w_in, s5_a_re, s5_a_im, s5_log_step, s5_b_re, s5_b_im, s5_c_re, s5_c_im, s5_d, s5_w_glu, s5_b_glu, ssd_conv_w, ssd_conv_b, ssd_dt_bias, ssd_a_log, ssd_d, ssd_norm_g, gdn_conv_w, gdn_dt_bias, gdn_a_log, gdn_norm_g, w_branch, b_gate, w_out, ln_g, ln_b, loss_target, m_meta, m_ln_in_g, m_ln_in_b, m_w_in, m_s5_a_re, m_s5_a_im, m_s5_log_step, m_s5_b_re, m_s5_b_im, m_s5_c_re, m_s5_c_im, m_s5_d, m_s5_w_glu, m_s5_b_glu, m_ssd_conv_w, m_ssd_conv_b, m_ssd_dt_bias, m_ssd_a_log, m_ssd_d, m_ssd_norm_g, m_gdn_conv_w, m_gdn_dt_bias, m_gdn_a_log, m_gdn_norm_g, m_w_branch, m_b_gate, m_w_out, m_ln_g, m_ln_b, v_meta, v_ln_in_g, v_ln_in_b, v_w_in, v_s5_a_re, v_s5_a_im, v_s5_log_step, v_s5_b_re, v_s5_b_im, v_s5_c_re, v_s5_c_im, v_s5_d, v_s5_w_glu, v_s5_b_glu, v_ssd_conv_w, v_ssd_conv_b, v_ssd_dt_bias, v_ssd_a_log, v_ssd_d, v_ssd_norm_g, v_gdn_conv_w, v_gdn_dt_bias, v_gdn_a_log, v_gdn_norm_g, v_w_branch, v_b_gate, v_w_out, v_ln_g, v_ln_b):
    args = dict(locals())
    shards = {k: args[k] for k in _WEIGHT_KEYS}
    moms = {k: (args["m_" + k], args["v_" + k]) for k in _WEIGHT_KEYS}

    core = lax.axis_index("c")
    halves = lambda a: a.reshape((2, a.shape[0] // 2) + a.shape[1:])

    names = list(_SHARDED)
    sent = [lax.dynamic_index_in_dim(halves(shards[k]), core, 0, keepdims=False) for k in names]
    sent = [s.astype(BF16) if _SHARDED[k][1] else s for k, s in zip(names, sent)]
    gathered = _gather_two_level(sent, "gather_weights")
    full = dict(shards)
    for k, g in zip(names, gathered):
        shp, ax = shards[k].shape, _SHARDED[k][0]
        full[k] = jnp.concatenate([g[j].reshape(shp) for j in range(4)], axis=ax).astype(F32)

    loss, (grads, grad_x) = jax.value_and_grad(_local_loss, argnums=(0, 1))(full, x[0], loss_target[0])
    loss = lax.psum(loss, ("x", "y", "c"))

    blocks = []
    for k in names:
        per_chip = jnp.stack(jnp.split(grads[k], 4, axis=_SHARDED[k][0]), axis=0)
        blocks.append(jnp.moveaxis(_as_rows(per_chip.reshape((4, 2, -1) + per_chip.shape[2:]), 2), 1, 0))
    theirs = _core_exchange(blocks, "other_half", "swap_halves")
    mine = [lax.dynamic_index_in_dim(b, core, 0, keepdims=False) for b in blocks]
    chip_sums = [_add_to_bf16_call(a, b, "sum_cores_" + k) for k, a, b in zip(names, mine, theirs)]
    arrived = _chip_exchange(chip_sums, True, "scatter_grads")
    owned = [_sum4_call(a, "sum_chips_" + k) for k, a in zip(names, arrived)]
    shared = _core_exchange(owned, "share", "share_halves")

    small_names = [k for k in _WEIGHT_KEYS if k not in _SHARDED]
    packed = _pack([grads[k] for k in small_names])
    (packed4,) = _chip_exchange([packed], False, "gather_small_grads")
    small_sum = _sum4_call(packed4, "sum_chips_small")
    (small_other,) = _core_exchange([small_sum], "swap", "swap_small")

    outs = {}
    for k, g in zip(names, shared):
        shp = shards[k].shape
        rows = _as_rows(shards[k]).shape
        res = _adam_call(_as_rows(shards[k]), [g.reshape(rows)], _as_rows(moms[k][0]), _as_rows(moms[k][1]), "adamw_" + k)
        outs[k] = [r.reshape(shp) for r in res]
    like = [shards[k] for k in small_names]
    res = _adam_call(_pack(like), [small_sum, small_other], _pack([moms[k][0] for k in small_names]),
                     _pack([moms[k][1] for k in small_names]), "adamw_small")
    for idx in range(4):
        for k, a in zip(small_names, _unpack(res[idx], like)):
            outs.setdefault(k, [None] * 4)[idx] = a

    result = [loss, grad_x[None]]
    for idx in range(4):
        result += [outs[k][idx] for k in _WEIGHT_KEYS]
    return tuple(result)
```

```python
import functools

import jax
import jax.numpy as jnp
from jax import lax
from jax.experimental import pallas as pl
from jax.experimental.pallas import tpu as pltpu

F32 = jnp.float32
BF16 = jnp.bfloat16
MESH = pl.DeviceIdType.MESH
F32_DOT = lax.Precision.HIGH

D_MODEL = 1024
DEPTH = 4
N_META = 16
CHUNK = 64
PAD = CHUNK - N_META
CONV_K = 4
HALO = 8
WIDTH = 768
S5_GROUPS, S5_GROUP, S5_STATE = 48, 16, 64
S5_BLOCKS = 6
S5_SEGMENTS = 8
SSD_HEADS, SSD_HEAD_DIM, SSD_GROUPS, SSD_STATE = 12, 64, 2, 128
SSD_PAIRS = 6
GDN_HEADS, GDN_DIM = 6, 128
LANES = 128
SMALL_ROWS = 24
ALPHA = (2 * DEPTH) ** 0.25
LN_EPS = 1e-5
P_S5U, P_S5Z, P_XBC, P_DT, P_SSDZ, P_QKV, P_GA, P_GB, P_GDNZ, P_GATE, P_END = (
    0, 768, 1536, 2816, 2828, 3596, 5900, 5906, 5912, 6680, 9752)
ADAM_LR, ADAM_B1, ADAM_B2, ADAM_EPS, ADAM_WD, ADAM_STEP = 0.001, 0.9, 0.999, 1e-08, 0.01, 10
NEG = -1e30


def _pick(n, cands):
    for c in cands:
        if n % c == 0:
            return c
    raise ValueError(f"no tile for {n} in {cands}")


def _cparams(sem, vmem_mb):
    return pltpu.CompilerParams(dimension_semantics=sem, vmem_limit_bytes=vmem_mb << 20)


_DIMS = {"nn": (((1,), (0,)), ((), ())), "nt": (((1,), (1,)), ((), ())), "tn": (((0,), (0,)), ((), ()))}


def _dot(a, b, mode, hi):
    if hi:
        prec = lax.Precision.HIGHEST if hi == "exact" else F32_DOT
        return lax.dot_general(a, b, _DIMS[mode], precision=prec, preferred_element_type=F32)
    return lax.dot_general(a.astype(BF16), b.astype(BF16), _DIMS[mode], preferred_element_type=F32)


@functools.partial(jax.custom_vjp, nondiff_argnums=(2, 3))
def _mm(a, b, mode="nn", hi=False):
    return _dot(a, b, mode, hi)


def _mm_fwd(a, b, mode, hi):
    return _dot(a, b, mode, hi), (a, b)


def _mm_bwd(mode, hi, res, g):
    a, b = res
    if mode == "nn":
        return _dot(g, b, "nt", hi), _dot(a, g, "tn", hi)
    if mode == "nt":
        return _dot(g, b, "nn", hi), _dot(g, a, "tn", hi)
    return _dot(b, g, "nt", hi), _dot(a, g, "nn", hi)


_mm.defvjp(_mm_fwd, _mm_bwd)


@functools.partial(jax.custom_vjp, nondiff_argnums=(1,))
def _roll_rows(x, k):
    return pltpu.roll(x, k % x.shape[0], 0)


def _roll_fwd(x, k):
    return _roll_rows(x, k), None


def _roll_bwd(k, _, g):
    return (_roll_rows(g, -k),)


_roll_rows.defvjp(_roll_fwd, _roll_bwd)


def _iota(shape, dim):
    return lax.broadcasted_iota(jnp.int32, shape, dim)


def _valid_rows(row0, n):
    return (row0 + _iota((n, 1), 0)) >= PAD


def _lane(x, idx):
    return jnp.sum(jnp.where(_iota(x.shape, 1) == idx, x, 0.0), axis=1, keepdims=True)


def _row(x, idx):
    return jnp.sum(jnp.where(_iota(x.shape, 0) == idx, x, 0.0), axis=0, keepdims=True)


def _layer_norm(z, g, b):
    mu = jnp.mean(z, axis=-1, keepdims=True)
    zc = z - mu
    var = jnp.mean(zc * zc, axis=-1, keepdims=True)
    return zc * lax.rsqrt(var + LN_EPS) * g + b


def _rms_norm(z, g):
    return z * lax.rsqrt(jnp.mean(z * z, axis=-1, keepdims=True) + LN_EPS) * g


def _causal_conv(halo, x, w, row0):
    t = x.shape[0]
    xc = jnp.concatenate([halo, x], axis=0)
    acc = None
    for j in range(CONV_K):
        term = _roll_rows(xc, CONV_K - 1 - j)[HALO:HALO + t] * _row(w, j)
        acc = term if acc is None else acc + term
    return acc, x[t - HALO:t]


def _tri(n, strict=False):
    r, c = _iota((n, n), 0), _iota((n, n), 1)
    return (r > c) if strict else (r >= c)


def _scan_op(step, name, *, tile, nb, row_kinds, param_kinds, carry_shapes, out_widths, vmem_mb=48):
    n_rows, n_par, n_car, n_out = len(row_kinds), len(param_kinds), len(carry_shapes), len(out_widths)

    def dims(rows):
        for k, a in zip(row_kinds, rows):
            if k in "bs":
                return a.shape[0], a.shape[0] // tile
        raise ValueError("need a row input")

    def row_spec(kind, a, rev, nt):
        ti = (lambda i: nt - 1 - i) if rev else (lambda i: i)
        if kind == "b":
            return pl.BlockSpec((tile, a.shape[1] // nb), lambda b, i: (ti(i), b))
        if kind == "s":
            return pl.BlockSpec((tile, a.shape[1]), lambda b, i: (ti(i), 0))
        return pl.BlockSpec((None, a.shape[1], a.shape[2]), lambda b, i: (ti(i), 0, 0))

    def par_spec(kind, a):
        if kind == "b":
            return pl.BlockSpec((None, a.shape[1], a.shape[2]), lambda b, i: (b, 0, 0))
        return pl.BlockSpec(a.shape, lambda b, i: (0, 0))

    def fwd_call(rows, params):
        length, nt = dims(rows)

        def body(*refs):
            r_in = refs[:n_rows]
            p_in = refs[n_rows:n_rows + n_par]
            o_out = refs[n_rows + n_par:n_rows + n_par + n_out]
            s_out = refs[n_rows + n_par + n_out:n_rows + n_par + n_out + n_car]
            c_scr = refs[n_rows + n_par + n_out + n_car:]
            b, i = pl.program_id(0), pl.program_id(1)

            if n_car:
                @pl.when(i == 0)
                def _():
                    for c in c_scr:
                        c[...] = jnp.zeros_like(c)

            cin = tuple(c[...] for c in c_scr)
            for s, c in zip(s_out, cin):
                s[...] = c
            new_c, outs = step(cin, tuple(r[...] for r in r_in), tuple(p[...] for p in p_in), b, i * tile)
            for c, v in zip(c_scr, new_c):
                c[...] = v
            for o, v in zip(o_out, outs):
                o[...] = v

        out_shape = [jax.ShapeDtypeStruct((length, nb * w), F32) for w in out_widths]
        out_shape += [jax.ShapeDtypeStruct((nb, nt) + tuple(s), F32) for s in carry_shapes]
        out_specs = [pl.BlockSpec((tile, w), lambda b, i: (i, b)) for w in out_widths]
        out_specs += [pl.BlockSpec((None, None) + tuple(s), lambda b, i: (b, i, 0, 0)) for s in carry_shapes]
        res = pl.pallas_call(
            body, name=name + "_fwd", grid=(nb, nt),
            in_specs=[row_spec(k, a, False, nt) for k, a in zip(row_kinds, rows)]
            + [par_spec(k, a) for k, a in zip(param_kinds, params)],
            out_specs=out_specs, out_shape=out_shape,
            scratch_shapes=[pltpu.VMEM(tuple(s), F32) for s in carry_shapes],
            compiler_params=_cparams(("arbitrary", "arbitrary"), vmem_mb),
        )(*rows, *params)
        return tuple(res[:n_out]), tuple(res[n_out:])

    def bwd_call(rows, params, saved, douts):
        length, nt = dims(rows)

        def body(*refs):
            k0 = 0
            r_in = refs[k0:k0 + n_rows]; k0 += n_rows
            p_in = refs[k0:k0 + n_par]; k0 += n_par
            s_in = refs[k0:k0 + n_car]; k0 += n_car
            g_in = refs[k0:k0 + n_out]; k0 += n_out
            dr_out = refs[k0:k0 + n_rows]; k0 += n_rows
            dp_out = refs[k0:k0 + n_par]; k0 += n_par
            dc_scr = refs[k0:]
            b, i = pl.program_id(0), pl.program_id(1)
            row0 = (nt - 1 - i) * tile

            @pl.when(i == 0)
            def _():
                for c in dc_scr:
                    c[...] = jnp.zeros_like(c)
                for p in dp_out:
                    p[...] = jnp.zeros_like(p)

            def f(c, r, p):
                return step(c, r, p, b, row0)

            _, vjp = jax.vjp(f, tuple(s[...] for s in s_in), tuple(r[...] for r in r_in),
                             tuple(p[...] for p in p_in))
            dc, dr, dp = vjp((tuple(c[...] for c in dc_scr), tuple(g[...] for g in g_in)))
            for c, v in zip(dc_scr, dc):
                c[...] = v
            for o, v in zip(dr_out, dr):
                o[...] = v
            for o, v in zip(dp_out, dp):
                o[...] += v

        rev = lambda i: nt - 1 - i
        in_specs = [row_spec(k, a, True, nt) for k, a in zip(row_kinds, rows)]
        in_specs += [par_spec(k, a) for k, a in zip(param_kinds, params)]
        in_specs += [pl.BlockSpec((None, None) + tuple(s), lambda b, i: (b, rev(i), 0, 0)) for s in carry_shapes]
        in_specs += [pl.BlockSpec((tile, w), lambda b, i: (rev(i), b)) for w in out_widths]
        out_shape, out_specs = [], []
        for k, a in zip(row_kinds, rows):
            if k == "b":
                out_shape.append(jax.ShapeDtypeStruct(a.shape, F32))
                out_specs.append(pl.BlockSpec((tile, a.shape[1] // nb), lambda b, i: (rev(i), b)))
            elif k == "s":
                out_shape.append(jax.ShapeDtypeStruct((nb,) + a.shape, F32))
                out_specs.append(pl.BlockSpec((None, tile, a.shape[1]), lambda b, i: (b, rev(i), 0)))
            else:
                out_shape.append(jax.ShapeDtypeStruct((nb,) + a.shape, F32))
                out_specs.append(pl.BlockSpec((None, None, a.shape[1], a.shape[2]), lambda b, i: (b, rev(i), 0, 0)))
        for k, a in zip(param_kinds, params):
            shp = a.shape[1:] if k == "b" else a.shape
            out_shape.append(jax.ShapeDtypeStruct((nb,) + tuple(shp), F32))
            out_specs.append(pl.BlockSpec((None,) + tuple(shp), lambda b, i: (b, 0, 0)))
        res = pl.pallas_call(
            body, name=name + "_bwd", grid=(nb, nt), in_specs=in_specs, out_specs=out_specs, out_shape=out_shape,
            scratch_shapes=[pltpu.VMEM(tuple(s), F32) for s in carry_shapes],
            compiler_params=_cparams(("arbitrary", "arbitrary"), vmem_mb),
        )(*rows, *params, *saved, *douts)
        fold = (lambda a: a[0]) if nb == 1 else (lambda a: jnp.sum(a, axis=0))
        drows = tuple(r if k == "b" else fold(r) for k, r in zip(row_kinds, res[:n_rows]))
        dpars = tuple(p if k == "b" else fold(p) for k, p in zip(param_kinds, res[n_rows:]))
        return drows, dpars

    @jax.custom_vjp
    def op(rows, params):
        return fwd_call(rows, params)[0]

    def op_fwd(rows, params):
        outs, saved = fwd_call(rows, params)
        return outs, (rows, params, saved)

    def op_bwd(res, douts):
        rows, params, saved = res
        return bwd_call(rows, params, saved, tuple(douts))

    op.defvjp(op_fwd, op_bwd)
    return op


_TM = (832, 512, 256, 128)
_TN = (768, 640, 512, 384, 256, 128)


def _mm_fwd_call(x, w, name):
    m, k = x.shape
    n = w.shape[1]
    tm, tn = _pick(m, _TM), _pick(n, _TN)

    def body(x_ref, w_ref, o_ref):
        o_ref[...] = _dot(x_ref[...], w_ref[...], "nn", False)

    return pl.pallas_call(
        body, name=name, grid=(n // tn, m // tm),
        in_specs=[pl.BlockSpec((tm, k), lambda j, i: (i, 0)), pl.BlockSpec((k, tn), lambda j, i: (0, j))],
        out_specs=pl.BlockSpec((tm, tn), lambda j, i: (i, j)),
        out_shape=jax.ShapeDtypeStruct((m, n), F32),
        compiler_params=_cparams(("parallel", "parallel"), 48),
    )(x, w)


def _mm_dx_call(g, w, name):
    m, n = g.shape
    k = w.shape[0]
    tm, tn = _pick(m, _TM), _pick(n, _TN)

    def body(g_ref, w_ref, o_ref):
        @pl.when(pl.program_id(1) == 0)
        def _():
            o_ref[...] = jnp.zeros_like(o_ref)

        o_ref[...] += _dot(g_ref[...], w_ref[...], "nt", False)

    return pl.pallas_call(
        body, name=name, grid=(m // tm, n // tn),
        in_specs=[pl.BlockSpec((tm, tn), lambda i, j: (i, j)), pl.BlockSpec((k, tn), lambda i, j: (0, j))],
        out_specs=pl.BlockSpec((tm, k), lambda i, j: (i, 0)),
        out_shape=jax.ShapeDtypeStruct((m, k), F32),
        compiler_params=_cparams(("parallel", "arbitrary"), 48),
    )(g, w)


def _mm_dw_call(x, g, name):
    m, k = x.shape
    n = g.shape[1]
    tm, tn = _pick(m, _TM), _pick(n, _TN)

    def body(x_ref, g_ref, o_ref):
        @pl.when(pl.program_id(1) == 0)
        def _():
            o_ref[...] = jnp.zeros_like(o_ref)

        o_ref[...] += _dot(x_ref[...], g_ref[...], "tn", False)

    return pl.pallas_call(
        body, name=name, grid=(n // tn, m // tm),
        in_specs=[pl.BlockSpec((tm, k), lambda j, i: (i, 0)), pl.BlockSpec((tm, tn), lambda j, i: (i, j))],
        out_specs=pl.BlockSpec((k, tn), lambda j, i: (0, j)),
        out_shape=jax.ShapeDtypeStruct((k, n), F32),
        compiler_params=_cparams(("parallel", "arbitrary"), 48),
    )(x, g)


def _dense(name):
    @jax.custom_vjp
    def op(x, w):
        return _mm_fwd_call(x, w, name + "_fwd")

    def op_fwd(x, w):
        return _mm_fwd_call(x, w, name + "_fwd"), (x, w)

    def op_bwd(res, g):
        x, w = res
        return _mm_dx_call(g, w, name + "_dx"), _mm_dw_call(x, g, name + "_dw")

    op.defvjp(op_fwd, op_bwd)
    return op


def _ln_in_step(c, rows, params, b, row0):
    (z,), (g, bb) = rows, params
    return (), (jnp.where(_valid_rows(row0, z.shape[0]), _layer_norm(z, g, bb), 0.0),)


def _ln_res_step(c, rows, params, b, row0):
    (h, o), (g, bb) = rows, params
    return (), (jnp.where(_valid_rows(row0, h.shape[0]), _layer_norm(ALPHA * h + o, g, bb), 0.0),)


def _s5_prep_step(c, rows, params, b, row0):
    a_re, a_im, log_step, b_re, b_im = rows
    lam_re = jnp.minimum(a_re, -1e-4)
    lam_im = a_im
    step = jnp.exp(log_step)
    mag = jnp.exp(lam_re * step)
    abar_re, abar_im = mag * jnp.cos(lam_im * step), mag * jnp.sin(lam_im * step)
    den = lam_re * lam_re + lam_im * lam_im
    nr, ni = abar_re - 1.0, abar_im
    coef_re = (nr * lam_re + ni * lam_im) / den
    coef_im = (ni * lam_re - nr * lam_im) / den
    return (), (abar_re, abar_im, coef_re * b_re - coef_im * b_im, coef_re * b_im + coef_im * b_re)


def _s5_scan_step(c, rows, params, b, row0):
    (c_re, c_im), (u,) = c, rows
    bd_re, bd_im, a_re, a_im, cd_re, cd_im = params
    t = u.shape[0]
    steps = t // S5_SEGMENTS
    bu_re, bu_im = _mm(u, bd_re), _mm(u, bd_im)
    a_re, a_im = (jnp.broadcast_to(v, (S5_SEGMENTS, v.shape[1])) for v in (a_re, a_im))
    at = lambda v, i: v[S5_SEGMENTS * i:S5_SEGMENTS * (i + 1)]
    s_re, s_im = at(bu_re, 0), at(bu_im, 0)
    p_re, p_im = a_re, a_im
    local, power = [(s_re, s_im)], [(p_re, p_im)]
    for i in range(1, steps):
        s_re, s_im = a_re * s_re - a_im * s_im + at(bu_re, i), a_re * s_im + a_im * s_re + at(bu_im, i)
        p_re, p_im = a_re * p_re - a_im * p_im, a_re * p_im + a_im * p_re
        local.append((s_re, s_im))
        power.append((p_re, p_im))
    seg = _iota((S5_SEGMENTS, 1), 0)
    in_re = jnp.where(seg == 0, c_re, _roll_rows(s_re, 1))
    in_im = jnp.where(seg == 0, c_im, _roll_rows(s_im, 1))
    q_re, q_im = p_re, p_im
    d = 1
    while d < S5_SEGMENTS:
        keep = seg >= d
        sh_re = jnp.where(keep, _roll_rows(in_re, d), 0.0)
        sh_im = jnp.where(keep, _roll_rows(in_im, d), 0.0)
        in_re, in_im = in_re + q_re * sh_re - q_im * sh_im, in_im + q_re * sh_im + q_im * sh_re
        q_re, q_im = q_re * q_re - q_im * q_im, 2.0 * q_re * q_im
        d *= 2
    full = [(l_re + w_re * in_re - w_im * in_im, l_im + w_re * in_im + w_im * in_re)
            for (l_re, l_im), (w_re, w_im) in zip(local, power)]
    y = _mm(jnp.concatenate([f[0] for f in full], axis=0), cd_re) - _mm(jnp.concatenate([f[1] for f in full], axis=0), cd_im)
    return (_row(full[-1][0], S5_SEGMENTS - 1), _row(full[-1][1], S5_SEGMENTS - 1)), (y,)


def _interleave(v, tile, inverse=False):
    length, width = v.shape
    shape = (length // tile, tile // S5_SEGMENTS, S5_SEGMENTS) if inverse else (length // tile, S5_SEGMENTS, tile // S5_SEGMENTS)
    return v.reshape(shape + (width,)).transpose(0, 2, 1, 3).reshape(length, width)


def _s5_post_step(c, rows, params, b, row0):
    (y, u, z), (d, w_glu, b_glu) = rows, params
    v = jax.nn.gelu(y + d * u)
    v = v * jax.nn.sigmoid(_mm(v, w_glu) + b_glu)
    return (), (v * jax.nn.silu(z),)


def _ssd_step(c, rows, params, b, row0):
    halo, state = c
    xbc_raw, z, small, small_t = rows
    cw, cb, d_l, bias_l, alog_l, bias_c, alog_c, norm_g = params
    t = xbc_raw.shape[0]
    grp = SSD_GROUPS * SSD_STATE
    valid = _valid_rows(row0, t)
    conv, halo2 = _causal_conv(halo, xbc_raw, cw, row0)
    act = jnp.where(valid, jax.nn.silu(conv + cb), 0.0)
    low = _iota((1, LANES), 1) < SSD_HEAD_DIM
    dt_all = jnp.where(valid, jax.nn.softplus(small + bias_l), 0.0)
    a_all = -jnp.exp(alog_l)
    valid_t = (row0 + _iota((1, t), 1)) >= PAD
    dta_t = jnp.where(valid_t, jax.nn.softplus(small_t + bias_c), 0.0) * (-jnp.exp(alog_c))
    acum_t = _mm(dta_t, jnp.where(_iota((t, t), 0) <= _iota((t, t), 1), 1.0, 0.0), "nn", True)
    causal = _tri(t)
    acum_all = _mm(jnp.where(causal, 1.0, 0.0), dt_all * a_all, "nn", True)
    last_all = _row(acum_all, t - 1)
    low_rows = _iota((LANES, 1), 0) < SSD_HEAD_DIM
    pairs, groups = range(SSD_PAIRS), range(SSD_GROUPS)
    grp_of = [p // (SSD_PAIRS // SSD_GROUPS) for p in pairs]
    bs = [act[:, WIDTH + g * SSD_STATE:WIDTH + (g + 1) * SSD_STATE] for g in groups]
    cs = [act[:, WIDTH + grp + g * SSD_STATE:WIDTH + grp + (g + 1) * SSD_STATE] for g in groups]
    scores = [_mm(cs[g], bs[g], "nt") for g in groups]
    per_lane = lambda v, p: jnp.where(low, v[:, 2 * p:2 * p + 1], v[:, 2 * p + 1:2 * p + 2])
    dt_l = [per_lane(dt_all, p) for p in pairs]
    acum_l = [per_lane(acum_all, p) for p in pairs]
    last_l = [per_lane(last_all, p) for p in pairs]
    st = [state[p * LANES:(p + 1) * LANES] for p in pairs]
    xd = [act[:, p * LANES:(p + 1) * LANES] * dt_l[p] for p in pairs]
    decay = [jnp.exp(jnp.where(causal, acum_all[:, h:h + 1] - _row(acum_t, h), NEG)) for h in range(SSD_HEADS)]
    y_lo = [_mm(scores[grp_of[p]] * decay[2 * p], jnp.where(low, xd[p], 0.0)) for p in pairs]
    y_hi = [_mm(scores[grp_of[p]] * decay[2 * p + 1], jnp.where(low, 0.0, xd[p])) for p in pairs]
    y_off = [_mm(cs[grp_of[p]], st[p], "nt") * jnp.exp(acum_l[p]) for p in pairs]
    new_st = [_mm(xd[p] * jnp.exp(last_l[p] - acum_l[p]), bs[grp_of[p]], "tn") for p in pairs]
    cd = jnp.exp(last_all)
    new_st = [st[p] * jnp.where(low_rows, cd[:, 2 * p:2 * p + 1], cd[:, 2 * p + 1:2 * p + 2]) + new_st[p] for p in pairs]
    y = jnp.concatenate([y_lo[p] + y_hi[p] + y_off[p] for p in pairs], axis=1) + act[:, :WIDTH] * d_l
    out = _rms_norm(y * jax.nn.silu(z), norm_g)
    return (halo2, jnp.concatenate(new_st, axis=0)), (out,)


def _unit_lower_inverse(mats):
    n = mats[0].shape[0]
    eye = jnp.where(_iota((n, n), 0) == _iota((n, n), 1), 1.0, 0.0)
    inv = [eye - a for a in mats]
    p = [_mm(a, a, "nn", True) for a in mats]
    k = 2
    while k < n:
        inv = [i + _mm(i, q, "nn", True) for i, q in zip(inv, p)]
        k *= 2
        if k < n:
            p = [_mm(q, q, "nn", True) for q in p]
    return inv


def _gdn_step(c, rows, params, b, row0):
    halo, state = c
    qkv_raw, z, small, small_t = rows
    cw, bias_l, alog_l, bias_c, alog_c, norm_g = params
    t = qkv_raw.shape[0]
    valid = _valid_rows(row0, t)
    conv, halo2 = _causal_conv(halo, qkv_raw, cw, row0)
    act = jnp.where(valid, jax.nn.silu(conv), 0.0)
    beta_all = jnp.where(valid, jax.nn.sigmoid(small), 0.0)
    g_all = jnp.where(valid, -jnp.exp(alog_l) * jax.nn.softplus(small + bias_l), 0.0)
    valid_t = (row0 + _iota((1, t), 1)) >= PAD
    g_t = jnp.where(valid_t, -jnp.exp(alog_c) * jax.nn.softplus(small_t + bias_c), 0.0)
    causal, strict = _tri(t), _tri(t, True)
    gcum_all = _mm(jnp.where(causal, 1.0, 0.0), g_all, "nn", True)
    gcum_t = _mm(g_t, jnp.where(_iota((t, t), 0) <= _iota((t, t), 1), 1.0, 0.0), "nn", True)
    heads = range(GDN_HEADS)
    part = lambda h, n: act[:, n * WIDTH + h * GDN_DIM:n * WIDTH + (h + 1) * GDN_DIM]
    unit = lambda x: x * lax.rsqrt(jnp.sum(x * x, axis=-1, keepdims=True) + 1e-6)
    q = [unit(part(h, 0)) * (GDN_DIM ** -0.5) for h in heads]
    k = [unit(part(h, 1)) for h in heads]
    st = [state[h * GDN_DIM:(h + 1) * GDN_DIM] for h in heads]
    ia = [SSD_HEADS + h for h in heads]
    beta = [beta_all[:, ia[h] + GDN_HEADS:ia[h] + GDN_HEADS + 1] for h in heads]
    gcum = [gcum_all[:, ia[h]:ia[h] + 1] for h in heads]
    gamma = [jnp.exp(jnp.where(causal, gcum[h] - _row(gcum_t, ia[h]), NEG)) for h in heads]
    egc = [jnp.exp(gcum[h]) for h in heads]
    a_mat = [jnp.where(strict, _mm(k[h], k[h], "nt") * gamma[h] * beta[h], 0.0) for h in heads]
    inv = _unit_lower_inverse(a_mat)
    rhs = [jnp.concatenate([part(h, 2) * beta[h], k[h] * (beta[h] * egc[h])], axis=1) for h in heads]
    sol = [_mm(inv[h], rhs[h], "nn", True) for h in heads]
    attn = [_mm(q[h], k[h], "nt") * gamma[h] for h in heads]
    from_state = [_mm(jnp.concatenate([sol[h][:, GDN_DIM:], q[h] * egc[h]], axis=0), st[h]) for h in heads]
    v_new = [sol[h][:, :GDN_DIM] - from_state[h][:t] for h in heads]
    o = [from_state[h][t:] + _mm(attn[h], v_new[h]) for h in heads]
    glast = [_row(gcum[h], t - 1) for h in heads]
    new_st = [st[h] * jnp.exp(glast[h]) + _mm(k[h] * jnp.exp(glast[h] - gcum[h]), v_new[h], "tn") for h in heads]
    out = jnp.concatenate([_rms_norm(o[h], norm_g) for h in heads], axis=1) * jax.nn.silu(z)
    return (halo2, jnp.concatenate(new_st, axis=0)), (out,)


def _gate_merge_step(c, rows, params, b, row0):
    (oa, ob, oc, gl), (bg,) = rows, params
    acc = None
    for n, o in enumerate((oa, ob, oc)):
        term = jax.nn.sigmoid(gl[:, n * D_MODEL:(n + 1) * D_MODEL] + bg[:, n * D_MODEL:(n + 1) * D_MODEL]) * o
        acc = term if acc is None else acc + term
    return (), (acc,)


def _loss_tile(n):
    return _pick(n, (512, 256, 128, 64))


def _loss_fwd_call(y, tgt):
    n, d = y.shape
    tile = _loss_tile(n)

    def body(y_ref, t_ref, o_ref):
        @pl.when(pl.program_id(0) == 0)
        def _():
            o_ref[...] = jnp.zeros_like(o_ref)

        e = y_ref[...] - t_ref[...]
        o_ref[...] += jnp.sum(jnp.sum(e * e, axis=1, keepdims=True), axis=0, keepdims=True) * (0.5 / d)

    out = pl.pallas_call(
        body, name="loss_fwd", grid=(n // tile,),
        in_specs=[pl.BlockSpec((tile, d), lambda i: (i, 0)), pl.BlockSpec((tile, d), lambda i: (i, 0))],
        out_specs=pl.BlockSpec((8, LANES), lambda i: (0, 0)),
        out_shape=jax.ShapeDtypeStruct((8, LANES), F32),
        compiler_params=_cparams(("arbitrary",), 32),
    )(y, tgt)
    return out[0, 0]


def _loss_bwd_call(y, tgt, g):
    n, d = y.shape
    tile = _loss_tile(n)

    def body(y_ref, t_ref, g_ref, o_ref):
        o_ref[...] = (y_ref[...] - t_ref[...]) * (g_ref[...][0:1, 0:1] * (1.0 / d))

    return pl.pallas_call(
        body, name="loss_bwd", grid=(n // tile,),
        in_specs=[pl.BlockSpec((tile, d), lambda i: (i, 0)), pl.BlockSpec((tile, d), lambda i: (i, 0)),
                  pl.BlockSpec((8, LANES), lambda i: (0, 0))],
        out_specs=pl.BlockSpec((tile, d), lambda i: (i, 0)),
        out_shape=jax.ShapeDtypeStruct((n, d), F32),
        compiler_params=_cparams(("parallel",), 32),
    )(y, tgt, jnp.broadcast_to(g, (8, LANES)).astype(F32))


@jax.custom_vjp
def _loss_op(y, tgt):
    return _loss_fwd_call(y, tgt)


def _loss_op_fwd(y, tgt):
    return _loss_fwd_call(y, tgt), (y, tgt)


def _loss_op_bwd(res, g):
    y, tgt = res
    return _loss_bwd_call(y, tgt, g), jnp.zeros_like(tgt)


_loss_op.defvjp(_loss_op_fwd, _loss_op_bwd)


def _rowwise(step, name, tile, n_rows, n_params, out_widths, vmem_mb=48):
    return _scan_op(step, name, tile=tile, nb=1, row_kinds="s" * n_rows, param_kinds="s" * n_params,
                    carry_shapes=(), out_widths=out_widths, vmem_mb=vmem_mb)


def _block_diag(x, nblk):
    bsz, _, r, c = x.shape
    eye = jnp.eye(nblk, dtype=x.dtype)
    return jnp.einsum("bgrc,gh->bgrhc", x, eye).reshape(bsz, nblk * r, nblk * c)


_PROJ = {"s5u": (P_S5U, P_S5Z), "s5z": (P_S5Z, P_XBC), "xbc": (P_XBC, P_DT), "ssdz": (P_SSDZ, P_QKV),
         "qkv": (P_QKV, P_GA), "gdnz": (P_GDNZ, P_GATE), "gate": (P_GATE, P_END)}


def _prepare(weights):
    w = weights
    w_in = w["w_in"]
    seg = lambda a, bnd: w_in[:, :, a:bnd]
    zeros = lambda *shape: jnp.zeros((DEPTH,) + shape, F32)
    row3 = lambda v: v.reshape(DEPTH, 1, -1)
    p = {"w_" + k: seg(a, bnd) for k, (a, bnd) in _PROJ.items()}
    p["w_small"] = jnp.concatenate([seg(P_DT, P_SSDZ), seg(P_GA, P_GDNZ), zeros(D_MODEL, LANES - SMALL_ROWS)], axis=2)

    rows, wide = DEPTH * S5_GROUPS, S5_STATE * S5_GROUP
    flat = lambda v: v.reshape(rows, -1)
    rep = lambda v: jnp.repeat(flat(v), S5_GROUP, axis=1)
    prep = _rowwise(_s5_prep_step, "s5_prep", S5_GROUPS, 5, 0, (wide,) * 4)
    abar_re, abar_im, bbar_re, bbar_im = prep(
        (rep(w["s5_a_re"]), rep(w["s5_a_im"]), jnp.broadcast_to(flat(w["s5_log_step"]), (rows, wide)),
         flat(w["s5_b_re"]), flat(w["s5_b_im"])), ())
    gpb = S5_GROUPS // S5_BLOCKS
    lanes, chans, nblk = gpb * S5_STATE, gpb * S5_GROUP, DEPTH * S5_BLOCKS
    to_bd = lambda bb: _block_diag(bb.reshape(nblk, gpb, S5_STATE, S5_GROUP).transpose(0, 1, 3, 2), gpb).reshape(
        DEPTH, S5_BLOCKS, chans, lanes)
    to_cd = lambda cc: _block_diag(cc.reshape(nblk, gpb, S5_GROUP, S5_STATE).transpose(0, 1, 3, 2), gpb).reshape(
        DEPTH, S5_BLOCKS, lanes, chans)
    to_a = lambda a: a[:, ::S5_GROUP].reshape(DEPTH, S5_BLOCKS, 1, lanes)
    p.update(s5_bd_re=to_bd(bbar_re), s5_bd_im=to_bd(bbar_im), s5_a_re=to_a(abar_re), s5_a_im=to_a(abar_im),
             s5_cd_re=to_cd(w["s5_c_re"]), s5_cd_im=to_cd(w["s5_c_im"]),
             s5_d=row3(w["s5_d"]), s5_w_glu=w["s5_w_glu"], s5_b_glu=row3(w["s5_b_glu"]))

    bias = jnp.concatenate([w["ssd_dt_bias"], w["gdn_dt_bias"]], axis=1)
    alog = jnp.concatenate([w["ssd_a_log"], w["gdn_a_log"]], axis=1)
    on_lanes = lambda v: jnp.concatenate([v, zeros(LANES - v.shape[1])], axis=1).reshape(DEPTH, 1, LANES)
    on_rows = lambda v: jnp.concatenate([v, zeros(SMALL_ROWS - v.shape[1])], axis=1).reshape(DEPTH, SMALL_ROWS, 1)
    pad_w = lambda cw: jnp.concatenate([cw, zeros(HALO - CONV_K, cw.shape[2])], axis=1)
    p.update(bias_l=on_lanes(bias), alog_l=on_lanes(alog), bias_c=on_rows(bias), alog_c=on_rows(alog),
             ssd_cw=pad_w(w["ssd_conv_w"]), ssd_cb=row3(w["ssd_conv_b"]),
             ssd_d=row3(jnp.repeat(w["ssd_d"], SSD_HEAD_DIM, axis=1)), ssd_norm_g=row3(w["ssd_norm_g"]),
             gdn_cw=pad_w(w["gdn_conv_w"]), gdn_norm_g=row3(w["gdn_norm_g"]),
             w_branch=w["w_branch"], b_gate=row3(w["b_gate"]), w_out=w["w_out"], ln_g=row3(w["ln_g"]), ln_b=row3(w["ln_b"]))
    return p


def _layer(h, p):
    length = h.shape[0]
    nt = length // CHUNK
    t_row = _pick(length, (208, 128))
    t_s5 = _pick(length, (320, 128))
    proj = {k: _dense("proj_" + k)(h, p["w_" + k]) for k in list(_PROJ) + ["small"]}
    small = proj["small"]
    small_t = small[:, :SMALL_ROWS].reshape(nt, CHUNK, SMALL_ROWS).transpose(0, 2, 1)

    lanes = p["s5_a_re"].shape[-1]
    s5_scan = _scan_op(_s5_scan_step, "s5_scan", tile=t_s5, nb=S5_BLOCKS, row_kinds="b", param_kinds="bbbbbb",
                       carry_shapes=((1, lanes), (1, lanes)), out_widths=(LANES,))
    (y_ssm,) = s5_scan((_interleave(proj["s5u"], t_s5),), (p["s5_bd_re"], p["s5_bd_im"], p["s5_a_re"], p["s5_a_im"],
                                                           p["s5_cd_re"], p["s5_cd_im"]))
    y_ssm = _interleave(y_ssm, t_s5, inverse=True)
    s5_post = _rowwise(_s5_post_step, "s5_post", t_row, 3, 3, (WIDTH,))
    (y_a,) = s5_post((y_ssm, proj["s5u"], proj["s5z"]), (p["s5_d"], p["s5_w_glu"], p["s5_b_glu"]))

    scalars = (p["bias_l"], p["alog_l"], p["bias_c"], p["alog_c"])
    ssd = _scan_op(_ssd_step, "ssd_scan", tile=CHUNK, nb=1, row_kinds="ssst", param_kinds="s" * 8,
                   carry_shapes=((HALO, P_DT - P_XBC), (SSD_HEADS * SSD_HEAD_DIM, SSD_STATE)), out_widths=(WIDTH,))
    (y_b,) = ssd((proj["xbc"], proj["ssdz"], small, small_t),
                 (p["ssd_cw"], p["ssd_cb"], p["ssd_d"]) + scalars + (p["ssd_norm_g"],))
    gdn = _scan_op(_gdn_step, "gdn_scan", tile=CHUNK, nb=1, row_kinds="ssst", param_kinds="s" * 6,
                   carry_shapes=((HALO, P_GA - P_QKV), (GDN_HEADS * GDN_DIM, GDN_DIM)), out_widths=(WIDTH,))
    (y_c,) = gdn((proj["qkv"], proj["gdnz"], small, small_t), (p["gdn_cw"],) + scalars + (p["gdn_norm_g"],))

    branch = [_dense("branch_" + n)(y, p["w_branch"][i]) for i, (n, y) in enumerate(zip("abc", (y_a, y_b, y_c)))]
    merge = _rowwise(_gate_merge_step, "gate_merge", t_row, 4, 1, (D_MODEL,))
    (merged,) = merge((*branch, proj["gate"]), (p["b_gate"],))
    out = _dense("out_proj")(merged, p["w_out"])
    ln = _rowwise(_ln_res_step, "ln_res", t_row, 2, 2, (D_MODEL,))
    (h_new,) = ln((h, out), (p["ln_g"], p["ln_b"]))
    return h_new


_LAYER_KEYS = ("w_in", "s5_a_re", "s5_a_im", "s5_log_step", "s5_b_re", "s5_b_im", "s5_c_re", "s5_c_im", "s5_d",
               "s5_w_glu", "s5_b_glu", "ssd_conv_w", "ssd_conv_b", "ssd_dt_bias", "ssd_a_log", "ssd_d", "ssd_norm_g",
               "gdn_conv_w", "gdn_dt_bias", "gdn_a_log", "gdn_norm_g", "w_branch", "b_gate", "w_out", "ln_g", "ln_b")
_WEIGHT_KEYS = ("meta", "ln_in_g", "ln_in_b") + _LAYER_KEYS


def _local_loss(weights, x, target):
    seq = x.shape[0]
    hcat = jnp.concatenate([jnp.zeros((PAD, D_MODEL), F32), weights["meta"], x], axis=0)
    length = hcat.shape[0]
    ln_in = _rowwise(_ln_in_step, "ln_in", _pick(length, (416, 256, 128)), 1, 2, (D_MODEL,))
    (h,) = ln_in((hcat,), (weights["ln_in_g"].reshape(1, -1), weights["ln_in_b"].reshape(1, -1)))

    prepared = _prepare(weights)
    for layer in range(DEPTH):
        h = _layer(h, {k: v[layer] for k, v in prepared.items()})
    return _loss_op(h[length - seq:], target)


_ANY = pl.BlockSpec(memory_space=pl.ANY)
_BLOCK_BYTES = 4 << 20


def _chip_exchange(arrays, all_to_all, name):
    n = len(arrays)

    def body(*refs):
        ins, outs = refs[:n], refs[n:2 * n]
        send_sems, recv_sems = refs[2 * n:]
        mx, my, mc = lax.axis_index("x"), lax.axis_index("y"), lax.axis_index("c")
        me = 2 * mx + my
        peers = [(1 - mx, my), (mx, 1 - my), (1 - mx, 1 - my)]
        sends = []
        for a, (src, dst) in enumerate(zip(ins, outs)):
            for k, (px, py) in enumerate(peers):
                cp = pltpu.make_async_remote_copy(
                    src_ref=src.at[2 * px + py] if all_to_all else src, dst_ref=dst.at[me],
                    send_sem=send_sems.at[a, k], recv_sem=recv_sems.at[a, k],
                    device_id=(px, py, mc), device_id_type=MESH)
                cp.start()
                sends.append(cp)
        for a, (src, dst) in enumerate(zip(ins, outs)):
            for k, (px, py) in enumerate(peers):
                pltpu.make_async_remote_copy(
                    src_ref=src.at[me] if all_to_all else src, dst_ref=dst.at[2 * px + py],
                    send_sem=send_sems.at[a, k], recv_sem=recv_sems.at[a, k],
                    device_id=(px, py, mc), device_id_type=MESH).wait_recv()
        for cp in sends:
            cp.wait_send()

    out_shape = [jax.ShapeDtypeStruct(a.shape if all_to_all else (4,) + a.shape, a.dtype) for a in arrays]
    return pl.pallas_call(
        body, name=name, in_specs=[_ANY] * n, out_specs=[_ANY] * n, out_shape=out_shape,
        scratch_shapes=[pltpu.SemaphoreType.DMA((n, 3)), pltpu.SemaphoreType.DMA((n, 3))],
    )(*arrays)


def _gather_two_level(arrays, name):
    n = len(arrays)

    def body(*refs):
        ins, outs = refs[:n], refs[n:2 * n]
        send_sems, recv_sems = refs[2 * n:]
        mx, my, mc = lax.axis_index("x"), lax.axis_index("y"), lax.axis_index("c")
        me = 2 * mx + my
        sibling = (mx, my, 1 - mc)
        chips = [(1 - mx, my), (mx, 1 - my), (1 - mx, 1 - my)]

        def copy(a, k, src, chip, core, to):
            return pltpu.make_async_remote_copy(
                src_ref=src, dst_ref=outs[a].at[chip, core], send_sem=send_sems.at[a, k], recv_sem=recv_sems.at[a, k],
                device_id=to, device_id_type=MESH)

        sends = [copy(a, j, ins[a], me, mc, (px, py, mc)) for a in range(n) for j, (px, py) in enumerate(chips)]
        for cp in sends:
            cp.start()
        passed = []
        for a in range(n):
            for j, (px, py) in enumerate(chips):
                chip = 2 * px + py
                copy(a, j, ins[a], chip, mc, sibling).wait_recv()
                cp = copy(a, 3 + j, outs[a].at[chip, mc], chip, mc, sibling)
                cp.start()
                passed.append(cp)
        for a in range(n):
            for j, (px, py) in enumerate(chips):
                copy(a, 3 + j, ins[a], 2 * px + py, 1 - mc, sibling).wait_recv()
        for cp in sends + passed:
            cp.wait_send()

    return pl.pallas_call(
        body, name=name, in_specs=[_ANY] * n, out_specs=[_ANY] * n,
        out_shape=[jax.ShapeDtypeStruct((4, 2) + a.shape, a.dtype) for a in arrays],
        scratch_shapes=[pltpu.SemaphoreType.DMA((n, 6)), pltpu.SemaphoreType.DMA((n, 6))],
    )(*arrays)


def _core_exchange(arrays, other_half, name):
    n = len(arrays)

    def body(*refs):
        ins, outs = refs[:n], refs[n:2 * n]
        send_sems, recv_sems = refs[2 * n:]
        mc = lax.axis_index("c")
        sibling = (lax.axis_index("x"), lax.axis_index("y"), 1 - mc)
        copies = [pltpu.make_async_remote_copy(
            src_ref=s.at[1 - mc] if other_half else s, dst_ref=d, send_sem=send_sems.at[a], recv_sem=recv_sems.at[a],
            device_id=sibling, device_id_type=MESH) for a, (s, d) in enumerate(zip(ins, outs))]
        for cp in copies:
            cp.start()
        for cp in copies:
            cp.wait()

    return pl.pallas_call(
        body, name=name, in_specs=[_ANY] * n, out_specs=[_ANY] * n,
        out_shape=[jax.ShapeDtypeStruct(a.shape[1:] if other_half else a.shape, a.dtype) for a in arrays],
        scratch_shapes=[pltpu.SemaphoreType.DMA((n,)), pltpu.SemaphoreType.DMA((n,))],
    )(*arrays)


def _as_rows(a, lead=0):
    shp = a.shape
    return a.reshape(shp[:lead] + (-1, shp[-1]))


def _sum4_call(x, name):
    _, r, c = x.shape
    tr = _pick(r, [t for t in (512, 256, 128, 64, 32, 16, 8) if 16 * t * c <= _BLOCK_BYTES] + [r])

    def body(x_ref, o_ref):
        part = [x_ref[j].astype(F32) for j in range(4)]
        o_ref[...] = (part[0] + part[1]) + (part[2] + part[3])

    return pl.pallas_call(
        body, name=name, grid=(r // tr,),
        in_specs=[pl.BlockSpec((4, tr, c), lambda i: (0, i, 0))],
        out_specs=pl.BlockSpec((tr, c), lambda i: (i, 0)),
        out_shape=jax.ShapeDtypeStruct((r, c), F32),
        compiler_params=_cparams(("parallel",), 48),
    )(x)


def _add_to_bf16_call(a, b, name):
    _, r, c = a.shape
    tr = _pick(r, [t for t in (512, 256, 128, 64, 32, 16) if 16 * t * c <= _BLOCK_BYTES] + [r])

    def body(a_ref, b_ref, o_ref):
        o_ref[...] = (a_ref[...] + b_ref[...]).astype(BF16)

    spec = pl.BlockSpec((4, tr, c), lambda i: (0, i, 0))
    return pl.pallas_call(
        body, name=name, grid=(r // tr,), in_specs=[spec, spec], out_specs=spec,
        out_shape=jax.ShapeDtypeStruct(a.shape, BF16), compiler_params=_cparams(("parallel",), 48),
    )(a, b)


def _adam_call(w, grads, m, v, name):
    r, c = w.shape
    n_g = len(grads)
    tr = _pick(r, [t for t in (512, 256, 128, 64, 32, 16, 8) if 4 * t * c <= _BLOCK_BYTES // 4] + [r])
    bc1 = 1.0 - ADAM_B1 ** ADAM_STEP
    bc2 = 1.0 - ADAM_B2 ** ADAM_STEP

    def body(*refs):
        w_ref, g_refs = refs[0], refs[1:1 + n_g]
        m_ref, v_ref, g_out, d_out, m_out, v_out = refs[1 + n_g:]
        g = g_refs[0][...]
        for g_ref in g_refs[1:]:
            g = g + g_ref[...]
        m_new = ADAM_B1 * m_ref[...] + (1.0 - ADAM_B1) * g
        v_new = ADAM_B2 * v_ref[...] + (1.0 - ADAM_B2) * (g * g)
        m_hat = m_new / bc1
        v_hat = v_new / bc2
        g_out[...] = g
        d_out[...] = -ADAM_LR * (m_hat / (jnp.sqrt(v_hat) + ADAM_EPS) + ADAM_WD * w_ref[...])
        m_out[...] = m_new
        v_out[...] = v_new

    spec = pl.BlockSpec((tr, c), lambda i: (i, 0))
    return pl.pallas_call(
        body, name=name, grid=(r // tr,), in_specs=[spec] * (3 + n_g), out_specs=[spec] * 4,
        out_shape=[jax.ShapeDtypeStruct((r, c), F32)] * 4,
        compiler_params=_cparams(("parallel",), 48),
    )(w, *grads, m, v)


_SHARDED = {"meta": (1, False), "w_in": (2, True), "s5_w_glu": (1, True), "ssd_conv_w": (2, False),
            "gdn_conv_w": (2, False), "w_branch": (3, True), "b_gate": (2, False), "w_out": (1, True)}


def _pack(arrs):
    flat = jnp.concatenate([a.reshape(-1) for a in arrs])
    n = flat.shape[0]
    rows = -(-n // (256 * LANES)) * 256
    return jnp.concatenate([flat, jnp.zeros((rows * LANES - n,), F32)]).reshape(rows, LANES)


def _unpack(packed, like):
    flat = packed.reshape(-1)
    out, off = [], 0
    for a in like:
        out.append(flat[off:off + a.size].reshape(a.shape))
        off += a.size
    return out


def kernel(x, meta, ln_in_g, ln_in_b, w_in, s5_a_re, s5_a_im, s5_log_step, s5_b_re, s5_b_im, s5_c_re, s5_c_im, s5_d, s5_w_glu, s5_b_glu, ssd_conv_w, ssd_conv_b, ssd_dt_bias, ssd_a_log, ssd_d, ssd_norm_g, gdn_conv_w, gdn_dt_bias, gdn_a_log, gdn_norm_g, w_branch, b_gate, w_out, ln_g, ln_b, loss_target, m_meta, m_ln_in_g, m_ln_in_b, m_w_in, m_s5_a_re, m_s5_a_im, m_s5_log_step, m_s5_b_re, m_s5_b_im, m_s5_c_re, m_s5_c_im, m_s5_d, m_s5_w_glu, m_s5_b_glu, m_ssd_conv_w, m_ssd_conv_b, m_ssd_dt_bias, m_ssd_a_log, m_ssd_d, m_ssd_norm_g, m_gdn_conv_w, m_gdn_dt_bias, m_gdn_a_log, m_gdn_norm_g, m_w_branch, m_b_gate, m_w_out, m_ln_g, m_ln_b, v_meta, v_ln_in_g, v_ln_in_b, v_w_in, v_s5_a_re, v_s5_a_im, v_s5_log_step, v_s5_b_re, v_s5_b_im, v_s5_c_re, v_s5_c_im, v_s5_d, v_s5_w_glu, v_s5_b_glu, v_ssd_conv_w, v_ssd_conv_b, v_ssd_dt_bias, v_ssd_a_log, v_ssd_d, v_ssd_norm_g, v_gdn_conv_w, v_gdn_dt_bias, v_gdn_a_log, v_gdn_norm_g, v_w_branch, v_b_gate, v_w_out, v_ln_g, v_ln_b):
    args = dict(locals())
    shards = {k: args[k] for k in _WEIGHT_KEYS}
    moms = {k: (args["m_" + k], args["v_" + k]) for k in _WEIGHT_KEYS}

    core = lax.axis_index("c")
    halves = lambda a: a.reshape((2, a.shape[0] // 2) + a.shape[1:])

    names = list(_SHARDED)
    sent = [lax.dynamic_index_in_dim(halves(shards[k]), core, 0, keepdims=False) for k in names]
    sent = [s.astype(BF16) if _SHARDED[k][1] else s for k, s in zip(names, sent)]
    gathered = _gather_two_level(sent, "gather_weights")
    my_chip = 2 * lax.axis_index("x") + lax.axis_index("y")
    own_block = lambda blocks, mine: lax.dynamic_update_index_in_dim(blocks, mine.astype(blocks.dtype), my_chip, 0)
    full = dict(shards)
    for k, g in zip(names, gathered):
        shp, ax = shards[k].shape, _SHARDED[k][0]
        g = own_block(g, halves(shards[k]))
        full[k] = jnp.concatenate([g[j].reshape(shp) for j in range(4)], axis=ax).astype(F32)

    loss, (grads, grad_x) = jax.value_and_grad(_local_loss, argnums=(0, 1))(full, x[0], loss_target[0])
    loss = lax.psum(loss, ("x", "y", "c"))

    blocks = []
    for k in names:
        per_chip = jnp.stack(jnp.split(grads[k], 4, axis=_SHARDED[k][0]), axis=0)
        blocks.append(jnp.moveaxis(_as_rows(per_chip.reshape((4, 2, -1) + per_chip.shape[2:]), 2), 1, 0))
    theirs = _core_exchange(blocks, True, "swap_halves")
    mine = [lax.dynamic_index_in_dim(b, core, 0, keepdims=False) for b in blocks]
    chip_sums = [_add_to_bf16_call(a, b, "sum_cores_" + k) for k, a, b in zip(names, mine, theirs)]
    arrived = _chip_exchange(chip_sums, True, "scatter_grads")
    arrived = [own_block(a, lax.dynamic_index_in_dim(s, my_chip, 0, keepdims=False)) for a, s in zip(arrived, chip_sums)]
    owned = [_sum4_call(a, "sum_chips_" + k) for k, a in zip(names, arrived)]
    others = _core_exchange(owned, False, "swap_owned")
    shared = [jnp.concatenate([jnp.where(core == 0, a, b), jnp.where(core == 0, b, a)], axis=0)
              for a, b in zip(owned, others)]

    small_names = [k for k in _WEIGHT_KEYS if k not in _SHARDED]
    packed = _pack([grads[k] for k in small_names])
    (packed4,) = _chip_exchange([packed], False, "gather_small_grads")
    small_sum = _sum4_call(own_block(packed4, packed), "sum_chips_small")
    (small_other,) = _core_exchange([small_sum], False, "swap_small")

    outs = {}
    for k, g in zip(names, shared):
        shp = shards[k].shape
        rows = _as_rows(shards[k]).shape
        res = _adam_call(_as_rows(shards[k]), [g.reshape(rows)], _as_rows(moms[k][0]), _as_rows(moms[k][1]), "adamw_" + k)
        outs[k] = [r.reshape(shp) for r in res]
    like = [shards[k] for k in small_names]
    res = _adam_call(_pack(like), [small_sum, small_other], _pack([moms[k][0] for k in small_names]),
                     _pack([moms[k][1] for k in small_names]), "adamw_small")
    for idx in range(4):
        for k, a in zip(small_names, _unpack(res[idx], like)):
            outs.setdefault(k, [None] * 4)[idx] = a

    result = [loss, grad_x[None]]
    for idx in range(4):
        result += [outs[k][idx] for k in _WEIGHT_KEYS]
    return tuple(result)
```

```python
import functools

import jax
import jax.numpy as jnp
from jax import lax
from jax.experimental import pallas as pl
from jax.experimental.pallas import tpu as pltpu

F32 = jnp.float32
BF16 = jnp.bfloat16
MESH = pl.DeviceIdType.MESH
F32_DOT = lax.Precision.HIGH

D_MODEL = 1024
DEPTH = 4
N_META = 16
CHUNK = 64
PAD = CHUNK - N_META
CONV_K = 4
HALO = 8
WIDTH = 768
S5_GROUPS, S5_GROUP, S5_STATE = 48, 16, 64
S5_BLOCKS = 6
S5_SEGMENTS = 8
SSD_HEADS, SSD_HEAD_DIM, SSD_GROUPS, SSD_STATE = 12, 64, 2, 128
SSD_PAIRS = 6
GDN_HEADS, GDN_DIM = 6, 128
LANES = 128
SMALL_ROWS = 24
ALPHA = (2 * DEPTH) ** 0.25
LN_EPS = 1e-5
P_S5U, P_S5Z, P_XBC, P_DT, P_SSDZ, P_QKV, P_GA, P_GB, P_GDNZ, P_GATE, P_END = (
    0, 768, 1536, 2816, 2828, 3596, 5900, 5906, 5912, 6680, 9752)
ADAM_LR, ADAM_B1, ADAM_B2, ADAM_EPS, ADAM_WD, ADAM_STEP = 0.001, 0.9, 0.999, 1e-08, 0.01, 10
NEG = -1e30


def _pick(n, cands):
    for c in cands:
        if n % c == 0:
            return c
    raise ValueError(f"no tile for {n} in {cands}")


def _cparams(sem, vmem_mb):
    return pltpu.CompilerParams(dimension_semantics=sem, vmem_limit_bytes=vmem_mb << 20)


_DIMS = {"nn": (((1,), (0,)), ((), ())), "nt": (((1,), (1,)), ((), ())), "tn": (((0,), (0,)), ((), ()))}


def _dot(a, b, mode, hi):
    if hi:
        prec = lax.Precision.HIGHEST if hi == "exact" else F32_DOT
        return lax.dot_general(a, b, _DIMS[mode], precision=prec, preferred_element_type=F32)
    return lax.dot_general(a.astype(BF16), b.astype(BF16), _DIMS[mode], preferred_element_type=F32)


@functools.partial(jax.custom_vjp, nondiff_argnums=(2, 3))
def _mm(a, b, mode="nn", hi=False):
    return _dot(a, b, mode, hi)


def _mm_fwd(a, b, mode, hi):
    return _dot(a, b, mode, hi), (a, b)


def _mm_bwd(mode, hi, res, g):
    a, b = res
    if mode == "nn":
        return _dot(g, b, "nt", hi), _dot(a, g, "tn", hi)
    if mode == "nt":
        return _dot(g, b, "nn", hi), _dot(g, a, "tn", hi)
    return _dot(b, g, "nt", hi), _dot(a, g, "nn", hi)


_mm.defvjp(_mm_fwd, _mm_bwd)


@functools.partial(jax.custom_vjp, nondiff_argnums=(1,))
def _roll_rows(x, k):
    return pltpu.roll(x, k % x.shape[0], 0)


def _roll_fwd(x, k):
    return _roll_rows(x, k), None


def _roll_bwd(k, _, g):
    return (_roll_rows(g, -k),)


_roll_rows.defvjp(_roll_fwd, _roll_bwd)


def _iota(shape, dim):
    return lax.broadcasted_iota(jnp.int32, shape, dim)


def _valid_rows(row0, n):
    return (row0 + _iota((n, 1), 0)) >= PAD


def _lane(x, idx):
    return jnp.sum(jnp.where(_iota(x.shape, 1) == idx, x, 0.0), axis=1, keepdims=True)


def _row(x, idx):
    return jnp.sum(jnp.where(_iota(x.shape, 0) == idx, x, 0.0), axis=0, keepdims=True)


def _layer_norm(z, g, b):
    mu = jnp.mean(z, axis=-1, keepdims=True)
    zc = z - mu
    var = jnp.mean(zc * zc, axis=-1, keepdims=True)
    return zc * lax.rsqrt(var + LN_EPS) * g + b


def _rms_norm(z, g):
    return z * lax.rsqrt(jnp.mean(z * z, axis=-1, keepdims=True) + LN_EPS) * g


def _causal_conv(halo, x, w, row0):
    t = x.shape[0]
    xc = jnp.concatenate([halo, x], axis=0)
    acc = None
    for j in range(CONV_K):
        term = _roll_rows(xc, CONV_K - 1 - j)[HALO:HALO + t] * _row(w, j)
        acc = term if acc is None else acc + term
    return acc, x[t - HALO:t]


def _tri(n, strict=False):
    r, c = _iota((n, n), 0), _iota((n, n), 1)
    return (r > c) if strict else (r >= c)


def _scan_op(step, name, *, tile, nb, row_kinds, param_kinds, carry_shapes, out_widths, vmem_mb=48):
    n_rows, n_par, n_car, n_out = len(row_kinds), len(param_kinds), len(carry_shapes), len(out_widths)

    def dims(rows):
        for k, a in zip(row_kinds, rows):
            if k in "bs":
                return a.shape[0], a.shape[0] // tile
        raise ValueError("need a row input")

    def row_spec(kind, a, rev, nt):
        ti = (lambda i: nt - 1 - i) if rev else (lambda i: i)
        if kind == "b":
            return pl.BlockSpec((tile, a.shape[1] // nb), lambda b, i: (ti(i), b))
        if kind == "s":
            return pl.BlockSpec((tile, a.shape[1]), lambda b, i: (ti(i), 0))
        return pl.BlockSpec((None, a.shape[1], a.shape[2]), lambda b, i: (ti(i), 0, 0))

    def par_spec(kind, a):
        if kind == "b":
            return pl.BlockSpec((None, a.shape[1], a.shape[2]), lambda b, i: (b, 0, 0))
        return pl.BlockSpec(a.shape, lambda b, i: (0, 0))

    def fwd_call(rows, params):
        length, nt = dims(rows)

        def body(*refs):
            r_in = refs[:n_rows]
            p_in = refs[n_rows:n_rows + n_par]
            o_out = refs[n_rows + n_par:n_rows + n_par + n_out]
            s_out = refs[n_rows + n_par + n_out:n_rows + n_par + n_out + n_car]
            c_scr = refs[n_rows + n_par + n_out + n_car:]
            b, i = pl.program_id(0), pl.program_id(1)

            if n_car:
                @pl.when(i == 0)
                def _():
                    for c in c_scr:
                        c[...] = jnp.zeros_like(c)

            cin = tuple(c[...] for c in c_scr)
            for s, c in zip(s_out, cin):
                s[...] = c
            new_c, outs = step(cin, tuple(r[...] for r in r_in), tuple(p[...] for p in p_in), b, i * tile)
            for c, v in zip(c_scr, new_c):
                c[...] = v
            for o, v in zip(o_out, outs):
                o[...] = v

        out_shape = [jax.ShapeDtypeStruct((length, nb * w), F32) for w in out_widths]
        out_shape += [jax.ShapeDtypeStruct((nb, nt) + tuple(s), F32) for s in carry_shapes]
        out_specs = [pl.BlockSpec((tile, w), lambda b, i: (i, b)) for w in out_widths]
        out_specs += [pl.BlockSpec((None, None) + tuple(s), lambda b, i: (b, i, 0, 0)) for s in carry_shapes]
        res = pl.pallas_call(
            body, name=name + "_fwd", grid=(nb, nt),
            in_specs=[row_spec(k, a, False, nt) for k, a in zip(row_kinds, rows)]
            + [par_spec(k, a) for k, a in zip(param_kinds, params)],
            out_specs=out_specs, out_shape=out_shape,
            scratch_shapes=[pltpu.VMEM(tuple(s), F32) for s in carry_shapes],
            compiler_params=_cparams(("arbitrary", "arbitrary"), vmem_mb),
        )(*rows, *params)
        return tuple(res[:n_out]), tuple(res[n_out:])

    def bwd_call(rows, params, saved, douts):
        length, nt = dims(rows)

        def body(*refs):
            k0 = 0
            r_in = refs[k0:k0 + n_rows]; k0 += n_rows
            p_in = refs[k0:k0 + n_par]; k0 += n_par
            s_in = refs[k0:k0 + n_car]; k0 += n_car
            g_in = refs[k0:k0 + n_out]; k0 += n_out
            dr_out = refs[k0:k0 + n_rows]; k0 += n_rows
            dp_out = refs[k0:k0 + n_par]; k0 += n_par
            dc_scr = refs[k0:]
            b, i = pl.program_id(0), pl.program_id(1)
            row0 = (nt - 1 - i) * tile

            @pl.when(i == 0)
            def _():
                for c in dc_scr:
                    c[...] = jnp.zeros_like(c)
                for p in dp_out:
                    p[...] = jnp.zeros_like(p)

            def f(c, r, p):
                return step(c, r, p, b, row0)

            _, vjp = jax.vjp(f, tuple(s[...] for s in s_in), tuple(r[...] for r in r_in),
                             tuple(p[...] for p in p_in))
            dc, dr, dp = vjp((tuple(c[...] for c in dc_scr), tuple(g[...] for g in g_in)))
            for c, v in zip(dc_scr, dc):
                c[...] = v
            for o, v in zip(dr_out, dr):
                o[...] = v
            for o, v in zip(dp_out, dp):
                o[...] += v

        rev = lambda i: nt - 1 - i
        in_specs = [row_spec(k, a, True, nt) for k, a in zip(row_kinds, rows)]
        in_specs += [par_spec(k, a) for k, a in zip(param_kinds, params)]
        in_specs += [pl.BlockSpec((None, None) + tuple(s), lambda b, i: (b, rev(i), 0, 0)) for s in carry_shapes]
        in_specs += [pl.BlockSpec((tile, w), lambda b, i: (rev(i), b)) for w in out_widths]
        out_shape, out_specs = [], []
        for k, a in zip(row_kinds, rows):
            if k == "b":
                out_shape.append(jax.ShapeDtypeStruct(a.shape, F32))
                out_specs.append(pl.BlockSpec((tile, a.shape[1] // nb), lambda b, i: (rev(i), b)))
            elif k == "s":
                out_shape.append(jax.ShapeDtypeStruct((nb,) + a.shape, F32))
                out_specs.append(pl.BlockSpec((None, tile, a.shape[1]), lambda b, i: (b, rev(i), 0)))
            else:
                out_shape.append(jax.ShapeDtypeStruct((nb,) + a.shape, F32))
                out_specs.append(pl.BlockSpec((None, None, a.shape[1], a.shape[2]), lambda b, i: (b, rev(i), 0, 0)))
        for k, a in zip(param_kinds, params):
            shp = a.shape[1:] if k == "b" else a.shape
            out_shape.append(jax.ShapeDtypeStruct((nb,) + tuple(shp), F32))
            out_specs.append(pl.BlockSpec((None,) + tuple(shp), lambda b, i: (b, 0, 0)))
        res = pl.pallas_call(
            body, name=name + "_bwd", grid=(nb, nt), in_specs=in_specs, out_specs=out_specs, out_shape=out_shape,
            scratch_shapes=[pltpu.VMEM(tuple(s), F32) for s in carry_shapes],
            compiler_params=_cparams(("arbitrary", "arbitrary"), vmem_mb),
        )(*rows, *params, *saved, *douts)
        fold = (lambda a: a[0]) if nb == 1 else (lambda a: jnp.sum(a, axis=0))
        drows = tuple(r if k == "b" else fold(r) for k, r in zip(row_kinds, res[:n_rows]))
        dpars = tuple(p if k == "b" else fold(p) for k, p in zip(param_kinds, res[n_rows:]))
        return drows, dpars

    @jax.custom_vjp
    def op(rows, params):
        return fwd_call(rows, params)[0]

    def op_fwd(rows, params):
        outs, saved = fwd_call(rows, params)
        return outs, (rows, params, saved)

    def op_bwd(res, douts):
        rows, params, saved = res
        return bwd_call(rows, params, saved, tuple(douts))

    op.defvjp(op_fwd, op_bwd)
    return op


_TM = (832, 512, 256, 128)
_TN = (768, 640, 512, 384, 256, 128)


def _mm_fwd_call(x, w, name):
    m, k = x.shape
    n = w.shape[1]
    tm, tn = _pick(m, _TM), _pick(n, _TN)

    def body(x_ref, w_ref, o_ref):
        o_ref[...] = _dot(x_ref[...], w_ref[...], "nn", False)

    return pl.pallas_call(
        body, name=name, grid=(n // tn, m // tm),
        in_specs=[pl.BlockSpec((tm, k), lambda j, i: (i, 0)), pl.BlockSpec((k, tn), lambda j, i: (0, j))],
        out_specs=pl.BlockSpec((tm, tn), lambda j, i: (i, j)),
        out_shape=jax.ShapeDtypeStruct((m, n), F32),
        compiler_params=_cparams(("parallel", "parallel"), 48),
    )(x, w)


def _mm_dx_call(g, w, name):
    m, n = g.shape
    k = w.shape[0]
    tm, tn = _pick(m, _TM), _pick(n, _TN)

    def body(g_ref, w_ref, o_ref):
        @pl.when(pl.program_id(1) == 0)
        def _():
            o_ref[...] = jnp.zeros_like(o_ref)

        o_ref[...] += _dot(g_ref[...], w_ref[...], "nt", False)

    return pl.pallas_call(
        body, name=name, grid=(m // tm, n // tn),
        in_specs=[pl.BlockSpec((tm, tn), lambda i, j: (i, j)), pl.BlockSpec((k, tn), lambda i, j: (0, j))],
        out_specs=pl.BlockSpec((tm, k), lambda i, j: (i, 0)),
        out_shape=jax.ShapeDtypeStruct((m, k), F32),
        compiler_params=_cparams(("parallel", "arbitrary"), 48),
    )(g, w)


def _mm_dw_call(x, g, name):
    m, k = x.shape
    n = g.shape[1]
    tm, tn = _pick(m, _TM), _pick(n, _TN)

    def body(x_ref, g_ref, o_ref):
        @pl.when(pl.program_id(1) == 0)
        def _():
            o_ref[...] = jnp.zeros_like(o_ref)

        o_ref[...] += _dot(x_ref[...], g_ref[...], "tn", False)

    return pl.pallas_call(
        body, name=name, grid=(n // tn, m // tm),
        in_specs=[pl.BlockSpec((tm, k), lambda j, i: (i, 0)), pl.BlockSpec((tm, tn), lambda j, i: (i, j))],
        out_specs=pl.BlockSpec((k, tn), lambda j, i: (0, j)),
        out_shape=jax.ShapeDtypeStruct((k, n), F32),
        compiler_params=_cparams(("parallel", "arbitrary"), 48),
    )(x, g)


def _dense(name):
    @jax.custom_vjp
    def op(x, w, xb, wb):
        return _mm_fwd_call(xb, wb, name + "_fwd")

    def op_fwd(x, w, xb, wb):
        return _mm_fwd_call(xb, wb, name + "_fwd"), (xb, wb)

    def op_bwd(res, g):
        xb, wb = res
        return _mm_dx_call(g, wb, name + "_dx"), _mm_dw_call(xb, g, name + "_dw"), jnp.zeros_like(xb), jnp.zeros_like(wb)

    op.defvjp(op_fwd, op_bwd)
    return op


def _bf16_copy(v):
    return lax.stop_gradient(v).astype(BF16)


def _ln_in_step(c, rows, params, b, row0):
    (z,), (g, bb) = rows, params
    return (), (jnp.where(_valid_rows(row0, z.shape[0]), _layer_norm(z, g, bb), 0.0),)


def _ln_res_step(c, rows, params, b, row0):
    (h, o), (g, bb) = rows, params
    return (), (jnp.where(_valid_rows(row0, h.shape[0]), _layer_norm(ALPHA * h + o, g, bb), 0.0),)


def _s5_prep_step(c, rows, params, b, row0):
    a_re, a_im, log_step, b_re, b_im = rows
    lam_re = jnp.minimum(a_re, -1e-4)
    lam_im = a_im
    step = jnp.exp(log_step)
    mag = jnp.exp(lam_re * step)
    abar_re, abar_im = mag * jnp.cos(lam_im * step), mag * jnp.sin(lam_im * step)
    den = lam_re * lam_re + lam_im * lam_im
    nr, ni = abar_re - 1.0, abar_im
    coef_re = (nr * lam_re + ni * lam_im) / den
    coef_im = (ni * lam_re - nr * lam_im) / den
    return (), (abar_re, abar_im, coef_re * b_re - coef_im * b_im, coef_re * b_im + coef_im * b_re)


def _s5_scan_step(c, rows, params, b, row0):
    (c_re, c_im), (u,) = c, rows
    bd_re, bd_im, a_re, a_im, cd_re, cd_im = params
    t = u.shape[0]
    steps = t // S5_SEGMENTS
    bu_re, bu_im = _mm(u, bd_re), _mm(u, bd_im)
    a_re, a_im = (jnp.broadcast_to(v, (S5_SEGMENTS, v.shape[1])) for v in (a_re, a_im))
    at = lambda v, i: v[S5_SEGMENTS * i:S5_SEGMENTS * (i + 1)]
    s_re, s_im = at(bu_re, 0), at(bu_im, 0)
    p_re, p_im = a_re, a_im
    local, power = [(s_re, s_im)], [(p_re, p_im)]
    for i in range(1, steps):
        s_re, s_im = a_re * s_re - a_im * s_im + at(bu_re, i), a_re * s_im + a_im * s_re + at(bu_im, i)
        p_re, p_im = a_re * p_re - a_im * p_im, a_re * p_im + a_im * p_re
        local.append((s_re, s_im))
        power.append((p_re, p_im))
    seg = _iota((S5_SEGMENTS, 1), 0)
    in_re = jnp.where(seg == 0, c_re, _roll_rows(s_re, 1))
    in_im = jnp.where(seg == 0, c_im, _roll_rows(s_im, 1))
    q_re, q_im = p_re, p_im
    d = 1
    while d < S5_SEGMENTS:
        keep = seg >= d
        sh_re = jnp.where(keep, _roll_rows(in_re, d), 0.0)
        sh_im = jnp.where(keep, _roll_rows(in_im, d), 0.0)
        in_re, in_im = in_re + q_re * sh_re - q_im * sh_im, in_im + q_re * sh_im + q_im * sh_re
        q_re, q_im = q_re * q_re - q_im * q_im, 2.0 * q_re * q_im
        d *= 2
    full = [(l_re + w_re * in_re - w_im * in_im, l_im + w_re * in_im + w_im * in_re)
            for (l_re, l_im), (w_re, w_im) in zip(local, power)]
    y = _mm(jnp.concatenate([f[0] for f in full], axis=0), cd_re) - _mm(jnp.concatenate([f[1] for f in full], axis=0), cd_im)
    return (_row(full[-1][0], S5_SEGMENTS - 1), _row(full[-1][1], S5_SEGMENTS - 1)), (y,)


def _interleave(v, tile, inverse=False):
    length, width = v.shape
    shape = (length // tile, tile // S5_SEGMENTS, S5_SEGMENTS) if inverse else (length // tile, S5_SEGMENTS, tile // S5_SEGMENTS)
    return v.reshape(shape + (width,)).transpose(0, 2, 1, 3).reshape(length, width)


def _s5_post_step(c, rows, params, b, row0):
    (y, u, z), (d, w_glu, b_glu) = rows, params
    v = jax.nn.gelu(y + d * u)
    v = v * jax.nn.sigmoid(_mm(v, w_glu) + b_glu)
    return (), (v * jax.nn.silu(z),)


def _ssd_step(c, rows, params, b, row0):
    halo, state = c
    xbc_raw, z, small, small_t = rows
    cw, cb, d_l, bias_l, alog_l, bias_c, alog_c, norm_g = params
    t = xbc_raw.shape[0]
    grp = SSD_GROUPS * SSD_STATE
    valid = _valid_rows(row0, t)
    conv, halo2 = _causal_conv(halo, xbc_raw, cw, row0)
    act = jnp.where(valid, jax.nn.silu(conv + cb), 0.0)
    low = _iota((1, LANES), 1) < SSD_HEAD_DIM
    dt_all = jnp.where(valid, jax.nn.softplus(small + bias_l), 0.0)
    a_all = -jnp.exp(alog_l)
    valid_t = (row0 + _iota((1, t), 1)) >= PAD
    dta_t = jnp.where(valid_t, jax.nn.softplus(small_t + bias_c), 0.0) * (-jnp.exp(alog_c))
    acum_t = _mm(dta_t, jnp.where(_iota((t, t), 0) <= _iota((t, t), 1), 1.0, 0.0), "nn", True)
    causal = _tri(t)
    acum_all = _mm(jnp.where(causal, 1.0, 0.0), dt_all * a_all, "nn", True)
    last_all = _row(acum_all, t - 1)
    low_rows = _iota((LANES, 1), 0) < SSD_HEAD_DIM
    pairs, groups = range(SSD_PAIRS), range(SSD_GROUPS)
    grp_of = [p // (SSD_PAIRS // SSD_GROUPS) for p in pairs]
    bs = [act[:, WIDTH + g * SSD_STATE:WIDTH + (g + 1) * SSD_STATE] for g in groups]
    cs = [act[:, WIDTH + grp + g * SSD_STATE:WIDTH + grp + (g + 1) * SSD_STATE] for g in groups]
    scores = [_mm(cs[g], bs[g], "nt") for g in groups]
    per_lane = lambda v, p: jnp.where(low, v[:, 2 * p:2 * p + 1], v[:, 2 * p + 1:2 * p + 2])
    dt_l = [per_lane(dt_all, p) for p in pairs]
    acum_l = [per_lane(acum_all, p) for p in pairs]
    last_l = [per_lane(last_all, p) for p in pairs]
    st = [state[p * LANES:(p + 1) * LANES] for p in pairs]
    xd = [act[:, p * LANES:(p + 1) * LANES] * dt_l[p] for p in pairs]
    decay = [jnp.exp(jnp.where(causal, acum_all[:, h:h + 1] - _row(acum_t, h), NEG)) for h in range(SSD_HEADS)]
    y_lo = [_mm(scores[grp_of[p]] * decay[2 * p], jnp.where(low, xd[p], 0.0)) for p in pairs]
    y_hi = [_mm(scores[grp_of[p]] * decay[2 * p + 1], jnp.where(low, 0.0, xd[p])) for p in pairs]
    y_off = [_mm(cs[grp_of[p]], st[p], "nt") * jnp.exp(acum_l[p]) for p in pairs]
    new_st = [_mm(xd[p] * jnp.exp(last_l[p] - acum_l[p]), bs[grp_of[p]], "tn") for p in pairs]
    cd = jnp.exp(last_all)
    new_st = [st[p] * jnp.where(low_rows, cd[:, 2 * p:2 * p + 1], cd[:, 2 * p + 1:2 * p + 2]) + new_st[p] for p in pairs]
    y = jnp.concatenate([y_lo[p] + y_hi[p] + y_off[p] for p in pairs], axis=1) + act[:, :WIDTH] * d_l
    out = _rms_norm(y * jax.nn.silu(z), norm_g)
    return (halo2, jnp.concatenate(new_st, axis=0)), (out,)


@jax.custom_vjp
def _unit_lower_inverse(mats):
    return _neumann_inverse(mats)


def _inverse_fwd(mats):
    inv = _neumann_inverse(mats)
    return inv, inv


def _inverse_bwd(inv, g):
    left = [_dot(t, d, "tn", True) for t, d in zip(inv, g)]
    return ([-_dot(l, t, "nt", True) for l, t in zip(left, inv)],)


_unit_lower_inverse.defvjp(_inverse_fwd, _inverse_bwd)


def _neumann_inverse(mats):
    n = mats[0].shape[0]
    eye = jnp.where(_iota((n, n), 0) == _iota((n, n), 1), 1.0, 0.0)
    inv = [eye - a for a in mats]
    p = [_mm(a, a, "nn", True) for a in mats]
    k = 2
    while k < n:
        inv = [i + _mm(i, q, "nn", True) for i, q in zip(inv, p)]
        k *= 2
        if k < n:
            p = [_mm(q, q, "nn", True) for q in p]
    return inv


def _gdn_step(c, rows, params, b, row0):
    halo, state = c
    qkv_raw, z, small, small_t = rows
    cw, bias_l, alog_l, bias_c, alog_c, norm_g = params
    t = qkv_raw.shape[0]
    valid = _valid_rows(row0, t)
    conv, halo2 = _causal_conv(halo, qkv_raw, cw, row0)
    act = jnp.where(valid, jax.nn.silu(conv), 0.0)
    beta_all = jnp.where(valid, jax.nn.sigmoid(small), 0.0)
    g_all = jnp.where(valid, -jnp.exp(alog_l) * jax.nn.softplus(small + bias_l), 0.0)
    valid_t = (row0 + _iota((1, t), 1)) >= PAD
    g_t = jnp.where(valid_t, -jnp.exp(alog_c) * jax.nn.softplus(small_t + bias_c), 0.0)
    causal, strict = _tri(t), _tri(t, True)
    gcum_all = _mm(jnp.where(causal, 1.0, 0.0), g_all, "nn", True)
    gcum_t = _mm(g_t, jnp.where(_iota((t, t), 0) <= _iota((t, t), 1), 1.0, 0.0), "nn", True)
    heads = range(GDN_HEADS)
    part = lambda h, n: act[:, n * WIDTH + h * GDN_DIM:n * WIDTH + (h + 1) * GDN_DIM]
    unit = lambda x: x * lax.rsqrt(jnp.sum(x * x, axis=-1, keepdims=True) + 1e-6)
    q = [unit(part(h, 0)) * (GDN_DIM ** -0.5) for h in heads]
    k = [unit(part(h, 1)) for h in heads]
    st = [state[h * GDN_DIM:(h + 1) * GDN_DIM] for h in heads]
    ia = [SSD_HEADS + h for h in heads]
    beta = [beta_all[:, ia[h] + GDN_HEADS:ia[h] + GDN_HEADS + 1] for h in heads]
    gcum = [gcum_all[:, ia[h]:ia[h] + 1] for h in heads]
    gamma = [jnp.exp(jnp.where(causal, gcum[h] - _row(gcum_t, ia[h]), NEG)) for h in heads]
    egc = [jnp.exp(gcum[h]) for h in heads]
    a_mat = [jnp.where(strict, _mm(k[h], k[h], "nt") * gamma[h] * beta[h], 0.0) for h in heads]
    inv = _unit_lower_inverse(a_mat)
    rhs = [jnp.concatenate([part(h, 2) * beta[h], k[h] * (beta[h] * egc[h])], axis=1) for h in heads]
    sol = [_mm(inv[h], rhs[h], "nn", True) for h in heads]
    attn = [_mm(q[h], k[h], "nt") * gamma[h] for h in heads]
    from_state = [_mm(jnp.concatenate([sol[h][:, GDN_DIM:], q[h] * egc[h]], axis=0), st[h]) for h in heads]
    v_new = [sol[h][:, :GDN_DIM] - from_state[h][:t] for h in heads]
    o = [from_state[h][t:] + _mm(attn[h], v_new[h]) for h in heads]
    glast = [_row(gcum[h], t - 1) for h in heads]
    new_st = [st[h] * jnp.exp(glast[h]) + _mm(k[h] * jnp.exp(glast[h] - gcum[h]), v_new[h], "tn") for h in heads]
    out = jnp.concatenate([_rms_norm(o[h], norm_g) for h in heads], axis=1) * jax.nn.silu(z)
    return (halo2, jnp.concatenate(new_st, axis=0)), (out,)


def _gate_merge_step(c, rows, params, b, row0):
    (oa, ob, oc, gl), (bg,) = rows, params
    acc = None
    for n, o in enumerate((oa, ob, oc)):
        term = jax.nn.sigmoid(gl[:, n * D_MODEL:(n + 1) * D_MODEL] + bg[:, n * D_MODEL:(n + 1) * D_MODEL]) * o
        acc = term if acc is None else acc + term
    return (), (acc,)


def _loss_tile(n):
    return _pick(n, (512, 256, 128, 64))


def _loss_fwd_call(y, tgt):
    n, d = y.shape
    tile = _loss_tile(n)

    def body(y_ref, t_ref, o_ref):
        @pl.when(pl.program_id(0) == 0)
        def _():
            o_ref[...] = jnp.zeros_like(o_ref)

        e = y_ref[...] - t_ref[...]
        o_ref[...] += jnp.sum(jnp.sum(e * e, axis=1, keepdims=True), axis=0, keepdims=True) * (0.5 / d)

    out = pl.pallas_call(
        body, name="loss_fwd", grid=(n // tile,),
        in_specs=[pl.BlockSpec((tile, d), lambda i: (i, 0)), pl.BlockSpec((tile, d), lambda i: (i, 0))],
        out_specs=pl.BlockSpec((8, LANES), lambda i: (0, 0)),
        out_shape=jax.ShapeDtypeStruct((8, LANES), F32),
        compiler_params=_cparams(("arbitrary",), 32),
    )(y, tgt)
    return out[0, 0]


def _loss_bwd_call(y, tgt, g):
    n, d = y.shape
    tile = _loss_tile(n)

    def body(y_ref, t_ref, g_ref, o_ref):
        o_ref[...] = (y_ref[...] - t_ref[...]) * (g_ref[...][0:1, 0:1] * (1.0 / d))

    return pl.pallas_call(
        body, name="loss_bwd", grid=(n // tile,),
        in_specs=[pl.BlockSpec((tile, d), lambda i: (i, 0)), pl.BlockSpec((tile, d), lambda i: (i, 0)),
                  pl.BlockSpec((8, LANES), lambda i: (0, 0))],
        out_specs=pl.BlockSpec((tile, d), lambda i: (i, 0)),
        out_shape=jax.ShapeDtypeStruct((n, d), F32),
        compiler_params=_cparams(("parallel",), 32),
    )(y, tgt, jnp.broadcast_to(g, (8, LANES)).astype(F32))


@jax.custom_vjp
def _loss_op(y, tgt):
    return _loss_fwd_call(y, tgt)


def _loss_op_fwd(y, tgt):
    return _loss_fwd_call(y, tgt), (y, tgt)


def _loss_op_bwd(res, g):
    y, tgt = res
    return _loss_bwd_call(y, tgt, g), jnp.zeros_like(tgt)


_loss_op.defvjp(_loss_op_fwd, _loss_op_bwd)


def _rowwise(step, name, tile, n_rows, n_params, out_widths, vmem_mb=48):
    return _scan_op(step, name, tile=tile, nb=1, row_kinds="s" * n_rows, param_kinds="s" * n_params,
                    carry_shapes=(), out_widths=out_widths, vmem_mb=vmem_mb)


def _block_diag(x, nblk):
    bsz, _, r, c = x.shape
    eye = jnp.eye(nblk, dtype=x.dtype)
    return jnp.einsum("bgrc,gh->bgrhc", x, eye).reshape(bsz, nblk * r, nblk * c)


_PROJ = {"s5u": (P_S5U, P_S5Z), "s5z": (P_S5Z, P_XBC), "xbc": (P_XBC, P_DT), "ssdz": (P_SSDZ, P_QKV),
         "qkv": (P_QKV, P_GA), "gdnz": (P_GDNZ, P_GATE), "gate": (P_GATE, P_END)}


def _prepare(weights, compute):
    w = weights
    zeros = lambda *shape: jnp.zeros((DEPTH,) + shape, F32)
    row3 = lambda v: v.reshape(DEPTH, 1, -1)
    p = {}
    for pre, w_in in (("w_", w["w_in"]), ("wb_", compute["w_in"])):
        seg = lambda a, bnd: w_in[:, :, a:bnd]
        p.update({pre + k: seg(a, bnd) for k, (a, bnd) in _PROJ.items()})
        p[pre + "small"] = jnp.concatenate(
            [seg(P_DT, P_SSDZ), seg(P_GA, P_GDNZ), zeros(D_MODEL, LANES - SMALL_ROWS).astype(w_in.dtype)], axis=2)
    p.update(wb_branch=compute["w_branch"], wb_out=compute["w_out"])

    rows, wide = DEPTH * S5_GROUPS, S5_STATE * S5_GROUP
    flat = lambda v: v.reshape(rows, -1)
    rep = lambda v: jnp.repeat(flat(v), S5_GROUP, axis=1)
    prep = _rowwise(_s5_prep_step, "s5_prep", S5_GROUPS, 5, 0, (wide,) * 4)
    abar_re, abar_im, bbar_re, bbar_im = prep(
        (rep(w["s5_a_re"]), rep(w["s5_a_im"]), jnp.broadcast_to(flat(w["s5_log_step"]), (rows, wide)),
         flat(w["s5_b_re"]), flat(w["s5_b_im"])), ())
    gpb = S5_GROUPS // S5_BLOCKS
    lanes, chans, nblk = gpb * S5_STATE, gpb * S5_GROUP, DEPTH * S5_BLOCKS
    to_bd = lambda bb: _block_diag(bb.reshape(nblk, gpb, S5_STATE, S5_GROUP).transpose(0, 1, 3, 2), gpb).reshape(
        DEPTH, S5_BLOCKS, chans, lanes)
    to_cd = lambda cc: _block_diag(cc.reshape(nblk, gpb, S5_GROUP, S5_STATE).transpose(0, 1, 3, 2), gpb).reshape(
        DEPTH, S5_BLOCKS, lanes, chans)
    to_a = lambda a: a[:, ::S5_GROUP].reshape(DEPTH, S5_BLOCKS, 1, lanes)
    p.update(s5_bd_re=to_bd(bbar_re), s5_bd_im=to_bd(bbar_im), s5_a_re=to_a(abar_re), s5_a_im=to_a(abar_im),
             s5_cd_re=to_cd(w["s5_c_re"]), s5_cd_im=to_cd(w["s5_c_im"]),
             s5_d=row3(w["s5_d"]), s5_w_glu=w["s5_w_glu"], s5_b_glu=row3(w["s5_b_glu"]))

    bias = jnp.concatenate([w["ssd_dt_bias"], w["gdn_dt_bias"]], axis=1)
    alog = jnp.concatenate([w["ssd_a_log"], w["gdn_a_log"]], axis=1)
    on_lanes = lambda v: jnp.concatenate([v, zeros(LANES - v.shape[1])], axis=1).reshape(DEPTH, 1, LANES)
    on_rows = lambda v: jnp.concatenate([v, zeros(SMALL_ROWS - v.shape[1])], axis=1).reshape(DEPTH, SMALL_ROWS, 1)
    pad_w = lambda cw: jnp.concatenate([cw, zeros(HALO - CONV_K, cw.shape[2])], axis=1)
    p.update(bias_l=on_lanes(bias), alog_l=on_lanes(alog), bias_c=on_rows(bias), alog_c=on_rows(alog),
             ssd_cw=pad_w(w["ssd_conv_w"]), ssd_cb=row3(w["ssd_conv_b"]),
             ssd_d=row3(jnp.repeat(w["ssd_d"], SSD_HEAD_DIM, axis=1)), ssd_norm_g=row3(w["ssd_norm_g"]),
             gdn_cw=pad_w(w["gdn_conv_w"]), gdn_norm_g=row3(w["gdn_norm_g"]),
             w_branch=w["w_branch"], b_gate=row3(w["b_gate"]), w_out=w["w_out"], ln_g=row3(w["ln_g"]), ln_b=row3(w["ln_b"]))
    return p


def _layer(h, p):
    length = h.shape[0]
    nt = length // CHUNK
    t_row = _pick(length, (208, 128))
    t_s5 = _pick(length, (320, 128))
    hb = _bf16_copy(h)
    proj = {k: _dense("proj_" + k)(h, p["w_" + k], hb, p["wb_" + k]) for k in list(_PROJ) + ["small"]}
    small = proj["small"]
    small_t = small[:, :SMALL_ROWS].reshape(nt, CHUNK, SMALL_ROWS).transpose(0, 2, 1)

    lanes = p["s5_a_re"].shape[-1]
    s5_scan = _scan_op(_s5_scan_step, "s5_scan", tile=t_s5, nb=S5_BLOCKS, row_kinds="b", param_kinds="bbbbbb",
                       carry_shapes=((1, lanes), (1, lanes)), out_widths=(LANES,))
    (y_ssm,) = s5_scan((_interleave(proj["s5u"], t_s5),), (p["s5_bd_re"], p["s5_bd_im"], p["s5_a_re"], p["s5_a_im"],
                                                           p["s5_cd_re"], p["s5_cd_im"]))
    y_ssm = _interleave(y_ssm, t_s5, inverse=True)
    s5_post = _rowwise(_s5_post_step, "s5_post", t_row, 3, 3, (WIDTH,))
    (y_a,) = s5_post((y_ssm, proj["s5u"], proj["s5z"]), (p["s5_d"], p["s5_w_glu"], p["s5_b_glu"]))

    scalars = (p["bias_l"], p["alog_l"], p["bias_c"], p["alog_c"])
    ssd = _scan_op(_ssd_step, "ssd_scan", tile=CHUNK, nb=1, row_kinds="ssst", param_kinds="s" * 8,
                   carry_shapes=((HALO, P_DT - P_XBC), (SSD_HEADS * SSD_HEAD_DIM, SSD_STATE)), out_widths=(WIDTH,))
    (y_b,) = ssd((proj["xbc"], proj["ssdz"], small, small_t),
                 (p["ssd_cw"], p["ssd_cb"], p["ssd_d"]) + scalars + (p["ssd_norm_g"],))
    gdn = _scan_op(_gdn_step, "gdn_scan", tile=CHUNK, nb=1, row_kinds="ssst", param_kinds="s" * 6,
                   carry_shapes=((HALO, P_GA - P_QKV), (GDN_HEADS * GDN_DIM, GDN_DIM)), out_widths=(WIDTH,))
    (y_c,) = gdn((proj["qkv"], proj["gdnz"], small, small_t), (p["gdn_cw"],) + scalars + (p["gdn_norm_g"],))

    branch = [_dense("branch_" + n)(y, p["w_branch"][i], _bf16_copy(y), p["wb_branch"][i])
              for i, (n, y) in enumerate(zip("abc", (y_a, y_b, y_c)))]
    merge = _rowwise(_gate_merge_step, "gate_merge", t_row, 4, 1, (D_MODEL,))
    (merged,) = merge((*branch, proj["gate"]), (p["b_gate"],))
    out = _dense("out_proj")(merged, p["w_out"], _bf16_copy(merged), p["wb_out"])
    ln = _rowwise(_ln_res_step, "ln_res", t_row, 2, 2, (D_MODEL,))
    (h_new,) = ln((h, out), (p["ln_g"], p["ln_b"]))
    return h_new


_LAYER_KEYS = ("w_in", "s5_a_re", "s5_a_im", "s5_log_step", "s5_b_re", "s5_b_im", "s5_c_re", "s5_c_im", "s5_d",
               "s5_w_glu", "s5_b_glu", "ssd_conv_w", "ssd_conv_b", "ssd_dt_bias", "ssd_a_log", "ssd_d", "ssd_norm_g",
               "gdn_conv_w", "gdn_dt_bias", "gdn_a_log", "gdn_norm_g", "w_branch", "b_gate", "w_out", "ln_g", "ln_b")
_WEIGHT_KEYS = ("meta", "ln_in_g", "ln_in_b") + _LAYER_KEYS


def _local_loss(weights, x, target, compute):
    seq = x.shape[0]
    hcat = jnp.concatenate([jnp.zeros((PAD, D_MODEL), F32), weights["meta"], x], axis=0)
    length = hcat.shape[0]
    ln_in = _rowwise(_ln_in_step, "ln_in", _pick(length, (416, 256, 128)), 1, 2, (D_MODEL,))
    (h,) = ln_in((hcat,), (weights["ln_in_g"].reshape(1, -1), weights["ln_in_b"].reshape(1, -1)))

    prepared = _prepare(weights, compute)
    for layer in range(DEPTH):
        h = _layer(h, {k: v[layer] for k, v in prepared.items()})
    return _loss_op(h[length - seq:], target)


_ANY = pl.BlockSpec(memory_space=pl.ANY)
_BLOCK_BYTES = 4 << 20


def _chip_exchange(arrays, all_to_all, name):
    n = len(arrays)

    def body(*refs):
        ins, outs = refs[:n], refs[n:2 * n]
        send_sems, recv_sems = refs[2 * n:]
        mx, my, mc = lax.axis_index("x"), lax.axis_index("y"), lax.axis_index("c")
        me = 2 * mx + my
        peers = [(1 - mx, my), (mx, 1 - my), (1 - mx, 1 - my)]
        sends = []
        for a, (src, dst) in enumerate(zip(ins, outs)):
            for k, (px, py) in enumerate(peers):
                cp = pltpu.make_async_remote_copy(
                    src_ref=src.at[2 * px + py] if all_to_all else src, dst_ref=dst.at[me],
                    send_sem=send_sems.at[a, k], recv_sem=recv_sems.at[a, k],
                    device_id=(px, py, mc), device_id_type=MESH)
                cp.start()
                sends.append(cp)
        for a, (src, dst) in enumerate(zip(ins, outs)):
            for k, (px, py) in enumerate(peers):
                pltpu.make_async_remote_copy(
                    src_ref=src.at[me] if all_to_all else src, dst_ref=dst.at[2 * px + py],
                    send_sem=send_sems.at[a, k], recv_sem=recv_sems.at[a, k],
                    device_id=(px, py, mc), device_id_type=MESH).wait_recv()
        for cp in sends:
            cp.wait_send()

    out_shape = [jax.ShapeDtypeStruct(a.shape if all_to_all else (4,) + a.shape, a.dtype) for a in arrays]
    return pl.pallas_call(
        body, name=name, in_specs=[_ANY] * n, out_specs=[_ANY] * n, out_shape=out_shape,
        scratch_shapes=[pltpu.SemaphoreType.DMA((n, 3)), pltpu.SemaphoreType.DMA((n, 3))],
    )(*arrays)


def _gather_two_level(arrays, name):
    n = len(arrays)

    def body(*refs):
        ins, outs = refs[:n], refs[n:2 * n]
        send_sems, recv_sems = refs[2 * n:]
        mx, my, mc = lax.axis_index("x"), lax.axis_index("y"), lax.axis_index("c")
        me = 2 * mx + my
        sibling = (mx, my, 1 - mc)
        chips = [(1 - mx, my), (mx, 1 - my), (1 - mx, 1 - my)]

        def copy(a, k, src, chip, core, to):
            return pltpu.make_async_remote_copy(
                src_ref=src, dst_ref=outs[a].at[chip, core], send_sem=send_sems.at[a, k], recv_sem=recv_sems.at[a, k],
                device_id=to, device_id_type=MESH)

        sends = [copy(a, j, ins[a], me, mc, (px, py, mc)) for a in range(n) for j, (px, py) in enumerate(chips)]
        for cp in sends:
            cp.start()
        passed = []
        for a in range(n):
            for j, (px, py) in enumerate(chips):
                chip = 2 * px + py
                copy(a, j, ins[a], chip, mc, sibling).wait_recv()
                cp = copy(a, 3 + j, outs[a].at[chip, mc], chip, mc, sibling)
                cp.start()
                passed.append(cp)
        for a in range(n):
            for j, (px, py) in enumerate(chips):
                copy(a, 3 + j, ins[a], 2 * px + py, 1 - mc, sibling).wait_recv()
        for cp in sends + passed:
            cp.wait_send()

    return pl.pallas_call(
        body, name=name, in_specs=[_ANY] * n, out_specs=[_ANY] * n,
        out_shape=[jax.ShapeDtypeStruct((4, 2) + a.shape, a.dtype) for a in arrays],
        scratch_shapes=[pltpu.SemaphoreType.DMA((n, 6)), pltpu.SemaphoreType.DMA((n, 6))],
    )(*arrays)


def _core_exchange(arrays, other_half, name):
    n = len(arrays)

    def body(*refs):
        ins, outs = refs[:n], refs[n:2 * n]
        send_sems, recv_sems = refs[2 * n:]
        mc = lax.axis_index("c")
        sibling = (lax.axis_index("x"), lax.axis_index("y"), 1 - mc)
        copies = [pltpu.make_async_remote_copy(
            src_ref=s.at[1 - mc] if other_half else s, dst_ref=d, send_sem=send_sems.at[a], recv_sem=recv_sems.at[a],
            device_id=sibling, device_id_type=MESH) for a, (s, d) in enumerate(zip(ins, outs))]
        for cp in copies:
            cp.start()
        for cp in copies:
            cp.wait()

    return pl.pallas_call(
        body, name=name, in_specs=[_ANY] * n, out_specs=[_ANY] * n,
        out_shape=[jax.ShapeDtypeStruct(a.shape[1:] if other_half else a.shape, a.dtype) for a in arrays],
        scratch_shapes=[pltpu.SemaphoreType.DMA((n,)), pltpu.SemaphoreType.DMA((n,))],
    )(*arrays)


def _as_rows(a, lead=0):
    shp = a.shape
    return a.reshape(shp[:lead] + (-1, shp[-1]))


def _sum4_call(x, name):
    _, r, c = x.shape
    tr = _pick(r, [t for t in (512, 256, 128, 64, 32, 16, 8) if 16 * t * c <= _BLOCK_BYTES] + [r])

    def body(x_ref, o_ref):
        part = [x_ref[j].astype(F32) for j in range(4)]
        o_ref[...] = (part[0] + part[1]) + (part[2] + part[3])

    return pl.pallas_call(
        body, name=name, grid=(r // tr,),
        in_specs=[pl.BlockSpec((4, tr, c), lambda i: (0, i, 0))],
        out_specs=pl.BlockSpec((tr, c), lambda i: (i, 0)),
        out_shape=jax.ShapeDtypeStruct((r, c), F32),
        compiler_params=_cparams(("parallel",), 48),
    )(x)


def _add_to_bf16_call(a, b, name):
    _, r, c = a.shape
    tr = _pick(r, [t for t in (512, 256, 128, 64, 32, 16) if 16 * t * c <= _BLOCK_BYTES] + [r])

    def body(a_ref, b_ref, o_ref):
        o_ref[...] = (a_ref[...] + b_ref[...]).astype(BF16)

    spec = pl.BlockSpec((4, tr, c), lambda i: (0, i, 0))
    return pl.pallas_call(
        body, name=name, grid=(r // tr,), in_specs=[spec, spec], out_specs=spec,
        out_shape=jax.ShapeDtypeStruct(a.shape, BF16), compiler_params=_cparams(("parallel",), 48),
    )(a, b)


def _adam_call(w, grads, m, v, name):
    r, c = w.shape
    n_g = len(grads)
    tr = _pick(r, [t for t in (512, 256, 128, 64, 32, 16, 8) if 4 * t * c <= _BLOCK_BYTES // 4] + [r])
    bc1 = 1.0 - ADAM_B1 ** ADAM_STEP
    bc2 = 1.0 - ADAM_B2 ** ADAM_STEP

    def body(*refs):
        w_ref, g_refs = refs[0], refs[1:1 + n_g]
        m_ref, v_ref, g_out, d_out, m_out, v_out = refs[1 + n_g:]
        g = g_refs[0][...]
        for g_ref in g_refs[1:]:
            g = g + g_ref[...]
        m_new = ADAM_B1 * m_ref[...] + (1.0 - ADAM_B1) * g
        v_new = ADAM_B2 * v_ref[...] + (1.0 - ADAM_B2) * (g * g)
        m_hat = m_new / bc1
        v_hat = v_new / bc2
        g_out[...] = g
        d_out[...] = -ADAM_LR * (m_hat / (jnp.sqrt(v_hat) + ADAM_EPS) + ADAM_WD * w_ref[...])
        m_out[...] = m_new
        v_out[...] = v_new

    spec = pl.BlockSpec((tr, c), lambda i: (i, 0))
    return pl.pallas_call(
        body, name=name, grid=(r // tr,), in_specs=[spec] * (3 + n_g), out_specs=[spec] * 4,
        out_shape=[jax.ShapeDtypeStruct((r, c), F32)] * 4,
        compiler_params=_cparams(("parallel",), 48),
    )(w, *grads, m, v)


_DENSE = ("w_in", "w_branch", "w_out")

_SHARDED = {"meta": (1, False), "w_in": (2, True), "s5_w_glu": (1, True), "ssd_conv_w": (2, False),
            "gdn_conv_w": (2, False), "w_branch": (3, True), "b_gate": (2, False), "w_out": (1, True)}


def _pack(arrs):
    flat = jnp.concatenate([a.reshape(-1) for a in arrs])
    n = flat.shape[0]
    rows = -(-n // (256 * LANES)) * 256
    return jnp.concatenate([flat, jnp.zeros((rows * LANES - n,), F32)]).reshape(rows, LANES)


def _unpack(packed, like):
    flat = packed.reshape(-1)
    out, off = [], 0
    for a in like:
        out.append(flat[off:off + a.size].reshape(a.shape))
        off += a.size
    return out


def kernel(x, meta, ln_in_g, ln_in_b, w_in, s5_a_re, s5_a_im, s5_log_step, s5_b_re, s5_b_im, s5_c_re, s5_c_im, s5_d, s5_w_glu, s5_b_glu, ssd_conv_w, ssd_conv_b, ssd_dt_bias, ssd_a_log, ssd_d, ssd_norm_g, gdn_conv_w, gdn_dt_bias, gdn_a_log, gdn_norm_g, w_branch, b_gate, w_out, ln_g, ln_b, loss_target, m_meta, m_ln_in_g, m_ln_in_b, m_w_in, m_s5_a_re, m_s5_a_im, m_s5_log_step, m_s5_b_re, m_s5_b_im, m_s5_c_re, m_s5_c_im, m_s5_d, m_s5_w_glu, m_s5_b_glu, m_ssd_conv_w, m_ssd_conv_b, m_ssd_dt_bias, m_ssd_a_log, m_ssd_d, m_ssd_norm_g, m_gdn_conv_w, m_gdn_dt_bias, m_gdn_a_log, m_gdn_norm_g, m_w_branch, m_b_gate, m_w_out, m_ln_g, m_ln_b, v_meta, v_ln_in_g, v_ln_in_b, v_w_in, v_s5_a_re, v_s5_a_im, v_s5_log_step, v_s5_b_re, v_s5_b_im, v_s5_c_re, v_s5_c_im, v_s5_d, v_s5_w_glu, v_s5_b_glu, v_ssd_conv_w, v_ssd_conv_b, v_ssd_dt_bias, v_ssd_a_log, v_ssd_d, v_ssd_norm_g, v_gdn_conv_w, v_gdn_dt_bias, v_gdn_a_log, v_gdn_norm_g, v_w_branch, v_b_gate, v_w_out, v_ln_g, v_ln_b):
    args = dict(locals())
    shards = {k: args[k] for k in _WEIGHT_KEYS}
    moms = {k: (args["m_" + k], args["v_" + k]) for k in _WEIGHT_KEYS}

    core = lax.axis_index("c")
    halves = lambda a: a.reshape((2, a.shape[0] // 2) + a.shape[1:])

    names = list(_SHARDED)
    sent = [lax.dynamic_index_in_dim(halves(shards[k]), core, 0, keepdims=False) for k in names]
    sent = [s.astype(BF16) if _SHARDED[k][1] else s for k, s in zip(names, sent)]
    gathered = _gather_two_level(sent, "gather_weights")
    my_chip = 2 * lax.axis_index("x") + lax.axis_index("y")
    own_block = lambda blocks, mine: lax.dynamic_update_index_in_dim(blocks, mine.astype(blocks.dtype), my_chip, 0)
    full, compute = dict(shards), {}
    for k, g in zip(names, gathered):
        shp, ax = shards[k].shape, _SHARDED[k][0]
        g = own_block(g, halves(shards[k]))
        gathered_k = jnp.concatenate([g[j].reshape(shp) for j in range(4)], axis=ax)
        full[k] = gathered_k.astype(F32)
        if k in _DENSE:
            compute[k] = gathered_k

    loss, (grads, grad_x) = jax.value_and_grad(_local_loss, argnums=(0, 1))(full, x[0], loss_target[0], compute)
    loss = lax.psum(loss, ("x", "y", "c"))

    blocks = []
    for k in names:
        per_chip = jnp.stack(jnp.split(grads[k], 4, axis=_SHARDED[k][0]), axis=0)
        blocks.append(jnp.moveaxis(_as_rows(per_chip.reshape((4, 2, -1) + per_chip.shape[2:]), 2), 1, 0))
    theirs = _core_exchange(blocks, True, "swap_halves")
    mine = [lax.dynamic_index_in_dim(b, core, 0, keepdims=False) for b in blocks]
    chip_sums = [_add_to_bf16_call(a, b, "sum_cores_" + k) for k, a, b in zip(names, mine, theirs)]
    arrived = _chip_exchange(chip_sums, True, "scatter_grads")
    arrived = [own_block(a, lax.dynamic_index_in_dim(s, my_chip, 0, keepdims=False)) for a, s in zip(arrived, chip_sums)]
    owned = [_sum4_call(a, "sum_chips_" + k) for k, a in zip(names, arrived)]
    others = _core_exchange(owned, False, "swap_owned")
    shared = [jnp.concatenate([jnp.where(core == 0, a, b), jnp.where(core == 0, b, a)], axis=0)
              for a, b in zip(owned, others)]

    small_names = [k for k in _WEIGHT_KEYS if k not in _SHARDED]
    packed = _pack([grads[k] for k in small_names])
    (packed4,) = _chip_exchange([packed], False, "gather_small_grads")
    small_sum = _sum4_call(own_block(packed4, packed), "sum_chips_small")
    (small_other,) = _core_exchange([small_sum], False, "swap_small")

    outs = {}
    for k, g in zip(names, shared):
        shp = shards[k].shape
        rows = _as_rows(shards[k]).shape
        res = _adam_call(_as_rows(shards[k]), [g.reshape(rows)], _as_rows(moms[k][0]), _as_rows(moms[k][1]), "adamw_" + k)
        outs[k] = [r.reshape(shp) for r in res]
    like = [shards[k] for k in small_names]
    res = _adam_call(_pack(like), [small_sum, small_other], _pack([moms[k][0] for k in small_names]),
                     _pack([moms[k][1] for k in small_names]), "adamw_small")
    for idx in range(4):
        for k, a in zip(small_names, _unpack(res[idx], like)):
            outs.setdefault(k, [None] * 4)[idx] = a

    result = [loss, grad_x[None]]
    for idx in range(4):
        result += [outs[k][idx] for k in _WEIGHT_KEYS]
    return tuple(result)
```

```python
import functools

import jax
import jax.numpy as jnp
from jax import lax
from jax.experimental import pallas as pl
from jax.experimental.pallas import tpu as pltpu

F32 = jnp.float32
BF16 = jnp.bfloat16
MESH = pl.DeviceIdType.MESH
F32_DOT = lax.Precision.HIGH

D_MODEL = 1024
DEPTH = 4
N_META = 16
CHUNK = 64
PAD = CHUNK - N_META
CONV_K = 4
HALO = 8
WIDTH = 768
S5_GROUPS, S5_GROUP, S5_STATE = 48, 16, 64
S5_BLOCKS = 6
S5_SEGMENTS = 8
SSD_HEADS, SSD_HEAD_DIM, SSD_GROUPS, SSD_STATE = 12, 64, 2, 128
SSD_PAIRS = 6
GDN_HEADS, GDN_DIM = 6, 128
LANES = 128
SMALL_ROWS = 24
ALPHA = (2 * DEPTH) ** 0.25
LN_EPS = 1e-5
P_S5U, P_S5Z, P_XBC, P_DT, P_SSDZ, P_QKV, P_GA, P_GB, P_GDNZ, P_GATE, P_END = (
    0, 768, 1536, 2816, 2828, 3596, 5900, 5906, 5912, 6680, 9752)
ADAM_LR, ADAM_B1, ADAM_B2, ADAM_EPS, ADAM_WD, ADAM_STEP = 0.001, 0.9, 0.999, 1e-08, 0.01, 10
NEG = -1e30


def _pick(n, cands):
    for c in cands:
        if n % c == 0:
            return c
    raise ValueError(f"no tile for {n} in {cands}")


def _cparams(sem, vmem_mb):
    return pltpu.CompilerParams(dimension_semantics=sem, vmem_limit_bytes=vmem_mb << 20)


_DIMS = {"nn": (((1,), (0,)), ((), ())), "nt": (((1,), (1,)), ((), ())), "tn": (((0,), (0,)), ((), ()))}


def _dot(a, b, mode, hi):
    if hi:
        prec = lax.Precision.HIGHEST if hi == "exact" else F32_DOT
        return lax.dot_general(a, b, _DIMS[mode], precision=prec, preferred_element_type=F32)
    return lax.dot_general(a.astype(BF16), b.astype(BF16), _DIMS[mode], preferred_element_type=F32)


@functools.partial(jax.custom_vjp, nondiff_argnums=(2, 3))
def _mm(a, b, mode="nn", hi=False):
    return _dot(a, b, mode, hi)


def _mm_fwd(a, b, mode, hi):
    return _dot(a, b, mode, hi), (a, b)


def _mm_bwd(mode, hi, res, g):
    a, b = res
    if mode == "nn":
        return _dot(g, b, "nt", hi), _dot(a, g, "tn", hi)
    if mode == "nt":
        return _dot(g, b, "nn", hi), _dot(g, a, "tn", hi)
    return _dot(b, g, "nt", hi), _dot(a, g, "nn", hi)


_mm.defvjp(_mm_fwd, _mm_bwd)


@functools.partial(jax.custom_vjp, nondiff_argnums=(1,))
def _roll_rows(x, k):
    return pltpu.roll(x, k % x.shape[0], 0)


def _roll_fwd(x, k):
    return _roll_rows(x, k), None


def _roll_bwd(k, _, g):
    return (_roll_rows(g, -k),)


_roll_rows.defvjp(_roll_fwd, _roll_bwd)


def _iota(shape, dim):
    return lax.broadcasted_iota(jnp.int32, shape, dim)


def _valid_rows(row0, n):
    return (row0 + _iota((n, 1), 0)) >= PAD


def _lane(x, idx):
    return jnp.sum(jnp.where(_iota(x.shape, 1) == idx, x, 0.0), axis=1, keepdims=True)


def _row(x, idx):
    return jnp.sum(jnp.where(_iota(x.shape, 0) == idx, x, 0.0), axis=0, keepdims=True)


def _layer_norm(z, g, b):
    mu = jnp.mean(z, axis=-1, keepdims=True)
    zc = z - mu
    var = jnp.mean(zc * zc, axis=-1, keepdims=True)
    return zc * lax.rsqrt(var + LN_EPS) * g + b


def _rms_norm(z, g):
    return z * lax.rsqrt(jnp.mean(z * z, axis=-1, keepdims=True) + LN_EPS) * g


def _causal_conv(halo, x, w, row0):
    t = x.shape[0]
    xc = jnp.concatenate([halo, x], axis=0)
    acc = None
    for j in range(CONV_K):
        term = _roll_rows(xc, CONV_K - 1 - j)[HALO:HALO + t] * _row(w, j)
        acc = term if acc is None else acc + term
    return acc, x[t - HALO:t]


def _tri(n, strict=False):
    r, c = _iota((n, n), 0), _iota((n, n), 1)
    return (r > c) if strict else (r >= c)


def _scan_op(step, name, *, tile, nb, row_kinds, param_kinds, carry_shapes, out_widths, vmem_mb=48):
    n_rows, n_par, n_car, n_out = len(row_kinds), len(param_kinds), len(carry_shapes), len(out_widths)

    def dims(rows):
        for k, a in zip(row_kinds, rows):
            if k in "bs":
                return a.shape[0], a.shape[0] // tile
        raise ValueError("need a row input")

    def row_spec(kind, a, rev, nt):
        ti = (lambda i: nt - 1 - i) if rev else (lambda i: i)
        if kind == "b":
            return pl.BlockSpec((tile, a.shape[1] // nb), lambda b, i: (ti(i), b))
        if kind == "s":
            return pl.BlockSpec((tile, a.shape[1]), lambda b, i: (ti(i), 0))
        return pl.BlockSpec((None, a.shape[1], a.shape[2]), lambda b, i: (ti(i), 0, 0))

    def par_spec(kind, a):
        if kind == "b":
            return pl.BlockSpec((None, a.shape[1], a.shape[2]), lambda b, i: (b, 0, 0))
        return pl.BlockSpec(a.shape, lambda b, i: (0, 0))

    def fwd_call(rows, params):
        length, nt = dims(rows)

        def body(*refs):
            r_in = refs[:n_rows]
            p_in = refs[n_rows:n_rows + n_par]
            o_out = refs[n_rows + n_par:n_rows + n_par + n_out]
            s_out = refs[n_rows + n_par + n_out:n_rows + n_par + n_out + n_car]
            c_scr = refs[n_rows + n_par + n_out + n_car:]
            b, i = pl.program_id(0), pl.program_id(1)

            if n_car:
                @pl.when(i == 0)
                def _():
                    for c in c_scr:
                        c[...] = jnp.zeros_like(c)

            cin = tuple(c[...] for c in c_scr)
            for s, c in zip(s_out, cin):
                s[...] = c
            new_c, outs = step(cin, tuple(r[...] for r in r_in), tuple(p[...] for p in p_in), b, i * tile)
            for c, v in zip(c_scr, new_c):
                c[...] = v
            for o, v in zip(o_out, outs):
                o[...] = v

        out_shape = [jax.ShapeDtypeStruct((length, nb * w), F32) for w in out_widths]
        out_shape += [jax.ShapeDtypeStruct((nb, nt) + tuple(s), F32) for s in carry_shapes]
        out_specs = [pl.BlockSpec((tile, w), lambda b, i: (i, b)) for w in out_widths]
        out_specs += [pl.BlockSpec((None, None) + tuple(s), lambda b, i: (b, i, 0, 0)) for s in carry_shapes]
        res = pl.pallas_call(
            body, name=name + "_fwd", grid=(nb, nt),
            in_specs=[row_spec(k, a, False, nt) for k, a in zip(row_kinds, rows)]
            + [par_spec(k, a) for k, a in zip(param_kinds, params)],
            out_specs=out_specs, out_shape=out_shape,
            scratch_shapes=[pltpu.VMEM(tuple(s), F32) for s in carry_shapes],
            compiler_params=_cparams(("arbitrary", "arbitrary"), vmem_mb),
        )(*rows, *params)
        return tuple(res[:n_out]), tuple(res[n_out:])

    def bwd_call(rows, params, saved, douts):
        length, nt = dims(rows)

        def body(*refs):
            k0 = 0
            r_in = refs[k0:k0 + n_rows]; k0 += n_rows
            p_in = refs[k0:k0 + n_par]; k0 += n_par
            s_in = refs[k0:k0 + n_car]; k0 += n_car
            g_in = refs[k0:k0 + n_out]; k0 += n_out
            dr_out = refs[k0:k0 + n_rows]; k0 += n_rows
            dp_out = refs[k0:k0 + n_par]; k0 += n_par
            dc_scr = refs[k0:]
            b, i = pl.program_id(0), pl.program_id(1)
            row0 = (nt - 1 - i) * tile

            @pl.when(i == 0)
            def _():
                for c in dc_scr:
                    c[...] = jnp.zeros_like(c)
                for p in dp_out:
                    p[...] = jnp.zeros_like(p)

            def f(c, r, p):
                return step(c, r, p, b, row0)

            _, vjp = jax.vjp(f, tuple(s[...] for s in s_in), tuple(r[...] for r in r_in),
                             tuple(p[...] for p in p_in))
            dc, dr, dp = vjp((tuple(c[...] for c in dc_scr), tuple(g[...] for g in g_in)))
            for c, v in zip(dc_scr, dc):
                c[...] = v
            for o, v in zip(dr_out, dr):
                o[...] = v
            for o, v in zip(dp_out, dp):
                o[...] += v

        rev = lambda i: nt - 1 - i
        in_specs = [row_spec(k, a, True, nt) for k, a in zip(row_kinds, rows)]
        in_specs += [par_spec(k, a) for k, a in zip(param_kinds, params)]
        in_specs += [pl.BlockSpec((None, None) + tuple(s), lambda b, i: (b, rev(i), 0, 0)) for s in carry_shapes]
        in_specs += [pl.BlockSpec((tile, w), lambda b, i: (rev(i), b)) for w in out_widths]
        out_shape, out_specs = [], []
        for k, a in zip(row_kinds, rows):
            if k == "b":
                out_shape.append(jax.ShapeDtypeStruct(a.shape, F32))
                out_specs.append(pl.BlockSpec((tile, a.shape[1] // nb), lambda b, i: (rev(i), b)))
            elif k == "s":
                out_shape.append(jax.ShapeDtypeStruct((nb,) + a.shape, F32))
                out_specs.append(pl.BlockSpec((None, tile, a.shape[1]), lambda b, i: (b, rev(i), 0)))
            else:
                out_shape.append(jax.ShapeDtypeStruct((nb,) + a.shape, F32))
                out_specs.append(pl.BlockSpec((None, None, a.shape[1], a.shape[2]), lambda b, i: (b, rev(i), 0, 0)))
        for k, a in zip(param_kinds, params):
            shp = a.shape[1:] if k == "b" else a.shape
            out_shape.append(jax.ShapeDtypeStruct((nb,) + tuple(shp), F32))
            out_specs.append(pl.BlockSpec((None,) + tuple(shp), lambda b, i: (b, 0, 0)))
        res = pl.pallas_call(
            body, name=name + "_bwd", grid=(nb, nt), in_specs=in_specs, out_specs=out_specs, out_shape=out_shape,
            scratch_shapes=[pltpu.VMEM(tuple(s), F32) for s in carry_shapes],
            compiler_params=_cparams(("arbitrary", "arbitrary"), vmem_mb),
        )(*rows, *params, *saved, *douts)
        fold = (lambda a: a[0]) if nb == 1 else (lambda a: jnp.sum(a, axis=0))
        drows = tuple(r if k == "b" else fold(r) for k, r in zip(row_kinds, res[:n_rows]))
        dpars = tuple(p if k == "b" else fold(p) for k, p in zip(param_kinds, res[n_rows:]))
        return drows, dpars

    @jax.custom_vjp
    def op(rows, params):
        return fwd_call(rows, params)[0]

    def op_fwd(rows, params):
        outs, saved = fwd_call(rows, params)
        return outs, (rows, params, saved)

    def op_bwd(res, douts):
        rows, params, saved = res
        return bwd_call(rows, params, saved, tuple(douts))

    op.defvjp(op_fwd, op_bwd)
    return op


_WIDE = 1280


def _mm_rows(m, n):
    return _pick(m, (832, 128)) if n <= _WIDE else _pick(m, (416, 128))


def _mm_fwd_call(x, w, name):
    m, k = x.shape
    n = w.shape[1]
    tm = _mm_rows(m, n)

    def body(x_ref, w_ref, o_ref):
        o_ref[...] = _dot(x_ref[...], w_ref[...], "nn", False)

    return pl.pallas_call(
        body, name=name, grid=(m // tm,),
        in_specs=[pl.BlockSpec((tm, k), lambda i: (i, 0)), pl.BlockSpec((k, n), lambda i: (0, 0))],
        out_specs=pl.BlockSpec((tm, n), lambda i: (i, 0)),
        out_shape=jax.ShapeDtypeStruct((m, n), F32),
        compiler_params=_cparams(("parallel",), 48),
    )(x, w)


def _mm_bwd_call(g, x, w, name):
    m, n = g.shape
    k = w.shape[0]
    tm = _mm_rows(m, n)

    def body(g_ref, x_ref, w_ref, dx_ref, dw_ref):
        @pl.when(pl.program_id(0) == 0)
        def _():
            dw_ref[...] = jnp.zeros_like(dw_ref)

        g = g_ref[...].astype(BF16)
        dx_ref[...] = _dot(g, w_ref[...], "nt", False)
        x = x_ref[...]
        step = _pick(n, (768, 640, 128))
        for c0 in range(0, n, step):
            dw_ref[:, c0:c0 + step] += _dot(x, g[:, c0:c0 + step], "tn", False)

    return pl.pallas_call(
        body, name=name, grid=(m // tm,),
        in_specs=[pl.BlockSpec((tm, n), lambda i: (i, 0)), pl.BlockSpec((tm, k), lambda i: (i, 0)),
                  pl.BlockSpec((k, n), lambda i: (0, 0))],
        out_specs=[pl.BlockSpec((tm, k), lambda i: (i, 0)), pl.BlockSpec((k, n), lambda i: (0, 0))],
        out_shape=[jax.ShapeDtypeStruct((m, k), F32), jax.ShapeDtypeStruct((k, n), F32)],
        compiler_params=_cparams(("arbitrary",), 56),
    )(g, x, w)


def _dense(name):
    @jax.custom_vjp
    def op(x, w, xb, wb):
        return _mm_fwd_call(xb, wb, name + "_fwd")

    def op_fwd(x, w, xb, wb):
        return _mm_fwd_call(xb, wb, name + "_fwd"), (xb, wb)

    def op_bwd(res, g):
        xb, wb = res
        dx, dw = _mm_bwd_call(g, xb, wb, name + "_bwd")
        return dx, dw, jnp.zeros_like(xb), jnp.zeros_like(wb)

    op.defvjp(op_fwd, op_bwd)
    return op


def _bf16_copy(v):
    return lax.stop_gradient(v).astype(BF16)


def _ln_in_step(c, rows, params, b, row0):
    (z,), (g, bb) = rows, params
    return (), (jnp.where(_valid_rows(row0, z.shape[0]), _layer_norm(z, g, bb), 0.0),)


def _ln_res_step(c, rows, params, b, row0):
    (h, o), (g, bb) = rows, params
    return (), (jnp.where(_valid_rows(row0, h.shape[0]), _layer_norm(ALPHA * h + o, g, bb), 0.0),)


def _s5_prep_step(c, rows, params, b, row0):
    a_re, a_im, log_step, b_re, b_im = rows
    lam_re = jnp.minimum(a_re, -1e-4)
    lam_im = a_im
    step = jnp.exp(log_step)
    mag = jnp.exp(lam_re * step)
    abar_re, abar_im = mag * jnp.cos(lam_im * step), mag * jnp.sin(lam_im * step)
    den = lam_re * lam_re + lam_im * lam_im
    nr, ni = abar_re - 1.0, abar_im
    coef_re = (nr * lam_re + ni * lam_im) / den
    coef_im = (ni * lam_re - nr * lam_im) / den
    return (), (abar_re, abar_im, coef_re * b_re - coef_im * b_im, coef_re * b_im + coef_im * b_re)


def _s5_scan_step(c, rows, params, b, row0):
    (c_re, c_im), (u,) = c, rows
    bd_re, bd_im, a_re, a_im, cd_re, cd_im = params
    t = u.shape[0]
    steps = t // S5_SEGMENTS
    bu_re, bu_im = _mm(u, bd_re), _mm(u, bd_im)
    a_re, a_im = (jnp.broadcast_to(v, (S5_SEGMENTS, v.shape[1])) for v in (a_re, a_im))
    at = lambda v, i: v[S5_SEGMENTS * i:S5_SEGMENTS * (i + 1)]
    s_re, s_im = at(bu_re, 0), at(bu_im, 0)
    p_re, p_im = a_re, a_im
    local, power = [(s_re, s_im)], [(p_re, p_im)]
    for i in range(1, steps):
        s_re, s_im = a_re * s_re - a_im * s_im + at(bu_re, i), a_re * s_im + a_im * s_re + at(bu_im, i)
        p_re, p_im = a_re * p_re - a_im * p_im, a_re * p_im + a_im * p_re
        local.append((s_re, s_im))
        power.append((p_re, p_im))
    seg = _iota((S5_SEGMENTS, 1), 0)
    in_re = jnp.where(seg == 0, c_re, _roll_rows(s_re, 1))
    in_im = jnp.where(seg == 0, c_im, _roll_rows(s_im, 1))
    q_re, q_im = p_re, p_im
    d = 1
    while d < S5_SEGMENTS:
        keep = seg >= d
        sh_re = jnp.where(keep, _roll_rows(in_re, d), 0.0)
        sh_im = jnp.where(keep, _roll_rows(in_im, d), 0.0)
        in_re, in_im = in_re + q_re * sh_re - q_im * sh_im, in_im + q_re * sh_im + q_im * sh_re
        q_re, q_im = q_re * q_re - q_im * q_im, 2.0 * q_re * q_im
        d *= 2
    full = [(l_re + w_re * in_re - w_im * in_im, l_im + w_re * in_im + w_im * in_re)
            for (l_re, l_im), (w_re, w_im) in zip(local, power)]
    y = _mm(jnp.concatenate([f[0] for f in full], axis=0), cd_re) - _mm(jnp.concatenate([f[1] for f in full], axis=0), cd_im)
    return (_row(full[-1][0], S5_SEGMENTS - 1), _row(full[-1][1], S5_SEGMENTS - 1)), (y,)


def _interleave(v, tile, inverse=False):
    length, width = v.shape
    shape = (length // tile, tile // S5_SEGMENTS, S5_SEGMENTS) if inverse else (length // tile, S5_SEGMENTS, tile // S5_SEGMENTS)
    return v.reshape(shape + (width,)).transpose(0, 2, 1, 3).reshape(length, width)


def _s5_post_step(c, rows, params, b, row0):
    (y, u, z), (d, w_glu, b_glu) = rows, params
    v = jax.nn.gelu(y + d * u)
    v = v * jax.nn.sigmoid(_mm(v, w_glu) + b_glu)
    return (), (v * jax.nn.silu(z),)


def _ssd_step(c, rows, params, b, row0):
    halo, state = c
    xbc_raw, z, small, small_t = rows
    cw, cb, d_l, bias_l, alog_l, bias_c, alog_c, norm_g = params
    t = xbc_raw.shape[0]
    grp = SSD_GROUPS * SSD_STATE
    valid = _valid_rows(row0, t)
    conv, halo2 = _causal_conv(halo, xbc_raw, cw, row0)
    act = jnp.where(valid, jax.nn.silu(conv + cb), 0.0)
    low = _iota((1, LANES), 1) < SSD_HEAD_DIM
    dt_all = jnp.where(valid, jax.nn.softplus(small + bias_l), 0.0)
    a_all = -jnp.exp(alog_l)
    valid_t = (row0 + _iota((1, t), 1)) >= PAD
    dta_t = jnp.where(valid_t, jax.nn.softplus(small_t + bias_c), 0.0) * (-jnp.exp(alog_c))
    acum_t = _mm(dta_t, jnp.where(_iota((t, t), 0) <= _iota((t, t), 1), 1.0, 0.0), "nn", True)
    causal = _tri(t)
    acum_all = _mm(jnp.where(causal, 1.0, 0.0), dt_all * a_all, "nn", True)
    last_all = _row(acum_all, t - 1)
    low_rows = _iota((LANES, 1), 0) < SSD_HEAD_DIM
    pairs, groups = range(SSD_PAIRS), range(SSD_GROUPS)
    grp_of = [p // (SSD_PAIRS // SSD_GROUPS) for p in pairs]
    bs = [act[:, WIDTH + g * SSD_STATE:WIDTH + (g + 1) * SSD_STATE] for g in groups]
    cs = [act[:, WIDTH + grp + g * SSD_STATE:WIDTH + grp + (g + 1) * SSD_STATE] for g in groups]
    scores = [_mm(cs[g], bs[g], "nt") for g in groups]
    per_lane = lambda v, p: jnp.where(low, v[:, 2 * p:2 * p + 1], v[:, 2 * p + 1:2 * p + 2])
    dt_l = [per_lane(dt_all, p) for p in pairs]
    acum_l = [per_lane(acum_all, p) for p in pairs]
    last_l = [per_lane(last_all, p) for p in pairs]
    st = [state[p * LANES:(p + 1) * LANES] for p in pairs]
    xd = [act[:, p * LANES:(p + 1) * LANES] * dt_l[p] for p in pairs]
    decay = [jnp.exp(jnp.where(causal, acum_all[:, h:h + 1] - _row(acum_t, h), NEG)) for h in range(SSD_HEADS)]
    y_lo = [_mm(scores[grp_of[p]] * decay[2 * p], jnp.where(low, xd[p], 0.0)) for p in pairs]
    y_hi = [_mm(scores[grp_of[p]] * decay[2 * p + 1], jnp.where(low, 0.0, xd[p])) for p in pairs]
    y_off = [_mm(cs[grp_of[p]], st[p], "nt") * jnp.exp(acum_l[p]) for p in pairs]
    new_st = [_mm(xd[p] * jnp.exp(last_l[p] - acum_l[p]), bs[grp_of[p]], "tn") for p in pairs]
    cd = jnp.exp(last_all)
    new_st = [st[p] * jnp.where(low_rows, cd[:, 2 * p:2 * p + 1], cd[:, 2 * p + 1:2 * p + 2]) + new_st[p] for p in pairs]
    y = jnp.concatenate([y_lo[p] + y_hi[p] + y_off[p] for p in pairs], axis=1) + act[:, :WIDTH] * d_l
    out = _rms_norm(y * jax.nn.silu(z), norm_g)
    return (halo2, jnp.concatenate(new_st, axis=0)), (out,)


@jax.custom_vjp
def _unit_lower_inverse(mats):
    return _neumann_inverse(mats)


def _inverse_fwd(mats):
    inv = _neumann_inverse(mats)
    return inv, inv


def _inverse_bwd(inv, g):
    left = [_dot(t, d, "tn", True) for t, d in zip(inv, g)]
    return ([-_dot(l, t, "nt", True) for l, t in zip(left, inv)],)


_unit_lower_inverse.defvjp(_inverse_fwd, _inverse_bwd)


def _neumann_inverse(mats):
    n = mats[0].shape[0]
    eye = jnp.where(_iota((n, n), 0) == _iota((n, n), 1), 1.0, 0.0)
    inv = [eye - a for a in mats]
    p = [_mm(a, a, "nn", True) for a in mats]
    k = 2
    while k < n:
        inv = [i + _mm(i, q, "nn", True) for i, q in zip(inv, p)]
        k *= 2
        if k < n:
            p = [_mm(q, q, "nn", True) for q in p]
    return inv


def _gdn_step(c, rows, params, b, row0):
    halo, state = c
    qkv_raw, z, small, small_t = rows
    cw, bias_l, alog_l, bias_c, alog_c, norm_g = params
    t = qkv_raw.shape[0]
    valid = _valid_rows(row0, t)
    conv, halo2 = _causal_conv(halo, qkv_raw, cw, row0)
    act = jnp.where(valid, jax.nn.silu(conv), 0.0)
    beta_all = jnp.where(valid, jax.nn.sigmoid(small), 0.0)
    g_all = jnp.where(valid, -jnp.exp(alog_l) * jax.nn.softplus(small + bias_l), 0.0)
    valid_t = (row0 + _iota((1, t), 1)) >= PAD
    g_t = jnp.where(valid_t, -jnp.exp(alog_c) * jax.nn.softplus(small_t + bias_c), 0.0)
    causal, strict = _tri(t), _tri(t, True)
    gcum_all = _mm(jnp.where(causal, 1.0, 0.0), g_all, "nn", True)
    gcum_t = _mm(g_t, jnp.where(_iota((t, t), 0) <= _iota((t, t), 1), 1.0, 0.0), "nn", True)
    heads = range(GDN_HEADS)
    part = lambda h, n: act[:, n * WIDTH + h * GDN_DIM:n * WIDTH + (h + 1) * GDN_DIM]
    unit = lambda x: x * lax.rsqrt(jnp.sum(x * x, axis=-1, keepdims=True) + 1e-6)
    q = [unit(part(h, 0)) * (GDN_DIM ** -0.5) for h in heads]
    k = [unit(part(h, 1)) for h in heads]
    st = [state[h * GDN_DIM:(h + 1) * GDN_DIM] for h in heads]
    ia = [SSD_HEADS + h for h in heads]
    beta = [beta_all[:, ia[h] + GDN_HEADS:ia[h] + GDN_HEADS + 1] for h in heads]
    gcum = [gcum_all[:, ia[h]:ia[h] + 1] for h in heads]
    gamma = [jnp.exp(jnp.where(causal, gcum[h] - _row(gcum_t, ia[h]), NEG)) for h in heads]
    egc = [jnp.exp(gcum[h]) for h in heads]
    a_mat = [jnp.where(strict, _mm(k[h], k[h], "nt") * gamma[h] * beta[h], 0.0) for h in heads]
    inv = _unit_lower_inverse(a_mat)
    rhs = [jnp.concatenate([part(h, 2) * beta[h], k[h] * (beta[h] * egc[h])], axis=1) for h in heads]
    sol = [_mm(inv[h], rhs[h], "nn", True) for h in heads]
    attn = [_mm(q[h], k[h], "nt") * gamma[h] for h in heads]
    from_state = [_mm(jnp.concatenate([sol[h][:, GDN_DIM:], q[h] * egc[h]], axis=0), st[h]) for h in heads]
    v_new = [sol[h][:, :GDN_DIM] - from_state[h][:t] for h in heads]
    o = [from_state[h][t:] + _mm(attn[h], v_new[h]) for h in heads]
    glast = [_row(gcum[h], t - 1) for h in heads]
    new_st = [st[h] * jnp.exp(glast[h]) + _mm(k[h] * jnp.exp(glast[h] - gcum[h]), v_new[h], "tn") for h in heads]
    out = jnp.concatenate([_rms_norm(o[h], norm_g) for h in heads], axis=1) * jax.nn.silu(z)
    return (halo2, jnp.concatenate(new_st, axis=0)), (out,)


def _gate_merge_step(c, rows, params, b, row0):
    (bg,) = params
    acc = None
    for n, (o, gl) in enumerate(zip(rows[:3], rows[3:])):
        term = jax.nn.sigmoid(gl + bg[:, n * D_MODEL:(n + 1) * D_MODEL]) * o
        acc = term if acc is None else acc + term
    return (), (acc,)


def _loss_tile(n):
    return _pick(n, (512, 256, 128, 64))


def _loss_fwd_call(y, tgt):
    n, d = y.shape
    tile = _loss_tile(n)

    def body(y_ref, t_ref, o_ref):
        @pl.when(pl.program_id(0) == 0)
        def _():
            o_ref[...] = jnp.zeros_like(o_ref)

        e = y_ref[...] - t_ref[...]
        o_ref[...] += jnp.sum(jnp.sum(e * e, axis=1, keepdims=True), axis=0, keepdims=True) * (0.5 / d)

    out = pl.pallas_call(
        body, name="loss_fwd", grid=(n // tile,),
        in_specs=[pl.BlockSpec((tile, d), lambda i: (i, 0)), pl.BlockSpec((tile, d), lambda i: (i, 0))],
        out_specs=pl.BlockSpec((8, LANES), lambda i: (0, 0)),
        out_shape=jax.ShapeDtypeStruct((8, LANES), F32),
        compiler_params=_cparams(("arbitrary",), 32),
    )(y, tgt)
    return out[0, 0]


def _loss_bwd_call(y, tgt, g):
    n, d = y.shape
    tile = _loss_tile(n)

    def body(y_ref, t_ref, g_ref, o_ref):
        o_ref[...] = (y_ref[...] - t_ref[...]) * (g_ref[...][0:1, 0:1] * (1.0 / d))

    return pl.pallas_call(
        body, name="loss_bwd", grid=(n // tile,),
        in_specs=[pl.BlockSpec((tile, d), lambda i: (i, 0)), pl.BlockSpec((tile, d), lambda i: (i, 0)),
                  pl.BlockSpec((8, LANES), lambda i: (0, 0))],
        out_specs=pl.BlockSpec((tile, d), lambda i: (i, 0)),
        out_shape=jax.ShapeDtypeStruct((n, d), F32),
        compiler_params=_cparams(("parallel",), 32),
    )(y, tgt, jnp.broadcast_to(g, (8, LANES)).astype(F32))


@jax.custom_vjp
def _loss_op(y, tgt):
    return _loss_fwd_call(y, tgt)


def _loss_op_fwd(y, tgt):
    return _loss_fwd_call(y, tgt), (y, tgt)


def _loss_op_bwd(res, g):
    y, tgt = res
    return _loss_bwd_call(y, tgt, g), jnp.zeros_like(tgt)


_loss_op.defvjp(_loss_op_fwd, _loss_op_bwd)


def _rowwise(step, name, tile, n_rows, n_params, out_widths, vmem_mb=48):
    return _scan_op(step, name, tile=tile, nb=1, row_kinds="s" * n_rows, param_kinds="s" * n_params,
                    carry_shapes=(), out_widths=out_widths, vmem_mb=vmem_mb)


def _block_diag(x, nblk):
    bsz, _, r, c = x.shape
    eye = jnp.eye(nblk, dtype=x.dtype)
    return jnp.einsum("bgrc,gh->bgrhc", x, eye).reshape(bsz, nblk * r, nblk * c)


_PROJ = {"s5u": (P_S5U, P_S5Z), "s5z": (P_S5Z, P_XBC), "xbc": (P_XBC, P_DT), "ssdz": (P_SSDZ, P_QKV),
         "qkv": (P_QKV, P_GA), "gdnz": (P_GDNZ, P_GATE), "gate_a": (P_GATE, P_GATE + D_MODEL),
         "gate_b": (P_GATE + D_MODEL, P_GATE + 2 * D_MODEL), "gate_c": (P_GATE + 2 * D_MODEL, P_END)}


def _prepare(weights, compute):
    w = weights
    zeros = lambda *shape: jnp.zeros((DEPTH,) + shape, F32)
    row3 = lambda v: v.reshape(DEPTH, 1, -1)
    p = {}
    for pre, w_in in (("w_", w["w_in"]), ("wb_", compute["w_in"])):
        seg = lambda a, bnd: w_in[:, :, a:bnd]
        p.update({pre + k: seg(a, bnd) for k, (a, bnd) in _PROJ.items()})
        p[pre + "small"] = jnp.concatenate(
            [seg(P_DT, P_SSDZ), seg(P_GA, P_GDNZ), zeros(D_MODEL, LANES - SMALL_ROWS).astype(w_in.dtype)], axis=2)
    p.update(wb_branch=compute["w_branch"], wb_out=compute["w_out"])

    rows, wide = DEPTH * S5_GROUPS, S5_STATE * S5_GROUP
    flat = lambda v: v.reshape(rows, -1)
    rep = lambda v: jnp.repeat(flat(v), S5_GROUP, axis=1)
    prep = _rowwise(_s5_prep_step, "s5_prep", S5_GROUPS, 5, 0, (wide,) * 4)
    abar_re, abar_im, bbar_re, bbar_im = prep(
        (rep(w["s5_a_re"]), rep(w["s5_a_im"]), jnp.broadcast_to(flat(w["s5_log_step"]), (rows, wide)),
         flat(w["s5_b_re"]), flat(w["s5_b_im"])), ())
    gpb = S5_GROUPS // S5_BLOCKS
    lanes, chans, nblk = gpb * S5_STATE, gpb * S5_GROUP, DEPTH * S5_BLOCKS
    to_bd = lambda bb: _block_diag(bb.reshape(nblk, gpb, S5_STATE, S5_GROUP).transpose(0, 1, 3, 2), gpb).reshape(
        DEPTH, S5_BLOCKS, chans, lanes)
    to_cd = lambda cc: _block_diag(cc.reshape(nblk, gpb, S5_GROUP, S5_STATE).transpose(0, 1, 3, 2), gpb).reshape(
        DEPTH, S5_BLOCKS, lanes, chans)
    to_a = lambda a: a[:, ::S5_GROUP].reshape(DEPTH, S5_BLOCKS, 1, lanes)
    p.update(s5_bd_re=to_bd(bbar_re), s5_bd_im=to_bd(bbar_im), s5_a_re=to_a(abar_re), s5_a_im=to_a(abar_im),
             s5_cd_re=to_cd(w["s5_c_re"]), s5_cd_im=to_cd(w["s5_c_im"]),
             s5_d=row3(w["s5_d"]), s5_w_glu=w["s5_w_glu"], s5_b_glu=row3(w["s5_b_glu"]))

    bias = jnp.concatenate([w["ssd_dt_bias"], w["gdn_dt_bias"]], axis=1)
    alog = jnp.concatenate([w["ssd_a_log"], w["gdn_a_log"]], axis=1)
    on_lanes = lambda v: jnp.concatenate([v, zeros(LANES - v.shape[1])], axis=1).reshape(DEPTH, 1, LANES)
    on_rows = lambda v: jnp.concatenate([v, zeros(SMALL_ROWS - v.shape[1])], axis=1).reshape(DEPTH, SMALL_ROWS, 1)
    pad_w = lambda cw: jnp.concatenate([cw, zeros(HALO - CONV_K, cw.shape[2])], axis=1)
    p.update(bias_l=on_lanes(bias), alog_l=on_lanes(alog), bias_c=on_rows(bias), alog_c=on_rows(alog),
             ssd_cw=pad_w(w["ssd_conv_w"]), ssd_cb=row3(w["ssd_conv_b"]),
             ssd_d=row3(jnp.repeat(w["ssd_d"], SSD_HEAD_DIM, axis=1)), ssd_norm_g=row3(w["ssd_norm_g"]),
             gdn_cw=pad_w(w["gdn_conv_w"]), gdn_norm_g=row3(w["gdn_norm_g"]),
             w_branch=w["w_branch"], b_gate=row3(w["b_gate"]), w_out=w["w_out"], ln_g=row3(w["ln_g"]), ln_b=row3(w["ln_b"]))
    return p


def _layer(h, p):
    length = h.shape[0]
    nt = length // CHUNK
    t_row = _pick(length, (208, 128))
    t_s5 = _pick(length, (320, 128))
    hb = _bf16_copy(h)
    proj = {k: _dense("proj_" + k)(h, p["w_" + k], hb, p["wb_" + k]) for k in list(_PROJ) + ["small"]}
    small = proj["small"]
    small_t = small[:, :SMALL_ROWS].reshape(nt, CHUNK, SMALL_ROWS).transpose(0, 2, 1)

    lanes = p["s5_a_re"].shape[-1]
    s5_scan = _scan_op(_s5_scan_step, "s5_scan", tile=t_s5, nb=S5_BLOCKS, row_kinds="b", param_kinds="bbbbbb",
                       carry_shapes=((1, lanes), (1, lanes)), out_widths=(LANES,))
    (y_ssm,) = s5_scan((_interleave(proj["s5u"], t_s5),), (p["s5_bd_re"], p["s5_bd_im"], p["s5_a_re"], p["s5_a_im"],
                                                           p["s5_cd_re"], p["s5_cd_im"]))
    y_ssm = _interleave(y_ssm, t_s5, inverse=True)
    s5_post = _rowwise(_s5_post_step, "s5_post", t_row, 3, 3, (WIDTH,))
    (y_a,) = s5_post((y_ssm, proj["s5u"], proj["s5z"]), (p["s5_d"], p["s5_w_glu"], p["s5_b_glu"]))

    scalars = (p["bias_l"], p["alog_l"], p["bias_c"], p["alog_c"])
    ssd = _scan_op(_ssd_step, "ssd_scan", tile=CHUNK, nb=1, row_kinds="ssst", param_kinds="s" * 8,
                   carry_shapes=((HALO, P_DT - P_XBC), (SSD_HEADS * SSD_HEAD_DIM, SSD_STATE)), out_widths=(WIDTH,))
    (y_b,) = ssd((proj["xbc"], proj["ssdz"], small, small_t),
                 (p["ssd_cw"], p["ssd_cb"], p["ssd_d"]) + scalars + (p["ssd_norm_g"],))
    gdn = _scan_op(_gdn_step, "gdn_scan", tile=CHUNK, nb=1, row_kinds="ssst", param_kinds="s" * 6,
                   carry_shapes=((HALO, P_GA - P_QKV), (GDN_HEADS * GDN_DIM, GDN_DIM)), out_widths=(WIDTH,))
    (y_c,) = gdn((proj["qkv"], proj["gdnz"], small, small_t), (p["gdn_cw"],) + scalars + (p["gdn_norm_g"],))

    branch = [_dense("branch_" + n)(y, p["w_branch"][i], _bf16_copy(y), p["wb_branch"][i])
              for i, (n, y) in enumerate(zip("abc", (y_a, y_b, y_c)))]
    merge = _rowwise(_gate_merge_step, "gate_merge", t_row, 6, 1, (D_MODEL,))
    (merged,) = merge((*branch, proj["gate_a"], proj["gate_b"], proj["gate_c"]), (p["b_gate"],))
    out = _dense("out_proj")(merged, p["w_out"], _bf16_copy(merged), p["wb_out"])
    ln = _rowwise(_ln_res_step, "ln_res", t_row, 2, 2, (D_MODEL,))
    (h_new,) = ln((h, out), (p["ln_g"], p["ln_b"]))
    return h_new


_LAYER_KEYS = ("w_in", "s5_a_re", "s5_a_im", "s5_log_step", "s5_b_re", "s5_b_im", "s5_c_re", "s5_c_im", "s5_d",
               "s5_w_glu", "s5_b_glu", "ssd_conv_w", "ssd_conv_b", "ssd_dt_bias", "ssd_a_log", "ssd_d", "ssd_norm_g",
               "gdn_conv_w", "gdn_dt_bias", "gdn_a_log", "gdn_norm_g", "w_branch", "b_gate", "w_out", "ln_g", "ln_b")
_WEIGHT_KEYS = ("meta", "ln_in_g", "ln_in_b") + _LAYER_KEYS


def _local_loss(weights, x, target, compute):
    seq = x.shape[0]
    hcat = jnp.concatenate([jnp.zeros((PAD, D_MODEL), F32), weights["meta"], x], axis=0)
    length = hcat.shape[0]
    ln_in = _rowwise(_ln_in_step, "ln_in", _pick(length, (416, 256, 128)), 1, 2, (D_MODEL,))
    (h,) = ln_in((hcat,), (weights["ln_in_g"].reshape(1, -1), weights["ln_in_b"].reshape(1, -1)))

    prepared = _prepare(weights, compute)
    for layer in range(DEPTH):
        h = _layer(h, {k: v[layer] for k, v in prepared.items()})
    return _loss_op(h[length - seq:], target)


_ANY = pl.BlockSpec(memory_space=pl.ANY)
_BLOCK_BYTES = 4 << 20


def _chip_exchange(arrays, all_to_all, name):
    n = len(arrays)

    def body(*refs):
        ins, outs = refs[:n], refs[n:2 * n]
        send_sems, recv_sems = refs[2 * n:]
        mx, my, mc = lax.axis_index("x"), lax.axis_index("y"), lax.axis_index("c")
        me = 2 * mx + my
        peers = [(1 - mx, my), (mx, 1 - my), (1 - mx, 1 - my)]
        sends = []
        for a, (src, dst) in enumerate(zip(ins, outs)):
            for k, (px, py) in enumerate(peers):
                cp = pltpu.make_async_remote_copy(
                    src_ref=src.at[2 * px + py] if all_to_all else src, dst_ref=dst.at[me],
                    send_sem=send_sems.at[a, k], recv_sem=recv_sems.at[a, k],
                    device_id=(px, py, mc), device_id_type=MESH)
                cp.start()
                sends.append(cp)
        for a, (src, dst) in enumerate(zip(ins, outs)):
            for k, (px, py) in enumerate(peers):
                pltpu.make_async_remote_copy(
                    src_ref=src.at[me] if all_to_all else src, dst_ref=dst.at[2 * px + py],
                    send_sem=send_sems.at[a, k], recv_sem=recv_sems.at[a, k],
                    device_id=(px, py, mc), device_id_type=MESH).wait_recv()
        for cp in sends:
            cp.wait_send()

    out_shape = [jax.ShapeDtypeStruct(a.shape if all_to_all else (4,) + a.shape, a.dtype) for a in arrays]
    return pl.pallas_call(
        body, name=name, in_specs=[_ANY] * n, out_specs=[_ANY] * n, out_shape=out_shape,
        scratch_shapes=[pltpu.SemaphoreType.DMA((n, 3)), pltpu.SemaphoreType.DMA((n, 3))],
    )(*arrays)


def _gather_two_level(arrays, name):
    n = len(arrays)

    def body(*refs):
        ins, outs = refs[:n], refs[n:2 * n]
        send_sems, recv_sems = refs[2 * n:]
        mx, my, mc = lax.axis_index("x"), lax.axis_index("y"), lax.axis_index("c")
        me = 2 * mx + my
        sibling = (mx, my, 1 - mc)
        chips = [(1 - mx, my), (mx, 1 - my), (1 - mx, 1 - my)]

        def copy(a, k, src, chip, core, to):
            return pltpu.make_async_remote_copy(
                src_ref=src, dst_ref=outs[a].at[chip, core], send_sem=send_sems.at[a, k], recv_sem=recv_sems.at[a, k],
                device_id=to, device_id_type=MESH)

        sends = [copy(a, j, ins[a], me, mc, (px, py, mc)) for a in range(n) for j, (px, py) in enumerate(chips)]
        for cp in sends:
            cp.start()
        passed = []
        for a in range(n):
            for j, (px, py) in enumerate(chips):
                chip = 2 * px + py
                copy(a, j, ins[a], chip, mc, sibling).wait_recv()
                cp = copy(a, 3 + j, outs[a].at[chip, mc], chip, mc, sibling)
                cp.start()
                passed.append(cp)
        for a in range(n):
            for j, (px, py) in enumerate(chips):
                copy(a, 3 + j, ins[a], 2 * px + py, 1 - mc, sibling).wait_recv()
        for cp in sends + passed:
            cp.wait_send()

    return pl.pallas_call(
        body, name=name, in_specs=[_ANY] * n, out_specs=[_ANY] * n,
        out_shape=[jax.ShapeDtypeStruct((4, 2) + a.shape, a.dtype) for a in arrays],
        scratch_shapes=[pltpu.SemaphoreType.DMA((n, 6)), pltpu.SemaphoreType.DMA((n, 6))],
    )(*arrays)


def _core_exchange(arrays, other_half, name):
    n = len(arrays)

    def body(*refs):
        ins, outs = refs[:n], refs[n:2 * n]
        send_sems, recv_sems = refs[2 * n:]
        mc = lax.axis_index("c")
        sibling = (lax.axis_index("x"), lax.axis_index("y"), 1 - mc)
        copies = [pltpu.make_async_remote_copy(
            src_ref=s.at[1 - mc] if other_half else s, dst_ref=d, send_sem=send_sems.at[a], recv_sem=recv_sems.at[a],
            device_id=sibling, device_id_type=MESH) for a, (s, d) in enumerate(zip(ins, outs))]
        for cp in copies:
            cp.start()
        for cp in copies:
            cp.wait()

    return pl.pallas_call(
        body, name=name, in_specs=[_ANY] * n, out_specs=[_ANY] * n,
        out_shape=[jax.ShapeDtypeStruct(a.shape[1:] if other_half else a.shape, a.dtype) for a in arrays],
        scratch_shapes=[pltpu.SemaphoreType.DMA((n,)), pltpu.SemaphoreType.DMA((n,))],
    )(*arrays)


def _as_rows(a, lead=0):
    shp = a.shape
    return a.reshape(shp[:lead] + (-1, shp[-1]))


def _sum4_call(x, name):
    _, r, c = x.shape
    tr = _pick(r, [t for t in (512, 256, 128, 64, 32, 16, 8) if 16 * t * c <= _BLOCK_BYTES] + [r])

    def body(x_ref, o_ref):
        part = [x_ref[j].astype(F32) for j in range(4)]
        o_ref[...] = (part[0] + part[1]) + (part[2] + part[3])

    return pl.pallas_call(
        body, name=name, grid=(r // tr,),
        in_specs=[pl.BlockSpec((4, tr, c), lambda i: (0, i, 0))],
        out_specs=pl.BlockSpec((tr, c), lambda i: (i, 0)),
        out_shape=jax.ShapeDtypeStruct((r, c), F32),
        compiler_params=_cparams(("parallel",), 48),
    )(x)


def _add_to_bf16_call(a, b, name):
    _, r, c = a.shape
    tr = _pick(r, [t for t in (512, 256, 128, 64, 32, 16) if 16 * t * c <= _BLOCK_BYTES] + [r])

    def body(a_ref, b_ref, o_ref):
        o_ref[...] = (a_ref[...].astype(F32) + b_ref[...].astype(F32)).astype(BF16)

    spec = pl.BlockSpec((4, tr, c), lambda i: (0, i, 0))
    return pl.pallas_call(
        body, name=name, grid=(r // tr,), in_specs=[spec, spec], out_specs=spec,
        out_shape=jax.ShapeDtypeStruct(a.shape, BF16), compiler_params=_cparams(("parallel",), 48),
    )(a, b)


def _adam_call(w, grads, m, v, name):
    r, c = w.shape
    n_g = len(grads)
    tr = _pick(r, [t for t in (512, 256, 128, 64, 32, 16, 8) if 4 * t * c <= _BLOCK_BYTES // 4] + [r])
    bc1 = 1.0 - ADAM_B1 ** ADAM_STEP
    bc2 = 1.0 - ADAM_B2 ** ADAM_STEP

    def body(*refs):
        w_ref, g_refs = refs[0], refs[1:1 + n_g]
        m_ref, v_ref, g_out, d_out, m_out, v_out = refs[1 + n_g:]
        g = g_refs[0][...]
        for g_ref in g_refs[1:]:
            g = g + g_ref[...]
        m_new = ADAM_B1 * m_ref[...] + (1.0 - ADAM_B1) * g
        v_new = ADAM_B2 * v_ref[...] + (1.0 - ADAM_B2) * (g * g)
        m_hat = m_new / bc1
        v_hat = v_new / bc2
        g_out[...] = g
        d_out[...] = -ADAM_LR * (m_hat / (jnp.sqrt(v_hat) + ADAM_EPS) + ADAM_WD * w_ref[...])
        m_out[...] = m_new
        v_out[...] = v_new

    spec = pl.BlockSpec((tr, c), lambda i: (i, 0))
    return pl.pallas_call(
        body, name=name, grid=(r // tr,), in_specs=[spec] * (3 + n_g), out_specs=[spec] * 4,
        out_shape=[jax.ShapeDtypeStruct((r, c), F32)] * 4,
        compiler_params=_cparams(("parallel",), 48),
    )(w, *grads, m, v)


_DENSE = ("w_in", "w_branch", "w_out")

_SHARDED = {"meta": (1, False), "w_in": (2, True), "s5_w_glu": (1, True), "ssd_conv_w": (2, False),
            "gdn_conv_w": (2, False), "w_branch": (3, True), "b_gate": (2, False), "w_out": (1, True)}


def _pack(arrs):
    flat = jnp.concatenate([a.reshape(-1) for a in arrs])
    n = flat.shape[0]
    rows = -(-n // (256 * LANES)) * 256
    return jnp.concatenate([flat, jnp.zeros((rows * LANES - n,), F32)]).reshape(rows, LANES)


def _unpack(packed, like):
    flat = packed.reshape(-1)
    out, off = [], 0
    for a in like:
        out.append(flat[off:off + a.size].reshape(a.shape))
        off += a.size
    return out


def kernel(x, meta, ln_in_g, ln_in_b, w_in, s5_a_re, s5_a_im, s5_log_step, s5_b_re, s5_b_im, s5_c_re, s5_c_im, s5_d, s5_w_glu, s5_b_glu, ssd_conv_w, ssd_conv_b, ssd_dt_bias, ssd_a_log, ssd_d, ssd_norm_g, gdn_conv_w, gdn_dt_bias, gdn_a_log, gdn_norm_g, w_branch, b_gate, w_out, ln_g, ln_b, loss_target, m_meta, m_ln_in_g, m_ln_in_b, m_w_in, m_s5_a_re, m_s5_a_im, m_s5_log_step, m_s5_b_re, m_s5_b_im, m_s5_c_re, m_s5_c_im, m_s5_d, m_s5_w_glu, m_s5_b_glu, m_ssd_conv_w, m_ssd_conv_b, m_ssd_dt_bias, m_ssd_a_log, m_ssd_d, m_ssd_norm_g, m_gdn_conv_w, m_gdn_dt_bias, m_gdn_a_log, m_gdn_norm_g, m_w_branch, m_b_gate, m_w_out, m_ln_g, m_ln_b, v_meta, v_ln_in_g, v_ln_in_b, v_w_in, v_s5_a_re, v_s5_a_im, v_s5_log_step, v_s5_b_re, v_s5_b_im, v_s5_c_re, v_s5_c_im, v_s5_d, v_s5_w_glu, v_s5_b_glu, v_ssd_conv_w, v_ssd_conv_b, v_ssd_dt_bias, v_ssd_a_log, v_ssd_d, v_ssd_norm_g, v_gdn_conv_w, v_gdn_dt_bias, v_gdn_a_log, v_gdn_norm_g, v_w_branch, v_b_gate, v_w_out, v_ln_g, v_ln_b):
    args = dict(locals())
    shards = {k: args[k] for k in _WEIGHT_KEYS}
    moms = {k: (args["m_" + k], args["v_" + k]) for k in _WEIGHT_KEYS}

    core = lax.axis_index("c")
    halves = lambda a: a.reshape((2, a.shape[0] // 2) + a.shape[1:])

    names = list(_SHARDED)
    sent = [lax.dynamic_index_in_dim(halves(shards[k]), core, 0, keepdims=False) for k in names]
    sent = [s.astype(BF16) if _SHARDED[k][1] else s for k, s in zip(names, sent)]
    gathered = _gather_two_level(sent, "gather_weights")
    my_chip = 2 * lax.axis_index("x") + lax.axis_index("y")
    own_block = lambda blocks, mine: lax.dynamic_update_index_in_dim(blocks, mine.astype(blocks.dtype), my_chip, 0)
    full, compute = dict(shards), {}
    for k, g in zip(names, gathered):
        shp, ax = shards[k].shape, _SHARDED[k][0]
        g = own_block(g, halves(shards[k]))
        gathered_k = jnp.concatenate([g[j].reshape(shp) for j in range(4)], axis=ax)
        full[k] = gathered_k.astype(F32)
        if k in _DENSE:
            compute[k] = gathered_k

    loss, (grads, grad_x) = jax.value_and_grad(_local_loss, argnums=(0, 1))(full, x[0], loss_target[0], compute)
    loss = lax.psum(loss, ("x", "y", "c"))

    blocks = []
    for k in names:
        per_chip = jnp.stack(jnp.split(grads[k].astype(BF16), 4, axis=_SHARDED[k][0]), axis=0)
        blocks.append(jnp.moveaxis(_as_rows(per_chip.reshape((4, 2, -1) + per_chip.shape[2:]), 2), 1, 0))
    theirs = _core_exchange(blocks, True, "swap_halves")
    mine = [lax.dynamic_index_in_dim(b, core, 0, keepdims=False) for b in blocks]
    chip_sums = [_add_to_bf16_call(a, b, "sum_cores_" + k) for k, a, b in zip(names, mine, theirs)]
    arrived = _chip_exchange(chip_sums, True, "scatter_grads")
    arrived = [own_block(a, lax.dynamic_index_in_dim(s, my_chip, 0, keepdims=False)) for a, s in zip(arrived, chip_sums)]
    owned = [_sum4_call(a, "sum_chips_" + k) for k, a in zip(names, arrived)]
    others = _core_exchange(owned, False, "swap_owned")
    shared = [jnp.concatenate([jnp.where(core == 0, a, b), jnp.where(core == 0, b, a)], axis=0)
              for a, b in zip(owned, others)]

    small_names = [k for k in _WEIGHT_KEYS if k not in _SHARDED]
    packed = _pack([grads[k] for k in small_names])
    (packed4,) = _chip_exchange([packed], False, "gather_small_grads")
    small_sum = _sum4_call(own_block(packed4, packed), "sum_chips_small")
    (small_other,) = _core_exchange([small_sum], False, "swap_small")

    outs = {}
    for k, g in zip(names, shared):
        shp = shards[k].shape
        rows = _as_rows(shards[k]).shape
        res = _adam_call(_as_rows(shards[k]), [g.reshape(rows)], _as_rows(moms[k][0]), _as_rows(moms[k][1]), "adamw_" + k)
        outs[k] = [r.reshape(shp) for r in res]
    like = [shards[k] for k in small_names]
    res = _adam_call(_pack(like), [small_sum, small_other], _pack([moms[k][0] for k in small_names]),
                     _pack([moms[k][1] for k in small_names]), "adamw_small")
    for idx in range(4):
        for k, a in zip(small_names, _unpack(res[idx], like)):
            outs.setdefault(k, [None] * 4)[idx] = a

    result = [loss, grad_x[None]]
    for idx in range(4):
        result += [outs[k][idx] for k in _WEIGHT_KEYS]
    return tuple(result)
```

```python
import functools

import jax
import jax.numpy as jnp
from jax import lax
from jax.experimental import pallas as pl
from jax.experimental.pallas import tpu as pltpu

F32 = jnp.float32
BF16 = jnp.bfloat16
MESH = pl.DeviceIdType.MESH
F32_DOT = lax.Precision.HIGH

D_MODEL = 1024
DEPTH = 4
N_META = 16
CHUNK = 64
PAD = CHUNK - N_META
CONV_K = 4
HALO = 8
WIDTH = 768
S5_GROUPS, S5_GROUP, S5_STATE = 48, 16, 64
S5_BLOCKS = 6
S5_SEGMENTS = 8
SSD_HEADS, SSD_HEAD_DIM, SSD_GROUPS, SSD_STATE = 12, 64, 2, 128
SSD_PAIRS = 6
GDN_HEADS, GDN_DIM = 6, 128
LANES = 128
SMALL_ROWS = 24
ALPHA = (2 * DEPTH) ** 0.25
LN_EPS = 1e-5
P_S5U, P_S5Z, P_XBC, P_DT, P_SSDZ, P_QKV, P_GA, P_GB, P_GDNZ, P_GATE, P_END = (
    0, 768, 1536, 2816, 2828, 3596, 5900, 5906, 5912, 6680, 9752)
ADAM_LR, ADAM_B1, ADAM_B2, ADAM_EPS, ADAM_WD, ADAM_STEP = 0.001, 0.9, 0.999, 1e-08, 0.01, 10
NEG = -1e30


def _pick(n, cands):
    for c in cands:
        if n % c == 0:
            return c
    raise ValueError(f"no tile for {n} in {cands}")


def _cparams(sem, vmem_mb):
    return pltpu.CompilerParams(dimension_semantics=sem, vmem_limit_bytes=vmem_mb << 20)


_DIMS = {"nn": (((1,), (0,)), ((), ())), "nt": (((1,), (1,)), ((), ())), "tn": (((0,), (0,)), ((), ()))}


def _dot(a, b, mode, hi):
    if hi:
        prec = lax.Precision.HIGHEST if hi == "exact" else F32_DOT
        return lax.dot_general(a, b, _DIMS[mode], precision=prec, preferred_element_type=F32)
    return lax.dot_general(a.astype(BF16), b.astype(BF16), _DIMS[mode], preferred_element_type=F32)


@functools.partial(jax.custom_vjp, nondiff_argnums=(2, 3))
def _mm(a, b, mode="nn", hi=False):
    return _dot(a, b, mode, hi)


def _mm_fwd(a, b, mode, hi):
    return _dot(a, b, mode, hi), (a, b)


def _mm_bwd(mode, hi, res, g):
    a, b = res
    if mode == "nn":
        return _dot(g, b, "nt", hi), _dot(a, g, "tn", hi)
    if mode == "nt":
        return _dot(g, b, "nn", hi), _dot(g, a, "tn", hi)
    return _dot(b, g, "nt", hi), _dot(a, g, "nn", hi)


_mm.defvjp(_mm_fwd, _mm_bwd)


@functools.partial(jax.custom_vjp, nondiff_argnums=(1,))
def _roll_rows(x, k):
    return pltpu.roll(x, k % x.shape[0], 0)


def _roll_fwd(x, k):
    return _roll_rows(x, k), None


def _roll_bwd(k, _, g):
    return (_roll_rows(g, -k),)


_roll_rows.defvjp(_roll_fwd, _roll_bwd)


def _iota(shape, dim):
    return lax.broadcasted_iota(jnp.int32, shape, dim)


def _valid_rows(row0, n):
    return (row0 + _iota((n, 1), 0)) >= PAD


def _lane(x, idx):
    return jnp.sum(jnp.where(_iota(x.shape, 1) == idx, x, 0.0), axis=1, keepdims=True)


def _row(x, idx):
    return jnp.sum(jnp.where(_iota(x.shape, 0) == idx, x, 0.0), axis=0, keepdims=True)


def _layer_norm(z, g, b):
    mu = jnp.mean(z, axis=-1, keepdims=True)
    zc = z - mu
    var = jnp.mean(zc * zc, axis=-1, keepdims=True)
    return zc * lax.rsqrt(var + LN_EPS) * g + b


def _rms_norm(z, g):
    return z * lax.rsqrt(jnp.mean(z * z, axis=-1, keepdims=True) + LN_EPS) * g


def _causal_conv(halo, x, w, row0):
    t = x.shape[0]
    xc = jnp.concatenate([halo, x], axis=0)
    acc = None
    for j in range(CONV_K):
        term = _roll_rows(xc, CONV_K - 1 - j)[HALO:HALO + t] * _row(w, j)
        acc = term if acc is None else acc + term
    return acc, x[t - HALO:t]


def _tri(n, strict=False):
    r, c = _iota((n, n), 0), _iota((n, n), 1)
    return (r > c) if strict else (r >= c)


def _scan_op(step, name, *, tile, nb, row_kinds, param_kinds, carry_shapes, out_widths, keep_shapes=(), vmem_mb=48):
    n_rows, n_par, n_car, n_out = len(row_kinds), len(param_kinds), len(carry_shapes), len(out_widths)
    n_keep = len(keep_shapes)
    saved_shapes = tuple(carry_shapes) + tuple(keep_shapes)

    def dims(rows):
        for k, a in zip(row_kinds, rows):
            if k in "bs":
                return a.shape[0], a.shape[0] // tile
        raise ValueError("need a row input")

    def row_spec(kind, a, rev, nt):
        ti = (lambda i: nt - 1 - i) if rev else (lambda i: i)
        if kind == "b":
            return pl.BlockSpec((tile, a.shape[1] // nb), lambda b, i: (ti(i), b))
        if kind == "s":
            return pl.BlockSpec((tile, a.shape[1]), lambda b, i: (ti(i), 0))
        return pl.BlockSpec((None, a.shape[1], a.shape[2]), lambda b, i: (ti(i), 0, 0))

    def par_spec(kind, a):
        if kind == "b":
            return pl.BlockSpec((None, a.shape[1], a.shape[2]), lambda b, i: (b, 0, 0))
        return pl.BlockSpec(a.shape, lambda b, i: (0, 0))

    def fwd_call(rows, params):
        length, nt = dims(rows)

        def body(*refs):
            r_in = refs[:n_rows]
            p_in = refs[n_rows:n_rows + n_par]
            o_out = refs[n_rows + n_par:n_rows + n_par + n_out]
            s_out = refs[n_rows + n_par + n_out:n_rows + n_par + n_out + n_car + n_keep]
            c_scr = refs[n_rows + n_par + n_out + n_car + n_keep:]
            b, i = pl.program_id(0), pl.program_id(1)

            if n_car:
                @pl.when(i == 0)
                def _():
                    for c in c_scr:
                        c[...] = jnp.zeros_like(c)

            cin = tuple(c[...] for c in c_scr)
            for s, c in zip(s_out, cin):
                s[...] = c
            res = step(cin, tuple(r[...] for r in r_in), tuple(p[...] for p in p_in), b, i * tile)
            new_c, outs = res[0], res[1]
            for c, v in zip(c_scr, new_c):
                c[...] = v
            for o, v in zip(o_out, outs):
                o[...] = v
            if n_keep:
                for s, v in zip(s_out[n_car:], res[2]):
                    s[...] = v

        out_shape = [jax.ShapeDtypeStruct((length, nb * w), F32) for w in out_widths]
        out_shape += [jax.ShapeDtypeStruct((nb, nt) + tuple(s), F32) for s in saved_shapes]
        out_specs = [pl.BlockSpec((tile, w), lambda b, i: (i, b)) for w in out_widths]
        out_specs += [pl.BlockSpec((None, None) + tuple(s), lambda b, i: (b, i, 0, 0)) for s in saved_shapes]
        res = pl.pallas_call(
            body, name=name + "_fwd", grid=(nb, nt),
            in_specs=[row_spec(k, a, False, nt) for k, a in zip(row_kinds, rows)]
            + [par_spec(k, a) for k, a in zip(param_kinds, params)],
            out_specs=out_specs, out_shape=out_shape,
            scratch_shapes=[pltpu.VMEM(tuple(s), F32) for s in carry_shapes],
            compiler_params=_cparams(("arbitrary", "arbitrary"), vmem_mb),
        )(*rows, *params)
        return tuple(res[:n_out]), tuple(res[n_out:])

    def bwd_call(rows, params, saved, douts):
        length, nt = dims(rows)

        def body(*refs):
            k0 = 0
            r_in = refs[k0:k0 + n_rows]; k0 += n_rows
            p_in = refs[k0:k0 + n_par]; k0 += n_par
            s_in = refs[k0:k0 + n_car]; k0 += n_car
            k_in = refs[k0:k0 + n_keep]; k0 += n_keep
            g_in = refs[k0:k0 + n_out]; k0 += n_out
            dr_out = refs[k0:k0 + n_rows]; k0 += n_rows
            dp_out = refs[k0:k0 + n_par]; k0 += n_par
            dc_scr = refs[k0:]
            b, i = pl.program_id(0), pl.program_id(1)
            row0 = (nt - 1 - i) * tile

            @pl.when(i == 0)
            def _():
                for c in dc_scr:
                    c[...] = jnp.zeros_like(c)
                for p in dp_out:
                    p[...] = jnp.zeros_like(p)

            def f(c, r, p):
                if n_keep:
                    return step(c, r, p, b, row0, kept=tuple(k[...] for k in k_in))[:2]
                return step(c, r, p, b, row0)

            _, vjp = jax.vjp(f, tuple(s[...] for s in s_in), tuple(r[...] for r in r_in),
                             tuple(p[...] for p in p_in))
            dc, dr, dp = vjp((tuple(c[...] for c in dc_scr), tuple(g[...] for g in g_in)))
            for c, v in zip(dc_scr, dc):
                c[...] = v
            for o, v in zip(dr_out, dr):
                o[...] = v
            for o, v in zip(dp_out, dp):
                o[...] += v

        rev = lambda i: nt - 1 - i
        in_specs = [row_spec(k, a, True, nt) for k, a in zip(row_kinds, rows)]
        in_specs += [par_spec(k, a) for k, a in zip(param_kinds, params)]
        in_specs += [pl.BlockSpec((None, None) + tuple(s), lambda b, i: (b, rev(i), 0, 0)) for s in saved_shapes]
        in_specs += [pl.BlockSpec((tile, w), lambda b, i: (rev(i), b)) for w in out_widths]
        out_shape, out_specs = [], []
        for k, a in zip(row_kinds, rows):
            if k == "b":
                out_shape.append(jax.ShapeDtypeStruct(a.shape, F32))
                out_specs.append(pl.BlockSpec((tile, a.shape[1] // nb), lambda b, i: (rev(i), b)))
            elif k == "s":
                out_shape.append(jax.ShapeDtypeStruct((nb,) + a.shape, F32))
                out_specs.append(pl.BlockSpec((None, tile, a.shape[1]), lambda b, i: (b, rev(i), 0)))
            else:
                out_shape.append(jax.ShapeDtypeStruct((nb,) + a.shape, F32))
                out_specs.append(pl.BlockSpec((None, None, a.shape[1], a.shape[2]), lambda b, i: (b, rev(i), 0, 0)))
        for k, a in zip(param_kinds, params):
            shp = a.shape[1:] if k == "b" else a.shape
            out_shape.append(jax.ShapeDtypeStruct((nb,) + tuple(shp), F32))
            out_specs.append(pl.BlockSpec((None,) + tuple(shp), lambda b, i: (b, 0, 0)))
        res = pl.pallas_call(
            body, name=name + "_bwd", grid=(nb, nt), in_specs=in_specs, out_specs=out_specs, out_shape=out_shape,
            scratch_shapes=[pltpu.VMEM(tuple(s), F32) for s in carry_shapes],
            compiler_params=_cparams(("arbitrary", "arbitrary"), vmem_mb),
        )(*rows, *params, *saved, *douts)
        fold = (lambda a: a[0]) if nb == 1 else (lambda a: jnp.sum(a, axis=0))
        drows = tuple(r if k == "b" else fold(r) for k, r in zip(row_kinds, res[:n_rows]))
        dpars = tuple(p if k == "b" else fold(p) for k, p in zip(param_kinds, res[n_rows:]))
        return drows, dpars

    @jax.custom_vjp
    def op(rows, params):
        return fwd_call(rows, params)[0]

    def op_fwd(rows, params):
        outs, saved = fwd_call(rows, params)
        return outs, (rows, params, saved)

    def op_bwd(res, douts):
        rows, params, saved = res
        return bwd_call(rows, params, saved, tuple(douts))

    op.defvjp(op_fwd, op_bwd)
    return op


_WIDE = 1280


def _mm_rows(m, n):
    return _pick(m, (832, 128)) if n <= _WIDE else _pick(m, (416, 128))


def _mm_fwd_call(x, w, name):
    m, k = x.shape
    n = w.shape[1]
    tm = _mm_rows(m, n)

    def body(x_ref, w_ref, o_ref):
        o_ref[...] = _dot(x_ref[...], w_ref[...], "nn", False)

    return pl.pallas_call(
        body, name=name, grid=(m // tm,),
        in_specs=[pl.BlockSpec((tm, k), lambda i: (i, 0)), pl.BlockSpec((k, n), lambda i: (0, 0))],
        out_specs=pl.BlockSpec((tm, n), lambda i: (i, 0)),
        out_shape=jax.ShapeDtypeStruct((m, n), F32),
        compiler_params=_cparams(("parallel",), 48),
    )(x, w)


def _mm_bwd_call(g, x, w, name):
    m, n = g.shape
    k = w.shape[0]
    tm = _mm_rows(m, n)

    def body(g_ref, x_ref, w_ref, dx_ref, dw_ref):
        @pl.when(pl.program_id(0) == 0)
        def _():
            dw_ref[...] = jnp.zeros_like(dw_ref)

        g = g_ref[...].astype(BF16)
        dx_ref[...] = _dot(g, w_ref[...], "nt", False)
        x = x_ref[...]
        step = _pick(n, (768, 640, 128))
        for c0 in range(0, n, step):
            dw_ref[:, c0:c0 + step] += _dot(x, g[:, c0:c0 + step], "tn", False)

    return pl.pallas_call(
        body, name=name, grid=(m // tm,),
        in_specs=[pl.BlockSpec((tm, n), lambda i: (i, 0)), pl.BlockSpec((tm, k), lambda i: (i, 0)),
                  pl.BlockSpec((k, n), lambda i: (0, 0))],
        out_specs=[pl.BlockSpec((tm, k), lambda i: (i, 0)), pl.BlockSpec((k, n), lambda i: (0, 0))],
        out_shape=[jax.ShapeDtypeStruct((m, k), F32), jax.ShapeDtypeStruct((k, n), F32)],
        compiler_params=_cparams(("arbitrary",), 56),
    )(g, x, w)


def _dense(name):
    @jax.custom_vjp
    def op(x, w, xb, wb):
        return _mm_fwd_call(xb, wb, name + "_fwd")

    def op_fwd(x, w, xb, wb):
        return _mm_fwd_call(xb, wb, name + "_fwd"), (xb, wb)

    def op_bwd(res, g):
        xb, wb = res
        dx, dw = _mm_bwd_call(g, xb, wb, name + "_bwd")
        return dx, dw, jnp.zeros_like(xb), jnp.zeros_like(wb)

    op.defvjp(op_fwd, op_bwd)
    return op


def _bf16_copy(v):
    return lax.stop_gradient(v).astype(BF16)


def _ln_in_step(c, rows, params, b, row0):
    (z,), (g, bb) = rows, params
    return (), (jnp.where(_valid_rows(row0, z.shape[0]), _layer_norm(z, g, bb), 0.0),)


def _ln_res_step(c, rows, params, b, row0):
    (h, o), (g, bb) = rows, params
    return (), (jnp.where(_valid_rows(row0, h.shape[0]), _layer_norm(ALPHA * h + o, g, bb), 0.0),)


def _s5_prep_step(c, rows, params, b, row0):
    a_re, a_im, log_step, b_re, b_im = rows
    lam_re = jnp.minimum(a_re, -1e-4)
    lam_im = a_im
    step = jnp.exp(log_step)
    mag = jnp.exp(lam_re * step)
    abar_re, abar_im = mag * jnp.cos(lam_im * step), mag * jnp.sin(lam_im * step)
    den = lam_re * lam_re + lam_im * lam_im
    nr, ni = abar_re - 1.0, abar_im
    coef_re = (nr * lam_re + ni * lam_im) / den
    coef_im = (ni * lam_re - nr * lam_im) / den
    return (), (abar_re, abar_im, coef_re * b_re - coef_im * b_im, coef_re * b_im + coef_im * b_re)


def _s5_scan_step(c, rows, params, b, row0):
    (c_re, c_im), (u,) = c, rows
    bd_re, bd_im, a_re, a_im, cd_re, cd_im = params
    t = u.shape[0]
    steps = t // S5_SEGMENTS
    bu_re, bu_im = _mm(u, bd_re), _mm(u, bd_im)
    a_re, a_im = (jnp.broadcast_to(v, (S5_SEGMENTS, v.shape[1])) for v in (a_re, a_im))
    at = lambda v, i: v[S5_SEGMENTS * i:S5_SEGMENTS * (i + 1)]
    s_re, s_im = at(bu_re, 0), at(bu_im, 0)
    p_re, p_im = a_re, a_im
    local, power = [(s_re, s_im)], [(p_re, p_im)]
    for i in range(1, steps):
        s_re, s_im = a_re * s_re - a_im * s_im + at(bu_re, i), a_re * s_im + a_im * s_re + at(bu_im, i)
        p_re, p_im = a_re * p_re - a_im * p_im, a_re * p_im + a_im * p_re
        local.append((s_re, s_im))
        power.append((p_re, p_im))
    seg = _iota((S5_SEGMENTS, 1), 0)
    in_re = jnp.where(seg == 0, c_re, _roll_rows(s_re, 1))
    in_im = jnp.where(seg == 0, c_im, _roll_rows(s_im, 1))
    q_re, q_im = p_re, p_im
    d = 1
    while d < S5_SEGMENTS:
        keep = seg >= d
        sh_re = jnp.where(keep, _roll_rows(in_re, d), 0.0)
        sh_im = jnp.where(keep, _roll_rows(in_im, d), 0.0)
        in_re, in_im = in_re + q_re * sh_re - q_im * sh_im, in_im + q_re * sh_im + q_im * sh_re
        q_re, q_im = q_re * q_re - q_im * q_im, 2.0 * q_re * q_im
        d *= 2
    full = [(l_re + w_re * in_re - w_im * in_im, l_im + w_re * in_im + w_im * in_re)
            for (l_re, l_im), (w_re, w_im) in zip(local, power)]
    y = _mm(jnp.concatenate([f[0] for f in full], axis=0), cd_re) - _mm(jnp.concatenate([f[1] for f in full], axis=0), cd_im)
    return (_row(full[-1][0], S5_SEGMENTS - 1), _row(full[-1][1], S5_SEGMENTS - 1)), (y,)


def _interleave(v, tile, inverse=False):
    length, width = v.shape
    shape = (length // tile, tile // S5_SEGMENTS, S5_SEGMENTS) if inverse else (length // tile, S5_SEGMENTS, tile // S5_SEGMENTS)
    return v.reshape(shape + (width,)).transpose(0, 2, 1, 3).reshape(length, width)


def _s5_post_step(c, rows, params, b, row0):
    (y, u, z), (d, w_glu, b_glu) = rows, params
    v = jax.nn.gelu(y + d * u)
    v = v * jax.nn.sigmoid(_mm(v, w_glu) + b_glu)
    return (), (v * jax.nn.silu(z),)


def _ssd_step(c, rows, params, b, row0):
    halo, state = c
    xbc_raw, z, small, small_t = rows
    cw, cb, d_l, bias_l, alog_l, bias_c, alog_c, norm_g = params
    t = xbc_raw.shape[0]
    grp = SSD_GROUPS * SSD_STATE
    valid = _valid_rows(row0, t)
    conv, halo2 = _causal_conv(halo, xbc_raw, cw, row0)
    act = jnp.where(valid, jax.nn.silu(conv + cb), 0.0)
    low = _iota((1, LANES), 1) < SSD_HEAD_DIM
    dt_all = jnp.where(valid, jax.nn.softplus(small + bias_l), 0.0)
    a_all = -jnp.exp(alog_l)
    valid_t = (row0 + _iota((1, t), 1)) >= PAD
    dta_t = jnp.where(valid_t, jax.nn.softplus(small_t + bias_c), 0.0) * (-jnp.exp(alog_c))
    acum_t = _mm(dta_t, jnp.where(_iota((t, t), 0) <= _iota((t, t), 1), 1.0, 0.0), "nn", True)
    causal = _tri(t)
    acum_all = _mm(jnp.where(causal, 1.0, 0.0), dt_all * a_all, "nn", True)
    last_all = _row(acum_all, t - 1)
    low_rows = _iota((LANES, 1), 0) < SSD_HEAD_DIM
    pairs, groups = range(SSD_PAIRS), range(SSD_GROUPS)
    grp_of = [p // (SSD_PAIRS // SSD_GROUPS) for p in pairs]
    bs = [act[:, WIDTH + g * SSD_STATE:WIDTH + (g + 1) * SSD_STATE] for g in groups]
    cs = [act[:, WIDTH + grp + g * SSD_STATE:WIDTH + grp + (g + 1) * SSD_STATE] for g in groups]
    scores = [_mm(cs[g], bs[g], "nt") for g in groups]
    per_lane = lambda v, p: jnp.where(low, v[:, 2 * p:2 * p + 1], v[:, 2 * p + 1:2 * p + 2])
    dt_l = [per_lane(dt_all, p) for p in pairs]
    acum_l = [per_lane(acum_all, p) for p in pairs]
    last_l = [per_lane(last_all, p) for p in pairs]
    st = [state[p * LANES:(p + 1) * LANES] for p in pairs]
    xd = [act[:, p * LANES:(p + 1) * LANES] * dt_l[p] for p in pairs]
    decay = [jnp.exp(jnp.where(causal, acum_all[:, h:h + 1] - _row(acum_t, h), NEG)) for h in range(SSD_HEADS)]
    y_lo = [_mm(scores[grp_of[p]] * decay[2 * p], jnp.where(low, xd[p], 0.0)) for p in pairs]
    y_hi = [_mm(scores[grp_of[p]] * decay[2 * p + 1], jnp.where(low, 0.0, xd[p])) for p in pairs]
    y_off = [_mm(cs[grp_of[p]], st[p], "nt") * jnp.exp(acum_l[p]) for p in pairs]
    new_st = [_mm(xd[p] * jnp.exp(last_l[p] - acum_l[p]), bs[grp_of[p]], "tn") for p in pairs]
    cd = jnp.exp(last_all)
    new_st = [st[p] * jnp.where(low_rows, cd[:, 2 * p:2 * p + 1], cd[:, 2 * p + 1:2 * p + 2]) + new_st[p] for p in pairs]
    y = jnp.concatenate([y_lo[p] + y_hi[p] + y_off[p] for p in pairs], axis=1) + act[:, :WIDTH] * d_l
    out = _rms_norm(y * jax.nn.silu(z), norm_g)
    return (halo2, jnp.concatenate(new_st, axis=0)), (out,)


@jax.custom_vjp
def _unit_lower_inverse(mats):
    return _neumann_inverse(mats)


def _inverse_fwd(mats):
    inv = _neumann_inverse(mats)
    return inv, inv


def _inverse_bwd(inv, g):
    left = [_dot(t, d, "tn", True) for t, d in zip(inv, g)]
    return ([-_dot(l, t, "nt", True) for l, t in zip(left, inv)],)


_unit_lower_inverse.defvjp(_inverse_fwd, _inverse_bwd)


@jax.custom_vjp
def _kept_inverse(mats, inv):
    return list(inv)


def _kept_inverse_fwd(mats, inv):
    return list(inv), list(inv)


def _kept_inverse_bwd(inv, g):
    return _inverse_bwd(inv, g)[0], [jnp.zeros_like(t) for t in inv]


_kept_inverse.defvjp(_kept_inverse_fwd, _kept_inverse_bwd)


def _neumann_inverse(mats):
    n = mats[0].shape[0]
    eye = jnp.where(_iota((n, n), 0) == _iota((n, n), 1), 1.0, 0.0)
    inv = [eye - a for a in mats]
    p = [_mm(a, a, "nn", True) for a in mats]
    k = 2
    while k < n:
        inv = [i + _mm(i, q, "nn", True) for i, q in zip(inv, p)]
        k *= 2
        if k < n:
            p = [_mm(q, q, "nn", True) for q in p]
    return inv


def _gdn_step(c, rows, params, b, row0, kept=None):
    halo, state = c
    qkv_raw, z, small, small_t = rows
    cw, bias_l, alog_l, bias_c, alog_c, norm_g = params
    t = qkv_raw.shape[0]
    valid = _valid_rows(row0, t)
    conv, halo2 = _causal_conv(halo, qkv_raw, cw, row0)
    act = jnp.where(valid, jax.nn.silu(conv), 0.0)
    beta_all = jnp.where(valid, jax.nn.sigmoid(small), 0.0)
    g_all = jnp.where(valid, -jnp.exp(alog_l) * jax.nn.softplus(small + bias_l), 0.0)
    valid_t = (row0 + _iota((1, t), 1)) >= PAD
    g_t = jnp.where(valid_t, -jnp.exp(alog_c) * jax.nn.softplus(small_t + bias_c), 0.0)
    causal, strict = _tri(t), _tri(t, True)
    gcum_all = _mm(jnp.where(causal, 1.0, 0.0), g_all, "nn", True)
    gcum_t = _mm(g_t, jnp.where(_iota((t, t), 0) <= _iota((t, t), 1), 1.0, 0.0), "nn", True)
    heads = range(GDN_HEADS)
    part = lambda h, n: act[:, n * WIDTH + h * GDN_DIM:n * WIDTH + (h + 1) * GDN_DIM]
    unit = lambda x: x * lax.rsqrt(jnp.sum(x * x, axis=-1, keepdims=True) + 1e-6)
    q = [unit(part(h, 0)) * (GDN_DIM ** -0.5) for h in heads]
    k = [unit(part(h, 1)) for h in heads]
    st = [state[h * GDN_DIM:(h + 1) * GDN_DIM] for h in heads]
    ia = [SSD_HEADS + h for h in heads]
    beta = [beta_all[:, ia[h] + GDN_HEADS:ia[h] + GDN_HEADS + 1] for h in heads]
    gcum = [gcum_all[:, ia[h]:ia[h] + 1] for h in heads]
    gamma = [jnp.exp(jnp.where(causal, gcum[h] - _row(gcum_t, ia[h]), NEG)) for h in heads]
    egc = [jnp.exp(gcum[h]) for h in heads]
    a_mat = [jnp.where(strict, _mm(k[h], k[h], "nt") * gamma[h] * beta[h], 0.0) for h in heads]
    if kept is None:
        inv = _unit_lower_inverse(a_mat)
    else:
        inv = _kept_inverse(a_mat, [kept[0][h * t:(h + 1) * t] for h in heads])
    rhs = [jnp.concatenate([part(h, 2) * beta[h], k[h] * (beta[h] * egc[h])], axis=1) for h in heads]
    sol = [_mm(inv[h], rhs[h], "nn", True) for h in heads]
    attn = [_mm(q[h], k[h], "nt") * gamma[h] for h in heads]
    from_state = [_mm(jnp.concatenate([sol[h][:, GDN_DIM:], q[h] * egc[h]], axis=0), st[h]) for h in heads]
    v_new = [sol[h][:, :GDN_DIM] - from_state[h][:t] for h in heads]
    o = [from_state[h][t:] + _mm(attn[h], v_new[h]) for h in heads]
    glast = [_row(gcum[h], t - 1) for h in heads]
    new_st = [st[h] * jnp.exp(glast[h]) + _mm(k[h] * jnp.exp(glast[h] - gcum[h]), v_new[h], "tn") for h in heads]
    out = jnp.concatenate([_rms_norm(o[h], norm_g) for h in heads], axis=1) * jax.nn.silu(z)
    return (halo2, jnp.concatenate(new_st, axis=0)), (out,), (jnp.concatenate(inv, axis=0),)


def _gate_merge_step(c, rows, params, b, row0):
    (bg,) = params
    acc = None
    for n, (o, gl) in enumerate(zip(rows[:3], rows[3:])):
        term = jax.nn.sigmoid(gl + bg[:, n * D_MODEL:(n + 1) * D_MODEL]) * o
        acc = term if acc is None else acc + term
    return (), (acc,)


def _loss_tile(n):
    return _pick(n, (512, 256, 128, 64))


def _loss_fwd_call(y, tgt):
    n, d = y.shape
    tile = _loss_tile(n)

    def body(y_ref, t_ref, o_ref):
        @pl.when(pl.program_id(0) == 0)
        def _():
            o_ref[...] = jnp.zeros_like(o_ref)

        e = y_ref[...] - t_ref[...]
        o_ref[...] += jnp.sum(jnp.sum(e * e, axis=1, keepdims=True), axis=0, keepdims=True) * (0.5 / d)

    out = pl.pallas_call(
        body, name="loss_fwd", grid=(n // tile,),
        in_specs=[pl.BlockSpec((tile, d), lambda i: (i, 0)), pl.BlockSpec((tile, d), lambda i: (i, 0))],
        out_specs=pl.BlockSpec((8, LANES), lambda i: (0, 0)),
        out_shape=jax.ShapeDtypeStruct((8, LANES), F32),
        compiler_params=_cparams(("arbitrary",), 32),
    )(y, tgt)
    return out[0, 0]


def _loss_bwd_call(y, tgt, g):
    n, d = y.shape
    tile = _loss_tile(n)

    def body(y_ref, t_ref, g_ref, o_ref):
        o_ref[...] = (y_ref[...] - t_ref[...]) * (g_ref[...][0:1, 0:1] * (1.0 / d))

    return pl.pallas_call(
        body, name="loss_bwd", grid=(n // tile,),
        in_specs=[pl.BlockSpec((tile, d), lambda i: (i, 0)), pl.BlockSpec((tile, d), lambda i: (i, 0)),
                  pl.BlockSpec((8, LANES), lambda i: (0, 0))],
        out_specs=pl.BlockSpec((tile, d), lambda i: (i, 0)),
        out_shape=jax.ShapeDtypeStruct((n, d), F32),
        compiler_params=_cparams(("parallel",), 32),
    )(y, tgt, jnp.broadcast_to(g, (8, LANES)).astype(F32))


@jax.custom_vjp
def _loss_op(y, tgt):
    return _loss_fwd_call(y, tgt)


def _loss_op_fwd(y, tgt):
    return _loss_fwd_call(y, tgt), (y, tgt)


def _loss_op_bwd(res, g):
    y, tgt = res
    return _loss_bwd_call(y, tgt, g), jnp.zeros_like(tgt)


_loss_op.defvjp(_loss_op_fwd, _loss_op_bwd)


def _rowwise(step, name, tile, n_rows, n_params, out_widths, vmem_mb=48):
    return _scan_op(step, name, tile=tile, nb=1, row_kinds="s" * n_rows, param_kinds="s" * n_params,
                    carry_shapes=(), out_widths=out_widths, vmem_mb=vmem_mb)


def _block_diag(x, nblk):
    bsz, _, r, c = x.shape
    eye = jnp.eye(nblk, dtype=x.dtype)
    return jnp.einsum("bgrc,gh->bgrhc", x, eye).reshape(bsz, nblk * r, nblk * c)


_PROJ = {"s5u": (P_S5U, P_S5Z), "s5z": (P_S5Z, P_XBC), "xbc": (P_XBC, P_DT), "ssdz": (P_SSDZ, P_QKV),
         "qkv": (P_QKV, P_GA), "gdnz": (P_GDNZ, P_GATE), "gate_a": (P_GATE, P_GATE + D_MODEL),
         "gate_b": (P_GATE + D_MODEL, P_GATE + 2 * D_MODEL), "gate_c": (P_GATE + 2 * D_MODEL, P_END)}


def _prepare(weights, compute):
    w = weights
    zeros = lambda *shape: jnp.zeros((DEPTH,) + shape, F32)
    row3 = lambda v: v.reshape(DEPTH, 1, -1)
    p = {}
    for pre, w_in in (("w_", w["w_in"]), ("wb_", compute["w_in"])):
        seg = lambda a, bnd: w_in[:, :, a:bnd]
        p.update({pre + k: seg(a, bnd) for k, (a, bnd) in _PROJ.items()})
        p[pre + "small"] = jnp.concatenate(
            [seg(P_DT, P_SSDZ), seg(P_GA, P_GDNZ), zeros(D_MODEL, LANES - SMALL_ROWS).astype(w_in.dtype)], axis=2)
    p.update(wb_branch=compute["w_branch"], wb_out=compute["w_out"])

    rows, wide = DEPTH * S5_GROUPS, S5_STATE * S5_GROUP
    flat = lambda v: v.reshape(rows, -1)
    rep = lambda v: jnp.repeat(flat(v), S5_GROUP, axis=1)
    prep = _rowwise(_s5_prep_step, "s5_prep", S5_GROUPS, 5, 0, (wide,) * 4)
    abar_re, abar_im, bbar_re, bbar_im = prep(
        (rep(w["s5_a_re"]), rep(w["s5_a_im"]), jnp.broadcast_to(flat(w["s5_log_step"]), (rows, wide)),
         flat(w["s5_b_re"]), flat(w["s5_b_im"])), ())
    gpb = S5_GROUPS // S5_BLOCKS
    lanes, chans, nblk = gpb * S5_STATE, gpb * S5_GROUP, DEPTH * S5_BLOCKS
    to_bd = lambda bb: _block_diag(bb.reshape(nblk, gpb, S5_STATE, S5_GROUP).transpose(0, 1, 3, 2), gpb).reshape(
        DEPTH, S5_BLOCKS, chans, lanes)
    to_cd = lambda cc: _block_diag(cc.reshape(nblk, gpb, S5_GROUP, S5_STATE).transpose(0, 1, 3, 2), gpb).reshape(
        DEPTH, S5_BLOCKS, lanes, chans)
    to_a = lambda a: a[:, ::S5_GROUP].reshape(DEPTH, S5_BLOCKS, 1, lanes)
    p.update(s5_bd_re=to_bd(bbar_re), s5_bd_im=to_bd(bbar_im), s5_a_re=to_a(abar_re), s5_a_im=to_a(abar_im),
             s5_cd_re=to_cd(w["s5_c_re"]), s5_cd_im=to_cd(w["s5_c_im"]),
             s5_d=row3(w["s5_d"]), s5_w_glu=w["s5_w_glu"], s5_b_glu=row3(w["s5_b_glu"]))

    bias = jnp.concatenate([w["ssd_dt_bias"], w["gdn_dt_bias"]], axis=1)
    alog = jnp.concatenate([w["ssd_a_log"], w["gdn_a_log"]], axis=1)
    on_lanes = lambda v: jnp.concatenate([v, zeros(LANES - v.shape[1])], axis=1).reshape(DEPTH, 1, LANES)
    on_rows = lambda v: jnp.concatenate([v, zeros(SMALL_ROWS - v.shape[1])], axis=1).reshape(DEPTH, SMALL_ROWS, 1)
    pad_w = lambda cw: jnp.concatenate([cw, zeros(HALO - CONV_K, cw.shape[2])], axis=1)
    p.update(bias_l=on_lanes(bias), alog_l=on_lanes(alog), bias_c=on_rows(bias), alog_c=on_rows(alog),
             ssd_cw=pad_w(w["ssd_conv_w"]), ssd_cb=row3(w["ssd_conv_b"]),
             ssd_d=row3(jnp.repeat(w["ssd_d"], SSD_HEAD_DIM, axis=1)), ssd_norm_g=row3(w["ssd_norm_g"]),
             gdn_cw=pad_w(w["gdn_conv_w"]), gdn_norm_g=row3(w["gdn_norm_g"]),
             w_branch=w["w_branch"], b_gate=row3(w["b_gate"]), w_out=w["w_out"], ln_g=row3(w["ln_g"]), ln_b=row3(w["ln_b"]))
    return p


def _layer(h, p):
    length = h.shape[0]
    nt = length // CHUNK
    t_row = _pick(length, (208, 128))
    t_s5 = _pick(length, (320, 128))
    hb = _bf16_copy(h)
    proj = {k: _dense("proj_" + k)(h, p["w_" + k], hb, p["wb_" + k]) for k in list(_PROJ) + ["small"]}
    small = proj["small"]
    small_t = small[:, :SMALL_ROWS].reshape(nt, CHUNK, SMALL_ROWS).transpose(0, 2, 1)

    lanes = p["s5_a_re"].shape[-1]
    s5_scan = _scan_op(_s5_scan_step, "s5_scan", tile=t_s5, nb=S5_BLOCKS, row_kinds="b", param_kinds="bbbbbb",
                       carry_shapes=((1, lanes), (1, lanes)), out_widths=(LANES,))
    (y_ssm,) = s5_scan((_interleave(proj["s5u"], t_s5),), (p["s5_bd_re"], p["s5_bd_im"], p["s5_a_re"], p["s5_a_im"],
                                                           p["s5_cd_re"], p["s5_cd_im"]))
    y_ssm = _interleave(y_ssm, t_s5, inverse=True)
    s5_post = _rowwise(_s5_post_step, "s5_post", t_row, 3, 3, (WIDTH,))
    (y_a,) = s5_post((y_ssm, proj["s5u"], proj["s5z"]), (p["s5_d"], p["s5_w_glu"], p["s5_b_glu"]))

    scalars = (p["bias_l"], p["alog_l"], p["bias_c"], p["alog_c"])
    ssd = _scan_op(_ssd_step, "ssd_scan", tile=CHUNK, nb=1, row_kinds="ssst", param_kinds="s" * 8,
                   carry_shapes=((HALO, P_DT - P_XBC), (SSD_HEADS * SSD_HEAD_DIM, SSD_STATE)), out_widths=(WIDTH,))
    (y_b,) = ssd((proj["xbc"], proj["ssdz"], small, small_t),
                 (p["ssd_cw"], p["ssd_cb"], p["ssd_d"]) + scalars + (p["ssd_norm_g"],))
    gdn = _scan_op(_gdn_step, "gdn_scan", tile=CHUNK, nb=1, row_kinds="ssst", param_kinds="s" * 6,
                   carry_shapes=((HALO, P_GA - P_QKV), (GDN_HEADS * GDN_DIM, GDN_DIM)), out_widths=(WIDTH,),
                   keep_shapes=((GDN_HEADS * CHUNK, CHUNK),))
    (y_c,) = gdn((proj["qkv"], proj["gdnz"], small, small_t), (p["gdn_cw"],) + scalars + (p["gdn_norm_g"],))

    branch = [_dense("branch_" + n)(y, p["w_branch"][i], _bf16_copy(y), p["wb_branch"][i])
              for i, (n, y) in enumerate(zip("abc", (y_a, y_b, y_c)))]
    merge = _rowwise(_gate_merge_step, "gate_merge", t_row, 6, 1, (D_MODEL,))
    (merged,) = merge((*branch, proj["gate_a"], proj["gate_b"], proj["gate_c"]), (p["b_gate"],))
    out = _dense("out_proj")(merged, p["w_out"], _bf16_copy(merged), p["wb_out"])
    ln = _rowwise(_ln_res_step, "ln_res", t_row, 2, 2, (D_MODEL,))
    (h_new,) = ln((h, out), (p["ln_g"], p["ln_b"]))
    return h_new


_LAYER_KEYS = ("w_in", "s5_a_re", "s5_a_im", "s5_log_step", "s5_b_re", "s5_b_im", "s5_c_re", "s5_c_im", "s5_d",
               "s5_w_glu", "s5_b_glu", "ssd_conv_w", "ssd_conv_b", "ssd_dt_bias", "ssd_a_log", "ssd_d", "ssd_norm_g",
               "gdn_conv_w", "gdn_dt_bias", "gdn_a_log", "gdn_norm_g", "w_branch", "b_gate", "w_out", "ln_g", "ln_b")
_WEIGHT_KEYS = ("meta", "ln_in_g", "ln_in_b") + _LAYER_KEYS


def _local_loss(weights, x, target, compute):
    seq = x.shape[0]
    hcat = jnp.concatenate([jnp.zeros((PAD, D_MODEL), F32), weights["meta"], x], axis=0)
    length = hcat.shape[0]
    ln_in = _rowwise(_ln_in_step, "ln_in", _pick(length, (416, 256, 128)), 1, 2, (D_MODEL,))
    (h,) = ln_in((hcat,), (weights["ln_in_g"].reshape(1, -1), weights["ln_in_b"].reshape(1, -1)))

    prepared = _prepare(weights, compute)
    for layer in range(DEPTH):
        h = _layer(h, {k: v[layer] for k, v in prepared.items()})
    return _loss_op(h[length - seq:], target)


_ANY = pl.BlockSpec(memory_space=pl.ANY)
_BLOCK_BYTES = 4 << 20


def _chip_exchange(arrays, all_to_all, name):
    n = len(arrays)

    def body(*refs):
        ins, outs = refs[:n], refs[n:2 * n]
        send_sems, recv_sems = refs[2 * n:]
        mx, my, mc = lax.axis_index("x"), lax.axis_index("y"), lax.axis_index("c")
        me = 2 * mx + my
        peers = [(1 - mx, my), (mx, 1 - my), (1 - mx, 1 - my)]
        sends = []
        for a, (src, dst) in enumerate(zip(ins, outs)):
            for k, (px, py) in enumerate(peers):
                cp = pltpu.make_async_remote_copy(
                    src_ref=src.at[2 * px + py] if all_to_all else src, dst_ref=dst.at[me],
                    send_sem=send_sems.at[a, k], recv_sem=recv_sems.at[a, k],
                    device_id=(px, py, mc), device_id_type=MESH)
                cp.start()
                sends.append(cp)
        for a, (src, dst) in enumerate(zip(ins, outs)):
            for k, (px, py) in enumerate(peers):
                pltpu.make_async_remote_copy(
                    src_ref=src.at[me] if all_to_all else src, dst_ref=dst.at[2 * px + py],
                    send_sem=send_sems.at[a, k], recv_sem=recv_sems.at[a, k],
                    device_id=(px, py, mc), device_id_type=MESH).wait_recv()
        for cp in sends:
            cp.wait_send()

    out_shape = [jax.ShapeDtypeStruct(a.shape if all_to_all else (4,) + a.shape, a.dtype) for a in arrays]
    return pl.pallas_call(
        body, name=name, in_specs=[_ANY] * n, out_specs=[_ANY] * n, out_shape=out_shape,
        scratch_shapes=[pltpu.SemaphoreType.DMA((n, 3)), pltpu.SemaphoreType.DMA((n, 3))],
    )(*arrays)


def _gather_two_level(arrays, name):
    n = len(arrays)

    def body(*refs):
        ins, outs = refs[:n], refs[n:2 * n]
        send_sems, recv_sems = refs[2 * n:]
        mx, my, mc = lax.axis_index("x"), lax.axis_index("y"), lax.axis_index("c")
        me = 2 * mx + my
        sibling = (mx, my, 1 - mc)
        chips = [(1 - mx, my), (mx, 1 - my), (1 - mx, 1 - my)]

        def copy(a, k, src, chip, core, to):
            return pltpu.make_async_remote_copy(
                src_ref=src, dst_ref=outs[a].at[chip, core], send_sem=send_sems.at[a, k], recv_sem=recv_sems.at[a, k],
                device_id=to, device_id_type=MESH)

        sends = [copy(a, j, ins[a], me, mc, (px, py, mc)) for a in range(n) for j, (px, py) in enumerate(chips)]
        for cp in sends:
            cp.start()
        passed = []
        for a in range(n):
            for j, (px, py) in enumerate(chips):
                chip = 2 * px + py
                copy(a, j, ins[a], chip, mc, sibling).wait_recv()
                cp = copy(a, 3 + j, outs[a].at[chip, mc], chip, mc, sibling)
                cp.start()
                passed.append(cp)
        for a in range(n):
            for j, (px, py) in enumerate(chips):
                copy(a, 3 + j, ins[a], 2 * px + py, 1 - mc, sibling).wait_recv()
        for cp in sends + passed:
            cp.wait_send()

    return pl.pallas_call(
        body, name=name, in_specs=[_ANY] * n, out_specs=[_ANY] * n,
        out_shape=[jax.ShapeDtypeStruct((4, 2) + a.shape, a.dtype) for a in arrays],
        scratch_shapes=[pltpu.SemaphoreType.DMA((n, 6)), pltpu.SemaphoreType.DMA((n, 6))],
    )(*arrays)


def _core_exchange(arrays, other_half, name):
    n = len(arrays)

    def body(*refs):
        ins, outs = refs[:n], refs[n:2 * n]
        send_sems, recv_sems = refs[2 * n:]
        mc = lax.axis_index("c")
        sibling = (lax.axis_index("x"), lax.axis_index("y"), 1 - mc)
        copies = [pltpu.make_async_remote_copy(
            src_ref=s.at[1 - mc] if other_half else s, dst_ref=d, send_sem=send_sems.at[a], recv_sem=recv_sems.at[a],
            device_id=sibling, device_id_type=MESH) for a, (s, d) in enumerate(zip(ins, outs))]
        for cp in copies:
            cp.start()
        for cp in copies:
            cp.wait()

    return pl.pallas_call(
        body, name=name, in_specs=[_ANY] * n, out_specs=[_ANY] * n,
        out_shape=[jax.ShapeDtypeStruct(a.shape[1:] if other_half else a.shape, a.dtype) for a in arrays],
        scratch_shapes=[pltpu.SemaphoreType.DMA((n,)), pltpu.SemaphoreType.DMA((n,))],
    )(*arrays)


def _as_rows(a, lead=0):
    shp = a.shape
    return a.reshape(shp[:lead] + (-1, shp[-1]))


def _sum4_call(x, name):
    _, r, c = x.shape
    tr = _pick(r, [t for t in (512, 256, 128, 64, 32, 16, 8) if 16 * t * c <= _BLOCK_BYTES] + [r])

    def body(x_ref, o_ref):
        part = [x_ref[j].astype(F32) for j in range(4)]
        o_ref[...] = (part[0] + part[1]) + (part[2] + part[3])

    return pl.pallas_call(
        body, name=name, grid=(r // tr,),
        in_specs=[pl.BlockSpec((4, tr, c), lambda i: (0, i, 0))],
        out_specs=pl.BlockSpec((tr, c), lambda i: (i, 0)),
        out_shape=jax.ShapeDtypeStruct((r, c), F32),
        compiler_params=_cparams(("parallel",), 48),
    )(x)


def _add_to_bf16_call(a, b, name):
    _, r, c = a.shape
    tr = _pick(r, [t for t in (512, 256, 128, 64, 32, 16) if 16 * t * c <= _BLOCK_BYTES] + [r])

    def body(a_ref, b_ref, o_ref):
        o_ref[...] = (a_ref[...].astype(F32) + b_ref[...].astype(F32)).astype(BF16)

    spec = pl.BlockSpec((4, tr, c), lambda i: (0, i, 0))
    return pl.pallas_call(
        body, name=name, grid=(r // tr,), in_specs=[spec, spec], out_specs=spec,
        out_shape=jax.ShapeDtypeStruct(a.shape, BF16), compiler_params=_cparams(("parallel",), 48),
    )(a, b)


def _adam_call(w, grads, m, v, name):
    r, c = w.shape
    n_g = len(grads)
    tr = _pick(r, [t for t in (512, 256, 128, 64, 32, 16, 8) if 4 * t * c <= _BLOCK_BYTES // 4] + [r])
    bc1 = 1.0 - ADAM_B1 ** ADAM_STEP
    bc2 = 1.0 - ADAM_B2 ** ADAM_STEP

    def body(*refs):
        w_ref, g_refs = refs[0], refs[1:1 + n_g]
        m_ref, v_ref, g_out, d_out, m_out, v_out = refs[1 + n_g:]
        g = g_refs[0][...]
        for g_ref in g_refs[1:]:
            g = g + g_ref[...]
        m_new = ADAM_B1 * m_ref[...] + (1.0 - ADAM_B1) * g
        v_new = ADAM_B2 * v_ref[...] + (1.0 - ADAM_B2) * (g * g)
        m_hat = m_new / bc1
        v_hat = v_new / bc2
        g_out[...] = g
        d_out[...] = -ADAM_LR * (m_hat / (jnp.sqrt(v_hat) + ADAM_EPS) + ADAM_WD * w_ref[...])
        m_out[...] = m_new
        v_out[...] = v_new

    spec = pl.BlockSpec((tr, c), lambda i: (i, 0))
    return pl.pallas_call(
        body, name=name, grid=(r // tr,), in_specs=[spec] * (3 + n_g), out_specs=[spec] * 4,
        out_shape=[jax.ShapeDtypeStruct((r, c), F32)] * 4,
        compiler_params=_cparams(("parallel",), 48),
    )(w, *grads, m, v)


_DENSE = ("w_in", "w_branch", "w_out")

_SHARDED = {"meta": (1, False), "w_in": (2, True), "s5_w_glu": (1, True), "ssd_conv_w": (2, False),
            "gdn_conv_w": (2, False), "w_branch": (3, True), "b_gate": (2, False), "w_out": (1, True)}


def _pack(arrs):
    flat = jnp.concatenate([a.reshape(-1) for a in arrs])
    n = flat.shape[0]
    rows = -(-n // (256 * LANES)) * 256
    return jnp.concatenate([flat, jnp.zeros((rows * LANES - n,), F32)]).reshape(rows, LANES)


def _unpack(packed, like):
    flat = packed.reshape(-1)
    out, off = [], 0
    for a in like:
        out.append(flat[off:off + a.size].reshape(a.shape))
        off += a.size
    return out


def kernel(x, meta, ln_in_g, ln_in_b, w_in, s5_a_re, s5_a_im, s5_log_step, s5_b_re, s5_b_im, s5_c_re, s5_c_im, s5_d, s5_w_glu, s5_b_glu, ssd_conv_w, ssd_conv_b, ssd_dt_bias, ssd_a_log, ssd_d, ssd_norm_g, gdn_conv_w, gdn_dt_bias, gdn_a_log, gdn_norm_g, w_branch, b_gate, w_out, ln_g, ln_b, loss_target, m_meta, m_ln_in_g, m_ln_in_b, m_w_in, m_s5_a_re, m_s5_a_im, m_s5_log_step, m_s5_b_re, m_s5_b_im, m_s5_c_re, m_s5_c_im, m_s5_d, m_s5_w_glu, m_s5_b_glu, m_ssd_conv_w, m_ssd_conv_b, m_ssd_dt_bias, m_ssd_a_log, m_ssd_d, m_ssd_norm_g, m_gdn_conv_w, m_gdn_dt_bias, m_gdn_a_log, m_gdn_norm_g, m_w_branch, m_b_gate, m_w_out, m_ln_g, m_ln_b, v_meta, v_ln_in_g, v_ln_in_b, v_w_in, v_s5_a_re, v_s5_a_im, v_s5_log_step, v_s5_b_re, v_s5_b_im, v_s5_c_re, v_s5_c_im, v_s5_d, v_s5_w_glu, v_s5_b_glu, v_ssd_conv_w, v_ssd_conv_b, v_ssd_dt_bias, v_ssd_a_log, v_ssd_d, v_ssd_norm_g, v_gdn_conv_w, v_gdn_dt_bias, v_gdn_a_log, v_gdn_norm_g, v_w_branch, v_b_gate, v_w_out, v_ln_g, v_ln_b):
    args = dict(locals())
    shards = {k: args[k] for k in _WEIGHT_KEYS}
    moms = {k: (args["m_" + k], args["v_" + k]) for k in _WEIGHT_KEYS}

    core = lax.axis_index("c")
    halves = lambda a: a.reshape((2, a.shape[0] // 2) + a.shape[1:])

    names = list(_SHARDED)
    sent = [lax.dynamic_index_in_dim(halves(shards[k]), core, 0, keepdims=False) for k in names]
    sent = [s.astype(BF16) if _SHARDED[k][1] else s for k, s in zip(names, sent)]
    gathered = _gather_two_level(sent, "gather_weights")
    my_chip = 2 * lax.axis_index("x") + lax.axis_index("y")
    own_block = lambda blocks, mine: lax.dynamic_update_index_in_dim(blocks, mine.astype(blocks.dtype), my_chip, 0)
    full, compute = dict(shards), {}
    for k, g in zip(names, gathered):
        shp, ax = shards[k].shape, _SHARDED[k][0]
        g = own_block(g, halves(shards[k]))
        gathered_k = jnp.concatenate([g[j].reshape(shp) for j in range(4)], axis=ax)
        full[k] = gathered_k.astype(F32)
        if k in _DENSE:
            compute[k] = gathered_k

    loss, (grads, grad_x) = jax.value_and_grad(_local_loss, argnums=(0, 1))(full, x[0], loss_target[0], compute)
    loss = lax.psum(loss, ("x", "y", "c"))

    blocks = []
    for k in names:
        per_chip = jnp.stack(jnp.split(grads[k].astype(BF16), 4, axis=_SHARDED[k][0]), axis=0)
        blocks.append(jnp.moveaxis(_as_rows(per_chip.reshape((4, 2, -1) + per_chip.shape[2:]), 2), 1, 0))
    theirs = _core_exchange(blocks, True, "swap_halves")
    mine = [lax.dynamic_index_in_dim(b, core, 0, keepdims=False) for b in blocks]
    chip_sums = [_add_to_bf16_call(a, b, "sum_cores_" + k) for k, a, b in zip(names, mine, theirs)]
    arrived = _chip_exchange(chip_sums, True, "scatter_grads")
    arrived = [own_block(a, lax.dynamic_index_in_dim(s, my_chip, 0, keepdims=False)) for a, s in zip(arrived, chip_sums)]
    owned = [_sum4_call(a, "sum_chips_" + k) for k, a in zip(names, arrived)]
    others = _core_exchange(owned, False, "swap_owned")
    shared = [jnp.concatenate([jnp.where(core == 0, a, b), jnp.where(core == 0, b, a)], axis=0)
              for a, b in zip(owned, others)]

    small_names = [k for k in _WEIGHT_KEYS if k not in _SHARDED]
    packed = _pack([grads[k] for k in small_names])
    (packed4,) = _chip_exchange([packed], False, "gather_small_grads")
    small_sum = _sum4_call(own_block(packed4, packed), "sum_chips_small")
    (small_other,) = _core_exchange([small_sum], False, "swap_small")

    outs = {}
    for k, g in zip(names, shared):
        shp = shards[k].shape
        rows = _as_rows(shards[k]).shape
        res = _adam_call(_as_rows(shards[k]), [g.reshape(rows)], _as_rows(moms[k][0]), _as_rows(moms[k][1]), "adamw_" + k)
        outs[k] = [r.reshape(shp) for r in res]
    like = [shards[k] for k in small_names]
    res = _adam_call(_pack(like), [small_sum, small_other], _pack([moms[k][0] for k in small_names]),
                     _pack([moms[k][1] for k in small_names]), "adamw_small")
    for idx in range(4):
        for k, a in zip(small_names, _unpack(res[idx], like)):
            outs.setdefault(k, [None] * 4)[idx] = a

    result = [loss, grad_x[None]]
    for idx in range(4):
        result += [outs[k][idx] for k in _WEIGHT_KEYS]
    return tuple(result)
```

```python
import functools

import jax
import jax.numpy as jnp
from jax import lax
from jax.experimental import pallas as pl
from jax.experimental.pallas import tpu as pltpu

F32 = jnp.float32
BF16 = jnp.bfloat16
MESH = pl.DeviceIdType.MESH
F32_DOT = lax.Precision.HIGH

D_MODEL = 1024
DEPTH = 4
N_META = 16
CHUNK = 64
PAD = CHUNK - N_META
CONV_K = 4
HALO = 8
WIDTH = 768
S5_GROUPS, S5_GROUP, S5_STATE = 48, 16, 64
S5_BLOCKS = 3
S5_SEGMENTS = 8
SSD_HEADS, SSD_HEAD_DIM, SSD_GROUPS, SSD_STATE = 12, 64, 2, 128
SSD_PAIRS = 6
GDN_HEADS, GDN_DIM = 6, 128
LANES = 128
SMALL_ROWS = 24
ALPHA = (2 * DEPTH) ** 0.25
LN_EPS = 1e-5
P_S5U, P_S5Z, P_XBC, P_DT, P_SSDZ, P_QKV, P_GA, P_GB, P_GDNZ, P_GATE, P_END = (
    0, 768, 1536, 2816, 2828, 3596, 5900, 5906, 5912, 6680, 9752)
ADAM_LR, ADAM_B1, ADAM_B2, ADAM_EPS, ADAM_WD, ADAM_STEP = 0.001, 0.9, 0.999, 1e-08, 0.01, 10
NEG = -1e30


def _pick(n, cands):
    for c in cands:
        if n % c == 0:
            return c
    raise ValueError(f"no tile for {n} in {cands}")


def _cparams(sem, vmem_mb):
    return pltpu.CompilerParams(dimension_semantics=sem, vmem_limit_bytes=vmem_mb << 20)


_DIMS = {"nn": (((1,), (0,)), ((), ())), "nt": (((1,), (1,)), ((), ())), "tn": (((0,), (0,)), ((), ()))}


def _dot(a, b, mode, hi):
    if hi:
        prec = lax.Precision.HIGHEST if hi == "exact" else F32_DOT
        return lax.dot_general(a, b, _DIMS[mode], precision=prec, preferred_element_type=F32)
    return lax.dot_general(a.astype(BF16), b.astype(BF16), _DIMS[mode], preferred_element_type=F32)


@functools.partial(jax.custom_vjp, nondiff_argnums=(2, 3))
def _mm(a, b, mode="nn", hi=False):
    return _dot(a, b, mode, hi)


def _mm_fwd(a, b, mode, hi):
    return _dot(a, b, mode, hi), (a, b)


def _mm_bwd(mode, hi, res, g):
    a, b = res
    if mode == "nn":
        return _dot(g, b, "nt", hi), _dot(a, g, "tn", hi)
    if mode == "nt":
        return _dot(g, b, "nn", hi), _dot(g, a, "tn", hi)
    return _dot(b, g, "nt", hi), _dot(a, g, "nn", hi)


_mm.defvjp(_mm_fwd, _mm_bwd)


@functools.partial(jax.custom_vjp, nondiff_argnums=(1,))
def _roll_rows(x, k):
    return pltpu.roll(x, k % x.shape[0], 0)


def _roll_fwd(x, k):
    return _roll_rows(x, k), None


def _roll_bwd(k, _, g):
    return (_roll_rows(g, -k),)


_roll_rows.defvjp(_roll_fwd, _roll_bwd)


def _iota(shape, dim):
    return lax.broadcasted_iota(jnp.int32, shape, dim)


def _valid_rows(row0, n):
    return (row0 + _iota((n, 1), 0)) >= PAD


def _lane(x, idx):
    return jnp.sum(jnp.where(_iota(x.shape, 1) == idx, x, 0.0), axis=1, keepdims=True)


def _row(x, idx):
    return jnp.sum(jnp.where(_iota(x.shape, 0) == idx, x, 0.0), axis=0, keepdims=True)


def _layer_norm(z, g, b):
    mu = jnp.mean(z, axis=-1, keepdims=True)
    zc = z - mu
    var = jnp.mean(zc * zc, axis=-1, keepdims=True)
    return zc * lax.rsqrt(var + LN_EPS) * g + b


def _rms_norm(z, g):
    return z * lax.rsqrt(jnp.mean(z * z, axis=-1, keepdims=True) + LN_EPS) * g


def _causal_conv(halo, x, w, row0):
    t = x.shape[0]
    xc = jnp.concatenate([halo, x], axis=0)
    acc = None
    for j in range(CONV_K):
        term = _roll_rows(xc, CONV_K - 1 - j)[HALO:HALO + t] * _row(w, j)
        acc = term if acc is None else acc + term
    return acc, x[t - HALO:t]


def _tri(n, strict=False):
    r, c = _iota((n, n), 0), _iota((n, n), 1)
    return (r > c) if strict else (r >= c)


def _scan_op(step, name, *, tile, nb, row_kinds, param_kinds, carry_shapes, out_widths, keep_shapes=(), vmem_mb=48):
    n_rows, n_par, n_car, n_out = len(row_kinds), len(param_kinds), len(carry_shapes), len(out_widths)
    n_keep = len(keep_shapes)
    saved_shapes = tuple(carry_shapes) + tuple(keep_shapes)

    def dims(rows):
        for k, a in zip(row_kinds, rows):
            if k in "bs":
                return a.shape[0], a.shape[0] // tile
        raise ValueError("need a row input")

    def row_spec(kind, a, rev, nt):
        ti = (lambda i: nt - 1 - i) if rev else (lambda i: i)
        if kind == "b":
            return pl.BlockSpec((tile, a.shape[1] // nb), lambda b, i: (ti(i), b))
        if kind == "s":
            return pl.BlockSpec((tile, a.shape[1]), lambda b, i: (ti(i), 0))
        return pl.BlockSpec((None, a.shape[1], a.shape[2]), lambda b, i: (ti(i), 0, 0))

    def par_spec(kind, a):
        if kind == "b":
            return pl.BlockSpec((None, a.shape[1], a.shape[2]), lambda b, i: (b, 0, 0))
        return pl.BlockSpec(a.shape, lambda b, i: (0, 0))

    def fwd_call(rows, params):
        length, nt = dims(rows)

        def body(*refs):
            r_in = refs[:n_rows]
            p_in = refs[n_rows:n_rows + n_par]
            o_out = refs[n_rows + n_par:n_rows + n_par + n_out]
            s_out = refs[n_rows + n_par + n_out:n_rows + n_par + n_out + n_car + n_keep]
            c_scr = refs[n_rows + n_par + n_out + n_car + n_keep:]
            b, i = pl.program_id(0), pl.program_id(1)

            if n_car:
                @pl.when(i == 0)
                def _():
                    for c in c_scr:
                        c[...] = jnp.zeros_like(c)

            cin = tuple(c[...] for c in c_scr)
            for s, c in zip(s_out, cin):
                s[...] = c
            res = step(cin, tuple(r[...] for r in r_in), tuple(p[...] for p in p_in), b, i * tile)
            new_c, outs = res[0], res[1]
            for c, v in zip(c_scr, new_c):
                c[...] = v
            for o, v in zip(o_out, outs):
                o[...] = v
            if n_keep:
                for s, v in zip(s_out[n_car:], res[2]):
                    s[...] = v

        out_shape = [jax.ShapeDtypeStruct((length, nb * w), F32) for w in out_widths]
        out_shape += [jax.ShapeDtypeStruct((nb, nt) + tuple(s), F32) for s in saved_shapes]
        out_specs = [pl.BlockSpec((tile, w), lambda b, i: (i, b)) for w in out_widths]
        out_specs += [pl.BlockSpec((None, None) + tuple(s), lambda b, i: (b, i, 0, 0)) for s in saved_shapes]
        res = pl.pallas_call(
            body, name=name + "_fwd", grid=(nb, nt),
            in_specs=[row_spec(k, a, False, nt) for k, a in zip(row_kinds, rows)]
            + [par_spec(k, a) for k, a in zip(param_kinds, params)],
            out_specs=out_specs, out_shape=out_shape,
            scratch_shapes=[pltpu.VMEM(tuple(s), F32) for s in carry_shapes],
            compiler_params=_cparams(("arbitrary", "arbitrary"), vmem_mb),
        )(*rows, *params)
        return tuple(res[:n_out]), tuple(res[n_out:])

    def bwd_call(rows, params, saved, douts):
        length, nt = dims(rows)

        def body(*refs):
            k0 = 0
            r_in = refs[k0:k0 + n_rows]; k0 += n_rows
            p_in = refs[k0:k0 + n_par]; k0 += n_par
            s_in = refs[k0:k0 + n_car]; k0 += n_car
            k_in = refs[k0:k0 + n_keep]; k0 += n_keep
            g_in = refs[k0:k0 + n_out]; k0 += n_out
            dr_out = refs[k0:k0 + n_rows]; k0 += n_rows
            dp_out = refs[k0:k0 + n_par]; k0 += n_par
            dc_scr = refs[k0:]
            b, i = pl.program_id(0), pl.program_id(1)
            row0 = (nt - 1 - i) * tile

            @pl.when(i == 0)
            def _():
                for c in dc_scr:
                    c[...] = jnp.zeros_like(c)
                for p in dp_out:
                    p[...] = jnp.zeros_like(p)

            def f(c, r, p):
                if n_keep:
                    return step(c, r, p, b, row0, kept=tuple(k[...] for k in k_in))[:2]
                return step(c, r, p, b, row0)

            _, vjp = jax.vjp(f, tuple(s[...] for s in s_in), tuple(r[...] for r in r_in),
                             tuple(p[...] for p in p_in))
            dc, dr, dp = vjp((tuple(c[...] for c in dc_scr), tuple(g[...] for g in g_in)))
            for c, v in zip(dc_scr, dc):
                c[...] = v
            for o, v in zip(dr_out, dr):
                o[...] = v
            for o, v in zip(dp_out, dp):
                o[...] += v

        rev = lambda i: nt - 1 - i
        in_specs = [row_spec(k, a, True, nt) for k, a in zip(row_kinds, rows)]
        in_specs += [par_spec(k, a) for k, a in zip(param_kinds, params)]
        in_specs += [pl.BlockSpec((None, None) + tuple(s), lambda b, i: (b, rev(i), 0, 0)) for s in saved_shapes]
        in_specs += [pl.BlockSpec((tile, w), lambda b, i: (rev(i), b)) for w in out_widths]
        out_shape, out_specs = [], []
        for k, a in zip(row_kinds, rows):
            if k == "b":
                out_shape.append(jax.ShapeDtypeStruct(a.shape, F32))
                out_specs.append(pl.BlockSpec((tile, a.shape[1] // nb), lambda b, i: (rev(i), b)))
            elif k == "s":
                out_shape.append(jax.ShapeDtypeStruct((nb,) + a.shape, F32))
                out_specs.append(pl.BlockSpec((None, tile, a.shape[1]), lambda b, i: (b, rev(i), 0)))
            else:
                out_shape.append(jax.ShapeDtypeStruct((nb,) + a.shape, F32))
                out_specs.append(pl.BlockSpec((None, None, a.shape[1], a.shape[2]), lambda b, i: (b, rev(i), 0, 0)))
        for k, a in zip(param_kinds, params):
            shp = a.shape[1:] if k == "b" else a.shape
            out_shape.append(jax.ShapeDtypeStruct((nb,) + tuple(shp), F32))
            out_specs.append(pl.BlockSpec((None,) + tuple(shp), lambda b, i: (b, 0, 0)))
        res = pl.pallas_call(
            body, name=name + "_bwd", grid=(nb, nt), in_specs=in_specs, out_specs=out_specs, out_shape=out_shape,
            scratch_shapes=[pltpu.VMEM(tuple(s), F32) for s in carry_shapes],
            compiler_params=_cparams(("arbitrary", "arbitrary"), vmem_mb),
        )(*rows, *params, *saved, *douts)
        fold = (lambda a: a[0]) if nb == 1 else (lambda a: jnp.sum(a, axis=0))
        drows = tuple(r if k == "b" else fold(r) for k, r in zip(row_kinds, res[:n_rows]))
        dpars = tuple(p if k == "b" else fold(p) for k, p in zip(param_kinds, res[n_rows:]))
        return drows, dpars

    @jax.custom_vjp
    def op(rows, params):
        return fwd_call(rows, params)[0]

    def op_fwd(rows, params):
        outs, saved = fwd_call(rows, params)
        return outs, (rows, params, saved)

    def op_bwd(res, douts):
        rows, params, saved = res
        return bwd_call(rows, params, saved, tuple(douts))

    op.defvjp(op_fwd, op_bwd)
    return op


_WIDE = 1280


def _mm_rows(m, n):
    return _pick(m, (832, 128)) if n <= _WIDE else _pick(m, (416, 128))


def _mm_fwd_call(x, w, name):
    m, k = x.shape
    n = w.shape[1]
    tm = _mm_rows(m, n)

    def body(x_ref, w_ref, o_ref):
        o_ref[...] = _dot(x_ref[...], w_ref[...], "nn", False)

    return pl.pallas_call(
        body, name=name, grid=(m // tm,),
        in_specs=[pl.BlockSpec((tm, k), lambda i: (i, 0)), pl.BlockSpec((k, n), lambda i: (0, 0))],
        out_specs=pl.BlockSpec((tm, n), lambda i: (i, 0)),
        out_shape=jax.ShapeDtypeStruct((m, n), F32),
        compiler_params=_cparams(("parallel",), 48),
    )(x, w)


def _mm_bwd_call(g, x, w, name):
    m, n = g.shape
    k = w.shape[0]
    tm = _mm_rows(m, n)

    def body(g_ref, x_ref, w_ref, dx_ref, dw_ref):
        @pl.when(pl.program_id(0) == 0)
        def _():
            dw_ref[...] = jnp.zeros_like(dw_ref)

        g = g_ref[...].astype(BF16)
        dx_ref[...] = _dot(g, w_ref[...], "nt", False)
        x = x_ref[...]
        step = _pick(n, (768, 640, 128))
        for c0 in range(0, n, step):
            dw_ref[:, c0:c0 + step] += _dot(x, g[:, c0:c0 + step], "tn", False)

    return pl.pallas_call(
        body, name=name, grid=(m // tm,),
        in_specs=[pl.BlockSpec((tm, n), lambda i: (i, 0)), pl.BlockSpec((tm, k), lambda i: (i, 0)),
                  pl.BlockSpec((k, n), lambda i: (0, 0))],
        out_specs=[pl.BlockSpec((tm, k), lambda i: (i, 0)), pl.BlockSpec((k, n), lambda i: (0, 0))],
        out_shape=[jax.ShapeDtypeStruct((m, k), F32), jax.ShapeDtypeStruct((k, n), F32)],
        compiler_params=_cparams(("arbitrary",), 56),
    )(g, x, w)


def _dense(name):
    @jax.custom_vjp
    def op(x, w, xb, wb):
        return _mm_fwd_call(xb, wb, name + "_fwd")

    def op_fwd(x, w, xb, wb):
        return _mm_fwd_call(xb, wb, name + "_fwd"), (xb, wb)

    def op_bwd(res, g):
        xb, wb = res
        dx, dw = _mm_bwd_call(g, xb, wb, name + "_bwd")
        return dx, dw, jnp.zeros_like(xb), jnp.zeros_like(wb)

    op.defvjp(op_fwd, op_bwd)
    return op


def _bf16_copy(v):
    return lax.stop_gradient(v).astype(BF16)


def _ln_in_step(c, rows, params, b, row0):
    (z,), (g, bb) = rows, params
    return (), (jnp.where(_valid_rows(row0, z.shape[0]), _layer_norm(z, g, bb), 0.0),)


def _ln_res_step(c, rows, params, b, row0):
    (h, o), (g, bb) = rows, params
    return (), (jnp.where(_valid_rows(row0, h.shape[0]), _layer_norm(ALPHA * h + o, g, bb), 0.0),)


def _s5_prep_step(c, rows, params, b, row0):
    a_re, a_im, log_step, b_re, b_im = rows
    lam_re = jnp.minimum(a_re, -1e-4)
    lam_im = a_im
    step = jnp.exp(log_step)
    mag = jnp.exp(lam_re * step)
    abar_re, abar_im = mag * jnp.cos(lam_im * step), mag * jnp.sin(lam_im * step)
    den = lam_re * lam_re + lam_im * lam_im
    nr, ni = abar_re - 1.0, abar_im
    coef_re = (nr * lam_re + ni * lam_im) / den
    coef_im = (ni * lam_re - nr * lam_im) / den
    return (), (abar_re, abar_im, coef_re * b_re - coef_im * b_im, coef_re * b_im + coef_im * b_re)


def _s5_scan_step(c, rows, params, b, row0):
    (c_re, c_im), (u,) = c, rows
    bd_re, bd_im, a_re, a_im, cd_re, cd_im = params
    t = u.shape[0]
    steps = t // S5_SEGMENTS
    bu_re, bu_im = _mm(u, bd_re), _mm(u, bd_im)
    a_re, a_im = (jnp.broadcast_to(v, (S5_SEGMENTS, v.shape[1])) for v in (a_re, a_im))
    at = lambda v, i: v[S5_SEGMENTS * i:S5_SEGMENTS * (i + 1)]

    def advance(s_re, s_im, first):
        states = []
        for i in range(first, steps):
            s_re, s_im = a_re * s_re - a_im * s_im + at(bu_re, i), a_re * s_im + a_im * s_re + at(bu_im, i)
            states.append((s_re, s_im))
        return s_re, s_im, states

    s_re, s_im, _ = advance(at(bu_re, 0), at(bu_im, 0), 1)
    q_re = q_im = None
    p_re, p_im, left = a_re, a_im, steps
    while left:
        if left & 1:
            q_re, q_im = (p_re, p_im) if q_re is None else (q_re * p_re - q_im * p_im, q_re * p_im + q_im * p_re)
        p_re, p_im = p_re * p_re - p_im * p_im, 2.0 * p_re * p_im
        left >>= 1
    seg = _iota((S5_SEGMENTS, 1), 0)
    in_re = jnp.where(seg == 0, c_re, _roll_rows(s_re, 1))
    in_im = jnp.where(seg == 0, c_im, _roll_rows(s_im, 1))
    d = 1
    while d < S5_SEGMENTS:
        keep = seg >= d
        sh_re = jnp.where(keep, _roll_rows(in_re, d), 0.0)
        sh_im = jnp.where(keep, _roll_rows(in_im, d), 0.0)
        in_re, in_im = in_re + q_re * sh_re - q_im * sh_im, in_im + q_re * sh_im + q_im * sh_re
        q_re, q_im = q_re * q_re - q_im * q_im, 2.0 * q_re * q_im
        d *= 2
    _, _, full = advance(in_re, in_im, 0)
    y = _mm(jnp.concatenate([f[0] for f in full], axis=0), cd_re) - _mm(jnp.concatenate([f[1] for f in full], axis=0), cd_im)
    return (_row(full[-1][0], S5_SEGMENTS - 1), _row(full[-1][1], S5_SEGMENTS - 1)), (y,)


def _interleave(v, tile, inverse=False):
    length, width = v.shape
    shape = (length // tile, tile // S5_SEGMENTS, S5_SEGMENTS) if inverse else (length // tile, S5_SEGMENTS, tile // S5_SEGMENTS)
    return v.reshape(shape + (width,)).transpose(0, 2, 1, 3).reshape(length, width)


def _s5_post_step(c, rows, params, b, row0):
    (y, u, z), (d, w_glu, b_glu) = rows, params
    v = jax.nn.gelu(y + d * u)
    v = v * jax.nn.sigmoid(_mm(v, w_glu) + b_glu)
    return (), (v * jax.nn.silu(z),)


def _ssd_step(c, rows, params, b, row0):
    halo, state = c
    xbc_raw, z, small, small_t = rows
    cw, cb, d_l, bias_l, alog_l, bias_c, alog_c, norm_g = params
    t = xbc_raw.shape[0]
    grp = SSD_GROUPS * SSD_STATE
    valid = _valid_rows(row0, t)
    conv, halo2 = _causal_conv(halo, xbc_raw, cw, row0)
    act = jnp.where(valid, jax.nn.silu(conv + cb), 0.0)
    low = _iota((1, LANES), 1) < SSD_HEAD_DIM
    dt_all = jnp.where(valid, jax.nn.softplus(small + bias_l), 0.0)
    a_all = -jnp.exp(alog_l)
    valid_t = (row0 + _iota((1, t), 1)) >= PAD
    dta_t = jnp.where(valid_t, jax.nn.softplus(small_t + bias_c), 0.0) * (-jnp.exp(alog_c))
    acum_t = _mm(dta_t, jnp.where(_iota((t, t), 0) <= _iota((t, t), 1), 1.0, 0.0), "nn", True)
    causal = _tri(t)
    acum_all = _mm(jnp.where(causal, 1.0, 0.0), dt_all * a_all, "nn", True)
    last_all = _row(acum_all, t - 1)
    low_rows = _iota((LANES, 1), 0) < SSD_HEAD_DIM
    pairs, groups = range(SSD_PAIRS), range(SSD_GROUPS)
    grp_of = [p // (SSD_PAIRS // SSD_GROUPS) for p in pairs]
    bs = [act[:, WIDTH + g * SSD_STATE:WIDTH + (g + 1) * SSD_STATE] for g in groups]
    cs = [act[:, WIDTH + grp + g * SSD_STATE:WIDTH + grp + (g + 1) * SSD_STATE] for g in groups]
    scores = [_mm(cs[g], bs[g], "nt") for g in groups]
    per_lane = lambda v, p: jnp.where(low, v[:, 2 * p:2 * p + 1], v[:, 2 * p + 1:2 * p + 2])
    dt_l = [per_lane(dt_all, p) for p in pairs]
    acum_l = [per_lane(acum_all, p) for p in pairs]
    last_l = [per_lane(last_all, p) for p in pairs]
    st = [state[p * LANES:(p + 1) * LANES] for p in pairs]
    xd = [act[:, p * LANES:(p + 1) * LANES] * dt_l[p] for p in pairs]
    decay = [jnp.exp(jnp.where(causal, acum_all[:, h:h + 1] - _row(acum_t, h), NEG)) for h in range(SSD_HEADS)]
    y_lo = [_mm(scores[grp_of[p]] * decay[2 * p], jnp.where(low, xd[p], 0.0)) for p in pairs]
    y_hi = [_mm(scores[grp_of[p]] * decay[2 * p + 1], jnp.where(low, 0.0, xd[p])) for p in pairs]
    y_off = [_mm(cs[grp_of[p]], st[p], "nt") * jnp.exp(acum_l[p]) for p in pairs]
    new_st = [_mm(xd[p] * jnp.exp(last_l[p] - acum_l[p]), bs[grp_of[p]], "tn") for p in pairs]
    cd = jnp.exp(last_all)
    new_st = [st[p] * jnp.where(low_rows, cd[:, 2 * p:2 * p + 1], cd[:, 2 * p + 1:2 * p + 2]) + new_st[p] for p in pairs]
    y = jnp.concatenate([y_lo[p] + y_hi[p] + y_off[p] for p in pairs], axis=1) + act[:, :WIDTH] * d_l
    out = _rms_norm(y * jax.nn.silu(z), norm_g)
    return (halo2, jnp.concatenate(new_st, axis=0)), (out,)


@jax.custom_vjp
def _unit_lower_inverse(mats):
    return _neumann_inverse(mats)


def _inverse_fwd(mats):
    inv = _neumann_inverse(mats)
    return inv, inv


def _inverse_bwd(inv, g):
    left = [_dot(t, d, "tn", True) for t, d in zip(inv, g)]
    return ([-_dot(l, t, "nt", True) for l, t in zip(left, inv)],)


_unit_lower_inverse.defvjp(_inverse_fwd, _inverse_bwd)


@jax.custom_vjp
def _kept_inverse(mats, inv):
    return list(inv)


def _kept_inverse_fwd(mats, inv):
    return list(inv), list(inv)


def _kept_inverse_bwd(inv, g):
    return _inverse_bwd(inv, g)[0], [jnp.zeros_like(t) for t in inv]


_kept_inverse.defvjp(_kept_inverse_fwd, _kept_inverse_bwd)


def _neumann_inverse(mats):
    n = mats[0].shape[0]
    eye = jnp.where(_iota((n, n), 0) == _iota((n, n), 1), 1.0, 0.0)
    inv = [eye - a for a in mats]
    p = [_mm(a, a, "nn", True) for a in mats]
    k = 2
    while k < n:
        inv = [i + _mm(i, q, "nn", True) for i, q in zip(inv, p)]
        k *= 2
        if k < n:
            p = [_mm(q, q, "nn", True) for q in p]
    return inv


def _gdn_step(c, rows, params, b, row0, kept=None):
    halo, state = c
    qkv_raw, z, small, small_t = rows
    cw, bias_l, alog_l, bias_c, alog_c, norm_g = params
    t = qkv_raw.shape[0]
    valid = _valid_rows(row0, t)
    conv, halo2 = _causal_conv(halo, qkv_raw, cw, row0)
    act = jnp.where(valid, jax.nn.silu(conv), 0.0)
    beta_all = jnp.where(valid, jax.nn.sigmoid(small), 0.0)
    g_all = jnp.where(valid, -jnp.exp(alog_l) * jax.nn.softplus(small + bias_l), 0.0)
    valid_t = (row0 + _iota((1, t), 1)) >= PAD
    g_t = jnp.where(valid_t, -jnp.exp(alog_c) * jax.nn.softplus(small_t + bias_c), 0.0)
    causal, strict = _tri(t), _tri(t, True)
    gcum_all = _mm(jnp.where(causal, 1.0, 0.0), g_all, "nn", True)
    gcum_t = _mm(g_t, jnp.where(_iota((t, t), 0) <= _iota((t, t), 1), 1.0, 0.0), "nn", True)
    heads = range(GDN_HEADS)
    part = lambda h, n: act[:, n * WIDTH + h * GDN_DIM:n * WIDTH + (h + 1) * GDN_DIM]
    unit = lambda x: x * lax.rsqrt(jnp.sum(x * x, axis=-1, keepdims=True) + 1e-6)
    q = [unit(part(h, 0)) * (GDN_DIM ** -0.5) for h in heads]
    k = [unit(part(h, 1)) for h in heads]
    st = [state[h * GDN_DIM:(h + 1) * GDN_DIM] for h in heads]
    ia = [SSD_HEADS + h for h in heads]
    beta = [beta_all[:, ia[h] + GDN_HEADS:ia[h] + GDN_HEADS + 1] for h in heads]
    gcum = [gcum_all[:, ia[h]:ia[h] + 1] for h in heads]
    gamma = [jnp.exp(jnp.where(causal, gcum[h] - _row(gcum_t, ia[h]), NEG)) for h in heads]
    egc = [jnp.exp(gcum[h]) for h in heads]
    a_mat = [jnp.where(strict, _mm(k[h], k[h], "nt") * gamma[h] * beta[h], 0.0) for h in heads]
    if kept is None:
        inv = _unit_lower_inverse(a_mat)
    else:
        inv = _kept_inverse(a_mat, [kept[0][h * t:(h + 1) * t] for h in heads])
    rhs = [jnp.concatenate([part(h, 2) * beta[h], k[h] * (beta[h] * egc[h])], axis=1) for h in heads]
    sol = [_mm(inv[h], rhs[h], "nn", True) for h in heads]
    attn = [_mm(q[h], k[h], "nt") * gamma[h] for h in heads]
    from_state = [_mm(jnp.concatenate([sol[h][:, GDN_DIM:], q[h] * egc[h]], axis=0), st[h]) for h in heads]
    v_new = [sol[h][:, :GDN_DIM] - from_state[h][:t] for h in heads]
    o = [from_state[h][t:] + _mm(attn[h], v_new[h]) for h in heads]
    glast = [_row(gcum[h], t - 1) for h in heads]
    new_st = [st[h] * jnp.exp(glast[h]) + _mm(k[h] * jnp.exp(glast[h] - gcum[h]), v_new[h], "tn") for h in heads]
    out = jnp.concatenate([_rms_norm(o[h], norm_g) for h in heads], axis=1) * jax.nn.silu(z)
    return (halo2, jnp.concatenate(new_st, axis=0)), (out,), (jnp.concatenate(inv, axis=0),)


def _gate_merge_step(c, rows, params, b, row0):
    (bg,) = params
    acc = None
    for n, (o, gl) in enumerate(zip(rows[:3], rows[3:])):
        term = jax.nn.sigmoid(gl + bg[:, n * D_MODEL:(n + 1) * D_MODEL]) * o
        acc = term if acc is None else acc + term
    return (), (acc,)


def _loss_tile(n):
    return _pick(n, (512, 256, 128, 64))


def _loss_fwd_call(y, tgt):
    n, d = y.shape
    tile = _loss_tile(n)

    def body(y_ref, t_ref, o_ref):
        @pl.when(pl.program_id(0) == 0)
        def _():
            o_ref[...] = jnp.zeros_like(o_ref)

        e = y_ref[...] - t_ref[...]
        o_ref[...] += jnp.sum(jnp.sum(e * e, axis=1, keepdims=True), axis=0, keepdims=True) * (0.5 / d)

    out = pl.pallas_call(
        body, name="loss_fwd", grid=(n // tile,),
        in_specs=[pl.BlockSpec((tile, d), lambda i: (i, 0)), pl.BlockSpec((tile, d), lambda i: (i, 0))],
        out_specs=pl.BlockSpec((8, LANES), lambda i: (0, 0)),
        out_shape=jax.ShapeDtypeStruct((8, LANES), F32),
        compiler_params=_cparams(("arbitrary",), 32),
    )(y, tgt)
    return out[0, 0]


def _loss_bwd_call(y, tgt, g):
    n, d = y.shape
    tile = _loss_tile(n)

    def body(y_ref, t_ref, g_ref, o_ref):
        o_ref[...] = (y_ref[...] - t_ref[...]) * (g_ref[...][0:1, 0:1] * (1.0 / d))

    return pl.pallas_call(
        body, name="loss_bwd", grid=(n // tile,),
        in_specs=[pl.BlockSpec((tile, d), lambda i: (i, 0)), pl.BlockSpec((tile, d), lambda i: (i, 0)),
                  pl.BlockSpec((8, LANES), lambda i: (0, 0))],
        out_specs=pl.BlockSpec((tile, d), lambda i: (i, 0)),
        out_shape=jax.ShapeDtypeStruct((n, d), F32),
        compiler_params=_cparams(("parallel",), 32),
    )(y, tgt, jnp.broadcast_to(g, (8, LANES)).astype(F32))


@jax.custom_vjp
def _loss_op(y, tgt):
    return _loss_fwd_call(y, tgt)


def _loss_op_fwd(y, tgt):
    return _loss_fwd_call(y, tgt), (y, tgt)


def _loss_op_bwd(res, g):
    y, tgt = res
    return _loss_bwd_call(y, tgt, g), jnp.zeros_like(tgt)


_loss_op.defvjp(_loss_op_fwd, _loss_op_bwd)


def _rowwise(step, name, tile, n_rows, n_params, out_widths, vmem_mb=48):
    return _scan_op(step, name, tile=tile, nb=1, row_kinds="s" * n_rows, param_kinds="s" * n_params,
                    carry_shapes=(), out_widths=out_widths, vmem_mb=vmem_mb)


def _block_diag(x, nblk):
    bsz, _, r, c = x.shape
    eye = jnp.eye(nblk, dtype=x.dtype)
    return jnp.einsum("bgrc,gh->bgrhc", x, eye).reshape(bsz, nblk * r, nblk * c)


_PROJ = {"s5u": (P_S5U, P_S5Z), "s5z": (P_S5Z, P_XBC), "xbc": (P_XBC, P_DT), "ssdz": (P_SSDZ, P_QKV),
         "qkv": (P_QKV, P_GA), "gdnz": (P_GDNZ, P_GATE), "gate_a": (P_GATE, P_GATE + D_MODEL),
         "gate_b": (P_GATE + D_MODEL, P_GATE + 2 * D_MODEL), "gate_c": (P_GATE + 2 * D_MODEL, P_END)}


def _prepare(weights, compute):
    w = weights
    zeros = lambda *shape: jnp.zeros((DEPTH,) + shape, F32)
    row3 = lambda v: v.reshape(DEPTH, 1, -1)
    p = {}
    for pre, w_in in (("w_", w["w_in"]), ("wb_", compute["w_in"])):
        seg = lambda a, bnd: w_in[:, :, a:bnd]
        p.update({pre + k: seg(a, bnd) for k, (a, bnd) in _PROJ.items()})
        p[pre + "small"] = jnp.concatenate(
            [seg(P_DT, P_SSDZ), seg(P_GA, P_GDNZ), zeros(D_MODEL, LANES - SMALL_ROWS).astype(w_in.dtype)], axis=2)
    p.update(wb_branch=compute["w_branch"], wb_out=compute["w_out"])

    rows, wide = DEPTH * S5_GROUPS, S5_STATE * S5_GROUP
    flat = lambda v: v.reshape(rows, -1)
    rep = lambda v: jnp.repeat(flat(v), S5_GROUP, axis=1)
    prep = _rowwise(_s5_prep_step, "s5_prep", S5_GROUPS, 5, 0, (wide,) * 4)
    abar_re, abar_im, bbar_re, bbar_im = prep(
        (rep(w["s5_a_re"]), rep(w["s5_a_im"]), jnp.broadcast_to(flat(w["s5_log_step"]), (rows, wide)),
         flat(w["s5_b_re"]), flat(w["s5_b_im"])), ())
    gpb = S5_GROUPS // S5_BLOCKS
    lanes, chans, nblk = gpb * S5_STATE, gpb * S5_GROUP, DEPTH * S5_BLOCKS
    to_bd = lambda bb: _block_diag(bb.reshape(nblk, gpb, S5_STATE, S5_GROUP).transpose(0, 1, 3, 2), gpb).reshape(
        DEPTH, S5_BLOCKS, chans, lanes)
    to_cd = lambda cc: _block_diag(cc.reshape(nblk, gpb, S5_GROUP, S5_STATE).transpose(0, 1, 3, 2), gpb).reshape(
        DEPTH, S5_BLOCKS, lanes, chans)
    to_a = lambda a: a[:, ::S5_GROUP].reshape(DEPTH, S5_BLOCKS, 1, lanes)
    p.update(s5_bd_re=to_bd(bbar_re), s5_bd_im=to_bd(bbar_im), s5_a_re=to_a(abar_re), s5_a_im=to_a(abar_im),
             s5_cd_re=to_cd(w["s5_c_re"]), s5_cd_im=to_cd(w["s5_c_im"]),
             s5_d=row3(w["s5_d"]), s5_w_glu=w["s5_w_glu"], s5_b_glu=row3(w["s5_b_glu"]))

    bias = jnp.concatenate([w["ssd_dt_bias"], w["gdn_dt_bias"]], axis=1)
    alog = jnp.concatenate([w["ssd_a_log"], w["gdn_a_log"]], axis=1)
    on_lanes = lambda v: jnp.concatenate([v, zeros(LANES - v.shape[1])], axis=1).reshape(DEPTH, 1, LANES)
    on_rows = lambda v: jnp.concatenate([v, zeros(SMALL_ROWS - v.shape[1])], axis=1).reshape(DEPTH, SMALL_ROWS, 1)
    pad_w = lambda cw: jnp.concatenate([cw, zeros(HALO - CONV_K, cw.shape[2])], axis=1)
    p.update(bias_l=on_lanes(bias), alog_l=on_lanes(alog), bias_c=on_rows(bias), alog_c=on_rows(alog),
             ssd_cw=pad_w(w["ssd_conv_w"]), ssd_cb=row3(w["ssd_conv_b"]),
             ssd_d=row3(jnp.repeat(w["ssd_d"], SSD_HEAD_DIM, axis=1)), ssd_norm_g=row3(w["ssd_norm_g"]),
             gdn_cw=pad_w(w["gdn_conv_w"]), gdn_norm_g=row3(w["gdn_norm_g"]),
             w_branch=w["w_branch"], b_gate=row3(w["b_gate"]), w_out=w["w_out"], ln_g=row3(w["ln_g"]), ln_b=row3(w["ln_b"]))
    return p


def _layer(h, p):
    length = h.shape[0]
    nt = length // CHUNK
    t_row = _pick(length, (208, 128))
    t_s5 = _pick(length, (832, 128))
    hb = _bf16_copy(h)
    proj = {k: _dense("proj_" + k)(h, p["w_" + k], hb, p["wb_" + k]) for k in list(_PROJ) + ["small"]}
    small = proj["small"]
    small_t = small[:, :SMALL_ROWS].reshape(nt, CHUNK, SMALL_ROWS).transpose(0, 2, 1)

    lanes = p["s5_a_re"].shape[-1]
    s5_scan = _scan_op(_s5_scan_step, "s5_scan", tile=t_s5, nb=S5_BLOCKS, row_kinds="b", param_kinds="bbbbbb",
                       carry_shapes=((1, lanes), (1, lanes)), out_widths=(WIDTH // S5_BLOCKS,))
    (y_ssm,) = s5_scan((_interleave(proj["s5u"], t_s5),), (p["s5_bd_re"], p["s5_bd_im"], p["s5_a_re"], p["s5_a_im"],
                                                           p["s5_cd_re"], p["s5_cd_im"]))
    y_ssm = _interleave(y_ssm, t_s5, inverse=True)
    s5_post = _rowwise(_s5_post_step, "s5_post", t_row, 3, 3, (WIDTH,))
    (y_a,) = s5_post((y_ssm, proj["s5u"], proj["s5z"]), (p["s5_d"], p["s5_w_glu"], p["s5_b_glu"]))

    scalars = (p["bias_l"], p["alog_l"], p["bias_c"], p["alog_c"])
    ssd = _scan_op(_ssd_step, "ssd_scan", tile=CHUNK, nb=1, row_kinds="ssst", param_kinds="s" * 8,
                   carry_shapes=((HALO, P_DT - P_XBC), (SSD_HEADS * SSD_HEAD_DIM, SSD_STATE)), out_widths=(WIDTH,))
    (y_b,) = ssd((proj["xbc"], proj["ssdz"], small, small_t),
                 (p["ssd_cw"], p["ssd_cb"], p["ssd_d"]) + scalars + (p["ssd_norm_g"],))
    gdn = _scan_op(_gdn_step, "gdn_scan", tile=CHUNK, nb=1, row_kinds="ssst", param_kinds="s" * 6,
                   carry_shapes=((HALO, P_GA - P_QKV), (GDN_HEADS * GDN_DIM, GDN_DIM)), out_widths=(WIDTH,),
                   keep_shapes=((GDN_HEADS * CHUNK, CHUNK),))
    (y_c,) = gdn((proj["qkv"], proj["gdnz"], small, small_t), (p["gdn_cw"],) + scalars + (p["gdn_norm_g"],))

    branch = [_dense("branch_" + n)(y, p["w_branch"][i], _bf16_copy(y), p["wb_branch"][i])
              for i, (n, y) in enumerate(zip("abc", (y_a, y_b, y_c)))]
    merge = _rowwise(_gate_merge_step, "gate_merge", t_row, 6, 1, (D_MODEL,))
    (merged,) = merge((*branch, proj["gate_a"], proj["gate_b"], proj["gate_c"]), (p["b_gate"],))
    out = _dense("out_proj")(merged, p["w_out"], _bf16_copy(merged), p["wb_out"])
    ln = _rowwise(_ln_res_step, "ln_res", t_row, 2, 2, (D_MODEL,))
    (h_new,) = ln((h, out), (p["ln_g"], p["ln_b"]))
    return h_new


_LAYER_KEYS = ("w_in", "s5_a_re", "s5_a_im", "s5_log_step", "s5_b_re", "s5_b_im", "s5_c_re", "s5_c_im", "s5_d",
               "s5_w_glu", "s5_b_glu", "ssd_conv_w", "ssd_conv_b", "ssd_dt_bias", "ssd_a_log", "ssd_d", "ssd_norm_g",
               "gdn_conv_w", "gdn_dt_bias", "gdn_a_log", "gdn_norm_g", "w_branch", "b_gate", "w_out", "ln_g", "ln_b")
_WEIGHT_KEYS = ("meta", "ln_in_g", "ln_in_b") + _LAYER_KEYS


def _local_loss(weights, x, target, compute):
    seq = x.shape[0]
    hcat = jnp.concatenate([jnp.zeros((PAD, D_MODEL), F32), weights["meta"], x], axis=0)
    length = hcat.shape[0]
    ln_in = _rowwise(_ln_in_step, "ln_in", _pick(length, (416, 256, 128)), 1, 2, (D_MODEL,))
    (h,) = ln_in((hcat,), (weights["ln_in_g"].reshape(1, -1), weights["ln_in_b"].reshape(1, -1)))

    prepared = _prepare(weights, compute)
    for layer in range(DEPTH):
        h = _layer(h, {k: v[layer] for k, v in prepared.items()})
    return _loss_op(h[length - seq:], target)


_ANY = pl.BlockSpec(memory_space=pl.ANY)
_BLOCK_BYTES = 4 << 20


def _chip_exchange(arrays, all_to_all, name):
    n = len(arrays)

    def body(*refs):
        ins, outs = refs[:n], refs[n:2 * n]
        send_sems, recv_sems = refs[2 * n:]
        mx, my, mc = lax.axis_index("x"), lax.axis_index("y"), lax.axis_index("c")
        me = 2 * mx + my
        peers = [(1 - mx, my), (mx, 1 - my), (1 - mx, 1 - my)]
        sends = []
        for a, (src, dst) in enumerate(zip(ins, outs)):
            for k, (px, py) in enumerate(peers):
                cp = pltpu.make_async_remote_copy(
                    src_ref=src.at[2 * px + py] if all_to_all else src, dst_ref=dst.at[me],
                    send_sem=send_sems.at[a, k], recv_sem=recv_sems.at[a, k],
                    device_id=(px, py, mc), device_id_type=MESH)
                cp.start()
                sends.append(cp)
        for a, (src, dst) in enumerate(zip(ins, outs)):
            for k, (px, py) in enumerate(peers):
                pltpu.make_async_remote_copy(
                    src_ref=src.at[me] if all_to_all else src, dst_ref=dst.at[2 * px + py],
                    send_sem=send_sems.at[a, k], recv_sem=recv_sems.at[a, k],
                    device_id=(px, py, mc), device_id_type=MESH).wait_recv()
        for cp in sends:
            cp.wait_send()

    out_shape = [jax.ShapeDtypeStruct(a.shape if all_to_all else (4,) + a.shape, a.dtype) for a in arrays]
    return pl.pallas_call(
        body, name=name, in_specs=[_ANY] * n, out_specs=[_ANY] * n, out_shape=out_shape,
        scratch_shapes=[pltpu.SemaphoreType.DMA((n, 3)), pltpu.SemaphoreType.DMA((n, 3))],
    )(*arrays)


def _gather_two_level(arrays, name):
    n = len(arrays)

    def body(*refs):
        ins, outs = refs[:n], refs[n:2 * n]
        send_sems, recv_sems = refs[2 * n:]
        mx, my, mc = lax.axis_index("x"), lax.axis_index("y"), lax.axis_index("c")
        me = 2 * mx + my
        sibling = (mx, my, 1 - mc)
        chips = [(1 - mx, my), (mx, 1 - my), (1 - mx, 1 - my)]

        def copy(a, k, src, chip, core, to):
            return pltpu.make_async_remote_copy(
                src_ref=src, dst_ref=outs[a].at[chip, core], send_sem=send_sems.at[a, k], recv_sem=recv_sems.at[a, k],
                device_id=to, device_id_type=MESH)

        sends = [copy(a, j, ins[a], me, mc, (px, py, mc)) for a in range(n) for j, (px, py) in enumerate(chips)]
        for cp in sends:
            cp.start()
        passed = []
        for a in range(n):
            for j, (px, py) in enumerate(chips):
                chip = 2 * px + py
                copy(a, j, ins[a], chip, mc, sibling).wait_recv()
                cp = copy(a, 3 + j, outs[a].at[chip, mc], chip, mc, sibling)
                cp.start()
                passed.append(cp)
        for a in range(n):
            for j, (px, py) in enumerate(chips):
                copy(a, 3 + j, ins[a], 2 * px + py, 1 - mc, sibling).wait_recv()
        for cp in sends + passed:
            cp.wait_send()

    return pl.pallas_call(
        body, name=name, in_specs=[_ANY] * n, out_specs=[_ANY] * n,
        out_shape=[jax.ShapeDtypeStruct((4, 2) + a.shape, a.dtype) for a in arrays],
        scratch_shapes=[pltpu.SemaphoreType.DMA((n, 6)), pltpu.SemaphoreType.DMA((n, 6))],
    )(*arrays)


def _core_exchange(arrays, other_half, name):
    n = len(arrays)

    def body(*refs):
        ins, outs = refs[:n], refs[n:2 * n]
        send_sems, recv_sems = refs[2 * n:]
        mc = lax.axis_index("c")
        sibling = (lax.axis_index("x"), lax.axis_index("y"), 1 - mc)
        copies = [pltpu.make_async_remote_copy(
            src_ref=s.at[1 - mc] if other_half else s, dst_ref=d, send_sem=send_sems.at[a], recv_sem=recv_sems.at[a],
            device_id=sibling, device_id_type=MESH) for a, (s, d) in enumerate(zip(ins, outs))]
        for cp in copies:
            cp.start()
        for cp in copies:
            cp.wait()

    return pl.pallas_call(
        body, name=name, in_specs=[_ANY] * n, out_specs=[_ANY] * n,
        out_shape=[jax.ShapeDtypeStruct(a.shape[1:] if other_half else a.shape, a.dtype) for a in arrays],
        scratch_shapes=[pltpu.SemaphoreType.DMA((n,)), pltpu.SemaphoreType.DMA((n,))],
    )(*arrays)


def _as_rows(a, lead=0):
    shp = a.shape
    return a.reshape(shp[:lead] + (-1, shp[-1]))


def _sum4_call(x, name):
    _, r, c = x.shape
    tr = _pick(r, [t for t in (512, 256, 128, 64, 32, 16, 8) if 16 * t * c <= _BLOCK_BYTES] + [r])

    def body(x_ref, o_ref):
        part = [x_ref[j].astype(F32) for j in range(4)]
        o_ref[...] = (part[0] + part[1]) + (part[2] + part[3])

    return pl.pallas_call(
        body, name=name, grid=(r // tr,),
        in_specs=[pl.BlockSpec((4, tr, c), lambda i: (0, i, 0))],
        out_specs=pl.BlockSpec((tr, c), lambda i: (i, 0)),
        out_shape=jax.ShapeDtypeStruct((r, c), F32),
        compiler_params=_cparams(("parallel",), 48),
    )(x)


def _add_to_bf16_call(a, b, name):
    _, r, c = a.shape
    tr = _pick(r, [t for t in (512, 256, 128, 64, 32, 16) if 16 * t * c <= _BLOCK_BYTES] + [r])

    def body(a_ref, b_ref, o_ref):
        o_ref[...] = (a_ref[...].astype(F32) + b_ref[...].astype(F32)).astype(BF16)

    spec = pl.BlockSpec((4, tr, c), lambda i: (0, i, 0))
    return pl.pallas_call(
        body, name=name, grid=(r // tr,), in_specs=[spec, spec], out_specs=spec,
        out_shape=jax.ShapeDtypeStruct(a.shape, BF16), compiler_params=_cparams(("parallel",), 48),
    )(a, b)


def _adam_call(w, grads, m, v, name):
    r, c = w.shape
    n_g = len(grads)
    tr = _pick(r, [t for t in (512, 256, 128, 64, 32, 16, 8) if 4 * t * c <= _BLOCK_BYTES // 4] + [r])
    bc1 = 1.0 - ADAM_B1 ** ADAM_STEP
    bc2 = 1.0 - ADAM_B2 ** ADAM_STEP

    def body(*refs):
        w_ref, g_refs = refs[0], refs[1:1 + n_g]
        m_ref, v_ref, g_out, d_out, m_out, v_out = refs[1 + n_g:]
        g = g_refs[0][...]
        for g_ref in g_refs[1:]:
            g = g + g_ref[...]
        m_new = ADAM_B1 * m_ref[...] + (1.0 - ADAM_B1) * g
        v_new = ADAM_B2 * v_ref[...] + (1.0 - ADAM_B2) * (g * g)
        m_hat = m_new / bc1
        v_hat = v_new / bc2
        g_out[...] = g
        d_out[...] = -ADAM_LR * (m_hat / (jnp.sqrt(v_hat) + ADAM_EPS) + ADAM_WD * w_ref[...])
        m_out[...] = m_new
        v_out[...] = v_new

    spec = pl.BlockSpec((tr, c), lambda i: (i, 0))
    return pl.pallas_call(
        body, name=name, grid=(r // tr,), in_specs=[spec] * (3 + n_g), out_specs=[spec] * 4,
        out_shape=[jax.ShapeDtypeStruct((r, c), F32)] * 4,
        compiler_params=_cparams(("parallel",), 48),
    )(w, *grads, m, v)


_DENSE = ("w_in", "w_branch", "w_out")

_SHARDED = {"meta": (1, False), "w_in": (2, True), "s5_w_glu": (1, True), "ssd_conv_w": (2, False),
            "gdn_conv_w": (2, False), "w_branch": (3, True), "b_gate": (2, False), "w_out": (1, True)}


def _pack(arrs):
    flat = jnp.concatenate([a.reshape(-1) for a in arrs])
    n = flat.shape[0]
    rows = -(-n // (256 * LANES)) * 256
    return jnp.concatenate([flat, jnp.zeros((rows * LANES - n,), F32)]).reshape(rows, LANES)


def _unpack(packed, like):
    flat = packed.reshape(-1)
    out, off = [], 0
    for a in like:
        out.append(flat[off:off + a.size].reshape(a.shape))
        off += a.size
    return out


def kernel(x, meta, ln_in_g, ln_in_b, w_in, s5_a_re, s5_a_im, s5_log_step, s5_b_re, s5_b_im, s5_c_re, s5_c_im, s5_d, s5_w_glu, s5_b_glu, ssd_conv_w, ssd_conv_b, ssd_dt_bias, ssd_a_log, ssd_d, ssd_norm_g, gdn_conv_w, gdn_dt_bias, gdn_a_log, gdn_norm_g, w_branch, b_gate, w_out, ln_g, ln_b, loss_target, m_meta, m_ln_in_g, m_ln_in_b, m_w_in, m_s5_a_re, m_s5_a_im, m_s5_log_step, m_s5_b_re, m_s5_b_im, m_s5_c_re, m_s5_c_im, m_s5_d, m_s5_w_glu, m_s5_b_glu, m_ssd_conv_w, m_ssd_conv_b, m_ssd_dt_bias, m_ssd_a_log, m_ssd_d, m_ssd_norm_g, m_gdn_conv_w, m_gdn_dt_bias, m_gdn_a_log, m_gdn_norm_g, m_w_branch, m_b_gate, m_w_out, m_ln_g, m_ln_b, v_meta, v_ln_in_g, v_ln_in_b, v_w_in, v_s5_a_re, v_s5_a_im, v_s5_log_step, v_s5_b_re, v_s5_b_im, v_s5_c_re, v_s5_c_im, v_s5_d, v_s5_w_glu, v_s5_b_glu, v_ssd_conv_w, v_ssd_conv_b, v_ssd_dt_bias, v_ssd_a_log, v_ssd_d, v_ssd_norm_g, v_gdn_conv_w, v_gdn_dt_bias, v_gdn_a_log, v_gdn_norm_g, v_w_branch, v_b_gate, v_w_out, v_ln_g, v_ln_b):
    args = dict(locals())
    shards = {k: args[k] for k in _WEIGHT_KEYS}
    moms = {k: (args["m_" + k], args["v_" + k]) for k in _WEIGHT_KEYS}

    core = lax.axis_index("c")
    halves = lambda a: a.reshape((2, a.shape[0] // 2) + a.shape[1:])

    names = list(_SHARDED)
    sent = [lax.dynamic_index_in_dim(halves(shards[k]), core, 0, keepdims=False) for k in names]
    sent = [s.astype(BF16) if _SHARDED[k][1] else s for k, s in zip(names, sent)]
    gathered = _gather_two_level(sent, "gather_weights")
    my_chip = 2 * lax.axis_index("x") + lax.axis_index("y")
    own_block = lambda blocks, mine: lax.dynamic_update_index_in_dim(blocks, mine.astype(blocks.dtype), my_chip, 0)
    full, compute = dict(shards), {}
    for k, g in zip(names, gathered):
        shp, ax = shards[k].shape, _SHARDED[k][0]
        g = own_block(g, halves(shards[k]))
        gathered_k = jnp.concatenate([g[j].reshape(shp) for j in range(4)], axis=ax)
        full[k] = gathered_k.astype(F32)
        if k in _DENSE:
            compute[k] = gathered_k

    loss, (grads, grad_x) = jax.value_and_grad(_local_loss, argnums=(0, 1))(full, x[0], loss_target[0], compute)
    loss = lax.psum(loss, ("x", "y", "c"))

    blocks = []
    for k in names:
        per_chip = jnp.stack(jnp.split(grads[k].astype(BF16), 4, axis=_SHARDED[k][0]), axis=0)
        blocks.append(jnp.moveaxis(_as_rows(per_chip.reshape((4, 2, -1) + per_chip.shape[2:]), 2), 1, 0))
    theirs = _core_exchange(blocks, True, "swap_halves")
    mine = [lax.dynamic_index_in_dim(b, core, 0, keepdims=False) for b in blocks]
    chip_sums = [_add_to_bf16_call(a, b, "sum_cores_" + k) for k, a, b in zip(names, mine, theirs)]
    arrived = _chip_exchange(chip_sums, True, "scatter_grads")
    arrived = [own_block(a, lax.dynamic_index_in_dim(s, my_chip, 0, keepdims=False)) for a, s in zip(arrived, chip_sums)]
    owned = [_sum4_call(a, "sum_chips_" + k) for k, a in zip(names, arrived)]
    others = _core_exchange(owned, False, "swap_owned")
    shared = [jnp.concatenate([jnp.where(core == 0, a, b), jnp.where(core == 0, b, a)], axis=0)
              for a, b in zip(owned, others)]

    small_names = [k for k in _WEIGHT_KEYS if k not in _SHARDED]
    packed = _pack([grads[k] for k in small_names])
    (packed4,) = _chip_exchange([packed], False, "gather_small_grads")
    small_sum = _sum4_call(own_block(packed4, packed), "sum_chips_small")
    (small_other,) = _core_exchange([small_sum], False, "swap_small")

    outs = {}
    for k, g in zip(names, shared):
        shp = shards[k].shape
        rows = _as_rows(shards[k]).shape
        res = _adam_call(_as_rows(shards[k]), [g.reshape(rows)], _as_rows(moms[k][0]), _as_rows(moms[k][1]), "adamw_" + k)
        outs[k] = [r.reshape(shp) for r in res]
    like = [shards[k] for k in small_names]
    res = _adam_call(_pack(like), [small_sum, small_other], _pack([moms[k][0] for k in small_names]),
                     _pack([moms[k][1] for k in small_names]), "adamw_small")
    for idx in range(4):
        for k, a in zip(small_names, _unpack(res[idx], like)):
            outs.setdefault(k, [None] * 4)[idx] = a

    result = [loss, grad_x[None]]
    for idx in range(4):
        result += [outs[k][idx] for k in _WEIGHT_KEYS]
    return tuple(result)
```

```python
import functools

import jax
import jax.numpy as jnp
from jax import lax
from jax.experimental import pallas as pl
from jax.experimental.pallas import tpu as pltpu

F32 = jnp.float32
BF16 = jnp.bfloat16
MESH = pl.DeviceIdType.MESH
F32_DOT = lax.Precision.HIGH

D_MODEL = 1024
DEPTH = 4
N_META = 16
CHUNK = 64
PAD = CHUNK - N_META
CONV_K = 4
HALO = 8
WIDTH = 768
S5_GROUPS, S5_GROUP, S5_STATE = 48, 16, 64
S5_BLOCKS = 3
S5_SEGMENTS = 8
SSD_HEADS, SSD_HEAD_DIM, SSD_GROUPS, SSD_STATE = 12, 64, 2, 128
SSD_PAIRS = 6
GDN_HEADS, GDN_DIM = 6, 128
LANES = 128
SMALL_ROWS = 24
ALPHA = (2 * DEPTH) ** 0.25
LN_EPS = 1e-5
P_S5U, P_S5Z, P_XBC, P_DT, P_SSDZ, P_QKV, P_GA, P_GB, P_GDNZ, P_GATE, P_END = (
    0, 768, 1536, 2816, 2828, 3596, 5900, 5906, 5912, 6680, 9752)
ADAM_LR, ADAM_B1, ADAM_B2, ADAM_EPS, ADAM_WD, ADAM_STEP = 0.001, 0.9, 0.999, 1e-08, 0.01, 10
NEG = -1e30


def _pick(n, cands):
    for c in cands:
        if n % c == 0:
            return c
    raise ValueError(f"no tile for {n} in {cands}")


def _cparams(sem, vmem_mb):
    return pltpu.CompilerParams(dimension_semantics=sem, vmem_limit_bytes=vmem_mb << 20)


_DIMS = {"nn": (((1,), (0,)), ((), ())), "nt": (((1,), (1,)), ((), ())), "tn": (((0,), (0,)), ((), ()))}


def _dot(a, b, mode, hi):
    if hi:
        prec = lax.Precision.HIGHEST if hi == "exact" else F32_DOT
        return lax.dot_general(a, b, _DIMS[mode], precision=prec, preferred_element_type=F32)
    return lax.dot_general(a.astype(BF16), b.astype(BF16), _DIMS[mode], preferred_element_type=F32)


@functools.partial(jax.custom_vjp, nondiff_argnums=(2, 3))
def _mm(a, b, mode="nn", hi=False):
    return _dot(a, b, mode, hi)


def _mm_fwd(a, b, mode, hi):
    return _dot(a, b, mode, hi), (a, b)


def _mm_bwd(mode, hi, res, g):
    a, b = res
    if mode == "nn":
        return _dot(g, b, "nt", hi), _dot(a, g, "tn", hi)
    if mode == "nt":
        return _dot(g, b, "nn", hi), _dot(g, a, "tn", hi)
    return _dot(b, g, "nt", hi), _dot(a, g, "nn", hi)


_mm.defvjp(_mm_fwd, _mm_bwd)


@functools.partial(jax.custom_vjp, nondiff_argnums=(1,))
def _roll_rows(x, k):
    return pltpu.roll(x, k % x.shape[0], 0)


def _roll_fwd(x, k):
    return _roll_rows(x, k), None


def _roll_bwd(k, _, g):
    return (_roll_rows(g, -k),)


_roll_rows.defvjp(_roll_fwd, _roll_bwd)


def _iota(shape, dim):
    return lax.broadcasted_iota(jnp.int32, shape, dim)


def _valid_rows(row0, n):
    return (row0 + _iota((n, 1), 0)) >= PAD


def _lane(x, idx):
    return jnp.sum(jnp.where(_iota(x.shape, 1) == idx, x, 0.0), axis=1, keepdims=True)


def _row(x, idx):
    return jnp.sum(jnp.where(_iota(x.shape, 0) == idx, x, 0.0), axis=0, keepdims=True)


def _layer_norm(z, g, b):
    mu = jnp.mean(z, axis=-1, keepdims=True)
    zc = z - mu
    var = jnp.mean(zc * zc, axis=-1, keepdims=True)
    return zc * lax.rsqrt(var + LN_EPS) * g + b


def _rms_norm(z, g):
    return z * lax.rsqrt(jnp.mean(z * z, axis=-1, keepdims=True) + LN_EPS) * g


def _causal_conv(halo, x, w, row0):
    t = x.shape[0]
    xc = jnp.concatenate([halo, x], axis=0)
    acc = None
    for j in range(CONV_K):
        term = _roll_rows(xc, CONV_K - 1 - j)[HALO:HALO + t] * _row(w, j)
        acc = term if acc is None else acc + term
    return acc, x[t - HALO:t]


def _tri(n, strict=False):
    r, c = _iota((n, n), 0), _iota((n, n), 1)
    return (r > c) if strict else (r >= c)


def _scan_op(step, name, *, tile, nb, row_kinds, param_kinds, carry_shapes, out_widths, keep_shapes=(), vmem_mb=48):
    n_rows, n_par, n_car, n_out = len(row_kinds), len(param_kinds), len(carry_shapes), len(out_widths)
    n_keep = len(keep_shapes)
    saved_shapes = tuple(carry_shapes) + tuple(keep_shapes)

    def dims(rows):
        for k, a in zip(row_kinds, rows):
            if k in "bs":
                return a.shape[0], a.shape[0] // tile
        raise ValueError("need a row input")

    def row_spec(kind, a, rev, nt):
        ti = (lambda i: nt - 1 - i) if rev else (lambda i: i)
        if kind == "b":
            return pl.BlockSpec((tile, a.shape[1] // nb), lambda b, i: (ti(i), b))
        if kind == "s":
            return pl.BlockSpec((tile, a.shape[1]), lambda b, i: (ti(i), 0))
        return pl.BlockSpec((None, a.shape[1], a.shape[2]), lambda b, i: (ti(i), 0, 0))

    def par_spec(kind, a):
        if kind == "b":
            return pl.BlockSpec((None, a.shape[1], a.shape[2]), lambda b, i: (b, 0, 0))
        return pl.BlockSpec(a.shape, lambda b, i: (0, 0))

    def fwd_call(rows, params):
        length, nt = dims(rows)

        def body(*refs):
            r_in = refs[:n_rows]
            p_in = refs[n_rows:n_rows + n_par]
            o_out = refs[n_rows + n_par:n_rows + n_par + n_out]
            s_out = refs[n_rows + n_par + n_out:n_rows + n_par + n_out + n_car + n_keep]
            c_scr = refs[n_rows + n_par + n_out + n_car + n_keep:]
            b, i = pl.program_id(0), pl.program_id(1)

            if n_car:
                @pl.when(i == 0)
                def _():
                    for c in c_scr:
                        c[...] = jnp.zeros_like(c)

            cin = tuple(c[...] for c in c_scr)
            for s, c in zip(s_out, cin):
                s[...] = c
            res = step(cin, tuple(r[...] for r in r_in), tuple(p[...] for p in p_in), b, i * tile)
            new_c, outs = res[0], res[1]
            for c, v in zip(c_scr, new_c):
                c[...] = v
            for o, v in zip(o_out, outs):
                o[...] = v
            if n_keep:
                for s, v in zip(s_out[n_car:], res[2]):
                    s[...] = v

        out_shape = [jax.ShapeDtypeStruct((length, nb * w), F32) for w in out_widths]
        out_shape += [jax.ShapeDtypeStruct((nb, nt) + tuple(s), F32) for s in saved_shapes]
        out_specs = [pl.BlockSpec((tile, w), lambda b, i: (i, b)) for w in out_widths]
        out_specs += [pl.BlockSpec((None, None) + tuple(s), lambda b, i: (b, i, 0, 0)) for s in saved_shapes]
        res = pl.pallas_call(
            body, name=name + "_fwd", grid=(nb, nt),
            in_specs=[row_spec(k, a, False, nt) for k, a in zip(row_kinds, rows)]
            + [par_spec(k, a) for k, a in zip(param_kinds, params)],
            out_specs=out_specs, out_shape=out_shape,
            scratch_shapes=[pltpu.VMEM(tuple(s), F32) for s in carry_shapes],
            compiler_params=_cparams(("arbitrary", "arbitrary"), vmem_mb),
        )(*rows, *params)
        return tuple(res[:n_out]), tuple(res[n_out:])

    def bwd_call(rows, params, saved, douts):
        length, nt = dims(rows)

        def body(*refs):
            k0 = 0
            r_in = refs[k0:k0 + n_rows]; k0 += n_rows
            p_in = refs[k0:k0 + n_par]; k0 += n_par
            s_in = refs[k0:k0 + n_car]; k0 += n_car
            k_in = refs[k0:k0 + n_keep]; k0 += n_keep
            g_in = refs[k0:k0 + n_out]; k0 += n_out
            dr_out = refs[k0:k0 + n_rows]; k0 += n_rows
            dp_out = refs[k0:k0 + n_par]; k0 += n_par
            dc_scr = refs[k0:]
            b, i = pl.program_id(0), pl.program_id(1)
            row0 = (nt - 1 - i) * tile

            @pl.when(i == 0)
            def _():
                for c in dc_scr:
                    c[...] = jnp.zeros_like(c)
                for p in dp_out:
                    p[...] = jnp.zeros_like(p)

            def f(c, r, p):
                if n_keep:
                    return step(c, r, p, b, row0, kept=tuple(k[...] for k in k_in))[:2]
                return step(c, r, p, b, row0)

            _, vjp = jax.vjp(f, tuple(s[...] for s in s_in), tuple(r[...] for r in r_in),
                             tuple(p[...] for p in p_in))
            dc, dr, dp = vjp((tuple(c[...] for c in dc_scr), tuple(g[...] for g in g_in)))
            for c, v in zip(dc_scr, dc):
                c[...] = v
            for o, v in zip(dr_out, dr):
                o[...] = v
            for o, v in zip(dp_out, dp):
                o[...] += v

        rev = lambda i: nt - 1 - i
        in_specs = [row_spec(k, a, True, nt) for k, a in zip(row_kinds, rows)]
        in_specs += [par_spec(k, a) for k, a in zip(param_kinds, params)]
        in_specs += [pl.BlockSpec((None, None) + tuple(s), lambda b, i: (b, rev(i), 0, 0)) for s in saved_shapes]
        in_specs += [pl.BlockSpec((tile, w), lambda b, i: (rev(i), b)) for w in out_widths]
        out_shape, out_specs = [], []
        for k, a in zip(row_kinds, rows):
            if k == "b":
                out_shape.append(jax.ShapeDtypeStruct(a.shape, F32))
                out_specs.append(pl.BlockSpec((tile, a.shape[1] // nb), lambda b, i: (rev(i), b)))
            elif k == "s":
                out_shape.append(jax.ShapeDtypeStruct((nb,) + a.shape, F32))
                out_specs.append(pl.BlockSpec((None, tile, a.shape[1]), lambda b, i: (b, rev(i), 0)))
            else:
                out_shape.append(jax.ShapeDtypeStruct((nb,) + a.shape, F32))
                out_specs.append(pl.BlockSpec((None, None, a.shape[1], a.shape[2]), lambda b, i: (b, rev(i), 0, 0)))
        for k, a in zip(param_kinds, params):
            shp = a.shape[1:] if k == "b" else a.shape
            out_shape.append(jax.ShapeDtypeStruct((nb,) + tuple(shp), F32))
            out_specs.append(pl.BlockSpec((None,) + tuple(shp), lambda b, i: (b, 0, 0)))
        res = pl.pallas_call(
            body, name=name + "_bwd", grid=(nb, nt), in_specs=in_specs, out_specs=out_specs, out_shape=out_shape,
            scratch_shapes=[pltpu.VMEM(tuple(s), F32) for s in carry_shapes],
            compiler_params=_cparams(("arbitrary", "arbitrary"), vmem_mb),
        )(*rows, *params, *saved, *douts)
        fold = (lambda a: a[0]) if nb == 1 else (lambda a: jnp.sum(a, axis=0))
        drows = tuple(r if k == "b" else fold(r) for k, r in zip(row_kinds, res[:n_rows]))
        dpars = tuple(p if k == "b" else fold(p) for k, p in zip(param_kinds, res[n_rows:]))
        return drows, dpars

    @jax.custom_vjp
    def op(rows, params):
        return fwd_call(rows, params)[0]

    def op_fwd(rows, params):
        outs, saved = fwd_call(rows, params)
        return outs, (rows, params, saved)

    def op_bwd(res, douts):
        rows, params, saved = res
        return bwd_call(rows, params, saved, tuple(douts))

    op.defvjp(op_fwd, op_bwd)
    return op


_WIDE = 1280


def _mm_rows(m, n):
    return _pick(m, (832, 128)) if n <= _WIDE else _pick(m, (416, 128))


def _mm_fwd_call(x, w, name):
    m, k = x.shape
    n = w.shape[1]
    tm = _mm_rows(m, n)

    def body(x_ref, w_ref, o_ref):
        o_ref[...] = _dot(x_ref[...], w_ref[...], "nn", False)

    return pl.pallas_call(
        body, name=name, grid=(m // tm,),
        in_specs=[pl.BlockSpec((tm, k), lambda i: (i, 0)), pl.BlockSpec((k, n), lambda i: (0, 0))],
        out_specs=pl.BlockSpec((tm, n), lambda i: (i, 0)),
        out_shape=jax.ShapeDtypeStruct((m, n), F32),
        compiler_params=_cparams(("parallel",), 48),
    )(x, w)


def _mm_bwd_call(g, x, w, name):
    m, n = g.shape
    k = w.shape[0]
    tm = _mm_rows(m, n)

    def body(g_ref, x_ref, w_ref, dx_ref, dw_ref):
        @pl.when(pl.program_id(0) == 0)
        def _():
            dw_ref[...] = jnp.zeros_like(dw_ref)

        g = g_ref[...].astype(BF16)
        dx_ref[...] = _dot(g, w_ref[...], "nt", False)
        x = x_ref[...]
        step = _pick(n, (768, 640, 128))
        for c0 in range(0, n, step):
            dw_ref[:, c0:c0 + step] += _dot(x, g[:, c0:c0 + step], "tn", False)

    return pl.pallas_call(
        body, name=name, grid=(m // tm,),
        in_specs=[pl.BlockSpec((tm, n), lambda i: (i, 0)), pl.BlockSpec((tm, k), lambda i: (i, 0)),
                  pl.BlockSpec((k, n), lambda i: (0, 0))],
        out_specs=[pl.BlockSpec((tm, k), lambda i: (i, 0)), pl.BlockSpec((k, n), lambda i: (0, 0))],
        out_shape=[jax.ShapeDtypeStruct((m, k), F32), jax.ShapeDtypeStruct((k, n), F32)],
        compiler_params=_cparams(("arbitrary",), 56),
    )(g, x, w)


def _dense(name):
    @jax.custom_vjp
    def op(x, w, xb, wb):
        return _mm_fwd_call(xb, wb, name + "_fwd")

    def op_fwd(x, w, xb, wb):
        return _mm_fwd_call(xb, wb, name + "_fwd"), (xb, wb)

    def op_bwd(res, g):
        xb, wb = res
        dx, dw = _mm_bwd_call(g, xb, wb, name + "_bwd")
        return dx, dw, jnp.zeros_like(xb), jnp.zeros_like(wb)

    op.defvjp(op_fwd, op_bwd)
    return op


def _bf16_copy(v):
    return lax.stop_gradient(v).astype(BF16)


def _merge_fwd_call(ys, gates, w, bias):
    m, k = ys[0].shape
    d = gates[0].shape[1]
    tm = _pick(m, (416, 128))

    def body(ya, yb, yc, ga, gb, gc, w_ref, b_ref, o_ref):
        acc = None
        for n, (y_ref, g_ref) in enumerate(((ya, ga), (yb, gb), (yc, gc))):
            term = jax.nn.sigmoid(g_ref[...] + b_ref[:, n * d:(n + 1) * d]) * _dot(y_ref[...], w_ref[n], "nn", False)
            acc = term if acc is None else acc + term
        o_ref[...] = acc

    rows = lambda width: pl.BlockSpec((tm, width), lambda i: (i, 0))
    return pl.pallas_call(
        body, name="gated_merge_fwd", grid=(m // tm,),
        in_specs=[rows(k)] * 3 + [rows(d)] * 3 + [pl.BlockSpec(w.shape, lambda i: (0, 0, 0)),
                                                 pl.BlockSpec(bias.shape, lambda i: (0, 0))],
        out_specs=rows(d), out_shape=jax.ShapeDtypeStruct((m, d), F32),
        compiler_params=_cparams(("parallel",), 48),
    )(*ys, *gates, w, bias)


def _merge_bwd_call(y, gate, dm, w, bias, name):
    m, k = y.shape
    d = gate.shape[1]
    tm = _pick(m, (416, 128))

    def body(y_ref, g_ref, dm_ref, w_ref, b_ref, dy_ref, dg_ref, dw_ref, db_ref):
        @pl.when(pl.program_id(0) == 0)
        def _():
            dw_ref[...] = jnp.zeros_like(dw_ref)
            db_ref[...] = jnp.zeros_like(db_ref)

        yv, dmv = y_ref[...], dm_ref[...]
        s = jax.nn.sigmoid(g_ref[...] + b_ref[...])
        d_gate = dmv * _dot(yv, w_ref[...], "nn", False) * s * (1.0 - s)
        dg_ref[...] = d_gate
        db_ref[...] += jnp.sum(d_gate, axis=0, keepdims=True)
        d_out = (dmv * s).astype(BF16)
        dy_ref[...] = _dot(d_out, w_ref[...], "nt", False)
        dw_ref[...] += _dot(yv, d_out, "tn", False)

    rows = lambda width: pl.BlockSpec((tm, width), lambda i: (i, 0))
    whole = lambda shape: pl.BlockSpec(shape, lambda i: (0, 0))
    return pl.pallas_call(
        body, name=name, grid=(m // tm,),
        in_specs=[rows(k), rows(d), rows(d), whole((k, d)), whole((1, d))],
        out_specs=[rows(k), rows(d), whole((k, d)), whole((1, d))],
        out_shape=[jax.ShapeDtypeStruct((m, k), F32), jax.ShapeDtypeStruct((m, d), F32),
                   jax.ShapeDtypeStruct((k, d), F32), jax.ShapeDtypeStruct((1, d), F32)],
        compiler_params=_cparams(("arbitrary",), 48),
    )(y, gate, dm, w, bias)


@jax.custom_vjp
def _gated_merge(ys, gates, w, bias, ysb, wb):
    return _merge_fwd_call(ysb, gates, wb, bias)


def _gated_merge_fwd(ys, gates, w, bias, ysb, wb):
    return _merge_fwd_call(ysb, gates, wb, bias), (gates, bias, ysb, wb)


def _gated_merge_bwd(res, dm):
    gates, bias, ysb, wb = res
    d = gates[0].shape[1]
    parts = [_merge_bwd_call(ysb[n], gates[n], dm, wb[n], bias[:, n * d:(n + 1) * d], "gated_merge_bwd_" + "abc"[n])
             for n in range(3)]
    dys, dgs, dws, dbs = zip(*parts)
    return (tuple(dys), tuple(dgs), jnp.stack(dws), jnp.concatenate(dbs, axis=1),
            tuple(jnp.zeros_like(v) for v in ysb), jnp.zeros_like(wb))


_gated_merge.defvjp(_gated_merge_fwd, _gated_merge_bwd)


def _ln_in_step(c, rows, params, b, row0):
    (z,), (g, bb) = rows, params
    return (), (jnp.where(_valid_rows(row0, z.shape[0]), _layer_norm(z, g, bb), 0.0),)


def _ln_res_step(c, rows, params, b, row0):
    (h, o), (g, bb) = rows, params
    return (), (jnp.where(_valid_rows(row0, h.shape[0]), _layer_norm(ALPHA * h + o, g, bb), 0.0),)


def _s5_prep_step(c, rows, params, b, row0):
    a_re, a_im, log_step, b_re, b_im = rows
    lam_re = jnp.minimum(a_re, -1e-4)
    lam_im = a_im
    step = jnp.exp(log_step)
    mag = jnp.exp(lam_re * step)
    abar_re, abar_im = mag * jnp.cos(lam_im * step), mag * jnp.sin(lam_im * step)
    den = lam_re * lam_re + lam_im * lam_im
    nr, ni = abar_re - 1.0, abar_im
    coef_re = (nr * lam_re + ni * lam_im) / den
    coef_im = (ni * lam_re - nr * lam_im) / den
    return (), (abar_re, abar_im, coef_re * b_re - coef_im * b_im, coef_re * b_im + coef_im * b_re)


def _s5_scan_step(c, rows, params, b, row0):
    (c_re, c_im), (u,) = c, rows
    bd_re, bd_im, a_re, a_im, cd_re, cd_im = params
    t = u.shape[0]
    steps = t // S5_SEGMENTS
    bu_re, bu_im = _mm(u, bd_re), _mm(u, bd_im)
    a_re, a_im = (jnp.broadcast_to(v, (S5_SEGMENTS, v.shape[1])) for v in (a_re, a_im))
    at = lambda v, i: v[S5_SEGMENTS * i:S5_SEGMENTS * (i + 1)]

    def advance(s_re, s_im, first):
        states = []
        for i in range(first, steps):
            s_re, s_im = a_re * s_re - a_im * s_im + at(bu_re, i), a_re * s_im + a_im * s_re + at(bu_im, i)
            states.append((s_re, s_im))
        return s_re, s_im, states

    s_re, s_im, _ = advance(at(bu_re, 0), at(bu_im, 0), 1)
    q_re = q_im = None
    p_re, p_im, left = a_re, a_im, steps
    while left:
        if left & 1:
            q_re, q_im = (p_re, p_im) if q_re is None else (q_re * p_re - q_im * p_im, q_re * p_im + q_im * p_re)
        p_re, p_im = p_re * p_re - p_im * p_im, 2.0 * p_re * p_im
        left >>= 1
    seg = _iota((S5_SEGMENTS, 1), 0)
    in_re = jnp.where(seg == 0, c_re, _roll_rows(s_re, 1))
    in_im = jnp.where(seg == 0, c_im, _roll_rows(s_im, 1))
    d = 1
    while d < S5_SEGMENTS:
        keep = seg >= d
        sh_re = jnp.where(keep, _roll_rows(in_re, d), 0.0)
        sh_im = jnp.where(keep, _roll_rows(in_im, d), 0.0)
        in_re, in_im = in_re + q_re * sh_re - q_im * sh_im, in_im + q_re * sh_im + q_im * sh_re
        q_re, q_im = q_re * q_re - q_im * q_im, 2.0 * q_re * q_im
        d *= 2
    _, _, full = advance(in_re, in_im, 0)
    y = _mm(jnp.concatenate([f[0] for f in full], axis=0), cd_re) - _mm(jnp.concatenate([f[1] for f in full], axis=0), cd_im)
    return (_row(full[-1][0], S5_SEGMENTS - 1), _row(full[-1][1], S5_SEGMENTS - 1)), (y,)


def _interleave(v, tile, inverse=False):
    length, width = v.shape
    shape = (length // tile, tile // S5_SEGMENTS, S5_SEGMENTS) if inverse else (length // tile, S5_SEGMENTS, tile // S5_SEGMENTS)
    return v.reshape(shape + (width,)).transpose(0, 2, 1, 3).reshape(length, width)


def _s5_post_step(c, rows, params, b, row0):
    (y, u, z), (d, w_glu, b_glu) = rows, params
    v = jax.nn.gelu(y + d * u)
    v = v * jax.nn.sigmoid(_mm(v, w_glu) + b_glu)
    return (), (v * jax.nn.silu(z),)


def _ssd_step(c, rows, params, b, row0):
    halo, state = c
    xbc_raw, z, small, small_t = rows
    cw, cb, d_l, bias_l, alog_l, bias_c, alog_c, norm_g = params
    t = xbc_raw.shape[0]
    grp = SSD_GROUPS * SSD_STATE
    valid = _valid_rows(row0, t)
    conv, halo2 = _causal_conv(halo, xbc_raw, cw, row0)
    act = jnp.where(valid, jax.nn.silu(conv + cb), 0.0)
    low = _iota((1, LANES), 1) < SSD_HEAD_DIM
    dt_all = jnp.where(valid, jax.nn.softplus(small + bias_l), 0.0)
    a_all = -jnp.exp(alog_l)
    valid_t = (row0 + _iota((1, t), 1)) >= PAD
    dta_t = jnp.where(valid_t, jax.nn.softplus(small_t + bias_c), 0.0) * (-jnp.exp(alog_c))
    acum_t = _mm(dta_t, jnp.where(_iota((t, t), 0) <= _iota((t, t), 1), 1.0, 0.0), "nn", True)
    causal = _tri(t)
    acum_all = _mm(jnp.where(causal, 1.0, 0.0), dt_all * a_all, "nn", True)
    last_all = _row(acum_all, t - 1)
    low_rows = _iota((LANES, 1), 0) < SSD_HEAD_DIM
    pairs, groups = range(SSD_PAIRS), range(SSD_GROUPS)
    grp_of = [p // (SSD_PAIRS // SSD_GROUPS) for p in pairs]
    bs = [act[:, WIDTH + g * SSD_STATE:WIDTH + (g + 1) * SSD_STATE] for g in groups]
    cs = [act[:, WIDTH + grp + g * SSD_STATE:WIDTH + grp + (g + 1) * SSD_STATE] for g in groups]
    scores = [_mm(cs[g], bs[g], "nt") for g in groups]
    per_lane = lambda v, p: jnp.where(low, v[:, 2 * p:2 * p + 1], v[:, 2 * p + 1:2 * p + 2])
    dt_l = [per_lane(dt_all, p) for p in pairs]
    acum_l = [per_lane(acum_all, p) for p in pairs]
    last_l = [per_lane(last_all, p) for p in pairs]
    st = [state[p * LANES:(p + 1) * LANES] for p in pairs]
    xd = [act[:, p * LANES:(p + 1) * LANES] * dt_l[p] for p in pairs]
    decay = [jnp.exp(jnp.where(causal, acum_all[:, h:h + 1] - _row(acum_t, h), NEG)) for h in range(SSD_HEADS)]
    y_lo = [_mm(scores[grp_of[p]] * decay[2 * p], jnp.where(low, xd[p], 0.0)) for p in pairs]
    y_hi = [_mm(scores[grp_of[p]] * decay[2 * p + 1], jnp.where(low, 0.0, xd[p])) for p in pairs]
    y_off = [_mm(cs[grp_of[p]], st[p], "nt") * jnp.exp(acum_l[p]) for p in pairs]
    new_st = [_mm(xd[p] * jnp.exp(last_l[p] - acum_l[p]), bs[grp_of[p]], "tn") for p in pairs]
    cd = jnp.exp(last_all)
    new_st = [st[p] * jnp.where(low_rows, cd[:, 2 * p:2 * p + 1], cd[:, 2 * p + 1:2 * p + 2]) + new_st[p] for p in pairs]
    y = jnp.concatenate([y_lo[p] + y_hi[p] + y_off[p] for p in pairs], axis=1) + act[:, :WIDTH] * d_l
    out = _rms_norm(y * jax.nn.silu(z), norm_g)
    return (halo2, jnp.concatenate(new_st, axis=0)), (out,)


@jax.custom_vjp
def _unit_lower_inverse(mats):
    return _neumann_inverse(mats)


def _inverse_fwd(mats):
    inv = _neumann_inverse(mats)
    return inv, inv


def _inverse_bwd(inv, g):
    left = [_dot(t, d, "tn", True) for t, d in zip(inv, g)]
    return ([-_dot(l, t, "nt", True) for l, t in zip(left, inv)],)


_unit_lower_inverse.defvjp(_inverse_fwd, _inverse_bwd)


@jax.custom_vjp
def _kept_inverse(mats, inv):
    return list(inv)


def _kept_inverse_fwd(mats, inv):
    return list(inv), list(inv)


def _kept_inverse_bwd(inv, g):
    return _inverse_bwd(inv, g)[0], [jnp.zeros_like(t) for t in inv]


_kept_inverse.defvjp(_kept_inverse_fwd, _kept_inverse_bwd)


def _neumann_inverse(mats):
    n = mats[0].shape[0]
    eye = jnp.where(_iota((n, n), 0) == _iota((n, n), 1), 1.0, 0.0)
    inv = [eye - a for a in mats]
    p = [_mm(a, a, "nn", True) for a in mats]
    k = 2
    while k < n:
        inv = [i + _mm(i, q, "nn", True) for i, q in zip(inv, p)]
        k *= 2
        if k < n:
            p = [_mm(q, q, "nn", True) for q in p]
    return inv


def _gdn_step(c, rows, params, b, row0, kept=None):
    halo, state = c
    qkv_raw, z, small, small_t = rows
    cw, bias_l, alog_l, bias_c, alog_c, norm_g = params
    t = qkv_raw.shape[0]
    valid = _valid_rows(row0, t)
    conv, halo2 = _causal_conv(halo, qkv_raw, cw, row0)
    act = jnp.where(valid, jax.nn.silu(conv), 0.0)
    beta_all = jnp.where(valid, jax.nn.sigmoid(small), 0.0)
    g_all = jnp.where(valid, -jnp.exp(alog_l) * jax.nn.softplus(small + bias_l), 0.0)
    valid_t = (row0 + _iota((1, t), 1)) >= PAD
    g_t = jnp.where(valid_t, -jnp.exp(alog_c) * jax.nn.softplus(small_t + bias_c), 0.0)
    causal, strict = _tri(t), _tri(t, True)
    gcum_all = _mm(jnp.where(causal, 1.0, 0.0), g_all, "nn", True)
    gcum_t = _mm(g_t, jnp.where(_iota((t, t), 0) <= _iota((t, t), 1), 1.0, 0.0), "nn", True)
    heads = range(GDN_HEADS)
    part = lambda h, n: act[:, n * WIDTH + h * GDN_DIM:n * WIDTH + (h + 1) * GDN_DIM]
    unit = lambda x: x * lax.rsqrt(jnp.sum(x * x, axis=-1, keepdims=True) + 1e-6)
    q = [unit(part(h, 0)) * (GDN_DIM ** -0.5) for h in heads]
    k = [unit(part(h, 1)) for h in heads]
    st = [state[h * GDN_DIM:(h + 1) * GDN_DIM] for h in heads]
    ia = [SSD_HEADS + h for h in heads]
    beta = [beta_all[:, ia[h] + GDN_HEADS:ia[h] + GDN_HEADS + 1] for h in heads]
    gcum = [gcum_all[:, ia[h]:ia[h] + 1] for h in heads]
    gamma = [jnp.exp(jnp.where(causal, gcum[h] - _row(gcum_t, ia[h]), NEG)) for h in heads]
    egc = [jnp.exp(gcum[h]) for h in heads]
    a_mat = [jnp.where(strict, _mm(k[h], k[h], "nt") * gamma[h] * beta[h], 0.0) for h in heads]
    if kept is None:
        inv = _unit_lower_inverse(a_mat)
    else:
        inv = _kept_inverse(a_mat, [kept[0][h * t:(h + 1) * t] for h in heads])
    rhs = [jnp.concatenate([part(h, 2) * beta[h], k[h] * (beta[h] * egc[h])], axis=1) for h in heads]
    sol = [_mm(inv[h], rhs[h], "nn", True) for h in heads]
    attn = [_mm(q[h], k[h], "nt") * gamma[h] for h in heads]
    from_state = [_mm(jnp.concatenate([sol[h][:, GDN_DIM:], q[h] * egc[h]], axis=0), st[h]) for h in heads]
    v_new = [sol[h][:, :GDN_DIM] - from_state[h][:t] for h in heads]
    o = [from_state[h][t:] + _mm(attn[h], v_new[h]) for h in heads]
    glast = [_row(gcum[h], t - 1) for h in heads]
    new_st = [st[h] * jnp.exp(glast[h]) + _mm(k[h] * jnp.exp(glast[h] - gcum[h]), v_new[h], "tn") for h in heads]
    out = jnp.concatenate([_rms_norm(o[h], norm_g) for h in heads], axis=1) * jax.nn.silu(z)
    return (halo2, jnp.concatenate(new_st, axis=0)), (out,), (jnp.concatenate(inv, axis=0),)


def _loss_tile(n):
    return _pick(n, (512, 256, 128, 64))


def _loss_fwd_call(y, tgt):
    n, d = y.shape
    tile = _loss_tile(n)

    def body(y_ref, t_ref, o_ref):
        @pl.when(pl.program_id(0) == 0)
        def _():
            o_ref[...] = jnp.zeros_like(o_ref)

        e = y_ref[...] - t_ref[...]
        o_ref[...] += jnp.sum(jnp.sum(e * e, axis=1, keepdims=True), axis=0, keepdims=True) * (0.5 / d)

    out = pl.pallas_call(
        body, name="loss_fwd", grid=(n // tile,),
        in_specs=[pl.BlockSpec((tile, d), lambda i: (i, 0)), pl.BlockSpec((tile, d), lambda i: (i, 0))],
        out_specs=pl.BlockSpec((8, LANES), lambda i: (0, 0)),
        out_shape=jax.ShapeDtypeStruct((8, LANES), F32),
        compiler_params=_cparams(("arbitrary",), 32),
    )(y, tgt)
    return out[0, 0]


def _loss_bwd_call(y, tgt, g):
    n, d = y.shape
    tile = _loss_tile(n)

    def body(y_ref, t_ref, g_ref, o_ref):
        o_ref[...] = (y_ref[...] - t_ref[...]) * (g_ref[...][0:1, 0:1] * (1.0 / d))

    return pl.pallas_call(
        body, name="loss_bwd", grid=(n // tile,),
        in_specs=[pl.BlockSpec((tile, d), lambda i: (i, 0)), pl.BlockSpec((tile, d), lambda i: (i, 0)),
                  pl.BlockSpec((8, LANES), lambda i: (0, 0))],
        out_specs=pl.BlockSpec((tile, d), lambda i: (i, 0)),
        out_shape=jax.ShapeDtypeStruct((n, d), F32),
        compiler_params=_cparams(("parallel",), 32),
    )(y, tgt, jnp.broadcast_to(g, (8, LANES)).astype(F32))


@jax.custom_vjp
def _loss_op(y, tgt):
    return _loss_fwd_call(y, tgt)


def _loss_op_fwd(y, tgt):
    return _loss_fwd_call(y, tgt), (y, tgt)


def _loss_op_bwd(res, g):
    y, tgt = res
    return _loss_bwd_call(y, tgt, g), jnp.zeros_like(tgt)


_loss_op.defvjp(_loss_op_fwd, _loss_op_bwd)


def _rowwise(step, name, tile, n_rows, n_params, out_widths, vmem_mb=48):
    return _scan_op(step, name, tile=tile, nb=1, row_kinds="s" * n_rows, param_kinds="s" * n_params,
                    carry_shapes=(), out_widths=out_widths, vmem_mb=vmem_mb)


def _block_diag(x, nblk):
    bsz, _, r, c = x.shape
    eye = jnp.eye(nblk, dtype=x.dtype)
    return jnp.einsum("bgrc,gh->bgrhc", x, eye).reshape(bsz, nblk * r, nblk * c)


_PROJ = {"s5u": (P_S5U, P_S5Z), "s5z": (P_S5Z, P_XBC), "xbc": (P_XBC, P_DT), "ssdz": (P_SSDZ, P_QKV),
         "qkv": (P_QKV, P_GA), "gdnz": (P_GDNZ, P_GATE), "gate_a": (P_GATE, P_GATE + D_MODEL),
         "gate_b": (P_GATE + D_MODEL, P_GATE + 2 * D_MODEL), "gate_c": (P_GATE + 2 * D_MODEL, P_END)}


def _prepare(weights, compute):
    w = weights
    zeros = lambda *shape: jnp.zeros((DEPTH,) + shape, F32)
    row3 = lambda v: v.reshape(DEPTH, 1, -1)
    p = {}
    for pre, w_in in (("w_", w["w_in"]), ("wb_", compute["w_in"])):
        seg = lambda a, bnd: w_in[:, :, a:bnd]
        p.update({pre + k: seg(a, bnd) for k, (a, bnd) in _PROJ.items()})
        p[pre + "small"] = jnp.concatenate(
            [seg(P_DT, P_SSDZ), seg(P_GA, P_GDNZ), zeros(D_MODEL, LANES - SMALL_ROWS).astype(w_in.dtype)], axis=2)
    p.update(wb_branch=compute["w_branch"], wb_out=compute["w_out"])

    rows, wide = DEPTH * S5_GROUPS, S5_STATE * S5_GROUP
    flat = lambda v: v.reshape(rows, -1)
    rep = lambda v: jnp.repeat(flat(v), S5_GROUP, axis=1)
    prep = _rowwise(_s5_prep_step, "s5_prep", S5_GROUPS, 5, 0, (wide,) * 4)
    abar_re, abar_im, bbar_re, bbar_im = prep(
        (rep(w["s5_a_re"]), rep(w["s5_a_im"]), jnp.broadcast_to(flat(w["s5_log_step"]), (rows, wide)),
         flat(w["s5_b_re"]), flat(w["s5_b_im"])), ())
    gpb = S5_GROUPS // S5_BLOCKS
    lanes, chans, nblk = gpb * S5_STATE, gpb * S5_GROUP, DEPTH * S5_BLOCKS
    to_bd = lambda bb: _block_diag(bb.reshape(nblk, gpb, S5_STATE, S5_GROUP).transpose(0, 1, 3, 2), gpb).reshape(
        DEPTH, S5_BLOCKS, chans, lanes)
    to_cd = lambda cc: _block_diag(cc.reshape(nblk, gpb, S5_GROUP, S5_STATE).transpose(0, 1, 3, 2), gpb).reshape(
        DEPTH, S5_BLOCKS, lanes, chans)
    to_a = lambda a: a[:, ::S5_GROUP].reshape(DEPTH, S5_BLOCKS, 1, lanes)
    p.update(s5_bd_re=to_bd(bbar_re), s5_bd_im=to_bd(bbar_im), s5_a_re=to_a(abar_re), s5_a_im=to_a(abar_im),
             s5_cd_re=to_cd(w["s5_c_re"]), s5_cd_im=to_cd(w["s5_c_im"]),
             s5_d=row3(w["s5_d"]), s5_w_glu=w["s5_w_glu"], s5_b_glu=row3(w["s5_b_glu"]))

    bias = jnp.concatenate([w["ssd_dt_bias"], w["gdn_dt_bias"]], axis=1)
    alog = jnp.concatenate([w["ssd_a_log"], w["gdn_a_log"]], axis=1)
    on_lanes = lambda v: jnp.concatenate([v, zeros(LANES - v.shape[1])], axis=1).reshape(DEPTH, 1, LANES)
    on_rows = lambda v: jnp.concatenate([v, zeros(SMALL_ROWS - v.shape[1])], axis=1).reshape(DEPTH, SMALL_ROWS, 1)
    pad_w = lambda cw: jnp.concatenate([cw, zeros(HALO - CONV_K, cw.shape[2])], axis=1)
    p.update(bias_l=on_lanes(bias), alog_l=on_lanes(alog), bias_c=on_rows(bias), alog_c=on_rows(alog),
             ssd_cw=pad_w(w["ssd_conv_w"]), ssd_cb=row3(w["ssd_conv_b"]),
             ssd_d=row3(jnp.repeat(w["ssd_d"], SSD_HEAD_DIM, axis=1)), ssd_norm_g=row3(w["ssd_norm_g"]),
             gdn_cw=pad_w(w["gdn_conv_w"]), gdn_norm_g=row3(w["gdn_norm_g"]),
             w_branch=w["w_branch"], b_gate=row3(w["b_gate"]), w_out=w["w_out"], ln_g=row3(w["ln_g"]), ln_b=row3(w["ln_b"]))
    return p


def _layer(h, p):
    length = h.shape[0]
    nt = length // CHUNK
    t_row = _pick(length, (208, 128))
    t_s5 = _pick(length, (832, 128))
    hb = _bf16_copy(h)
    proj = {k: _dense("proj_" + k)(h, p["w_" + k], hb, p["wb_" + k]) for k in list(_PROJ) + ["small"]}
    small = proj["small"]
    small_t = small[:, :SMALL_ROWS].reshape(nt, CHUNK, SMALL_ROWS).transpose(0, 2, 1)

    lanes = p["s5_a_re"].shape[-1]
    s5_scan = _scan_op(_s5_scan_step, "s5_scan", tile=t_s5, nb=S5_BLOCKS, row_kinds="b", param_kinds="bbbbbb",
                       carry_shapes=((1, lanes), (1, lanes)), out_widths=(WIDTH // S5_BLOCKS,))
    (y_ssm,) = s5_scan((_interleave(proj["s5u"], t_s5),), (p["s5_bd_re"], p["s5_bd_im"], p["s5_a_re"], p["s5_a_im"],
                                                           p["s5_cd_re"], p["s5_cd_im"]))
    y_ssm = _interleave(y_ssm, t_s5, inverse=True)
    s5_post = _rowwise(_s5_post_step, "s5_post", t_row, 3, 3, (WIDTH,))
    (y_a,) = s5_post((y_ssm, proj["s5u"], proj["s5z"]), (p["s5_d"], p["s5_w_glu"], p["s5_b_glu"]))

    scalars = (p["bias_l"], p["alog_l"], p["bias_c"], p["alog_c"])
    ssd = _scan_op(_ssd_step, "ssd_scan", tile=CHUNK, nb=1, row_kinds="ssst", param_kinds="s" * 8,
                   carry_shapes=((HALO, P_DT - P_XBC), (SSD_HEADS * SSD_HEAD_DIM, SSD_STATE)), out_widths=(WIDTH,))
    (y_b,) = ssd((proj["xbc"], proj["ssdz"], small, small_t),
                 (p["ssd_cw"], p["ssd_cb"], p["ssd_d"]) + scalars + (p["ssd_norm_g"],))
    gdn = _scan_op(_gdn_step, "gdn_scan", tile=CHUNK, nb=1, row_kinds="ssst", param_kinds="s" * 6,
                   carry_shapes=((HALO, P_GA - P_QKV), (GDN_HEADS * GDN_DIM, GDN_DIM)), out_widths=(WIDTH,),
                   keep_shapes=((GDN_HEADS * CHUNK, CHUNK),))
    (y_c,) = gdn((proj["qkv"], proj["gdnz"], small, small_t), (p["gdn_cw"],) + scalars + (p["gdn_norm_g"],))

    ys = (y_a, y_b, y_c)
    merged = _gated_merge(ys, (proj["gate_a"], proj["gate_b"], proj["gate_c"]), p["w_branch"], p["b_gate"],
                          tuple(_bf16_copy(y) for y in ys), p["wb_branch"])
    out = _dense("out_proj")(merged, p["w_out"], _bf16_copy(merged), p["wb_out"])
    ln = _rowwise(_ln_res_step, "ln_res", t_row, 2, 2, (D_MODEL,))
    (h_new,) = ln((h, out), (p["ln_g"], p["ln_b"]))
    return h_new


_LAYER_KEYS = ("w_in", "s5_a_re", "s5_a_im", "s5_log_step", "s5_b_re", "s5_b_im", "s5_c_re", "s5_c_im", "s5_d",
               "s5_w_glu", "s5_b_glu", "ssd_conv_w", "ssd_conv_b", "ssd_dt_bias", "ssd_a_log", "ssd_d", "ssd_norm_g",
               "gdn_conv_w", "gdn_dt_bias", "gdn_a_log", "gdn_norm_g", "w_branch", "b_gate", "w_out", "ln_g", "ln_b")
_WEIGHT_KEYS = ("meta", "ln_in_g", "ln_in_b") + _LAYER_KEYS


def _local_loss(weights, x, target, compute):
    seq = x.shape[0]
    hcat = jnp.concatenate([jnp.zeros((PAD, D_MODEL), F32), weights["meta"], x], axis=0)
    length = hcat.shape[0]
    ln_in = _rowwise(_ln_in_step, "ln_in", _pick(length, (416, 256, 128)), 1, 2, (D_MODEL,))
    (h,) = ln_in((hcat,), (weights["ln_in_g"].reshape(1, -1), weights["ln_in_b"].reshape(1, -1)))

    prepared = _prepare(weights, compute)
    for layer in range(DEPTH):
        h = _layer(h, {k: v[layer] for k, v in prepared.items()})
    return _loss_op(h[length - seq:], target)


_ANY = pl.BlockSpec(memory_space=pl.ANY)
_BLOCK_BYTES = 4 << 20


def _chip_exchange(arrays, all_to_all, name):
    n = len(arrays)

    def body(*refs):
        ins, outs = refs[:n], refs[n:2 * n]
        send_sems, recv_sems = refs[2 * n:]
        mx, my, mc = lax.axis_index("x"), lax.axis_index("y"), lax.axis_index("c")
        me = 2 * mx + my
        peers = [(1 - mx, my), (mx, 1 - my), (1 - mx, 1 - my)]
        sends = []
        for a, (src, dst) in enumerate(zip(ins, outs)):
            for k, (px, py) in enumerate(peers):
                cp = pltpu.make_async_remote_copy(
                    src_ref=src.at[2 * px + py] if all_to_all else src, dst_ref=dst.at[me],
                    send_sem=send_sems.at[a, k], recv_sem=recv_sems.at[a, k],
                    device_id=(px, py, mc), device_id_type=MESH)
                cp.start()
                sends.append(cp)
        for a, (src, dst) in enumerate(zip(ins, outs)):
            for k, (px, py) in enumerate(peers):
                pltpu.make_async_remote_copy(
                    src_ref=src.at[me] if all_to_all else src, dst_ref=dst.at[2 * px + py],
                    send_sem=send_sems.at[a, k], recv_sem=recv_sems.at[a, k],
                    device_id=(px, py, mc), device_id_type=MESH).wait_recv()
        for cp in sends:
            cp.wait_send()

    out_shape = [jax.ShapeDtypeStruct(a.shape if all_to_all else (4,) + a.shape, a.dtype) for a in arrays]
    return pl.pallas_call(
        body, name=name, in_specs=[_ANY] * n, out_specs=[_ANY] * n, out_shape=out_shape,
        scratch_shapes=[pltpu.SemaphoreType.DMA((n, 3)), pltpu.SemaphoreType.DMA((n, 3))],
    )(*arrays)


def _gather_two_level(arrays, name):
    n = len(arrays)

    def body(*refs):
        ins, outs = refs[:n], refs[n:2 * n]
        send_sems, recv_sems = refs[2 * n:]
        mx, my, mc = lax.axis_index("x"), lax.axis_index("y"), lax.axis_index("c")
        me = 2 * mx + my
        sibling = (mx, my, 1 - mc)
        chips = [(1 - mx, my), (mx, 1 - my), (1 - mx, 1 - my)]

        def copy(a, k, src, chip, core, to):
            return pltpu.make_async_remote_copy(
                src_ref=src, dst_ref=outs[a].at[chip, core], send_sem=send_sems.at[a, k], recv_sem=recv_sems.at[a, k],
                device_id=to, device_id_type=MESH)

        sends = [copy(a, j, ins[a], me, mc, (px, py, mc)) for a in range(n) for j, (px, py) in enumerate(chips)]
        for cp in sends:
            cp.start()
        passed = []
        for a in range(n):
            for j, (px, py) in enumerate(chips):
                chip = 2 * px + py
                copy(a, j, ins[a], chip, mc, sibling).wait_recv()
                cp = copy(a, 3 + j, outs[a].at[chip, mc], chip, mc, sibling)
                cp.start()
                passed.append(cp)
        for a in range(n):
            for j, (px, py) in enumerate(chips):
                copy(a, 3 + j, ins[a], 2 * px + py, 1 - mc, sibling).wait_recv()
        for cp in sends + passed:
            cp.wait_send()

    return pl.pallas_call(
        body, name=name, in_specs=[_ANY] * n, out_specs=[_ANY] * n,
        out_shape=[jax.ShapeDtypeStruct((4, 2) + a.shape, a.dtype) for a in arrays],
        scratch_shapes=[pltpu.SemaphoreType.DMA((n, 6)), pltpu.SemaphoreType.DMA((n, 6))],
    )(*arrays)


def _core_exchange(arrays, other_half, name):
    n = len(arrays)

    def body(*refs):
        ins, outs = refs[:n], refs[n:2 * n]
        send_sems, recv_sems = refs[2 * n:]
        mc = lax.axis_index("c")
        sibling = (lax.axis_index("x"), lax.axis_index("y"), 1 - mc)
        copies = [pltpu.make_async_remote_copy(
            src_ref=s.at[1 - mc] if other_half else s, dst_ref=d, send_sem=send_sems.at[a], recv_sem=recv_sems.at[a],
            device_id=sibling, device_id_type=MESH) for a, (s, d) in enumerate(zip(ins, outs))]
        for cp in copies:
            cp.start()
        for cp in copies:
            cp.wait()

    return pl.pallas_call(
        body, name=name, in_specs=[_ANY] * n, out_specs=[_ANY] * n,
        out_shape=[jax.ShapeDtypeStruct(a.shape[1:] if other_half else a.shape, a.dtype) for a in arrays],
        scratch_shapes=[pltpu.SemaphoreType.DMA((n,)), pltpu.SemaphoreType.DMA((n,))],
    )(*arrays)


def _as_rows(a, lead=0):
    shp = a.shape
    return a.reshape(shp[:lead] + (-1, shp[-1]))


def _sum4_call(x, name):
    _, r, c = x.shape
    tr = _pick(r, [t for t in (512, 256, 128, 64, 32, 16, 8) if 16 * t * c <= _BLOCK_BYTES] + [r])

    def body(x_ref, o_ref):
        part = [x_ref[j].astype(F32) for j in range(4)]
        o_ref[...] = (part[0] + part[1]) + (part[2] + part[3])

    return pl.pallas_call(
        body, name=name, grid=(r // tr,),
        in_specs=[pl.BlockSpec((4, tr, c), lambda i: (0, i, 0))],
        out_specs=pl.BlockSpec((tr, c), lambda i: (i, 0)),
        out_shape=jax.ShapeDtypeStruct((r, c), F32),
        compiler_params=_cparams(("parallel",), 48),
    )(x)


def _add_to_bf16_call(a, b, name):
    _, r, c = a.shape
    tr = _pick(r, [t for t in (512, 256, 128, 64, 32, 16) if 16 * t * c <= _BLOCK_BYTES] + [r])

    def body(a_ref, b_ref, o_ref):
        o_ref[...] = (a_ref[...].astype(F32) + b_ref[...].astype(F32)).astype(BF16)

    spec = pl.BlockSpec((4, tr, c), lambda i: (0, i, 0))
    return pl.pallas_call(
        body, name=name, grid=(r // tr,), in_specs=[spec, spec], out_specs=spec,
        out_shape=jax.ShapeDtypeStruct(a.shape, BF16), compiler_params=_cparams(("parallel",), 48),
    )(a, b)


def _adam_call(w, grads, m, v, name):
    r, c = w.shape
    n_g = len(grads)
    tr = _pick(r, [t for t in (512, 256, 128, 64, 32, 16, 8) if 4 * t * c <= _BLOCK_BYTES // 4] + [r])
    bc1 = 1.0 - ADAM_B1 ** ADAM_STEP
    bc2 = 1.0 - ADAM_B2 ** ADAM_STEP

    def body(*refs):
        w_ref, g_refs = refs[0], refs[1:1 + n_g]
        m_ref, v_ref, g_out, d_out, m_out, v_out = refs[1 + n_g:]
        g = g_refs[0][...]
        for g_ref in g_refs[1:]:
            g = g + g_ref[...]
        m_new = ADAM_B1 * m_ref[...] + (1.0 - ADAM_B1) * g
        v_new = ADAM_B2 * v_ref[...] + (1.0 - ADAM_B2) * (g * g)
        m_hat = m_new / bc1
        v_hat = v_new / bc2
        g_out[...] = g
        d_out[...] = -ADAM_LR * (m_hat / (jnp.sqrt(v_hat) + ADAM_EPS) + ADAM_WD * w_ref[...])
        m_out[...] = m_new
        v_out[...] = v_new

    spec = pl.BlockSpec((tr, c), lambda i: (i, 0))
    return pl.pallas_call(
        body, name=name, grid=(r // tr,), in_specs=[spec] * (3 + n_g), out_specs=[spec] * 4,
        out_shape=[jax.ShapeDtypeStruct((r, c), F32)] * 4,
        compiler_params=_cparams(("parallel",), 48),
    )(w, *grads, m, v)


_DENSE = ("w_in", "w_branch", "w_out")

_SHARDED = {"meta": (1, False), "w_in": (2, True), "s5_w_glu": (1, True), "ssd_conv_w": (2, False),
            "gdn_conv_w": (2, False), "w_branch": (3, True), "b_gate": (2, False), "w_out": (1, True)}


def _pack(arrs):
    flat = jnp.concatenate([a.reshape(-1) for a in arrs])
    n = flat.shape[0]
    rows = -(-n // (256 * LANES)) * 256
    return jnp.concatenate([flat, jnp.zeros((rows * LANES - n,), F32)]).reshape(rows, LANES)


def _unpack(packed, like):
    flat = packed.reshape(-1)
    out, off = [], 0
    for a in like:
        out.append(flat[off:off + a.size].reshape(a.shape))
        off += a.size
    return out


def kernel(x, meta, ln_in_g, ln_in_b, w_in, s5_a_re, s5_a_im, s5_log_step, s5_b_re, s5_b_im, s5_c_re, s5_c_im, s5_d, s5_w_glu, s5_b_glu, ssd_conv_w, ssd_conv_b, ssd_dt_bias, ssd_a_log, ssd_d, ssd_norm_g, gdn_conv_w, gdn_dt_bias, gdn_a_log, gdn_norm_g, w_branch, b_gate, w_out, ln_g, ln_b, loss_target, m_meta, m_ln_in_g, m_ln_in_b, m_w_in, m_s5_a_re, m_s5_a_im, m_s5_log_step, m_s5_b_re, m_s5_b_im, m_s5_c_re, m_s5_c_im, m_s5_d, m_s5_w_glu, m_s5_b_glu, m_ssd_conv_w, m_ssd_conv_b, m_ssd_dt_bias, m_ssd_a_log, m_ssd_d, m_ssd_norm_g, m_gdn_conv_w, m_gdn_dt_bias, m_gdn_a_log, m_gdn_norm_g, m_w_branch, m_b_gate, m_w_out, m_ln_g, m_ln_b, v_meta, v_ln_in_g, v_ln_in_b, v_w_in, v_s5_a_re, v_s5_a_im, v_s5_log_step, v_s5_b_re, v_s5_b_im, v_s5_c_re, v_s5_c_im, v_s5_d, v_s5_w_glu, v_s5_b_glu, v_ssd_conv_w, v_ssd_conv_b, v_ssd_dt_bias, v_ssd_a_log, v_ssd_d, v_ssd_norm_g, v_gdn_conv_w, v_gdn_dt_bias, v_gdn_a_log, v_gdn_norm_g, v_w_branch, v_b_gate, v_w_out, v_ln_g, v_ln_b):
    args = dict(locals())
    shards = {k: args[k] for k in _WEIGHT_KEYS}
    moms = {k: (args["m_" + k], args["v_" + k]) for k in _WEIGHT_KEYS}

    core = lax.axis_index("c")
    halves = lambda a: a.reshape((2, a.shape[0] // 2) + a.shape[1:])

    names = list(_SHARDED)
    sent = [lax.dynamic_index_in_dim(halves(shards[k]), core, 0, keepdims=False) for k in names]
    sent = [s.astype(BF16) if _SHARDED[k][1] else s for k, s in zip(names, sent)]
    gathered = _gather_two_level(sent, "gather_weights")
    my_chip = 2 * lax.axis_index("x") + lax.axis_index("y")
    own_block = lambda blocks, mine: lax.dynamic_update_index_in_dim(blocks, mine.astype(blocks.dtype), my_chip, 0)
    full, compute = dict(shards), {}
    for k, g in zip(names, gathered):
        shp, ax = shards[k].shape, _SHARDED[k][0]
        g = own_block(g, halves(shards[k]))
        gathered_k = jnp.concatenate([g[j].reshape(shp) for j in range(4)], axis=ax)
        full[k] = gathered_k.astype(F32)
        if k in _DENSE:
            compute[k] = gathered_k

    loss, (grads, grad_x) = jax.value_and_grad(_local_loss, argnums=(0, 1))(full, x[0], loss_target[0], compute)
    loss = lax.psum(loss, ("x", "y", "c"))

    blocks = []
    for k in names:
        per_chip = jnp.stack(jnp.split(grads[k].astype(BF16), 4, axis=_SHARDED[k][0]), axis=0)
        blocks.append(jnp.moveaxis(_as_rows(per_chip.reshape((4, 2, -1) + per_chip.shape[2:]), 2), 1, 0))
    theirs = _core_exchange(blocks, True, "swap_halves")
    mine = [lax.dynamic_index_in_dim(b, core, 0, keepdims=False) for b in blocks]
    chip_sums = [_add_to_bf16_call(a, b, "sum_cores_" + k) for k, a, b in zip(names, mine, theirs)]
    arrived = _chip_exchange(chip_sums, True, "scatter_grads")
    arrived = [own_block(a, lax.dynamic_index_in_dim(s, my_chip, 0, keepdims=False)) for a, s in zip(arrived, chip_sums)]
    owned = [_sum4_call(a, "sum_chips_" + k) for k, a in zip(names, arrived)]
    others = _core_exchange(owned, False, "swap_owned")
    shared = [jnp.concatenate([jnp.where(core == 0, a, b), jnp.where(core == 0, b, a)], axis=0)
              for a, b in zip(owned, others)]

    small_names = [k for k in _WEIGHT_KEYS if k not in _SHARDED]
    packed = _pack([grads[k] for k in small_names])
    (packed4,) = _chip_exchange([packed], False, "gather_small_grads")
    small_sum = _sum4_call(own_block(packed4, packed), "sum_chips_small")
    (small_other,) = _core_exchange([small_sum], False, "swap_small")

    outs = {}
    for k, g in zip(names, shared):
        shp = shards[k].shape
        rows = _as_rows(shards[k]).shape
        res = _adam_call(_as_rows(shards[k]), [g.reshape(rows)], _as_rows(moms[k][0]), _as_rows(moms[k][1]), "adamw_" + k)
        outs[k] = [r.reshape(shp) for r in res]
    like = [shards[k] for k in small_names]
    res = _adam_call(_pack(like), [small_sum, small_other], _pack([moms[k][0] for k in small_names]),
                     _pack([moms[k][1] for k in small_names]), "adamw_small")
    for idx in range(4):
        for k, a in zip(small_names, _unpack(res[idx], like)):
            outs.setdefault(k, [None] * 4)[idx] = a

    result = [loss, grad_x[None]]
    for idx in range(4):
        result += [outs[k][idx] for k in _WEIGHT_KEYS]
    return tuple(result)
```

```python
import functools

import jax
import jax.numpy as jnp
from jax import lax
from jax.experimental import pallas as pl
from jax.experimental.pallas import tpu as pltpu

F32 = jnp.float32
BF16 = jnp.bfloat16
MESH = pl.DeviceIdType.MESH
F32_DOT = lax.Precision.HIGH

D_MODEL = 1024
DEPTH = 4
N_META = 16
CHUNK = 64
PAD = CHUNK - N_META
CONV_K = 4
HALO = 8
WIDTH = 768
S5_GROUPS, S5_GROUP, S5_STATE = 48, 16, 64
S5_BLOCKS = 3
S5_SEGMENTS = 8
SSD_HEADS, SSD_HEAD_DIM, SSD_GROUPS, SSD_STATE = 12, 64, 2, 128
SSD_PAIRS = 6
GDN_HEADS, GDN_DIM = 6, 128
LANES = 128
SMALL_ROWS = 24
ALPHA = (2 * DEPTH) ** 0.25
LN_EPS = 1e-5
P_S5U, P_S5Z, P_XBC, P_DT, P_SSDZ, P_QKV, P_GA, P_GB, P_GDNZ, P_GATE, P_END = (
    0, 768, 1536, 2816, 2828, 3596, 5900, 5906, 5912, 6680, 9752)
ADAM_LR, ADAM_B1, ADAM_B2, ADAM_EPS, ADAM_WD, ADAM_STEP = 0.001, 0.9, 0.999, 1e-08, 0.01, 10
NEG = -1e30


def _pick(n, cands):
    for c in cands:
        if n % c == 0:
            return c
    raise ValueError(f"no tile for {n} in {cands}")


def _cparams(sem, vmem_mb):
    return pltpu.CompilerParams(dimension_semantics=sem, vmem_limit_bytes=vmem_mb << 20)


_DIMS = {"nn": (((1,), (0,)), ((), ())), "nt": (((1,), (1,)), ((), ())), "tn": (((0,), (0,)), ((), ()))}


def _dot(a, b, mode, hi):
    if hi:
        prec = lax.Precision.HIGHEST if hi == "exact" else F32_DOT
        return lax.dot_general(a, b, _DIMS[mode], precision=prec, preferred_element_type=F32)
    return lax.dot_general(a.astype(BF16), b.astype(BF16), _DIMS[mode], preferred_element_type=F32)


@functools.partial(jax.custom_vjp, nondiff_argnums=(2, 3))
def _mm(a, b, mode="nn", hi=False):
    return _dot(a, b, mode, hi)


def _mm_fwd(a, b, mode, hi):
    return _dot(a, b, mode, hi), (a, b)


def _mm_bwd(mode, hi, res, g):
    a, b = res
    if mode == "nn":
        return _dot(g, b, "nt", hi), _dot(a, g, "tn", hi)
    if mode == "nt":
        return _dot(g, b, "nn", hi), _dot(g, a, "tn", hi)
    return _dot(b, g, "nt", hi), _dot(a, g, "nn", hi)


_mm.defvjp(_mm_fwd, _mm_bwd)


@functools.partial(jax.custom_vjp, nondiff_argnums=(1,))
def _roll_rows(x, k):
    return pltpu.roll(x, k % x.shape[0], 0)


def _roll_fwd(x, k):
    return _roll_rows(x, k), None


def _roll_bwd(k, _, g):
    return (_roll_rows(g, -k),)


_roll_rows.defvjp(_roll_fwd, _roll_bwd)


def _iota(shape, dim):
    return lax.broadcasted_iota(jnp.int32, shape, dim)


def _valid_rows(row0, n):
    return (row0 + _iota((n, 1), 0)) >= PAD


def _lane(x, idx):
    return jnp.sum(jnp.where(_iota(x.shape, 1) == idx, x, 0.0), axis=1, keepdims=True)


def _row(x, idx):
    return jnp.sum(jnp.where(_iota(x.shape, 0) == idx, x, 0.0), axis=0, keepdims=True)


def _layer_norm(z, g, b):
    mu = jnp.mean(z, axis=-1, keepdims=True)
    zc = z - mu
    var = jnp.mean(zc * zc, axis=-1, keepdims=True)
    return zc * lax.rsqrt(var + LN_EPS) * g + b


def _rms_norm(z, g):
    return z * lax.rsqrt(jnp.mean(z * z, axis=-1, keepdims=True) + LN_EPS) * g


def _causal_conv(halo, x, w, row0):
    t = x.shape[0]
    xc = jnp.concatenate([halo, x], axis=0)
    acc = None
    for j in range(CONV_K):
        term = _roll_rows(xc, CONV_K - 1 - j)[HALO:HALO + t] * _row(w, j)
        acc = term if acc is None else acc + term
    return acc, x[t - HALO:t]


def _tri(n, strict=False):
    r, c = _iota((n, n), 0), _iota((n, n), 1)
    return (r > c) if strict else (r >= c)


def _scan_op(step, name, *, tile, nb, row_kinds, param_kinds, carry_shapes, out_widths, keep_shapes=(), vmem_mb=48):
    n_rows, n_par, n_car, n_out = len(row_kinds), len(param_kinds), len(carry_shapes), len(out_widths)
    n_keep = len(keep_shapes)
    saved_shapes = tuple(carry_shapes) + tuple(keep_shapes)

    def dims(rows):
        for k, a in zip(row_kinds, rows):
            if k in "bs":
                return a.shape[0], a.shape[0] // tile
        raise ValueError("need a row input")

    def row_spec(kind, a, rev, nt):
        ti = (lambda i: nt - 1 - i) if rev else (lambda i: i)
        if kind == "b":
            return pl.BlockSpec((tile, a.shape[1] // nb), lambda b, i: (ti(i), b))
        if kind == "s":
            return pl.BlockSpec((tile, a.shape[1]), lambda b, i: (ti(i), 0))
        return pl.BlockSpec((None, a.shape[1], a.shape[2]), lambda b, i: (ti(i), 0, 0))

    def par_spec(kind, a):
        if kind == "b":
            return pl.BlockSpec((None, a.shape[1], a.shape[2]), lambda b, i: (b, 0, 0))
        return pl.BlockSpec(a.shape, lambda b, i: (0, 0))

    def fwd_call(rows, params):
        length, nt = dims(rows)

        def body(*refs):
            r_in = refs[:n_rows]
            p_in = refs[n_rows:n_rows + n_par]
            o_out = refs[n_rows + n_par:n_rows + n_par + n_out]
            s_out = refs[n_rows + n_par + n_out:n_rows + n_par + n_out + n_car + n_keep]
            c_scr = refs[n_rows + n_par + n_out + n_car + n_keep:]
            b, i = pl.program_id(0), pl.program_id(1)

            if n_car:
                @pl.when(i == 0)
                def _():
                    for c in c_scr:
                        c[...] = jnp.zeros_like(c)

            cin = tuple(c[...] for c in c_scr)
            for s, c in zip(s_out, cin):
                s[...] = c
            res = step(cin, tuple(r[...] for r in r_in), tuple(p[...] for p in p_in), b, i * tile)
            new_c, outs = res[0], res[1]
            for c, v in zip(c_scr, new_c):
                c[...] = v
            for o, v in zip(o_out, outs):
                o[...] = v
            if n_keep:
                for s, v in zip(s_out[n_car:], res[2]):
                    s[...] = v

        out_shape = [jax.ShapeDtypeStruct((length, nb * w), F32) for w in out_widths]
        out_shape += [jax.ShapeDtypeStruct((nb, nt) + tuple(s), F32) for s in saved_shapes]
        out_specs = [pl.BlockSpec((tile, w), lambda b, i: (i, b)) for w in out_widths]
        out_specs += [pl.BlockSpec((None, None) + tuple(s), lambda b, i: (b, i, 0, 0)) for s in saved_shapes]
        res = pl.pallas_call(
            body, name=name + "_fwd", grid=(nb, nt),
            in_specs=[row_spec(k, a, False, nt) for k, a in zip(row_kinds, rows)]
            + [par_spec(k, a) for k, a in zip(param_kinds, params)],
            out_specs=out_specs, out_shape=out_shape,
            scratch_shapes=[pltpu.VMEM(tuple(s), F32) for s in carry_shapes],
            compiler_params=_cparams(("arbitrary", "arbitrary"), vmem_mb),
        )(*rows, *params)
        return tuple(res[:n_out]), tuple(res[n_out:])

    def bwd_call(rows, params, saved, douts):
        length, nt = dims(rows)

        def body(*refs):
            k0 = 0
            r_in = refs[k0:k0 + n_rows]; k0 += n_rows
            p_in = refs[k0:k0 + n_par]; k0 += n_par
            s_in = refs[k0:k0 + n_car]; k0 += n_car
            k_in = refs[k0:k0 + n_keep]; k0 += n_keep
            g_in = refs[k0:k0 + n_out]; k0 += n_out
            dr_out = refs[k0:k0 + n_rows]; k0 += n_rows
            dp_out = refs[k0:k0 + n_par]; k0 += n_par
            dc_scr = refs[k0:]
            b, i = pl.program_id(0), pl.program_id(1)
            row0 = (nt - 1 - i) * tile

            @pl.when(i == 0)
            def _():
                for c in dc_scr:
                    c[...] = jnp.zeros_like(c)
                for p in dp_out:
                    p[...] = jnp.zeros_like(p)

            def f(c, r, p):
                if n_keep:
                    return step(c, r, p, b, row0, kept=tuple(k[...] for k in k_in))[:2]
                return step(c, r, p, b, row0)

            _, vjp = jax.vjp(f, tuple(s[...] for s in s_in), tuple(r[...] for r in r_in),
                             tuple(p[...] for p in p_in))
            dc, dr, dp = vjp((tuple(c[...] for c in dc_scr), tuple(g[...] for g in g_in)))
            for c, v in zip(dc_scr, dc):
                c[...] = v
            for o, v in zip(dr_out, dr):
                o[...] = v
            for o, v in zip(dp_out, dp):
                o[...] += v

        rev = lambda i: nt - 1 - i
        in_specs = [row_spec(k, a, True, nt) for k, a in zip(row_kinds, rows)]
        in_specs += [par_spec(k, a) for k, a in zip(param_kinds, params)]
        in_specs += [pl.BlockSpec((None, None) + tuple(s), lambda b, i: (b, rev(i), 0, 0)) for s in saved_shapes]
        in_specs += [pl.BlockSpec((tile, w), lambda b, i: (rev(i), b)) for w in out_widths]
        out_shape, out_specs = [], []
        for k, a in zip(row_kinds, rows):
            if k == "b":
                out_shape.append(jax.ShapeDtypeStruct(a.shape, F32))
                out_specs.append(pl.BlockSpec((tile, a.shape[1] // nb), lambda b, i: (rev(i), b)))
            elif k == "s":
                out_shape.append(jax.ShapeDtypeStruct((nb,) + a.shape, F32))
                out_specs.append(pl.BlockSpec((None, tile, a.shape[1]), lambda b, i: (b, rev(i), 0)))
            else:
                out_shape.append(jax.ShapeDtypeStruct((nb,) + a.shape, F32))
                out_specs.append(pl.BlockSpec((None, None, a.shape[1], a.shape[2]), lambda b, i: (b, rev(i), 0, 0)))
        for k, a in zip(param_kinds, params):
            shp = a.shape[1:] if k == "b" else a.shape
            out_shape.append(jax.ShapeDtypeStruct((nb,) + tuple(shp), F32))
            out_specs.append(pl.BlockSpec((None,) + tuple(shp), lambda b, i: (b, 0, 0)))
        res = pl.pallas_call(
            body, name=name + "_bwd", grid=(nb, nt), in_specs=in_specs, out_specs=out_specs, out_shape=out_shape,
            scratch_shapes=[pltpu.VMEM(tuple(s), F32) for s in carry_shapes],
            compiler_params=_cparams(("arbitrary", "arbitrary"), vmem_mb),
        )(*rows, *params, *saved, *douts)
        fold = (lambda a: a[0]) if nb == 1 else (lambda a: jnp.sum(a, axis=0))
        drows = tuple(r if k == "b" else fold(r) for k, r in zip(row_kinds, res[:n_rows]))
        dpars = tuple(p if k == "b" else fold(p) for k, p in zip(param_kinds, res[n_rows:]))
        return drows, dpars

    @jax.custom_vjp
    def op(rows, params):
        return fwd_call(rows, params)[0]

    def op_fwd(rows, params):
        outs, saved = fwd_call(rows, params)
        return outs, (rows, params, saved)

    def op_bwd(res, douts):
        rows, params, saved = res
        return bwd_call(rows, params, saved, tuple(douts))

    op.defvjp(op_fwd, op_bwd)
    return op


_WIDE = 1280


def _mm_rows(m, n):
    return _pick(m, (832, 128)) if n <= _WIDE else _pick(m, (416, 128))


def _mm_fwd_call(x, w, name):
    m, k = x.shape
    n = w.shape[1]
    tm = _mm_rows(m, n)

    def body(x_ref, w_ref, o_ref):
        o_ref[...] = _dot(x_ref[...], w_ref[...], "nn", False)

    return pl.pallas_call(
        body, name=name, grid=(m // tm,),
        in_specs=[pl.BlockSpec((tm, k), lambda i: (i, 0)), pl.BlockSpec((k, n), lambda i: (0, 0))],
        out_specs=pl.BlockSpec((tm, n), lambda i: (i, 0)),
        out_shape=jax.ShapeDtypeStruct((m, n), F32),
        compiler_params=_cparams(("parallel",), 48),
    )(x, w)


def _mm_bwd_call(g, x, w, name):
    m, n = g.shape
    k = w.shape[0]
    tm = _mm_rows(m, n)

    def body(g_ref, x_ref, w_ref, dx_ref, dw_ref):
        @pl.when(pl.program_id(0) == 0)
        def _():
            dw_ref[...] = jnp.zeros_like(dw_ref)

        g = g_ref[...].astype(BF16)
        dx_ref[...] = _dot(g, w_ref[...], "nt", False)
        x = x_ref[...]
        step = _pick(n, (768, 640, 128))
        for c0 in range(0, n, step):
            dw_ref[:, c0:c0 + step] += _dot(x, g[:, c0:c0 + step], "tn", False)

    return pl.pallas_call(
        body, name=name, grid=(m // tm,),
        in_specs=[pl.BlockSpec((tm, n), lambda i: (i, 0)), pl.BlockSpec((tm, k), lambda i: (i, 0)),
                  pl.BlockSpec((k, n), lambda i: (0, 0))],
        out_specs=[pl.BlockSpec((tm, k), lambda i: (i, 0)), pl.BlockSpec((k, n), lambda i: (0, 0))],
        out_shape=[jax.ShapeDtypeStruct((m, k), F32), jax.ShapeDtypeStruct((k, n), F32)],
        compiler_params=_cparams(("arbitrary",), 56),
    )(g, x, w)


def _dense(name):
    @jax.custom_vjp
    def op(x, w, xb, wb):
        return _mm_fwd_call(xb, wb, name + "_fwd")

    def op_fwd(x, w, xb, wb):
        return _mm_fwd_call(xb, wb, name + "_fwd"), (xb, wb)

    def op_bwd(res, g):
        xb, wb = res
        dx, dw = _mm_bwd_call(g, xb, wb, name + "_bwd")
        return dx, dw, jnp.zeros_like(xb), jnp.zeros_like(wb)

    op.defvjp(op_fwd, op_bwd)
    return op


def _bf16_copy(v):
    return lax.stop_gradient(v).astype(BF16)


def _merge_fwd_call(ys, gates, w, bias):
    m, k = ys[0].shape
    d = gates[0].shape[1]
    tm = _pick(m, (416, 128))

    def body(ya, yb, yc, ga, gb, gc, w_ref, b_ref, o_ref):
        acc = None
        for n, (y_ref, g_ref) in enumerate(((ya, ga), (yb, gb), (yc, gc))):
            term = jax.nn.sigmoid(g_ref[...] + b_ref[:, n * d:(n + 1) * d]) * _dot(y_ref[...], w_ref[n], "nn", False)
            acc = term if acc is None else acc + term
        o_ref[...] = acc

    rows = lambda width: pl.BlockSpec((tm, width), lambda i: (i, 0))
    return pl.pallas_call(
        body, name="gated_merge_fwd", grid=(m // tm,),
        in_specs=[rows(k)] * 3 + [rows(d)] * 3 + [pl.BlockSpec(w.shape, lambda i: (0, 0, 0)),
                                                 pl.BlockSpec(bias.shape, lambda i: (0, 0))],
        out_specs=rows(d), out_shape=jax.ShapeDtypeStruct((m, d), F32),
        compiler_params=_cparams(("parallel",), 48),
    )(*ys, *gates, w, bias)


def _merge_bwd_call(y, gate, dm, w, bias, name):
    m, k = y.shape
    d = gate.shape[1]
    tm = _pick(m, (416, 128))

    def body(y_ref, g_ref, dm_ref, w_ref, b_ref, dy_ref, dg_ref, dw_ref, db_ref):
        @pl.when(pl.program_id(0) == 0)
        def _():
            dw_ref[...] = jnp.zeros_like(dw_ref)
            db_ref[...] = jnp.zeros_like(db_ref)

        yv, dmv = y_ref[...], dm_ref[...]
        s = jax.nn.sigmoid(g_ref[...] + b_ref[...])
        d_gate = dmv * _dot(yv, w_ref[...], "nn", False) * s * (1.0 - s)
        dg_ref[...] = d_gate
        db_ref[...] += jnp.sum(d_gate, axis=0, keepdims=True)
        d_out = (dmv * s).astype(BF16)
        dy_ref[...] = _dot(d_out, w_ref[...], "nt", False)
        dw_ref[...] += _dot(yv, d_out, "tn", False)

    rows = lambda width: pl.BlockSpec((tm, width), lambda i: (i, 0))
    whole = lambda shape: pl.BlockSpec(shape, lambda i: (0, 0))
    return pl.pallas_call(
        body, name=name, grid=(m // tm,),
        in_specs=[rows(k), rows(d), rows(d), whole((k, d)), whole((1, d))],
        out_specs=[rows(k), rows(d), whole((k, d)), whole((1, d))],
        out_shape=[jax.ShapeDtypeStruct((m, k), F32), jax.ShapeDtypeStruct((m, d), F32),
                   jax.ShapeDtypeStruct((k, d), F32), jax.ShapeDtypeStruct((1, d), F32)],
        compiler_params=_cparams(("arbitrary",), 48),
    )(y, gate, dm, w, bias)


@jax.custom_vjp
def _gated_merge(ys, gates, w, bias, ysb, wb):
    return _merge_fwd_call(ysb, gates, wb, bias)


def _gated_merge_fwd(ys, gates, w, bias, ysb, wb):
    return _merge_fwd_call(ysb, gates, wb, bias), (gates, bias, ysb, wb)


def _gated_merge_bwd(res, dm):
    gates, bias, ysb, wb = res
    d = gates[0].shape[1]
    parts = [_merge_bwd_call(ysb[n], gates[n], dm, wb[n], bias[:, n * d:(n + 1) * d], "gated_merge_bwd_" + "abc"[n])
             for n in range(3)]
    dys, dgs, dws, dbs = zip(*parts)
    return (tuple(dys), tuple(dgs), jnp.stack(dws), jnp.concatenate(dbs, axis=1),
            tuple(jnp.zeros_like(v) for v in ysb), jnp.zeros_like(wb))


_gated_merge.defvjp(_gated_merge_fwd, _gated_merge_bwd)


def _ln_in_step(c, rows, params, b, row0):
    (z,), (g, bb) = rows, params
    return (), (jnp.where(_valid_rows(row0, z.shape[0]), _layer_norm(z, g, bb), 0.0),)


def _ln_res_step(c, rows, params, b, row0):
    (h, merged), (w_out, g, bb) = rows, params
    return (), (jnp.where(_valid_rows(row0, h.shape[0]), _layer_norm(ALPHA * h + _mm(merged, w_out), g, bb), 0.0),)


def _s5_prep_step(c, rows, params, b, row0):
    a_re, a_im, log_step, b_re, b_im = rows
    lam_re = jnp.minimum(a_re, -1e-4)
    lam_im = a_im
    step = jnp.exp(log_step)
    mag = jnp.exp(lam_re * step)
    abar_re, abar_im = mag * jnp.cos(lam_im * step), mag * jnp.sin(lam_im * step)
    den = lam_re * lam_re + lam_im * lam_im
    nr, ni = abar_re - 1.0, abar_im
    coef_re = (nr * lam_re + ni * lam_im) / den
    coef_im = (ni * lam_re - nr * lam_im) / den
    return (), (abar_re, abar_im, coef_re * b_re - coef_im * b_im, coef_re * b_im + coef_im * b_re)


def _s5_scan_step(c, rows, params, b, row0):
    (c_re, c_im), (u,) = c, rows
    bd_re, bd_im, a_re, a_im, cd_re, cd_im = params
    t = u.shape[0]
    steps = t // S5_SEGMENTS
    bu_re, bu_im = _mm(u, bd_re), _mm(u, bd_im)
    a_re, a_im = (jnp.broadcast_to(v, (S5_SEGMENTS, v.shape[1])) for v in (a_re, a_im))
    at = lambda v, i: v[S5_SEGMENTS * i:S5_SEGMENTS * (i + 1)]

    def advance(s_re, s_im, first):
        states = []
        for i in range(first, steps):
            s_re, s_im = a_re * s_re - a_im * s_im + at(bu_re, i), a_re * s_im + a_im * s_re + at(bu_im, i)
            states.append((s_re, s_im))
        return s_re, s_im, states

    s_re, s_im, _ = advance(at(bu_re, 0), at(bu_im, 0), 1)
    q_re = q_im = None
    p_re, p_im, left = a_re, a_im, steps
    while left:
        if left & 1:
            q_re, q_im = (p_re, p_im) if q_re is None else (q_re * p_re - q_im * p_im, q_re * p_im + q_im * p_re)
        p_re, p_im = p_re * p_re - p_im * p_im, 2.0 * p_re * p_im
        left >>= 1
    seg = _iota((S5_SEGMENTS, 1), 0)
    in_re = jnp.where(seg == 0, c_re, _roll_rows(s_re, 1))
    in_im = jnp.where(seg == 0, c_im, _roll_rows(s_im, 1))
    d = 1
    while d < S5_SEGMENTS:
        keep = seg >= d
        sh_re = jnp.where(keep, _roll_rows(in_re, d), 0.0)
        sh_im = jnp.where(keep, _roll_rows(in_im, d), 0.0)
        in_re, in_im = in_re + q_re * sh_re - q_im * sh_im, in_im + q_re * sh_im + q_im * sh_re
        q_re, q_im = q_re * q_re - q_im * q_im, 2.0 * q_re * q_im
        d *= 2
    _, _, full = advance(in_re, in_im, 0)
    y = _mm(jnp.concatenate([f[0] for f in full], axis=0), cd_re) - _mm(jnp.concatenate([f[1] for f in full], axis=0), cd_im)
    return (_row(full[-1][0], S5_SEGMENTS - 1), _row(full[-1][1], S5_SEGMENTS - 1)), (y,)


def _interleave(v, tile, inverse=False):
    length, width = v.shape
    shape = (length // tile, tile // S5_SEGMENTS, S5_SEGMENTS) if inverse else (length // tile, S5_SEGMENTS, tile // S5_SEGMENTS)
    return v.reshape(shape + (width,)).transpose(0, 2, 1, 3).reshape(length, width)


def _s5_post_step(c, rows, params, b, row0):
    (y, u, z), (d, w_glu, b_glu) = rows, params
    v = jax.nn.gelu(y + d * u)
    v = v * jax.nn.sigmoid(_mm(v, w_glu) + b_glu)
    return (), (v * jax.nn.silu(z),)


def _ssd_step(c, rows, params, b, row0):
    halo, state = c
    xbc_raw, z, small, small_t = rows
    cw, cb, d_l, bias_l, alog_l, bias_c, alog_c, norm_g = params
    t = xbc_raw.shape[0]
    grp = SSD_GROUPS * SSD_STATE
    valid = _valid_rows(row0, t)
    conv, halo2 = _causal_conv(halo, xbc_raw, cw, row0)
    act = jnp.where(valid, jax.nn.silu(conv + cb), 0.0)
    low = _iota((1, LANES), 1) < SSD_HEAD_DIM
    dt_all = jnp.where(valid, jax.nn.softplus(small + bias_l), 0.0)
    a_all = -jnp.exp(alog_l)
    valid_t = (row0 + _iota((1, t), 1)) >= PAD
    dta_t = jnp.where(valid_t, jax.nn.softplus(small_t + bias_c), 0.0) * (-jnp.exp(alog_c))
    acum_t = _mm(dta_t, jnp.where(_iota((t, t), 0) <= _iota((t, t), 1), 1.0, 0.0), "nn", True)
    causal = _tri(t)
    acum_all = _mm(jnp.where(causal, 1.0, 0.0), dt_all * a_all, "nn", True)
    last_all = _row(acum_all, t - 1)
    low_rows = _iota((LANES, 1), 0) < SSD_HEAD_DIM
    pairs, groups = range(SSD_PAIRS), range(SSD_GROUPS)
    grp_of = [p // (SSD_PAIRS // SSD_GROUPS) for p in pairs]
    bs = [act[:, WIDTH + g * SSD_STATE:WIDTH + (g + 1) * SSD_STATE] for g in groups]
    cs = [act[:, WIDTH + grp + g * SSD_STATE:WIDTH + grp + (g + 1) * SSD_STATE] for g in groups]
    scores = [_mm(cs[g], bs[g], "nt") for g in groups]
    per_lane = lambda v, p: jnp.where(low, v[:, 2 * p:2 * p + 1], v[:, 2 * p + 1:2 * p + 2])
    dt_l = [per_lane(dt_all, p) for p in pairs]
    acum_l = [per_lane(acum_all, p) for p in pairs]
    last_l = [per_lane(last_all, p) for p in pairs]
    st = [state[p * LANES:(p + 1) * LANES] for p in pairs]
    xd = [act[:, p * LANES:(p + 1) * LANES] * dt_l[p] for p in pairs]
    decay = [jnp.exp(jnp.where(causal, acum_all[:, h:h + 1] - _row(acum_t, h), NEG)) for h in range(SSD_HEADS)]
    y_lo = [_mm(scores[grp_of[p]] * decay[2 * p], jnp.where(low, xd[p], 0.0)) for p in pairs]
    y_hi = [_mm(scores[grp_of[p]] * decay[2 * p + 1], jnp.where(low, 0.0, xd[p])) for p in pairs]
    y_off = [_mm(cs[grp_of[p]], st[p], "nt") * jnp.exp(acum_l[p]) for p in pairs]
    new_st = [_mm(xd[p] * jnp.exp(last_l[p] - acum_l[p]), bs[grp_of[p]], "tn") for p in pairs]
    cd = jnp.exp(last_all)
    new_st = [st[p] * jnp.where(low_rows, cd[:, 2 * p:2 * p + 1], cd[:, 2 * p + 1:2 * p + 2]) + new_st[p] for p in pairs]
    y = jnp.concatenate([y_lo[p] + y_hi[p] + y_off[p] for p in pairs], axis=1) + act[:, :WIDTH] * d_l
    out = _rms_norm(y * jax.nn.silu(z), norm_g)
    return (halo2, jnp.concatenate(new_st, axis=0)), (out,)


@jax.custom_vjp
def _unit_lower_inverse(mats):
    return _neumann_inverse(mats)


def _inverse_fwd(mats):
    inv = _neumann_inverse(mats)
    return inv, inv


def _inverse_bwd(inv, g):
    left = [_dot(t, d, "tn", True) for t, d in zip(inv, g)]
    return ([-_dot(l, t, "nt", True) for l, t in zip(left, inv)],)


_unit_lower_inverse.defvjp(_inverse_fwd, _inverse_bwd)


@jax.custom_vjp
def _kept_inverse(mats, inv):
    return list(inv)


def _kept_inverse_fwd(mats, inv):
    return list(inv), list(inv)


def _kept_inverse_bwd(inv, g):
    return _inverse_bwd(inv, g)[0], [jnp.zeros_like(t) for t in inv]


_kept_inverse.defvjp(_kept_inverse_fwd, _kept_inverse_bwd)


def _neumann_inverse(mats):
    n = mats[0].shape[0]
    eye = jnp.where(_iota((n, n), 0) == _iota((n, n), 1), 1.0, 0.0)
    inv = [eye - a for a in mats]
    p = [_mm(a, a, "nn", True) for a in mats]
    k = 2
    while k < n:
        inv = [i + _mm(i, q, "nn", True) for i, q in zip(inv, p)]
        k *= 2
        if k < n:
            p = [_mm(q, q, "nn", True) for q in p]
    return inv


def _gdn_step(c, rows, params, b, row0, kept=None):
    halo, state = c
    qkv_raw, z, small, small_t = rows
    cw, bias_l, alog_l, bias_c, alog_c, norm_g = params
    t = qkv_raw.shape[0]
    valid = _valid_rows(row0, t)
    conv, halo2 = _causal_conv(halo, qkv_raw, cw, row0)
    act = jnp.where(valid, jax.nn.silu(conv), 0.0)
    beta_all = jnp.where(valid, jax.nn.sigmoid(small), 0.0)
    g_all = jnp.where(valid, -jnp.exp(alog_l) * jax.nn.softplus(small + bias_l), 0.0)
    valid_t = (row0 + _iota((1, t), 1)) >= PAD
    g_t = jnp.where(valid_t, -jnp.exp(alog_c) * jax.nn.softplus(small_t + bias_c), 0.0)
    causal, strict = _tri(t), _tri(t, True)
    gcum_all = _mm(jnp.where(causal, 1.0, 0.0), g_all, "nn", True)
    gcum_t = _mm(g_t, jnp.where(_iota((t, t), 0) <= _iota((t, t), 1), 1.0, 0.0), "nn", True)
    heads = range(GDN_HEADS)
    part = lambda h, n: act[:, n * WIDTH + h * GDN_DIM:n * WIDTH + (h + 1) * GDN_DIM]
    unit = lambda x: x * lax.rsqrt(jnp.sum(x * x, axis=-1, keepdims=True) + 1e-6)
    q = [unit(part(h, 0)) * (GDN_DIM ** -0.5) for h in heads]
    k = [unit(part(h, 1)) for h in heads]
    st = [state[h * GDN_DIM:(h + 1) * GDN_DIM] for h in heads]
    ia = [SSD_HEADS + h for h in heads]
    beta = [beta_all[:, ia[h] + GDN_HEADS:ia[h] + GDN_HEADS + 1] for h in heads]
    gcum = [gcum_all[:, ia[h]:ia[h] + 1] for h in heads]
    gamma = [jnp.exp(jnp.where(causal, gcum[h] - _row(gcum_t, ia[h]), NEG)) for h in heads]
    egc = [jnp.exp(gcum[h]) for h in heads]
    a_mat = [jnp.where(strict, _mm(k[h], k[h], "nt") * gamma[h] * beta[h], 0.0) for h in heads]
    if kept is None:
        inv = _unit_lower_inverse(a_mat)
    else:
        inv = _kept_inverse(a_mat, [kept[0][h * t:(h + 1) * t] for h in heads])
    rhs = [jnp.concatenate([part(h, 2) * beta[h], k[h] * (beta[h] * egc[h])], axis=1) for h in heads]
    sol = [_mm(inv[h], rhs[h], "nn", True) for h in heads]
    attn = [_mm(q[h], k[h], "nt") * gamma[h] for h in heads]
    from_state = [_mm(jnp.concatenate([sol[h][:, GDN_DIM:], q[h] * egc[h]], axis=0), st[h]) for h in heads]
    v_new = [sol[h][:, :GDN_DIM] - from_state[h][:t] for h in heads]
    o = [from_state[h][t:] + _mm(attn[h], v_new[h]) for h in heads]
    glast = [_row(gcum[h], t - 1) for h in heads]
    new_st = [st[h] * jnp.exp(glast[h]) + _mm(k[h] * jnp.exp(glast[h] - gcum[h]), v_new[h], "tn") for h in heads]
    out = jnp.concatenate([_rms_norm(o[h], norm_g) for h in heads], axis=1) * jax.nn.silu(z)
    return (halo2, jnp.concatenate(new_st, axis=0)), (out,), (jnp.concatenate(inv, axis=0),)


def _loss_tile(n):
    return _pick(n, (512, 256, 128, 64))


def _loss_fwd_call(y, tgt):
    n, d = y.shape
    tile = _loss_tile(n)

    def body(y_ref, t_ref, o_ref):
        @pl.when(pl.program_id(0) == 0)
        def _():
            o_ref[...] = jnp.zeros_like(o_ref)

        e = y_ref[...] - t_ref[...]
        o_ref[...] += jnp.sum(jnp.sum(e * e, axis=1, keepdims=True), axis=0, keepdims=True) * (0.5 / d)

    out = pl.pallas_call(
        body, name="loss_fwd", grid=(n // tile,),
        in_specs=[pl.BlockSpec((tile, d), lambda i: (i, 0)), pl.BlockSpec((tile, d), lambda i: (i, 0))],
        out_specs=pl.BlockSpec((8, LANES), lambda i: (0, 0)),
        out_shape=jax.ShapeDtypeStruct((8, LANES), F32),
        compiler_params=_cparams(("arbitrary",), 32),
    )(y, tgt)
    return out[0, 0]


def _loss_bwd_call(y, tgt, g):
    n, d = y.shape
    tile = _loss_tile(n)

    def body(y_ref, t_ref, g_ref, o_ref):
        o_ref[...] = (y_ref[...] - t_ref[...]) * (g_ref[...][0:1, 0:1] * (1.0 / d))

    return pl.pallas_call(
        body, name="loss_bwd", grid=(n // tile,),
        in_specs=[pl.BlockSpec((tile, d), lambda i: (i, 0)), pl.BlockSpec((tile, d), lambda i: (i, 0)),
                  pl.BlockSpec((8, LANES), lambda i: (0, 0))],
        out_specs=pl.BlockSpec((tile, d), lambda i: (i, 0)),
        out_shape=jax.ShapeDtypeStruct((n, d), F32),
        compiler_params=_cparams(("parallel",), 32),
    )(y, tgt, jnp.broadcast_to(g, (8, LANES)).astype(F32))


@jax.custom_vjp
def _loss_op(y, tgt):
    return _loss_fwd_call(y, tgt)


def _loss_op_fwd(y, tgt):
    return _loss_fwd_call(y, tgt), (y, tgt)


def _loss_op_bwd(res, g):
    y, tgt = res
    return _loss_bwd_call(y, tgt, g), jnp.zeros_like(tgt)


_loss_op.defvjp(_loss_op_fwd, _loss_op_bwd)


def _rowwise(step, name, tile, n_rows, n_params, out_widths, vmem_mb=48):
    return _scan_op(step, name, tile=tile, nb=1, row_kinds="s" * n_rows, param_kinds="s" * n_params,
                    carry_shapes=(), out_widths=out_widths, vmem_mb=vmem_mb)


def _block_diag(x, nblk):
    bsz, _, r, c = x.shape
    eye = jnp.eye(nblk, dtype=x.dtype)
    return jnp.einsum("bgrc,gh->bgrhc", x, eye).reshape(bsz, nblk * r, nblk * c)


_PROJ = {"s5u": (P_S5U, P_S5Z), "s5z": (P_S5Z, P_XBC), "xbc": (P_XBC, P_DT), "ssdz": (P_SSDZ, P_QKV),
         "qkv": (P_QKV, P_GA), "gdnz": (P_GDNZ, P_GATE), "gate_a": (P_GATE, P_GATE + D_MODEL),
         "gate_b": (P_GATE + D_MODEL, P_GATE + 2 * D_MODEL), "gate_c": (P_GATE + 2 * D_MODEL, P_END)}


def _prepare(weights, compute):
    w = weights
    zeros = lambda *shape: jnp.zeros((DEPTH,) + shape, F32)
    row3 = lambda v: v.reshape(DEPTH, 1, -1)
    p = {}
    for pre, w_in in (("w_", w["w_in"]), ("wb_", compute["w_in"])):
        seg = lambda a, bnd: w_in[:, :, a:bnd]
        p.update({pre + k: seg(a, bnd) for k, (a, bnd) in _PROJ.items()})
        p[pre + "small"] = jnp.concatenate(
            [seg(P_DT, P_SSDZ), seg(P_GA, P_GDNZ), zeros(D_MODEL, LANES - SMALL_ROWS).astype(w_in.dtype)], axis=2)
    p["wb_branch"] = compute["w_branch"]

    rows, wide = DEPTH * S5_GROUPS, S5_STATE * S5_GROUP
    flat = lambda v: v.reshape(rows, -1)
    rep = lambda v: jnp.repeat(flat(v), S5_GROUP, axis=1)
    prep = _rowwise(_s5_prep_step, "s5_prep", S5_GROUPS, 5, 0, (wide,) * 4)
    abar_re, abar_im, bbar_re, bbar_im = prep(
        (rep(w["s5_a_re"]), rep(w["s5_a_im"]), jnp.broadcast_to(flat(w["s5_log_step"]), (rows, wide)),
         flat(w["s5_b_re"]), flat(w["s5_b_im"])), ())
    gpb = S5_GROUPS // S5_BLOCKS
    lanes, chans, nblk = gpb * S5_STATE, gpb * S5_GROUP, DEPTH * S5_BLOCKS
    to_bd = lambda bb: _block_diag(bb.reshape(nblk, gpb, S5_STATE, S5_GROUP).transpose(0, 1, 3, 2), gpb).reshape(
        DEPTH, S5_BLOCKS, chans, lanes)
    to_cd = lambda cc: _block_diag(cc.reshape(nblk, gpb, S5_GROUP, S5_STATE).transpose(0, 1, 3, 2), gpb).reshape(
        DEPTH, S5_BLOCKS, lanes, chans)
    to_a = lambda a: a[:, ::S5_GROUP].reshape(DEPTH, S5_BLOCKS, 1, lanes)
    p.update(s5_bd_re=to_bd(bbar_re), s5_bd_im=to_bd(bbar_im), s5_a_re=to_a(abar_re), s5_a_im=to_a(abar_im),
             s5_cd_re=to_cd(w["s5_c_re"]), s5_cd_im=to_cd(w["s5_c_im"]),
             s5_d=row3(w["s5_d"]), s5_w_glu=w["s5_w_glu"], s5_b_glu=row3(w["s5_b_glu"]))

    bias = jnp.concatenate([w["ssd_dt_bias"], w["gdn_dt_bias"]], axis=1)
    alog = jnp.concatenate([w["ssd_a_log"], w["gdn_a_log"]], axis=1)
    on_lanes = lambda v: jnp.concatenate([v, zeros(LANES - v.shape[1])], axis=1).reshape(DEPTH, 1, LANES)
    on_rows = lambda v: jnp.concatenate([v, zeros(SMALL_ROWS - v.shape[1])], axis=1).reshape(DEPTH, SMALL_ROWS, 1)
    pad_w = lambda cw: jnp.concatenate([cw, zeros(HALO - CONV_K, cw.shape[2])], axis=1)
    p.update(bias_l=on_lanes(bias), alog_l=on_lanes(alog), bias_c=on_rows(bias), alog_c=on_rows(alog),
             ssd_cw=pad_w(w["ssd_conv_w"]), ssd_cb=row3(w["ssd_conv_b"]),
             ssd_d=row3(jnp.repeat(w["ssd_d"], SSD_HEAD_DIM, axis=1)), ssd_norm_g=row3(w["ssd_norm_g"]),
             gdn_cw=pad_w(w["gdn_conv_w"]), gdn_norm_g=row3(w["gdn_norm_g"]),
             w_branch=w["w_branch"], b_gate=row3(w["b_gate"]), w_out=w["w_out"], ln_g=row3(w["ln_g"]), ln_b=row3(w["ln_b"]))
    return p


def _layer(h, p):
    length = h.shape[0]
    nt = length // CHUNK
    t_row = _pick(length, (208, 128))
    t_s5 = _pick(length, (832, 128))
    hb = _bf16_copy(h)
    proj = {k: _dense("proj_" + k)(h, p["w_" + k], hb, p["wb_" + k]) for k in list(_PROJ) + ["small"]}
    small = proj["small"]
    small_t = small[:, :SMALL_ROWS].reshape(nt, CHUNK, SMALL_ROWS).transpose(0, 2, 1)

    lanes = p["s5_a_re"].shape[-1]
    s5_scan = _scan_op(_s5_scan_step, "s5_scan", tile=t_s5, nb=S5_BLOCKS, row_kinds="b", param_kinds="bbbbbb",
                       carry_shapes=((1, lanes), (1, lanes)), out_widths=(WIDTH // S5_BLOCKS,))
    (y_ssm,) = s5_scan((_interleave(proj["s5u"], t_s5),), (p["s5_bd_re"], p["s5_bd_im"], p["s5_a_re"], p["s5_a_im"],
                                                           p["s5_cd_re"], p["s5_cd_im"]))
    y_ssm = _interleave(y_ssm, t_s5, inverse=True)
    s5_post = _rowwise(_s5_post_step, "s5_post", t_row, 3, 3, (WIDTH,))
    (y_a,) = s5_post((y_ssm, proj["s5u"], proj["s5z"]), (p["s5_d"], p["s5_w_glu"], p["s5_b_glu"]))

    scalars = (p["bias_l"], p["alog_l"], p["bias_c"], p["alog_c"])
    ssd = _scan_op(_ssd_step, "ssd_scan", tile=CHUNK, nb=1, row_kinds="ssst", param_kinds="s" * 8,
                   carry_shapes=((HALO, P_DT - P_XBC), (SSD_HEADS * SSD_HEAD_DIM, SSD_STATE)), out_widths=(WIDTH,))
    (y_b,) = ssd((proj["xbc"], proj["ssdz"], small, small_t),
                 (p["ssd_cw"], p["ssd_cb"], p["ssd_d"]) + scalars + (p["ssd_norm_g"],))
    gdn = _scan_op(_gdn_step, "gdn_scan", tile=CHUNK, nb=1, row_kinds="ssst", param_kinds="s" * 6,
                   carry_shapes=((HALO, P_GA - P_QKV), (GDN_HEADS * GDN_DIM, GDN_DIM)), out_widths=(WIDTH,),
                   keep_shapes=((GDN_HEADS * CHUNK, CHUNK),))
    (y_c,) = gdn((proj["qkv"], proj["gdnz"], small, small_t), (p["gdn_cw"],) + scalars + (p["gdn_norm_g"],))

    ys = (y_a, y_b, y_c)
    merged = _gated_merge(ys, (proj["gate_a"], proj["gate_b"], proj["gate_c"]), p["w_branch"], p["b_gate"],
                          tuple(_bf16_copy(y) for y in ys), p["wb_branch"])
    ln = _rowwise(_ln_res_step, "out_proj_ln", t_row, 2, 3, (D_MODEL,))
    (h_new,) = ln((h, merged), (p["w_out"], p["ln_g"], p["ln_b"]))
    return h_new


_LAYER_KEYS = ("w_in", "s5_a_re", "s5_a_im", "s5_log_step", "s5_b_re", "s5_b_im", "s5_c_re", "s5_c_im", "s5_d",
               "s5_w_glu", "s5_b_glu", "ssd_conv_w", "ssd_conv_b", "ssd_dt_bias", "ssd_a_log", "ssd_d", "ssd_norm_g",
               "gdn_conv_w", "gdn_dt_bias", "gdn_a_log", "gdn_norm_g", "w_branch", "b_gate", "w_out", "ln_g", "ln_b")
_WEIGHT_KEYS = ("meta", "ln_in_g", "ln_in_b") + _LAYER_KEYS


def _local_loss(weights, x, target, compute):
    seq = x.shape[0]
    hcat = jnp.concatenate([jnp.zeros((PAD, D_MODEL), F32), weights["meta"], x], axis=0)
    length = hcat.shape[0]
    ln_in = _rowwise(_ln_in_step, "ln_in", _pick(length, (416, 256, 128)), 1, 2, (D_MODEL,))
    (h,) = ln_in((hcat,), (weights["ln_in_g"].reshape(1, -1), weights["ln_in_b"].reshape(1, -1)))

    prepared = _prepare(weights, compute)
    for layer in range(DEPTH):
        h = _layer(h, {k: v[layer] for k, v in prepared.items()})
    return _loss_op(h[length - seq:], target)


_ANY = pl.BlockSpec(memory_space=pl.ANY)
_BLOCK_BYTES = 4 << 20


def _chip_exchange(arrays, all_to_all, name):
    n = len(arrays)

    def body(*refs):
        ins, outs = refs[:n], refs[n:2 * n]
        send_sems, recv_sems = refs[2 * n:]
        mx, my, mc = lax.axis_index("x"), lax.axis_index("y"), lax.axis_index("c")
        me = 2 * mx + my
        peers = [(1 - mx, my), (mx, 1 - my), (1 - mx, 1 - my)]
        sends = []
        for a, (src, dst) in enumerate(zip(ins, outs)):
            for k, (px, py) in enumerate(peers):
                cp = pltpu.make_async_remote_copy(
                    src_ref=src.at[2 * px + py] if all_to_all else src, dst_ref=dst.at[me],
                    send_sem=send_sems.at[a, k], recv_sem=recv_sems.at[a, k],
                    device_id=(px, py, mc), device_id_type=MESH)
                cp.start()
                sends.append(cp)
        for a, (src, dst) in enumerate(zip(ins, outs)):
            for k, (px, py) in enumerate(peers):
                pltpu.make_async_remote_copy(
                    src_ref=src.at[me] if all_to_all else src, dst_ref=dst.at[2 * px + py],
                    send_sem=send_sems.at[a, k], recv_sem=recv_sems.at[a, k],
                    device_id=(px, py, mc), device_id_type=MESH).wait_recv()
        for cp in sends:
            cp.wait_send()

    out_shape = [jax.ShapeDtypeStruct(a.shape if all_to_all else (4,) + a.shape, a.dtype) for a in arrays]
    return pl.pallas_call(
        body, name=name, in_specs=[_ANY] * n, out_specs=[_ANY] * n, out_shape=out_shape,
        scratch_shapes=[pltpu.SemaphoreType.DMA((n, 3)), pltpu.SemaphoreType.DMA((n, 3))],
    )(*arrays)


def _gather_two_level(arrays, name):
    n = len(arrays)

    def body(*refs):
        ins, outs = refs[:n], refs[n:2 * n]
        send_sems, recv_sems = refs[2 * n:]
        mx, my, mc = lax.axis_index("x"), lax.axis_index("y"), lax.axis_index("c")
        me = 2 * mx + my
        sibling = (mx, my, 1 - mc)
        chips = [(1 - mx, my), (mx, 1 - my), (1 - mx, 1 - my)]

        def copy(a, k, src, chip, core, to):
            return pltpu.make_async_remote_copy(
                src_ref=src, dst_ref=outs[a].at[chip, core], send_sem=send_sems.at[a, k], recv_sem=recv_sems.at[a, k],
                device_id=to, device_id_type=MESH)

        sends = [copy(a, j, ins[a], me, mc, (px, py, mc)) for a in range(n) for j, (px, py) in enumerate(chips)]
        for cp in sends:
            cp.start()
        passed = []
        for a in range(n):
            for j, (px, py) in enumerate(chips):
                chip = 2 * px + py
                copy(a, j, ins[a], chip, mc, sibling).wait_recv()
                cp = copy(a, 3 + j, outs[a].at[chip, mc], chip, mc, sibling)
                cp.start()
                passed.append(cp)
        for a in range(n):
            for j, (px, py) in enumerate(chips):
                copy(a, 3 + j, ins[a], 2 * px + py, 1 - mc, sibling).wait_recv()
        for cp in sends + passed:
            cp.wait_send()

    return pl.pallas_call(
        body, name=name, in_specs=[_ANY] * n, out_specs=[_ANY] * n,
        out_shape=[jax.ShapeDtypeStruct((4, 2) + a.shape, a.dtype) for a in arrays],
        scratch_shapes=[pltpu.SemaphoreType.DMA((n, 6)), pltpu.SemaphoreType.DMA((n, 6))],
    )(*arrays)


def _core_exchange(arrays, other_half, name):
    n = len(arrays)

    def body(*refs):
        ins, outs = refs[:n], refs[n:2 * n]
        send_sems, recv_sems = refs[2 * n:]
        mc = lax.axis_index("c")
        sibling = (lax.axis_index("x"), lax.axis_index("y"), 1 - mc)
        copies = [pltpu.make_async_remote_copy(
            src_ref=s.at[1 - mc] if other_half else s, dst_ref=d, send_sem=send_sems.at[a], recv_sem=recv_sems.at[a],
            device_id=sibling, device_id_type=MESH) for a, (s, d) in enumerate(zip(ins, outs))]
        for cp in copies:
            cp.start()
        for cp in copies:
            cp.wait()

    return pl.pallas_call(
        body, name=name, in_specs=[_ANY] * n, out_specs=[_ANY] * n,
        out_shape=[jax.ShapeDtypeStruct(a.shape[1:] if other_half else a.shape, a.dtype) for a in arrays],
        scratch_shapes=[pltpu.SemaphoreType.DMA((n,)), pltpu.SemaphoreType.DMA((n,))],
    )(*arrays)


def _as_rows(a, lead=0):
    shp = a.shape
    return a.reshape(shp[:lead] + (-1, shp[-1]))


def _sum4_call(x, name):
    _, r, c = x.shape
    tr = _pick(r, [t for t in (512, 256, 128, 64, 32, 16, 8) if 16 * t * c <= _BLOCK_BYTES] + [r])

    def body(x_ref, o_ref):
        part = [x_ref[j].astype(F32) for j in range(4)]
        o_ref[...] = (part[0] + part[1]) + (part[2] + part[3])

    return pl.pallas_call(
        body, name=name, grid=(r // tr,),
        in_specs=[pl.BlockSpec((4, tr, c), lambda i: (0, i, 0))],
        out_specs=pl.BlockSpec((tr, c), lambda i: (i, 0)),
        out_shape=jax.ShapeDtypeStruct((r, c), F32),
        compiler_params=_cparams(("parallel",), 48),
    )(x)


def _add_to_bf16_call(a, b, name):
    _, r, c = a.shape
    tr = _pick(r, [t for t in (512, 256, 128, 64, 32, 16) if 16 * t * c <= _BLOCK_BYTES] + [r])

    def body(a_ref, b_ref, o_ref):
        o_ref[...] = (a_ref[...].astype(F32) + b_ref[...].astype(F32)).astype(BF16)

    spec = pl.BlockSpec((4, tr, c), lambda i: (0, i, 0))
    return pl.pallas_call(
        body, name=name, grid=(r // tr,), in_specs=[spec, spec], out_specs=spec,
        out_shape=jax.ShapeDtypeStruct(a.shape, BF16), compiler_params=_cparams(("parallel",), 48),
    )(a, b)


def _adam_call(w, grads, m, v, name):
    r, c = w.shape
    n_g = len(grads)
    tr = _pick(r, [t for t in (512, 256, 128, 64, 32, 16, 8) if 4 * t * c <= _BLOCK_BYTES // 4] + [r])
    bc1 = 1.0 - ADAM_B1 ** ADAM_STEP
    bc2 = 1.0 - ADAM_B2 ** ADAM_STEP

    def body(*refs):
        w_ref, g_refs = refs[0], refs[1:1 + n_g]
        m_ref, v_ref, g_out, d_out, m_out, v_out = refs[1 + n_g:]
        g = g_refs[0][...]
        for g_ref in g_refs[1:]:
            g = g + g_ref[...]
        m_new = ADAM_B1 * m_ref[...] + (1.0 - ADAM_B1) * g
        v_new = ADAM_B2 * v_ref[...] + (1.0 - ADAM_B2) * (g * g)
        m_hat = m_new / bc1
        v_hat = v_new / bc2
        g_out[...] = g
        d_out[...] = -ADAM_LR * (m_hat / (jnp.sqrt(v_hat) + ADAM_EPS) + ADAM_WD * w_ref[...])
        m_out[...] = m_new
        v_out[...] = v_new

    spec = pl.BlockSpec((tr, c), lambda i: (i, 0))
    return pl.pallas_call(
        body, name=name, grid=(r // tr,), in_specs=[spec] * (3 + n_g), out_specs=[spec] * 4,
        out_shape=[jax.ShapeDtypeStruct((r, c), F32)] * 4,
        compiler_params=_cparams(("parallel",), 48),
    )(w, *grads, m, v)


_DENSE = ("w_in", "w_branch")

_SHARDED = {"meta": (1, False), "w_in": (2, True), "s5_w_glu": (1, True), "ssd_conv_w": (2, False),
            "gdn_conv_w": (2, False), "w_branch": (3, True), "b_gate": (2, False), "w_out": (1, True)}


def _pack(arrs):
    flat = jnp.concatenate([a.reshape(-1) for a in arrs])
    n = flat.shape[0]
    rows = -(-n // (256 * LANES)) * 256
    return jnp.concatenate([flat, jnp.zeros((rows * LANES - n,), F32)]).reshape(rows, LANES)


def _unpack(packed, like):
    flat = packed.reshape(-1)
    out, off = [], 0
    for a in like:
        out.append(flat[off:off + a.size].reshape(a.shape))
        off += a.size
    return out


def kernel(x, meta, ln_in_g, ln_in_b, w_in, s5_a_re, s5_a_im, s5_log_step, s5_b_re, s5_b_im, s5_c_re, s5_c_im, s5_d, s5_w_glu, s5_b_glu, ssd_conv_w, ssd_conv_b, ssd_dt_bias, ssd_a_log, ssd_d, ssd_norm_g, gdn_conv_w, gdn_dt_bias, gdn_a_log, gdn_norm_g, w_branch, b_gate, w_out, ln_g, ln_b, loss_target, m_meta, m_ln_in_g, m_ln_in_b, m_w_in, m_s5_a_re, m_s5_a_im, m_s5_log_step, m_s5_b_re, m_s5_b_im, m_s5_c_re, m_s5_c_im, m_s5_d, m_s5_w_glu, m_s5_b_glu, m_ssd_conv_w, m_ssd_conv_b, m_ssd_dt_bias, m_ssd_a_log, m_ssd_d, m_ssd_norm_g, m_gdn_conv_w, m_gdn_dt_bias, m_gdn_a_log, m_gdn_norm_g, m_w_branch, m_b_gate, m_w_out, m_ln_g, m_ln_b, v_meta, v_ln_in_g, v_ln_in_b, v_w_in, v_s5_a_re, v_s5_a_im, v_s5_log_step, v_s5_b_re, v_s5_b_im, v_s5_c_re, v_s5_c_im, v_s5_d, v_s5_w_glu, v_s5_b_glu, v_ssd_conv_w, v_ssd_conv_b, v_ssd_dt_bias, v_ssd_a_log, v_ssd_d, v_ssd_norm_g, v_gdn_conv_w, v_gdn_dt_bias, v_gdn_a_log, v_gdn_norm_g, v_w_branch, v_b_gate, v_w_out, v_ln_g, v_ln_b):
    args = dict(locals())
    shards = {k: args[k] for k in _WEIGHT_KEYS}
    moms = {k: (args["m_" + k], args["v_" + k]) for k in _WEIGHT_KEYS}

    core = lax.axis_index("c")
    halves = lambda a: a.reshape((2, a.shape[0] // 2) + a.shape[1:])

    names = list(_SHARDED)
    sent = [lax.dynamic_index_in_dim(halves(shards[k]), core, 0, keepdims=False) for k in names]
    sent = [s.astype(BF16) if _SHARDED[k][1] else s for k, s in zip(names, sent)]
    gathered = _gather_two_level(sent, "gather_weights")
    my_chip = 2 * lax.axis_index("x") + lax.axis_index("y")
    own_block = lambda blocks, mine: lax.dynamic_update_index_in_dim(blocks, mine.astype(blocks.dtype), my_chip, 0)
    full, compute = dict(shards), {}
    for k, g in zip(names, gathered):
        shp, ax = shards[k].shape, _SHARDED[k][0]
        g = own_block(g, halves(shards[k]))
        gathered_k = jnp.concatenate([g[j].reshape(shp) for j in range(4)], axis=ax)
        full[k] = gathered_k.astype(F32)
        if k in _DENSE:
            compute[k] = gathered_k

    loss, (grads, grad_x) = jax.value_and_grad(_local_loss, argnums=(0, 1))(full, x[0], loss_target[0], compute)
    loss = lax.psum(loss, ("x", "y", "c"))

    blocks = []
    for k in names:
        per_chip = jnp.stack(jnp.split(grads[k].astype(BF16), 4, axis=_SHARDED[k][0]), axis=0)
        blocks.append(jnp.moveaxis(_as_rows(per_chip.reshape((4, 2, -1) + per_chip.shape[2:]), 2), 1, 0))
    theirs = _core_exchange(blocks, True, "swap_halves")
    mine = [lax.dynamic_index_in_dim(b, core, 0, keepdims=False) for b in blocks]
    chip_sums = [_add_to_bf16_call(a, b, "sum_cores_" + k) for k, a, b in zip(names, mine, theirs)]
    arrived = _chip_exchange(chip_sums, True, "scatter_grads")
    arrived = [own_block(a, lax.dynamic_index_in_dim(s, my_chip, 0, keepdims=False)) for a, s in zip(arrived, chip_sums)]
    owned = [_sum4_call(a, "sum_chips_" + k) for k, a in zip(names, arrived)]
    others = _core_exchange(owned, False, "swap_owned")
    shared = [jnp.concatenate([jnp.where(core == 0, a, b), jnp.where(core == 0, b, a)], axis=0)
              for a, b in zip(owned, others)]

    small_names = [k for k in _WEIGHT_KEYS if k not in _SHARDED]
    packed = _pack([grads[k] for k in small_names])
    (packed4,) = _chip_exchange([packed], False, "gather_small_grads")
    small_sum = _sum4_call(own_block(packed4, packed), "sum_chips_small")
    (small_other,) = _core_exchange([small_sum], False, "swap_small")

    outs = {}
    for k, g in zip(names, shared):
        shp = shards[k].shape
        rows = _as_rows(shards[k]).shape
        res = _adam_call(_as_rows(shards[k]), [g.reshape(rows)], _as_rows(moms[k][0]), _as_rows(moms[k][1]), "adamw_" + k)
        outs[k] = [r.reshape(shp) for r in res]
    like = [shards[k] for k in small_names]
    res = _adam_call(_pack(like), [small_sum, small_other], _pack([moms[k][0] for k in small_names]),
                     _pack([moms[k][1] for k in small_names]), "adamw_small")
    for idx in range(4):
        for k, a in zip(small_names, _unpack(res[idx], like)):
            outs.setdefault(k, [None] * 4)[idx] = a

    result = [loss, grad_x[None]]
    for idx in range(4):
        result += [outs[k][idx] for k in _WEIGHT_KEYS]
    return tuple(result)
```

```python
import functools

import jax
import jax.numpy as jnp
from jax import lax
from jax.experimental import pallas as pl
from jax.experimental.pallas import tpu as pltpu

F32 = jnp.float32
BF16 = jnp.bfloat16
MESH = pl.DeviceIdType.MESH
F32_DOT = lax.Precision.HIGH

D_MODEL = 1024
DEPTH = 4
N_META = 16
CHUNK = 64
PAD = CHUNK - N_META
CONV_K = 4
HALO = 8
WIDTH = 768
S5_GROUPS, S5_GROUP, S5_STATE = 48, 16, 64
S5_BLOCKS = 3
S5_SEGMENTS = 8
SSD_HEADS, SSD_HEAD_DIM, SSD_GROUPS, SSD_STATE = 12, 64, 2, 128
SSD_PAIRS = 6
GDN_HEADS, GDN_DIM = 6, 128
LANES = 128
SMALL_ROWS = 24
ALPHA = (2 * DEPTH) ** 0.25
LN_EPS = 1e-5
P_S5U, P_S5Z, P_XBC, P_DT, P_SSDZ, P_QKV, P_GA, P_GB, P_GDNZ, P_GATE, P_END = (
    0, 768, 1536, 2816, 2828, 3596, 5900, 5906, 5912, 6680, 9752)
ADAM_LR, ADAM_B1, ADAM_B2, ADAM_EPS, ADAM_WD, ADAM_STEP = 0.001, 0.9, 0.999, 1e-08, 0.01, 10
NEG = -1e30


def _pick(n, cands):
    for c in cands:
        if n % c == 0:
            return c
    raise ValueError(f"no tile for {n} in {cands}")


def _cparams(sem, vmem_mb):
    return pltpu.CompilerParams(dimension_semantics=sem, vmem_limit_bytes=vmem_mb << 20)


_DIMS = {"nn": (((1,), (0,)), ((), ())), "nt": (((1,), (1,)), ((), ())), "tn": (((0,), (0,)), ((), ()))}


def _dot(a, b, mode, hi):
    if hi:
        prec = lax.Precision.HIGHEST if hi == "exact" else F32_DOT
        return lax.dot_general(a, b, _DIMS[mode], precision=prec, preferred_element_type=F32)
    return lax.dot_general(a.astype(BF16), b.astype(BF16), _DIMS[mode], preferred_element_type=F32)


@functools.partial(jax.custom_vjp, nondiff_argnums=(2, 3))
def _mm(a, b, mode="nn", hi=False):
    return _dot(a, b, mode, hi)


def _mm_fwd(a, b, mode, hi):
    return _dot(a, b, mode, hi), (a, b)


def _mm_bwd(mode, hi, res, g):
    a, b = res
    if mode == "nn":
        return _dot(g, b, "nt", hi), _dot(a, g, "tn", hi)
    if mode == "nt":
        return _dot(g, b, "nn", hi), _dot(g, a, "tn", hi)
    return _dot(b, g, "nt", hi), _dot(a, g, "nn", hi)


_mm.defvjp(_mm_fwd, _mm_bwd)


@functools.partial(jax.custom_vjp, nondiff_argnums=(1,))
def _roll_rows(x, k):
    return pltpu.roll(x, k % x.shape[0], 0)


def _roll_fwd(x, k):
    return _roll_rows(x, k), None


def _roll_bwd(k, _, g):
    return (_roll_rows(g, -k),)


_roll_rows.defvjp(_roll_fwd, _roll_bwd)


def _iota(shape, dim):
    return lax.broadcasted_iota(jnp.int32, shape, dim)


def _valid_rows(row0, n):
    return (row0 + _iota((n, 1), 0)) >= PAD


def _lane(x, idx):
    return jnp.sum(jnp.where(_iota(x.shape, 1) == idx, x, 0.0), axis=1, keepdims=True)


def _row(x, idx):
    return jnp.sum(jnp.where(_iota(x.shape, 0) == idx, x, 0.0), axis=0, keepdims=True)


def _layer_norm(z, g, b):
    mu = jnp.mean(z, axis=-1, keepdims=True)
    zc = z - mu
    var = jnp.mean(zc * zc, axis=-1, keepdims=True)
    return zc * lax.rsqrt(var + LN_EPS) * g + b


def _rms_norm(z, g):
    return z * lax.rsqrt(jnp.mean(z * z, axis=-1, keepdims=True) + LN_EPS) * g


def _causal_conv(halo, x, w, row0):
    t = x.shape[0]
    xc = jnp.concatenate([halo, x], axis=0)
    acc = None
    for j in range(CONV_K):
        term = _roll_rows(xc, CONV_K - 1 - j)[HALO:HALO + t] * _row(w, j)
        acc = term if acc is None else acc + term
    return acc, x[t - HALO:t]


def _tri(n, strict=False):
    r, c = _iota((n, n), 0), _iota((n, n), 1)
    return (r > c) if strict else (r >= c)


def _scan_op(step, name, *, tile, nb, row_kinds, param_kinds, carry_shapes, out_widths, keep_shapes=(), vmem_mb=48):
    n_rows, n_par, n_car, n_out = len(row_kinds), len(param_kinds), len(carry_shapes), len(out_widths)
    n_keep = len(keep_shapes)
    saved_shapes = tuple(carry_shapes) + tuple(keep_shapes)

    def dims(rows):
        for k, a in zip(row_kinds, rows):
            if k in "bs":
                return a.shape[0], a.shape[0] // tile
        raise ValueError("need a row input")

    def row_spec(kind, a, rev, nt):
        ti = (lambda i: nt - 1 - i) if rev else (lambda i: i)
        if kind == "b":
            return pl.BlockSpec((tile, a.shape[1] // nb), lambda b, i: (ti(i), b))
        if kind == "s":
            return pl.BlockSpec((tile, a.shape[1]), lambda b, i: (ti(i), 0))
        return pl.BlockSpec((None, a.shape[1], a.shape[2]), lambda b, i: (ti(i), 0, 0))

    def par_spec(kind, a):
        if kind == "b":
            return pl.BlockSpec((None, a.shape[1], a.shape[2]), lambda b, i: (b, 0, 0))
        return pl.BlockSpec(a.shape, lambda b, i: (0, 0))

    def fwd_call(rows, params):
        length, nt = dims(rows)

        def body(*refs):
            r_in = refs[:n_rows]
            p_in = refs[n_rows:n_rows + n_par]
            o_out = refs[n_rows + n_par:n_rows + n_par + n_out]
            s_out = refs[n_rows + n_par + n_out:n_rows + n_par + n_out + n_car + n_keep]
            c_scr = refs[n_rows + n_par + n_out + n_car + n_keep:]
            b, i = pl.program_id(0), pl.program_id(1)

            if n_car:
                @pl.when(i == 0)
                def _():
                    for c in c_scr:
                        c[...] = jnp.zeros_like(c)

            cin = tuple(c[...] for c in c_scr)
            for s, c in zip(s_out, cin):
                s[...] = c
            res = step(cin, tuple(r[...] for r in r_in), tuple(p[...] for p in p_in), b, i * tile)
            new_c, outs = res[0], res[1]
            for c, v in zip(c_scr, new_c):
                c[...] = v
            for o, v in zip(o_out, outs):
                o[...] = v
            if n_keep:
                for s, v in zip(s_out[n_car:], res[2]):
                    s[...] = v

        out_shape = [jax.ShapeDtypeStruct((length, nb * w), F32) for w in out_widths]
        out_shape += [jax.ShapeDtypeStruct((nb, nt) + tuple(s), F32) for s in saved_shapes]
        out_specs = [pl.BlockSpec((tile, w), lambda b, i: (i, b)) for w in out_widths]
        out_specs += [pl.BlockSpec((None, None) + tuple(s), lambda b, i: (b, i, 0, 0)) for s in saved_shapes]
        res = pl.pallas_call(
            body, name=name + "_fwd", grid=(nb, nt),
            in_specs=[row_spec(k, a, False, nt) for k, a in zip(row_kinds, rows)]
            + [par_spec(k, a) for k, a in zip(param_kinds, params)],
            out_specs=out_specs, out_shape=out_shape,
            scratch_shapes=[pltpu.VMEM(tuple(s), F32) for s in carry_shapes],
            compiler_params=_cparams(("arbitrary", "arbitrary"), vmem_mb),
        )(*rows, *params)
        return tuple(res[:n_out]), tuple(res[n_out:])

    def bwd_call(rows, params, saved, douts):
        length, nt = dims(rows)

        def body(*refs):
            k0 = 0
            r_in = refs[k0:k0 + n_rows]; k0 += n_rows
            p_in = refs[k0:k0 + n_par]; k0 += n_par
            s_in = refs[k0:k0 + n_car]; k0 += n_car
            k_in = refs[k0:k0 + n_keep]; k0 += n_keep
            g_in = refs[k0:k0 + n_out]; k0 += n_out
            dr_out = refs[k0:k0 + n_rows]; k0 += n_rows
            dp_out = refs[k0:k0 + n_par]; k0 += n_par
            dc_scr = refs[k0:]
            b, i = pl.program_id(0), pl.program_id(1)
            row0 = (nt - 1 - i) * tile

            @pl.when(i == 0)
            def _():
                for c in dc_scr:
                    c[...] = jnp.zeros_like(c)
                for p in dp_out:
                    p[...] = jnp.zeros_like(p)

            def f(c, r, p):
                if n_keep:
                    return step(c, r, p, b, row0, kept=tuple(k[...] for k in k_in))[:2]
                return step(c, r, p, b, row0)

            _, vjp = jax.vjp(f, tuple(s[...] for s in s_in), tuple(r[...] for r in r_in),
                             tuple(p[...] for p in p_in))
            dc, dr, dp = vjp((tuple(c[...] for c in dc_scr), tuple(g[...] for g in g_in)))
            for c, v in zip(dc_scr, dc):
                c[...] = v
            for o, v in zip(dr_out, dr):
                o[...] = v
            for o, v in zip(dp_out, dp):
                o[...] += v

        rev = lambda i: nt - 1 - i
        in_specs = [row_spec(k, a, True, nt) for k, a in zip(row_kinds, rows)]
        in_specs += [par_spec(k, a) for k, a in zip(param_kinds, params)]
        in_specs += [pl.BlockSpec((None, None) + tuple(s), lambda b, i: (b, rev(i), 0, 0)) for s in saved_shapes]
        in_specs += [pl.BlockSpec((tile, w), lambda b, i: (rev(i), b)) for w in out_widths]
        out_shape, out_specs = [], []
        for k, a in zip(row_kinds, rows):
            if k == "b":
                out_shape.append(jax.ShapeDtypeStruct(a.shape, F32))
                out_specs.append(pl.BlockSpec((tile, a.shape[1] // nb), lambda b, i: (rev(i), b)))
            elif k == "s":
                out_shape.append(jax.ShapeDtypeStruct((nb,) + a.shape, F32))
                out_specs.append(pl.BlockSpec((None, tile, a.shape[1]), lambda b, i: (b, rev(i), 0)))
            else:
                out_shape.append(jax.ShapeDtypeStruct((nb,) + a.shape, F32))
                out_specs.append(pl.BlockSpec((None, None, a.shape[1], a.shape[2]), lambda b, i: (b, rev(i), 0, 0)))
        for k, a in zip(param_kinds, params):
            shp = a.shape[1:] if k == "b" else a.shape
            out_shape.append(jax.ShapeDtypeStruct((nb,) + tuple(shp), F32))
            out_specs.append(pl.BlockSpec((None,) + tuple(shp), lambda b, i: (b, 0, 0)))
        res = pl.pallas_call(
            body, name=name + "_bwd", grid=(nb, nt), in_specs=in_specs, out_specs=out_specs, out_shape=out_shape,
            scratch_shapes=[pltpu.VMEM(tuple(s), F32) for s in carry_shapes],
            compiler_params=_cparams(("arbitrary", "arbitrary"), vmem_mb),
        )(*rows, *params, *saved, *douts)
        fold = (lambda a: a[0]) if nb == 1 else (lambda a: jnp.sum(a, axis=0))
        drows = tuple(r if k == "b" else fold(r) for k, r in zip(row_kinds, res[:n_rows]))
        dpars = tuple(p if k == "b" else fold(p) for k, p in zip(param_kinds, res[n_rows:]))
        return drows, dpars

    @jax.custom_vjp
    def op(rows, params):
        return fwd_call(rows, params)[0]

    def op_fwd(rows, params):
        outs, saved = fwd_call(rows, params)
        return outs, (rows, params, saved)

    def op_bwd(res, douts):
        rows, params, saved = res
        return bwd_call(rows, params, saved, tuple(douts))

    op.defvjp(op_fwd, op_bwd)
    return op


_WIDE = 1280


def _mm_rows(m, n):
    return _pick(m, (832, 128)) if n <= _WIDE else _pick(m, (416, 128))


def _mm_fwd_call(x, w, name):
    m, k = x.shape
    n = w.shape[1]
    tm = _mm_rows(m, n)

    def body(x_ref, w_ref, o_ref):
        o_ref[...] = _dot(x_ref[...], w_ref[...], "nn", False)

    return pl.pallas_call(
        body, name=name, grid=(m // tm,),
        in_specs=[pl.BlockSpec((tm, k), lambda i: (i, 0)), pl.BlockSpec((k, n), lambda i: (0, 0))],
        out_specs=pl.BlockSpec((tm, n), lambda i: (i, 0)),
        out_shape=jax.ShapeDtypeStruct((m, n), F32),
        compiler_params=_cparams(("parallel",), 48),
    )(x, w)


def _mm_bwd_call(g, x, w, name):
    m, n = g.shape
    k = w.shape[0]
    tm = _mm_rows(m, n)

    def body(g_ref, x_ref, w_ref, dx_ref, dw_ref):
        @pl.when(pl.program_id(0) == 0)
        def _():
            dw_ref[...] = jnp.zeros_like(dw_ref)

        g = g_ref[...].astype(BF16)
        dx_ref[...] = _dot(g, w_ref[...], "nt", False)
        x = x_ref[...]
        step = _pick(n, (768, 640, 128))
        for c0 in range(0, n, step):
            dw_ref[:, c0:c0 + step] += _dot(x, g[:, c0:c0 + step], "tn", False)

    return pl.pallas_call(
        body, name=name, grid=(m // tm,),
        in_specs=[pl.BlockSpec((tm, n), lambda i: (i, 0)), pl.BlockSpec((tm, k), lambda i: (i, 0)),
                  pl.BlockSpec((k, n), lambda i: (0, 0))],
        out_specs=[pl.BlockSpec((tm, k), lambda i: (i, 0)), pl.BlockSpec((k, n), lambda i: (0, 0))],
        out_shape=[jax.ShapeDtypeStruct((m, k), F32), jax.ShapeDtypeStruct((k, n), F32)],
        compiler_params=_cparams(("arbitrary",), 56),
    )(g, x, w)


def _dense(name):
    @jax.custom_vjp
    def op(x, w, xb, wb):
        return _mm_fwd_call(xb, wb, name + "_fwd")

    def op_fwd(x, w, xb, wb):
        return _mm_fwd_call(xb, wb, name + "_fwd"), (xb, wb)

    def op_bwd(res, g):
        xb, wb = res
        dx, dw = _mm_bwd_call(g, xb, wb, name + "_bwd")
        return dx, dw, jnp.zeros_like(xb), jnp.zeros_like(wb)

    op.defvjp(op_fwd, op_bwd)
    return op


def _bf16_copy(v):
    return lax.stop_gradient(v).astype(BF16)


def _merge_fwd_call(ys, gates, w, bias):
    m, k = ys[0].shape
    d = gates[0].shape[1]
    tm = _pick(m, (416, 128))

    def body(ya, yb, yc, ga, gb, gc, w_ref, b_ref, o_ref):
        acc = None
        for n, (y_ref, g_ref) in enumerate(((ya, ga), (yb, gb), (yc, gc))):
            term = jax.nn.sigmoid(g_ref[...] + b_ref[:, n * d:(n + 1) * d]) * _dot(y_ref[...], w_ref[n], "nn", False)
            acc = term if acc is None else acc + term
        o_ref[...] = acc

    rows = lambda width: pl.BlockSpec((tm, width), lambda i: (i, 0))
    return pl.pallas_call(
        body, name="gated_merge_fwd", grid=(m // tm,),
        in_specs=[rows(k)] * 3 + [rows(d)] * 3 + [pl.BlockSpec(w.shape, lambda i: (0, 0, 0)),
                                                 pl.BlockSpec(bias.shape, lambda i: (0, 0))],
        out_specs=rows(d), out_shape=jax.ShapeDtypeStruct((m, d), F32),
        compiler_params=_cparams(("parallel",), 48),
    )(*ys, *gates, w, bias)


def _merge_bwd_call(y, gate, dm, w, bias, name):
    m, k = y.shape
    d = gate.shape[1]
    tm = _pick(m, (416, 128))

    def body(y_ref, g_ref, dm_ref, w_ref, b_ref, dy_ref, dg_ref, dw_ref, db_ref):
        @pl.when(pl.program_id(0) == 0)
        def _():
            dw_ref[...] = jnp.zeros_like(dw_ref)
            db_ref[...] = jnp.zeros_like(db_ref)

        yv, dmv = y_ref[...], dm_ref[...]
        s = jax.nn.sigmoid(g_ref[...] + b_ref[...])
        d_gate = dmv * _dot(yv, w_ref[...], "nn", False) * s * (1.0 - s)
        dg_ref[...] = d_gate
        db_ref[...] += jnp.sum(d_gate, axis=0, keepdims=True)
        d_out = (dmv * s).astype(BF16)
        dy_ref[...] = _dot(d_out, w_ref[...], "nt", False)
        dw_ref[...] += _dot(yv, d_out, "tn", False)

    rows = lambda width: pl.BlockSpec((tm, width), lambda i: (i, 0))
    whole = lambda shape: pl.BlockSpec(shape, lambda i: (0, 0))
    return pl.pallas_call(
        body, name=name, grid=(m // tm,),
        in_specs=[rows(k), rows(d), rows(d), whole((k, d)), whole((1, d))],
        out_specs=[rows(k), rows(d), whole((k, d)), whole((1, d))],
        out_shape=[jax.ShapeDtypeStruct((m, k), F32), jax.ShapeDtypeStruct((m, d), F32),
                   jax.ShapeDtypeStruct((k, d), F32), jax.ShapeDtypeStruct((1, d), F32)],
        compiler_params=_cparams(("arbitrary",), 48),
    )(y, gate, dm, w, bias)


@jax.custom_vjp
def _gated_merge(ys, gates, w, bias, ysb, wb):
    return _merge_fwd_call(ysb, gates, wb, bias)


def _gated_merge_fwd(ys, gates, w, bias, ysb, wb):
    return _merge_fwd_call(ysb, gates, wb, bias), (gates, bias, ysb, wb)


def _gated_merge_bwd(res, dm):
    gates, bias, ysb, wb = res
    d = gates[0].shape[1]
    parts = [_merge_bwd_call(ysb[n], gates[n], dm, wb[n], bias[:, n * d:(n + 1) * d], "gated_merge_bwd_" + "abc"[n])
             for n in range(3)]
    dys, dgs, dws, dbs = zip(*parts)
    return (tuple(dys), tuple(dgs), jnp.stack(dws), jnp.concatenate(dbs, axis=1),
            tuple(jnp.zeros_like(v) for v in ysb), jnp.zeros_like(wb))


_gated_merge.defvjp(_gated_merge_fwd, _gated_merge_bwd)


def _ln_in_step(c, rows, params, b, row0):
    (z,), (g, bb) = rows, params
    return (), (jnp.where(_valid_rows(row0, z.shape[0]), _layer_norm(z, g, bb), 0.0),)


def _ln_res_step(c, rows, params, b, row0):
    (h, merged), (w_out, g, bb) = rows, params
    return (), (jnp.where(_valid_rows(row0, h.shape[0]), _layer_norm(ALPHA * h + _mm(merged, w_out), g, bb), 0.0),)


def _s5_prep_step(c, rows, params, b, row0):
    a_re, a_im, log_step, b_re, b_im = rows
    lam_re = jnp.minimum(a_re, -1e-4)
    lam_im = a_im
    step = jnp.exp(log_step)
    mag = jnp.exp(lam_re * step)
    abar_re, abar_im = mag * jnp.cos(lam_im * step), mag * jnp.sin(lam_im * step)
    den = lam_re * lam_re + lam_im * lam_im
    nr, ni = abar_re - 1.0, abar_im
    coef_re = (nr * lam_re + ni * lam_im) / den
    coef_im = (ni * lam_re - nr * lam_im) / den
    return (), (abar_re, abar_im, coef_re * b_re - coef_im * b_im, coef_re * b_im + coef_im * b_re)


def _s5_scan_step(c, rows, params, b, row0):
    (c_re, c_im), (u,) = c, rows
    bd_re, bd_im, a_re, a_im, cd_re, cd_im = params
    t = u.shape[0]
    steps = t // S5_SEGMENTS
    bu_re, bu_im = _mm(u, bd_re), _mm(u, bd_im)
    a_re, a_im = (jnp.broadcast_to(v, (S5_SEGMENTS, v.shape[1])) for v in (a_re, a_im))
    at = lambda v, i: v[S5_SEGMENTS * i:S5_SEGMENTS * (i + 1)]

    def advance(s_re, s_im, first):
        states = []
        for i in range(first, steps):
            s_re, s_im = a_re * s_re - a_im * s_im + at(bu_re, i), a_re * s_im + a_im * s_re + at(bu_im, i)
            states.append((s_re, s_im))
        return s_re, s_im, states

    s_re, s_im, _ = advance(at(bu_re, 0), at(bu_im, 0), 1)
    q_re = q_im = None
    p_re, p_im, left = a_re, a_im, steps
    while left:
        if left & 1:
            q_re, q_im = (p_re, p_im) if q_re is None else (q_re * p_re - q_im * p_im, q_re * p_im + q_im * p_re)
        p_re, p_im = p_re * p_re - p_im * p_im, 2.0 * p_re * p_im
        left >>= 1
    seg = _iota((S5_SEGMENTS, 1), 0)
    in_re = jnp.where(seg == 0, c_re, _roll_rows(s_re, 1))
    in_im = jnp.where(seg == 0, c_im, _roll_rows(s_im, 1))
    d = 1
    while d < S5_SEGMENTS:
        keep = seg >= d
        sh_re = jnp.where(keep, _roll_rows(in_re, d), 0.0)
        sh_im = jnp.where(keep, _roll_rows(in_im, d), 0.0)
        in_re, in_im = in_re + q_re * sh_re - q_im * sh_im, in_im + q_re * sh_im + q_im * sh_re
        q_re, q_im = q_re * q_re - q_im * q_im, 2.0 * q_re * q_im
        d *= 2
    _, _, full = advance(in_re, in_im, 0)
    y = _mm(jnp.concatenate([f[0] for f in full], axis=0), cd_re) - _mm(jnp.concatenate([f[1] for f in full], axis=0), cd_im)
    return (_row(full[-1][0], S5_SEGMENTS - 1), _row(full[-1][1], S5_SEGMENTS - 1)), (y,)


def _interleave(v, tile, inverse=False):
    length, width = v.shape
    shape = (length // tile, tile // S5_SEGMENTS, S5_SEGMENTS) if inverse else (length // tile, S5_SEGMENTS, tile // S5_SEGMENTS)
    return v.reshape(shape + (width,)).transpose(0, 2, 1, 3).reshape(length, width)


def _s5_post_step(c, rows, params, b, row0):
    (y, u, z), (d, w_glu, b_glu) = rows, params
    v = jax.nn.gelu(y + d * u)
    v = v * jax.nn.sigmoid(_mm(v, w_glu) + b_glu)
    return (), (v * jax.nn.silu(z),)


def _ssd_step(c, rows, params, b, row0):
    halo, state = c
    xbc_raw, z, small, small_t = rows
    cw, cb, d_l, bias_l, alog_l, bias_c, alog_c, norm_g = params
    t = xbc_raw.shape[0]
    grp = SSD_GROUPS * SSD_STATE
    valid = _valid_rows(row0, t)
    conv, halo2 = _causal_conv(halo, xbc_raw, cw, row0)
    act = jnp.where(valid, jax.nn.silu(conv + cb), 0.0)
    low = _iota((1, LANES), 1) < SSD_HEAD_DIM
    dt_all = jnp.where(valid, jax.nn.softplus(small + bias_l), 0.0)
    a_all = -jnp.exp(alog_l)
    valid_t = (row0 + _iota((1, t), 1)) >= PAD
    dta_t = jnp.where(valid_t, jax.nn.softplus(small_t + bias_c), 0.0) * (-jnp.exp(alog_c))
    acum_t = _mm(dta_t, jnp.where(_iota((t, t), 0) <= _iota((t, t), 1), 1.0, 0.0), "nn", True)
    causal = _tri(t)
    acum_all = _mm(jnp.where(causal, 1.0, 0.0), dt_all * a_all, "nn", True)
    last_all = _row(acum_all, t - 1)
    low_rows = _iota((LANES, 1), 0) < SSD_HEAD_DIM
    pairs, groups = range(SSD_PAIRS), range(SSD_GROUPS)
    grp_of = [p // (SSD_PAIRS // SSD_GROUPS) for p in pairs]
    bs = [act[:, WIDTH + g * SSD_STATE:WIDTH + (g + 1) * SSD_STATE] for g in groups]
    cs = [act[:, WIDTH + grp + g * SSD_STATE:WIDTH + grp + (g + 1) * SSD_STATE] for g in groups]
    scores = [_mm(cs[g], bs[g], "nt") for g in groups]
    per_lane = lambda v, p: jnp.where(low, v[:, 2 * p:2 * p + 1], v[:, 2 * p + 1:2 * p + 2])
    dt_l = [per_lane(dt_all, p) for p in pairs]
    acum_l = [per_lane(acum_all, p) for p in pairs]
    last_l = [per_lane(last_all, p) for p in pairs]
    st = [state[p * LANES:(p + 1) * LANES] for p in pairs]
    xd = [act[:, p * LANES:(p + 1) * LANES] * dt_l[p] for p in pairs]
    decay = [jnp.exp(jnp.where(causal, acum_all[:, h:h + 1] - _row(acum_t, h), NEG)) for h in range(SSD_HEADS)]
    y_lo = [_mm(scores[grp_of[p]] * decay[2 * p], jnp.where(low, xd[p], 0.0)) for p in pairs]
    y_hi = [_mm(scores[grp_of[p]] * decay[2 * p + 1], jnp.where(low, 0.0, xd[p])) for p in pairs]
    y_off = [_mm(cs[grp_of[p]], st[p], "nt") * jnp.exp(acum_l[p]) for p in pairs]
    new_st = [_mm(xd[p] * jnp.exp(last_l[p] - acum_l[p]), bs[grp_of[p]], "tn") for p in pairs]
    cd = jnp.exp(last_all)
    new_st = [st[p] * jnp.where(low_rows, cd[:, 2 * p:2 * p + 1], cd[:, 2 * p + 1:2 * p + 2]) + new_st[p] for p in pairs]
    y = jnp.concatenate([y_lo[p] + y_hi[p] + y_off[p] for p in pairs], axis=1) + act[:, :WIDTH] * d_l
    out = _rms_norm(y * jax.nn.silu(z), norm_g)
    return (halo2, jnp.concatenate(new_st, axis=0)), (out,)


@jax.custom_vjp
def _unit_lower_inverse(mats):
    return _neumann_inverse(mats)


def _inverse_fwd(mats):
    inv = _neumann_inverse(mats)
    return inv, inv


def _inverse_bwd(inv, g):
    left = [_dot(t, d, "tn", True) for t, d in zip(inv, g)]
    return ([-_dot(l, t, "nt", True) for l, t in zip(left, inv)],)


_unit_lower_inverse.defvjp(_inverse_fwd, _inverse_bwd)


@jax.custom_vjp
def _kept_inverse(mats, inv):
    return list(inv)


def _kept_inverse_fwd(mats, inv):
    return list(inv), list(inv)


def _kept_inverse_bwd(inv, g):
    return _inverse_bwd(inv, g)[0], [jnp.zeros_like(t) for t in inv]


_kept_inverse.defvjp(_kept_inverse_fwd, _kept_inverse_bwd)


def _neumann_inverse(mats):
    n = mats[0].shape[0]
    eye = jnp.where(_iota((n, n), 0) == _iota((n, n), 1), 1.0, 0.0)
    inv = [eye - a for a in mats]
    p = [_mm(a, a, "nn", True) for a in mats]
    k = 2
    while k < n:
        inv = [i + _mm(i, q, "nn", True) for i, q in zip(inv, p)]
        k *= 2
        if k < n:
            p = [_mm(q, q, "nn", True) for q in p]
    return inv


def _gdn_step(c, rows, params, b, row0, kept=None):
    halo, state = c
    qkv_raw, z, small, small_t = rows
    cw, bias_l, alog_l, bias_c, alog_c, norm_g = params
    t = qkv_raw.shape[0]
    valid = _valid_rows(row0, t)
    conv, halo2 = _causal_conv(halo, qkv_raw, cw, row0)
    act = jnp.where(valid, jax.nn.silu(conv), 0.0)
    beta_all = jnp.where(valid, jax.nn.sigmoid(small), 0.0)
    g_all = jnp.where(valid, -jnp.exp(alog_l) * jax.nn.softplus(small + bias_l), 0.0)
    valid_t = (row0 + _iota((1, t), 1)) >= PAD
    g_t = jnp.where(valid_t, -jnp.exp(alog_c) * jax.nn.softplus(small_t + bias_c), 0.0)
    causal, strict = _tri(t), _tri(t, True)
    gcum_all = _mm(jnp.where(causal, 1.0, 0.0), g_all, "nn", True)
    gcum_t = _mm(g_t, jnp.where(_iota((t, t), 0) <= _iota((t, t), 1), 1.0, 0.0), "nn", True)
    heads = range(GDN_HEADS)
    part = lambda h, n: act[:, n * WIDTH + h * GDN_DIM:n * WIDTH + (h + 1) * GDN_DIM]
    unit = lambda x: x * lax.rsqrt(jnp.sum(x * x, axis=-1, keepdims=True) + 1e-6)
    q = [unit(part(h, 0)) * (GDN_DIM ** -0.5) for h in heads]
    k = [unit(part(h, 1)) for h in heads]
    st = [state[h * GDN_DIM:(h + 1) * GDN_DIM] for h in heads]
    ia = [SSD_HEADS + h for h in heads]
    beta = [beta_all[:, ia[h] + GDN_HEADS:ia[h] + GDN_HEADS + 1] for h in heads]
    gcum = [gcum_all[:, ia[h]:ia[h] + 1] for h in heads]
    gamma = [jnp.exp(jnp.where(causal, gcum[h] - _row(gcum_t, ia[h]), NEG)) for h in heads]
    egc = [jnp.exp(gcum[h]) for h in heads]
    a_mat = [jnp.where(strict, _mm(k[h], k[h], "nt") * gamma[h] * beta[h], 0.0) for h in heads]
    if kept is None:
        inv = _unit_lower_inverse(a_mat)
    else:
        inv = _kept_inverse(a_mat, [kept[0][h * t:(h + 1) * t] for h in heads])
    rhs = [jnp.concatenate([part(h, 2) * beta[h], k[h] * (beta[h] * egc[h])], axis=1) for h in heads]
    sol = [_mm(inv[h], rhs[h], "nn", True) for h in heads]
    attn = [_mm(q[h], k[h], "nt") * gamma[h] for h in heads]
    from_state = [_mm(jnp.concatenate([sol[h][:, GDN_DIM:], q[h] * egc[h]], axis=0), st[h]) for h in heads]
    v_new = [sol[h][:, :GDN_DIM] - from_state[h][:t] for h in heads]
    o = [from_state[h][t:] + _mm(attn[h], v_new[h]) for h in heads]
    glast = [_row(gcum[h], t - 1) for h in heads]
    new_st = [st[h] * jnp.exp(glast[h]) + _mm(k[h] * jnp.exp(glast[h] - gcum[h]), v_new[h], "tn") for h in heads]
    out = jnp.concatenate([_rms_norm(o[h], norm_g) for h in heads], axis=1) * jax.nn.silu(z)
    return (halo2, jnp.concatenate(new_st, axis=0)), (out,), (jnp.concatenate(inv, axis=0),)


def _loss_tile(n):
    return _pick(n, (512, 256, 128, 64))


def _loss_fwd_call(y, tgt):
    n, d = y.shape
    tile = _loss_tile(n)

    def body(y_ref, t_ref, o_ref):
        @pl.when(pl.program_id(0) == 0)
        def _():
            o_ref[...] = jnp.zeros_like(o_ref)

        e = y_ref[...] - t_ref[...]
        o_ref[...] += jnp.sum(jnp.sum(e * e, axis=1, keepdims=True), axis=0, keepdims=True) * (0.5 / d)

    out = pl.pallas_call(
        body, name="loss_fwd", grid=(n // tile,),
        in_specs=[pl.BlockSpec((tile, d), lambda i: (i, 0)), pl.BlockSpec((tile, d), lambda i: (i, 0))],
        out_specs=pl.BlockSpec((8, LANES), lambda i: (0, 0)),
        out_shape=jax.ShapeDtypeStruct((8, LANES), F32),
        compiler_params=_cparams(("arbitrary",), 32),
    )(y, tgt)
    return out[0, 0]


def _loss_bwd_call(y, tgt, g):
    n, d = y.shape
    tile = _loss_tile(n)

    def body(y_ref, t_ref, g_ref, o_ref):
        o_ref[...] = (y_ref[...] - t_ref[...]) * (g_ref[...][0:1, 0:1] * (1.0 / d))

    return pl.pallas_call(
        body, name="loss_bwd", grid=(n // tile,),
        in_specs=[pl.BlockSpec((tile, d), lambda i: (i, 0)), pl.BlockSpec((tile, d), lambda i: (i, 0)),
                  pl.BlockSpec((8, LANES), lambda i: (0, 0))],
        out_specs=pl.BlockSpec((tile, d), lambda i: (i, 0)),
        out_shape=jax.ShapeDtypeStruct((n, d), F32),
        compiler_params=_cparams(("parallel",), 32),
    )(y, tgt, jnp.broadcast_to(g, (8, LANES)).astype(F32))


@jax.custom_vjp
def _loss_op(y, tgt):
    return _loss_fwd_call(y, tgt)


def _loss_op_fwd(y, tgt):
    return _loss_fwd_call(y, tgt), (y, tgt)


def _loss_op_bwd(res, g):
    y, tgt = res
    return _loss_bwd_call(y, tgt, g), jnp.zeros_like(tgt)


_loss_op.defvjp(_loss_op_fwd, _loss_op_bwd)


def _rowwise(step, name, tile, n_rows, n_params, out_widths, vmem_mb=48):
    return _scan_op(step, name, tile=tile, nb=1, row_kinds="s" * n_rows, param_kinds="s" * n_params,
                    carry_shapes=(), out_widths=out_widths, vmem_mb=vmem_mb)


def _block_diag(x, nblk):
    bsz, _, r, c = x.shape
    eye = jnp.eye(nblk, dtype=x.dtype)
    return jnp.einsum("bgrc,gh->bgrhc", x, eye).reshape(bsz, nblk * r, nblk * c)


_PROJ = {"s5u": (P_S5U, P_S5Z), "s5z": (P_S5Z, P_XBC), "xbc": (P_XBC, P_DT), "ssdz": (P_SSDZ, P_QKV),
         "qkv": (P_QKV, P_GA), "gdnz": (P_GDNZ, P_GATE), "gate_a": (P_GATE, P_GATE + D_MODEL),
         "gate_b": (P_GATE + D_MODEL, P_GATE + 2 * D_MODEL), "gate_c": (P_GATE + 2 * D_MODEL, P_END)}


def _prepare(weights, compute):
    w = weights
    zeros = lambda *shape: jnp.zeros((DEPTH,) + shape, F32)
    row3 = lambda v: v.reshape(DEPTH, 1, -1)
    p = {}
    for pre, w_in in (("w_", w["w_in"]), ("wb_", compute["w_in"])):
        seg = lambda a, bnd: w_in[:, :, a:bnd]
        p.update({pre + k: seg(a, bnd) for k, (a, bnd) in _PROJ.items()})
        p[pre + "small"] = jnp.concatenate(
            [seg(P_DT, P_SSDZ), seg(P_GA, P_GDNZ), zeros(D_MODEL, LANES - SMALL_ROWS).astype(w_in.dtype)], axis=2)
    p["wb_branch"] = compute["w_branch"]

    rows, wide = DEPTH * S5_GROUPS, S5_STATE * S5_GROUP
    flat = lambda v: v.reshape(rows, -1)
    rep = lambda v: jnp.repeat(flat(v), S5_GROUP, axis=1)
    prep = _rowwise(_s5_prep_step, "s5_prep", S5_GROUPS, 5, 0, (wide,) * 4)
    abar_re, abar_im, bbar_re, bbar_im = prep(
        (rep(w["s5_a_re"]), rep(w["s5_a_im"]), jnp.broadcast_to(flat(w["s5_log_step"]), (rows, wide)),
         flat(w["s5_b_re"]), flat(w["s5_b_im"])), ())
    gpb = S5_GROUPS // S5_BLOCKS
    lanes, chans, nblk = gpb * S5_STATE, gpb * S5_GROUP, DEPTH * S5_BLOCKS
    to_bd = lambda bb: _block_diag(bb.reshape(nblk, gpb, S5_STATE, S5_GROUP).transpose(0, 1, 3, 2), gpb).reshape(
        DEPTH, S5_BLOCKS, chans, lanes)
    to_cd = lambda cc: _block_diag(cc.reshape(nblk, gpb, S5_GROUP, S5_STATE).transpose(0, 1, 3, 2), gpb).reshape(
        DEPTH, S5_BLOCKS, lanes, chans)
    to_a = lambda a: a[:, ::S5_GROUP].reshape(DEPTH, S5_BLOCKS, 1, lanes)
    p.update(s5_bd_re=to_bd(bbar_re), s5_bd_im=to_bd(bbar_im), s5_a_re=to_a(abar_re), s5_a_im=to_a(abar_im),
             s5_cd_re=to_cd(w["s5_c_re"]), s5_cd_im=to_cd(w["s5_c_im"]),
             s5_d=row3(w["s5_d"]), s5_w_glu=w["s5_w_glu"], s5_b_glu=row3(w["s5_b_glu"]))

    bias = jnp.concatenate([w["ssd_dt_bias"], w["gdn_dt_bias"]], axis=1)
    alog = jnp.concatenate([w["ssd_a_log"], w["gdn_a_log"]], axis=1)
    on_lanes = lambda v: jnp.concatenate([v, zeros(LANES - v.shape[1])], axis=1).reshape(DEPTH, 1, LANES)
    on_rows = lambda v: jnp.concatenate([v, zeros(SMALL_ROWS - v.shape[1])], axis=1).reshape(DEPTH, SMALL_ROWS, 1)
    pad_w = lambda cw: jnp.concatenate([cw, zeros(HALO - CONV_K, cw.shape[2])], axis=1)
    p.update(bias_l=on_lanes(bias), alog_l=on_lanes(alog), bias_c=on_rows(bias), alog_c=on_rows(alog),
             ssd_cw=pad_w(w["ssd_conv_w"]), ssd_cb=row3(w["ssd_conv_b"]),
             ssd_d=row3(jnp.repeat(w["ssd_d"], SSD_HEAD_DIM, axis=1)), ssd_norm_g=row3(w["ssd_norm_g"]),
             gdn_cw=pad_w(w["gdn_conv_w"]), gdn_norm_g=row3(w["gdn_norm_g"]),
             w_branch=w["w_branch"], b_gate=row3(w["b_gate"]), w_out=w["w_out"], ln_g=row3(w["ln_g"]), ln_b=row3(w["ln_b"]))
    return p


def _layer(h, p):
    length = h.shape[0]
    nt = length // CHUNK
    t_row = _pick(length, (416, 128))
    t_s5 = _pick(length, (832, 128))
    hb = _bf16_copy(h)
    proj = {k: _dense("proj_" + k)(h, p["w_" + k], hb, p["wb_" + k]) for k in list(_PROJ) + ["small"]}
    small = proj["small"]
    small_t = small[:, :SMALL_ROWS].reshape(nt, CHUNK, SMALL_ROWS).transpose(0, 2, 1)

    lanes = p["s5_a_re"].shape[-1]
    s5_scan = _scan_op(_s5_scan_step, "s5_scan", tile=t_s5, nb=S5_BLOCKS, row_kinds="b", param_kinds="bbbbbb",
                       carry_shapes=((1, lanes), (1, lanes)), out_widths=(WIDTH // S5_BLOCKS,))
    (y_ssm,) = s5_scan((_interleave(proj["s5u"], t_s5),), (p["s5_bd_re"], p["s5_bd_im"], p["s5_a_re"], p["s5_a_im"],
                                                           p["s5_cd_re"], p["s5_cd_im"]))
    y_ssm = _interleave(y_ssm, t_s5, inverse=True)
    s5_post = _rowwise(_s5_post_step, "s5_post", t_row, 3, 3, (WIDTH,))
    (y_a,) = s5_post((y_ssm, proj["s5u"], proj["s5z"]), (p["s5_d"], p["s5_w_glu"], p["s5_b_glu"]))

    scalars = (p["bias_l"], p["alog_l"], p["bias_c"], p["alog_c"])
    ssd = _scan_op(_ssd_step, "ssd_scan", tile=CHUNK, nb=1, row_kinds="ssst", param_kinds="s" * 8,
                   carry_shapes=((HALO, P_DT - P_XBC), (SSD_HEADS * SSD_HEAD_DIM, SSD_STATE)), out_widths=(WIDTH,))
    (y_b,) = ssd((proj["xbc"], proj["ssdz"], small, small_t),
                 (p["ssd_cw"], p["ssd_cb"], p["ssd_d"]) + scalars + (p["ssd_norm_g"],))
    gdn = _scan_op(_gdn_step, "gdn_scan", tile=CHUNK, nb=1, row_kinds="ssst", param_kinds="s" * 6,
                   carry_shapes=((HALO, P_GA - P_QKV), (GDN_HEADS * GDN_DIM, GDN_DIM)), out_widths=(WIDTH,),
                   keep_shapes=((GDN_HEADS * CHUNK, CHUNK),))
    (y_c,) = gdn((proj["qkv"], proj["gdnz"], small, small_t), (p["gdn_cw"],) + scalars + (p["gdn_norm_g"],))

    ys = (y_a, y_b, y_c)
    merged = _gated_merge(ys, (proj["gate_a"], proj["gate_b"], proj["gate_c"]), p["w_branch"], p["b_gate"],
                          tuple(_bf16_copy(y) for y in ys), p["wb_branch"])
    ln = _rowwise(_ln_res_step, "out_proj_ln", t_row, 2, 3, (D_MODEL,))
    (h_new,) = ln((h, merged), (p["w_out"], p["ln_g"], p["ln_b"]))
    return h_new


_LAYER_KEYS = ("w_in", "s5_a_re", "s5_a_im", "s5_log_step", "s5_b_re", "s5_b_im", "s5_c_re", "s5_c_im", "s5_d",
               "s5_w_glu", "s5_b_glu", "ssd_conv_w", "ssd_conv_b", "ssd_dt_bias", "ssd_a_log", "ssd_d", "ssd_norm_g",
               "gdn_conv_w", "gdn_dt_bias", "gdn_a_log", "gdn_norm_g", "w_branch", "b_gate", "w_out", "ln_g", "ln_b")
_WEIGHT_KEYS = ("meta", "ln_in_g", "ln_in_b") + _LAYER_KEYS


def _local_loss(weights, x, target, compute):
    seq = x.shape[0]
    hcat = jnp.concatenate([jnp.zeros((PAD, D_MODEL), F32), weights["meta"], x], axis=0)
    length = hcat.shape[0]
    ln_in = _rowwise(_ln_in_step, "ln_in", _pick(length, (416, 256, 128)), 1, 2, (D_MODEL,))
    (h,) = ln_in((hcat,), (weights["ln_in_g"].reshape(1, -1), weights["ln_in_b"].reshape(1, -1)))

    prepared = _prepare(weights, compute)
    for layer in range(DEPTH):
        h = _layer(h, {k: v[layer] for k, v in prepared.items()})
    return _loss_op(h[length - seq:], target)


_ANY = pl.BlockSpec(memory_space=pl.ANY)
_BLOCK_BYTES = 4 << 20


def _chip_exchange(arrays, all_to_all, name):
    n = len(arrays)

    def body(*refs):
        ins, outs = refs[:n], refs[n:2 * n]
        send_sems, recv_sems = refs[2 * n:]
        mx, my, mc = lax.axis_index("x"), lax.axis_index("y"), lax.axis_index("c")
        me = 2 * mx + my
        peers = [(1 - mx, my), (mx, 1 - my), (1 - mx, 1 - my)]
        sends = []
        for a, (src, dst) in enumerate(zip(ins, outs)):
            for k, (px, py) in enumerate(peers):
                cp = pltpu.make_async_remote_copy(
                    src_ref=src.at[2 * px + py] if all_to_all else src, dst_ref=dst.at[me],
                    send_sem=send_sems.at[a, k], recv_sem=recv_sems.at[a, k],
                    device_id=(px, py, mc), device_id_type=MESH)
                cp.start()
                sends.append(cp)
        for a, (src, dst) in enumerate(zip(ins, outs)):
            for k, (px, py) in enumerate(peers):
                pltpu.make_async_remote_copy(
                    src_ref=src.at[me] if all_to_all else src, dst_ref=dst.at[2 * px + py],
                    send_sem=send_sems.at[a, k], recv_sem=recv_sems.at[a, k],
                    device_id=(px, py, mc), device_id_type=MESH).wait_recv()
        for cp in sends:
            cp.wait_send()

    out_shape = [jax.ShapeDtypeStruct(a.shape if all_to_all else (4,) + a.shape, a.dtype) for a in arrays]
    return pl.pallas_call(
        body, name=name, in_specs=[_ANY] * n, out_specs=[_ANY] * n, out_shape=out_shape,
        scratch_shapes=[pltpu.SemaphoreType.DMA((n, 3)), pltpu.SemaphoreType.DMA((n, 3))],
    )(*arrays)


def _gather_two_level(arrays, name):
    n = len(arrays)

    def body(*refs):
        ins, outs = refs[:n], refs[n:2 * n]
        send_sems, recv_sems = refs[2 * n:]
        mx, my, mc = lax.axis_index("x"), lax.axis_index("y"), lax.axis_index("c")
        me = 2 * mx + my
        sibling = (mx, my, 1 - mc)
        chips = [(1 - mx, my), (mx, 1 - my), (1 - mx, 1 - my)]

        def copy(a, k, src, chip, core, to):
            return pltpu.make_async_remote_copy(
                src_ref=src, dst_ref=outs[a].at[chip, core], send_sem=send_sems.at[a, k], recv_sem=recv_sems.at[a, k],
                device_id=to, device_id_type=MESH)

        sends = [copy(a, j, ins[a], me, mc, (px, py, mc)) for a in range(n) for j, (px, py) in enumerate(chips)]
        for cp in sends:
            cp.start()
        passed = []
        for a in range(n):
            for j, (px, py) in enumerate(chips):
                chip = 2 * px + py
                copy(a, j, ins[a], chip, mc, sibling).wait_recv()
                cp = copy(a, 3 + j, outs[a].at[chip, mc], chip, mc, sibling)
                cp.start()
                passed.append(cp)
        for a in range(n):
            for j, (px, py) in enumerate(chips):
                copy(a, 3 + j, ins[a], 2 * px + py, 1 - mc, sibling).wait_recv()
        for cp in sends + passed:
            cp.wait_send()

    return pl.pallas_call(
        body, name=name, in_specs=[_ANY] * n, out_specs=[_ANY] * n,
        out_shape=[jax.ShapeDtypeStruct((4, 2) + a.shape, a.dtype) for a in arrays],
        scratch_shapes=[pltpu.SemaphoreType.DMA((n, 6)), pltpu.SemaphoreType.DMA((n, 6))],
    )(*arrays)


def _core_exchange(arrays, other_half, name):
    n = len(arrays)

    def body(*refs):
        ins, outs = refs[:n], refs[n:2 * n]
        send_sems, recv_sems = refs[2 * n:]
        mc = lax.axis_index("c")
        sibling = (lax.axis_index("x"), lax.axis_index("y"), 1 - mc)
        copies = [pltpu.make_async_remote_copy(
            src_ref=s.at[1 - mc] if other_half else s, dst_ref=d, send_sem=send_sems.at[a], recv_sem=recv_sems.at[a],
            device_id=sibling, device_id_type=MESH) for a, (s, d) in enumerate(zip(ins, outs))]
        for cp in copies:
            cp.start()
        for cp in copies:
            cp.wait()

    return pl.pallas_call(
        body, name=name, in_specs=[_ANY] * n, out_specs=[_ANY] * n,
        out_shape=[jax.ShapeDtypeStruct(a.shape[1:] if other_half else a.shape, a.dtype) for a in arrays],
        scratch_shapes=[pltpu.SemaphoreType.DMA((n,)), pltpu.SemaphoreType.DMA((n,))],
    )(*arrays)


def _as_rows(a, lead=0):
    shp = a.shape
    return a.reshape(shp[:lead] + (-1, shp[-1]))


def _sum4_call(x, name):
    _, r, c = x.shape
    tr = _pick(r, [t for t in (512, 256, 128, 64, 32, 16, 8) if 16 * t * c <= _BLOCK_BYTES] + [r])

    def body(x_ref, o_ref):
        part = [x_ref[j].astype(F32) for j in range(4)]
        o_ref[...] = (part[0] + part[1]) + (part[2] + part[3])

    return pl.pallas_call(
        body, name=name, grid=(r // tr,),
        in_specs=[pl.BlockSpec((4, tr, c), lambda i: (0, i, 0))],
        out_specs=pl.BlockSpec((tr, c), lambda i: (i, 0)),
        out_shape=jax.ShapeDtypeStruct((r, c), F32),
        compiler_params=_cparams(("parallel",), 48),
    )(x)


def _add_to_bf16_call(a, b, name):
    _, r, c = a.shape
    tr = _pick(r, [t for t in (512, 256, 128, 64, 32, 16) if 16 * t * c <= _BLOCK_BYTES] + [r])

    def body(a_ref, b_ref, o_ref):
        o_ref[...] = (a_ref[...].astype(F32) + b_ref[...].astype(F32)).astype(BF16)

    spec = pl.BlockSpec((4, tr, c), lambda i: (0, i, 0))
    return pl.pallas_call(
        body, name=name, grid=(r // tr,), in_specs=[spec, spec], out_specs=spec,
        out_shape=jax.ShapeDtypeStruct(a.shape, BF16), compiler_params=_cparams(("parallel",), 48),
    )(a, b)


def _adam_call(w, grads, m, v, name):
    r, c = w.shape
    n_g = len(grads)
    tr = _pick(r, [t for t in (512, 256, 128, 64, 32, 16, 8) if 4 * t * c <= _BLOCK_BYTES // 4] + [r])
    bc1 = 1.0 - ADAM_B1 ** ADAM_STEP
    bc2 = 1.0 - ADAM_B2 ** ADAM_STEP

    def body(*refs):
        w_ref, g_refs = refs[0], refs[1:1 + n_g]
        m_ref, v_ref, g_out, d_out, m_out, v_out = refs[1 + n_g:]
        g = g_refs[0][...]
        for g_ref in g_refs[1:]:
            g = g + g_ref[...]
        m_new = ADAM_B1 * m_ref[...] + (1.0 - ADAM_B1) * g
        v_new = ADAM_B2 * v_ref[...] + (1.0 - ADAM_B2) * (g * g)
        m_hat = m_new / bc1
        v_hat = v_new / bc2
        g_out[...] = g
        d_out[...] = -ADAM_LR * (m_hat / (jnp.sqrt(v_hat) + ADAM_EPS) + ADAM_WD * w_ref[...])
        m_out[...] = m_new
        v_out[...] = v_new

    spec = pl.BlockSpec((tr, c), lambda i: (i, 0))
    return pl.pallas_call(
        body, name=name, grid=(r // tr,), in_specs=[spec] * (3 + n_g), out_specs=[spec] * 4,
        out_shape=[jax.ShapeDtypeStruct((r, c), F32)] * 4,
        compiler_params=_cparams(("parallel",), 48),
    )(w, *grads, m, v)


_DENSE = ("w_in", "w_branch")

_SHARDED = {"meta": (1, False), "w_in": (2, True), "s5_w_glu": (1, True), "ssd_conv_w": (2, False),
            "gdn_conv_w": (2, False), "w_branch": (3, True), "b_gate": (2, False), "w_out": (1, True)}


def _pack(arrs):
    flat = jnp.concatenate([a.reshape(-1) for a in arrs])
    n = flat.shape[0]
    rows = -(-n // (256 * LANES)) * 256
    return jnp.concatenate([flat, jnp.zeros((rows * LANES - n,), F32)]).reshape(rows, LANES)


def _unpack(packed, like):
    flat = packed.reshape(-1)
    out, off = [], 0
    for a in like:
        out.append(flat[off:off + a.size].reshape(a.shape))
        off += a.size
    return out


def kernel(x, meta, ln_in_g, ln_in_b, w_in, s5_a_re, s5_a_im, s5_log_step, s5_b_re, s5_b_im, s5_c_re, s5_c_im, s5_d, s5_w_glu, s5_b_glu, ssd_conv_w, ssd_conv_b, ssd_dt_bias, ssd_a_log, ssd_d, ssd_norm_g, gdn_conv_w, gdn_dt_bias, gdn_a_log, gdn_norm_g, w_branch, b_gate, w_out, ln_g, ln_b, loss_target, m_meta, m_ln_in_g, m_ln_in_b, m_w_in, m_s5_a_re, m_s5_a_im, m_s5_log_step, m_s5_b_re, m_s5_b_im, m_s5_c_re, m_s5_c_im, m_s5_d, m_s5_w_glu, m_s5_b_glu, m_ssd_conv_w, m_ssd_conv_b, m_ssd_dt_bias, m_ssd_a_log, m_ssd_d, m_ssd_norm_g, m_gdn_conv_w, m_gdn_dt_bias, m_gdn_a_log, m_gdn_norm_g, m_w_branch, m_b_gate, m_w_out, m_ln_g, m_ln_b, v_meta, v_ln_in_g, v_ln_in_b, v_w_in, v_s5_a_re, v_s5_a_im, v_s5_log_step, v_s5_b_re, v_s5_b_im, v_s5_c_re, v_s5_c_im, v_s5_d, v_s5_w_glu, v_s5_b_glu, v_ssd_conv_w, v_ssd_conv_b, v_ssd_dt_bias, v_ssd_a_log, v_ssd_d, v_ssd_norm_g, v_gdn_conv_w, v_gdn_dt_bias, v_gdn_a_log, v_gdn_norm_g, v_w_branch, v_b_gate, v_w_out, v_ln_g, v_ln_b):
    args = dict(locals())
    shards = {k: args[k] for k in _WEIGHT_KEYS}
    moms = {k: (args["m_" + k], args["v_" + k]) for k in _WEIGHT_KEYS}

    core = lax.axis_index("c")
    halves = lambda a: a.reshape((2, a.shape[0] // 2) + a.shape[1:])

    names = list(_SHARDED)
    sent = [lax.dynamic_index_in_dim(halves(shards[k]), core, 0, keepdims=False) for k in names]
    sent = [s.astype(BF16) if _SHARDED[k][1] else s for k, s in zip(names, sent)]
    gathered = _gather_two_level(sent, "gather_weights")
    my_chip = 2 * lax.axis_index("x") + lax.axis_index("y")
    own_block = lambda blocks, mine: lax.dynamic_update_index_in_dim(blocks, mine.astype(blocks.dtype), my_chip, 0)
    full, compute = dict(shards), {}
    for k, g in zip(names, gathered):
        shp, ax = shards[k].shape, _SHARDED[k][0]
        g = own_block(g, halves(shards[k]))
        gathered_k = jnp.concatenate([g[j].reshape(shp) for j in range(4)], axis=ax)
        full[k] = gathered_k.astype(F32)
        if k in _DENSE:
            compute[k] = gathered_k

    loss, (grads, grad_x) = jax.value_and_grad(_local_loss, argnums=(0, 1))(full, x[0], loss_target[0], compute)
    loss = lax.psum(loss, ("x", "y", "c"))

    blocks = []
    for k in names:
        per_chip = jnp.stack(jnp.split(grads[k].astype(BF16), 4, axis=_SHARDED[k][0]), axis=0)
        blocks.append(jnp.moveaxis(_as_rows(per_chip.reshape((4, 2, -1) + per_chip.shape[2:]), 2), 1, 0))
    theirs = _core_exchange(blocks, True, "swap_halves")
    mine = [lax.dynamic_index_in_dim(b, core, 0, keepdims=False) for b in blocks]
    chip_sums = [_add_to_bf16_call(a, b, "sum_cores_" + k) for k, a, b in zip(names, mine, theirs)]
    arrived = _chip_exchange(chip_sums, True, "scatter_grads")
    arrived = [own_block(a, lax.dynamic_index_in_dim(s, my_chip, 0, keepdims=False)) for a, s in zip(arrived, chip_sums)]
    owned = [_sum4_call(a, "sum_chips_" + k) for k, a in zip(names, arrived)]
    others = _core_exchange(owned, False, "swap_owned")
    shared = [jnp.concatenate([jnp.where(core == 0, a, b), jnp.where(core == 0, b, a)], axis=0)
              for a, b in zip(owned, others)]

    small_names = [k for k in _WEIGHT_KEYS if k not in _SHARDED]
    packed = _pack([grads[k] for k in small_names])
    (packed4,) = _chip_exchange([packed], False, "gather_small_grads")
    small_sum = _sum4_call(own_block(packed4, packed), "sum_chips_small")
    (small_other,) = _core_exchange([small_sum], False, "swap_small")

    outs = {}
    for k, g in zip(names, shared):
        shp = shards[k].shape
        rows = _as_rows(shards[k]).shape
        res = _adam_call(_as_rows(shards[k]), [g.reshape(rows)], _as_rows(moms[k][0]), _as_rows(moms[k][1]), "adamw_" + k)
        outs[k] = [r.reshape(shp) for r in res]
    like = [shards[k] for k in small_names]
    res = _adam_call(_pack(like), [small_sum, small_other], _pack([moms[k][0] for k in small_names]),
                     _pack([moms[k][1] for k in small_names]), "adamw_small")
    for idx in range(4):
        for k, a in zip(small_names, _unpack(res[idx], like)):
            outs.setdefault(k, [None] * 4)[idx] = a

    result = [loss, grad_x[None]]
    for idx in range(4):
        result += [outs[k][idx] for k in _WEIGHT_KEYS]
    return tuple(result)
```
